```python
import math
import jax
import jax.numpy as jnp
from jax import lax
import numpy as np

D_MODEL = 1024
BATCH = 8
SEQ = 2048
DEPTH = 2

CTX_LEN = 256
GRID_W = 64
MIX_WIDTH = D_MODEL
BRANCH_WIDTH = MIX_WIDTH // 4
DA_HEADS = 4
DA_VDIM = BRANCH_WIDTH // DA_HEADS
DA_QK = DA_VDIM // 2
HG_HEADS = 4
HG_DK = BRANCH_WIDTH // HG_HEADS
HG_DV = BRANCH_WIDTH // HG_HEADS
HG_CHUNK = 16
SW_HEADS = 4
SW_KV_HEADS = 2
SW_HD = BRANCH_WIDTH // SW_HEADS
SW_WINDOW = 128
SW_BLOCK = 128
ML_HEADS = 4
ML_HD = BRANCH_WIDTH // ML_HEADS
ML_CHUNK = 64

ATTN_QBLOCK = 128
ROPE_BASE = 10000.0
NORM_EPS = 1e-6
F32 = jnp.float32

PROJ_SIZES = (
    DA_HEADS * DA_VDIM, DA_HEADS * DA_VDIM, DA_HEADS * DA_VDIM,
    HG_HEADS * HG_DK, HG_HEADS * HG_DK, HG_HEADS * HG_DK, HG_HEADS * HG_DV,
    SW_HEADS * SW_HD, SW_KV_HEADS * SW_HD, SW_KV_HEADS * SW_HD,
    ML_HEADS * ML_HD, ML_HEADS * ML_HD, ML_HEADS * ML_HD,
    ML_HEADS, ML_HEADS, ML_HEADS, ML_HEADS,
    ML_HEADS * ML_HD,
    MIX_WIDTH,
)
PROJ_WIDTH = sum(PROJ_SIZES)

kernel_name = "hybrid_parallel_diffattn_hgrn2_swa_mlstm"


def rmsnorm(x, g):
    xf = x.astype(F32)
    y = xf * lax.rsqrt(jnp.mean(xf * xf, axis=-1, keepdims=True) + NORM_EPS)
    return (y * g.astype(F32)).astype(x.dtype)


def to_heads(t, n_heads):
    b, n, w = t.shape
    return t.reshape(b, n, n_heads, w // n_heads).transpose(0, 2, 1, 3)


def from_heads(t):
    b, h, n, d = t.shape
    return t.transpose(0, 2, 1, 3).reshape(b, n, h * d)


def flip_time(t):
    return jnp.flip(t, axis=2)


def split_proj(p):
    return jnp.split(p, np.cumsum(PROJ_SIZES)[:-1].tolist(), axis=-1)


def axial_rope_tables(n_tok, dim):
    rows = n_tok // GRID_W
    row = jnp.repeat(jnp.arange(rows, dtype=F32), GRID_W)
    col = jnp.tile(jnp.arange(GRID_W, dtype=F32), rows)
    half = dim // 2
    inv = ROPE_BASE ** (-jnp.arange(0, half, 2, dtype=F32) / half)
    ar = row[:, None] * inv
    ac = col[:, None] * inv
    return (jnp.cos(ar), jnp.sin(ar), jnp.cos(ac), jnp.sin(ac))


def rope2d(x, tabs):
    cr, sr, cc, sc = [t.astype(x.dtype) for t in tabs]
    x1, x2, x3, x4 = jnp.split(x, 4, axis=-1)
    return jnp.concatenate([x1 * cr - x2 * sr, x2 * cr + x1 * sr,
                            x3 * cc - x4 * sc, x4 * cc + x3 * sc], axis=-1)


def diff_attention(qa, ka, va, qc, kc, vc, lam_p, g, layer_idx, tabs, need_ctx_out):
    b, n_tok, _ = qa.shape
    lam_init = 0.8 - 0.6 * math.exp(-0.3 * layer_idx)
    lp = lam_p.astype(F32)
    lam = jnp.exp(jnp.sum(lp[0] * lp[1])) - jnp.exp(jnp.sum(lp[2] * lp[3])) + lam_init
    scale = DA_QK ** -0.5

    def maps(t):
        return t.reshape(t.shape[0], t.shape[1], DA_HEADS, 2, DA_QK).transpose(3, 0, 2, 1, 4)

    q = rope2d(maps(qa), tabs)
    k = rope2d(maps(ka), tabs)
    q_c, k_c = maps(qc), maps(kc)
    v, v_c = to_heads(va, DA_HEADS), to_heads(vc, DA_HEADS)
    keys = jnp.concatenate([k, k_c], axis=3)
    vals = jnp.concatenate([v, v_c], axis=2)

    def attend(qs, ks, vs):
        s = jnp.einsum('mbhqd,mbhkd->mbhqk', qs, ks).astype(F32) * scale
        p = jax.nn.softmax(s, axis=-1)
        w = (p[0] - lam * p[1]).astype(vs.dtype)
        return jnp.einsum('bhqk,bhkv->bhqv', w, vs)

    nb = n_tok // ATTN_QBLOCK
    qb = jnp.moveaxis(q.reshape(2, b, DA_HEADS, nb, ATTN_QBLOCK, DA_QK), 3, 0)
    o = lax.map(lambda qq: attend(qq, keys, vals), qb)
    o = jnp.moveaxis(o, 0, 2).reshape(b, DA_HEADS, n_tok, DA_VDIM)
    y = from_heads(rmsnorm(o, g) * (1.0 - lam_init))
    yc = None
    if need_ctx_out:
        yc = from_heads(rmsnorm(attend(q_c, k_c, v_c), g) * (1.0 - lam_init))
    return y, yc


def hgrn_gates(z, lb):
    z = z.astype(F32)
    log_f = jnp.logaddexp(jnp.log(lb), jnp.log1p(-lb) + jax.nn.log_sigmoid(z))
    k = (1.0 - lb) * jax.nn.sigmoid(-z)
    return log_f, k


def gla_chunked(q, k, v, log_f, s0, chunk, with_output):
    b, h, n_tok, dk = q.shape
    dv = v.shape[-1]
    nc = n_tok // chunk
    q, k, log_f = [t.reshape(b, h, nc, chunk, dk) for t in (q, k, log_f)]
    v = v.reshape(b, h, nc, chunk, dv)
    cum = jnp.cumsum(log_f, axis=3)
    cum_last = cum[:, :, :, -1]
    k_end = k * jnp.exp(cum_last[:, :, :, None] - cum)
    ds = jnp.einsum('bhnlk,bhnlv->bhnkv', k_end, v)

    def step(s, inp):
        ds_j, g_j = inp
        return jnp.exp(g_j)[..., None] * s + ds_j, s

    s_fin, s_start = lax.scan(step, s0, (jnp.moveaxis(ds, 2, 0), jnp.moveaxis(cum_last, 2, 0)))
    if not with_output:
        return None, s_fin
    s_start = jnp.moveaxis(s_start, 0, 2)
    tri = jnp.tril(jnp.ones((chunk, chunk), dtype=bool))
    diff = cum[:, :, :, :, None, :] - cum[:, :, :, None, :, :]
    decay = jnp.exp(jnp.where(tri[:, :, None], diff, -jnp.inf))
    a = jnp.einsum('bhnlk,bhnsk,bhnlsk->bhnls', q, k, decay)
    o = (jnp.einsum('bhnls,bhnsv->bhnlv', a, v)
         + jnp.einsum('bhnlk,bhnkv->bhnlv', q * jnp.exp(cum), s_start))
    return o.reshape(b, h, n_tok, dv), s_fin


def hgrn2_mixer(p_lat, p_ctx, lb, g, need_ctx_out):
    lb_h = lb.reshape(1, HG_HEADS, 1, HG_DK)

    def prep(parts):
        q, ff, fb, i = parts
        q = to_heads(q, HG_HEADS).astype(F32) * HG_DK ** -0.5
        i = to_heads(i, HG_HEADS).astype(F32)
        return q, i, hgrn_gates(to_heads(ff, HG_HEADS), lb_h), hgrn_gates(to_heads(fb, HG_HEADS), lb_h)

    q, i, (lf_f, k_f), (lf_b, k_b) = prep(p_lat)
    qc, ic, (lfc_f, kc_f), (lfc_b, kc_b) = prep(p_ctx)
    s0 = jnp.zeros((q.shape[0], HG_HEADS, HG_DK, HG_DV), F32)
    oc_f, s_f = gla_chunked(qc, kc_f, ic, lfc_f, s0, HG_CHUNK, need_ctx_out)
    oc_b, s_b = gla_chunked(flip_time(qc), flip_time(kc_b), flip_time(ic), flip_time(lfc_b),
                            s0, HG_CHUNK, need_ctx_out)
    o_f, _ = gla_chunked(q, k_f, i, lf_f, s_f, HG_CHUNK, True)
    o_b, _ = gla_chunked(flip_time(q), flip_time(k_b), flip_time(i), flip_time(lf_b),
                         s_b, HG_CHUNK, True)
    dt = p_lat[0].dtype
    y = from_heads(rmsnorm(o_f + flip_time(o_b), g)).astype(dt)
    yc = None
    if need_ctx_out:
        yc = from_heads(rmsnorm(oc_f + flip_time(oc_b), g)).astype(dt)
    return y, yc


def window_gqa(qa, ka, va, qc, kc, vc, sink, tabs, need_ctx_out):
    b, n_tok, _ = qa.shape
    n_ctx = qc.shape[1]
    grp = SW_HEADS // SW_KV_HEADS
    scale = SW_HD ** -0.5

    def qheads(t):
        return t.reshape(b, t.shape[1], SW_KV_HEADS, grp, SW_HD).transpose(0, 2, 3, 1, 4)

    q = rope2d(qheads(qa), tabs)
    k = rope2d(to_heads(ka, SW_KV_HEADS), tabs)
    v = to_heads(va, SW_KV_HEADS)
    q_c, k_c, v_c = qheads(qc), to_heads(kc, SW_KV_HEADS), to_heads(vc, SW_KV_HEADS)
    blk = SW_BLOCK
    nb = n_tok // blk

    def band(t):
        tp = jnp.pad(t, ((0, 0), (0, 0), (blk, blk), (0, 0))).reshape(b, SW_KV_HEADS, nb + 2, blk, SW_HD)
        return jnp.concatenate([tp[:, :, :-2], tp[:, :, 1:-1], tp[:, :, 2:]], axis=3)

    kb, vb = band(k), band(v)
    qb = q.reshape(b, SW_KV_HEADS, grp, nb, blk, SW_HD)
    s_band = jnp.einsum('bkgnqd,bknsd->bkgnqs', qb, kb).astype(F32) * scale
    s_ctx = jnp.einsum('bkgnqd,bkcd->bkgnqc', qb, k_c).astype(F32) * scale
    qpos = jnp.arange(nb)[:, None, None] * blk + jnp.arange(blk)[None, :, None]
    kpos = jnp.arange(nb)[:, None, None] * blk + jnp.arange(3 * blk)[None, None, :] - blk
    valid = (jnp.abs(qpos - kpos) <= SW_WINDOW) & (kpos >= 0) & (kpos < n_tok)
    s_band = jnp.where(valid, s_band, -jnp.inf)
    sink_f = sink.astype(F32)
    sink_l = jnp.broadcast_to(sink_f.reshape(1, SW_KV_HEADS, grp, 1, 1, 1), s_band.shape[:-1] + (1,))
    p = jax.nn.softmax(jnp.concatenate([sink_l, s_ctx, s_band], axis=-1), axis=-1)
    p_ctx = p[..., 1:1 + n_ctx].astype(v.dtype)
    p_band = p[..., 1 + n_ctx:].astype(v.dtype)
    o = (jnp.einsum('bkgnqc,bkcd->bkgnqd', p_ctx, v_c)
         + jnp.einsum('bkgnqs,bknsd->bkgnqd', p_band, vb))
    y = o.reshape(b, SW_KV_HEADS, grp, n_tok, SW_HD).transpose(0, 3, 1, 2, 4).reshape(b, n_tok, SW_HEADS * SW_HD)
    yc = None
    if need_ctx_out:
        s_c = jnp.einsum('bkgqd,bkcd->bkgqc', q_c, k_c).astype(F32) * scale
        sink_c = jnp.broadcast_to(sink_f.reshape(1, SW_KV_HEADS, grp, 1, 1), s_c.shape[:-1] + (1,))
        pc = jax.nn.softmax(jnp.concatenate([sink_c, s_c], axis=-1), axis=-1)[..., 1:].astype(v.dtype)
        oc = jnp.einsum('bkgqc,bkcd->bkgqd', pc, v_c)
        yc = oc.transpose(0, 3, 1, 2, 4).reshape(b, n_ctx, SW_HEADS * SW_HD)
    return y, yc


def mlstm_chunked(q, k, v, ig, lf, state, chunk, with_output):
    b, h, n_tok, d = q.shape
    nc = n_tok // chunk
    q, k, v = [t.reshape(b, h, nc, chunk, d) for t in (q, k, v)]
    ig, lf = [t.reshape(b, h, nc, chunk) for t in (ig, lf)]
    cum = jnp.cumsum(lf, axis=-1)
    cum_last = cum[..., -1]
    a = cum_last[..., None] - cum + ig
    m_loc = jnp.max(a, axis=-1)
    w = jnp.exp(a - m_loc[..., None])
    d_c = jnp.einsum('bhnl,bhnlk,bhnlv->bhnkv', w, k, v)
    d_n = jnp.einsum('bhnl,bhnlk->bhnk', w, k)

    def step(carry, inp):
        c_s, n_s, m_s = carry
        dc_j, dn_j, ml_j, cl_j = inp
        m_new = jnp.maximum(cl_j + m_s, ml_j)
        sp = jnp.exp(cl_j + m_s - m_new)
        sl = jnp.exp(ml_j - m_new)
        c_new = sp[..., None, None] * c_s + sl[..., None, None] * dc_j
        n_new = sp[..., None] * n_s + sl[..., None] * dn_j
        return (c_new, n_new, m_new), (c_s, n_s, m_s)

    final, starts = lax.scan(step, state, (jnp.moveaxis(d_c, 2, 0), jnp.moveaxis(d_n, 2, 0),
                                           jnp.moveaxis(m_loc, 2, 0), jnp.moveaxis(cum_last, 2, 0)))
    if not with_output:
        return None, final
    c0, n0, m0 = [jnp.moveaxis(t, 0, 2) for t in starts]
    tri = jnp.tril(jnp.ones((chunk, chunk), dtype=bool))
    logd = jnp.where(tri, cum[..., :, None] - cum[..., None, :] + ig[..., None, :], -jnp.inf)
    inter = cum + m0[..., None]
    m_t = jnp.maximum(jnp.max(logd, axis=-1), inter)
    dmat = jnp.exp(logd - m_t[..., None])
    g0 = jnp.exp(inter - m_t)
    s = jnp.einsum('bhnld,bhnsd->bhnls', q, k) * dmat
    num = (jnp.einsum('bhnls,bhnsv->bhnlv', s, v)
           + g0[..., None] * jnp.einsum('bhnlk,bhnkv->bhnlv', q, c0))
    den = jnp.sum(s, axis=-1) + g0 * jnp.einsum('bhnlk,bhnk->bhnl', q, n0)
    hid = num / jnp.maximum(jnp.abs(den), jnp.exp(-m_t))[..., None]
    return hid.reshape(b, h, n_tok, d), final


def mlstm_mixer(p_lat, p_ctx, g, need_ctx_out):
    def prep(parts):
        q, k, v, ig_f, ig_b, fg_f, fg_b, og = parts
        q = to_heads(q, ML_HEADS).astype(F32)
        k = to_heads(k, ML_HEADS).astype(F32) * ML_HD ** -0.5
        v = to_heads(v, ML_HEADS).astype(F32)
        tg = lambda t: jnp.swapaxes(t.astype(F32), 1, 2)
        return (q, k, v, tg(ig_f), tg(ig_b),
                jax.nn.log_sigmoid(tg(fg_f)), jax.nn.log_sigmoid(tg(fg_b)), og)

    q, k, v, ig_f, ig_b, lf_f, lf_b, og = prep(p_lat)
    qc, kc, vc, igc_f, igc_b, lfc_f, lfc_b, ogc = prep(p_ctx)
    bsz = q.shape[0]
    st0 = (jnp.zeros((bsz, ML_HEADS, ML_HD, ML_HD), F32),
           jnp.zeros((bsz, ML_HEADS, ML_HD), F32),
           jnp.zeros((bsz, ML_HEADS), F32))
    hc_f, st_f = mlstm_chunked(qc, kc, vc, igc_f, lfc_f, st0, ML_CHUNK, need_ctx_out)
    hc_b, st_b = mlstm_chunked(flip_time(qc), flip_time(kc), flip_time(vc), flip_time(igc_b),
                               flip_time(lfc_b), st0, ML_CHUNK, need_ctx_out)
    h_f, _ = mlstm_chunked(q, k, v, ig_f, lf_f, st_f, ML_CHUNK, True)
    h_b, _ = mlstm_chunked(flip_time(q), flip_time(k), flip_time(v), flip_time(ig_b),
                           flip_time(lf_b), st_b, ML_CHUNK, True)
    y = (from_heads(rmsnorm(h_f + flip_time(h_b), g)) * jax.nn.sigmoid(og.astype(F32))).astype(og.dtype)
    yc = None
    if need_ctx_out:
        yc = (from_heads(rmsnorm(hc_f + flip_time(hc_b), g)) * jax.nn.sigmoid(ogc.astype(F32))).astype(ogc.dtype)
    return y, yc


def hybrid_layer(x, ctx, c, c_ctx, w_mod, b_mod, norm_g, w_in, b_in, diff_lam, diff_g,
                 lb, hg_g, sw_sink, ml_g, w_out, layer_idx, tabs_a, tabs_c, need_ctx_out):
    shift, scale, gate = jnp.split(jax.nn.silu(c) @ w_mod + b_mod, 3, axis=-1)
    shift_c, scale_c, gate_c = jnp.split(jax.nn.silu(c_ctx) @ w_mod + b_mod, 3, axis=-1)
    h = rmsnorm(x, norm_g) * (1.0 + scale[:, None]) + shift[:, None]
    hc = rmsnorm(ctx, norm_g) * (1.0 + scale_c) + shift_c
    p = split_proj(h @ w_in + b_in)
    pc = split_proj(hc @ w_in + b_in)
    ya, ya_c = diff_attention(*p[0:3], *pc[0:3], diff_lam, diff_g, layer_idx, tabs_a, need_ctx_out)
    yb, yb_c = hgrn2_mixer(p[3:7], pc[3:7], lb, hg_g, need_ctx_out)
    yc, yc_c = window_gqa(*p[7:10], *pc[7:10], sw_sink, tabs_c, need_ctx_out)
    yd, yd_c = mlstm_mixer(p[10:18], pc[10:18], ml_g, need_ctx_out)
    mixed = jnp.concatenate([ya, yb, yc, yd], axis=-1) * jax.nn.silu(p[18])
    x = x + gate[:, None] * (mixed @ w_out)
    if need_ctx_out:
        mixed_c = jnp.concatenate([ya_c, yb_c, yc_c, yd_c], axis=-1) * jax.nn.silu(pc[18])
        ctx = ctx + gate_c * (mixed_c @ w_out)
    return x, ctx


def setup_inputs(seed: int = 0) -> dict:
    key = jax.random.key(seed)
    ks = jax.random.split(key, 18)
    nrm = jax.random.normal
    d = D_MODEL
    return {
        "x": nrm(ks[0], (BATCH, SEQ, d), F32),
        "c": nrm(ks[1], (BATCH, d), F32),
        "ctx": nrm(ks[2], (BATCH, CTX_LEN, d), F32),
        "c_ctx": nrm(ks[3], (d,), F32),
        "w_mod": nrm(ks[4], (DEPTH, d, 3 * d), F32) * (0.5 * d ** -0.5),
        "b_mod": nrm(ks[5], (DEPTH, 3 * d), F32) * 0.02,
        "norm_g": 1.0 + 0.1 * nrm(ks[6], (DEPTH, d), F32),
        "w_in": nrm(ks[7], (DEPTH, d, PROJ_WIDTH), F32) * d ** -0.5,
        "b_in": nrm(ks[8], (DEPTH, PROJ_WIDTH), F32) * 0.02,
        "diff_lam": nrm(ks[9], (DEPTH, 4, DA_QK), F32) * 0.1,
        "diff_g": 1.0 + 0.1 * nrm(ks[10], (DEPTH, DA_VDIM), F32),
        "hg_lb": nrm(ks[11], (DEPTH, HG_HEADS * HG_DK), F32),
        "hg_g": 1.0 + 0.1 * nrm(ks[12], (DEPTH, HG_DV), F32),
        "sw_sink": nrm(ks[13], (DEPTH, SW_HEADS), F32),
        "ml_g": 1.0 + 0.1 * nrm(ks[14], (DEPTH, ML_HD), F32),
        "w_out": nrm(ks[15], (DEPTH, MIX_WIDTH, d), F32) * MIX_WIDTH ** -0.5,
        "final_g": 1.0 + 0.1 * nrm(ks[16], (d,), F32),
    }


def reference(x, c, ctx, c_ctx, w_mod, b_mod, norm_g, w_in, b_in, diff_lam, diff_g,
              hg_lb, hg_g, sw_sink, ml_g, w_out, final_g):
    n_tok = x.shape[1]
    tabs_a = axial_rope_tables(n_tok, DA_QK)
    tabs_c = axial_rope_tables(n_tok, SW_HD)
    lb_all = jnp.cumsum(jax.nn.softmax(hg_lb.astype(F32), axis=0), axis=0)
    lb_all = lb_all - lb_all[0]
    for l in range(DEPTH):
        x, ctx = hybrid_layer(x, ctx, c, c_ctx, w_mod[l], b_mod[l], norm_g[l], w_in[l], b_in[l],
                              diff_lam[l], diff_g[l], lb_all[l], hg_g[l], sw_sink[l], ml_g[l],
                              w_out[l], l, tabs_a, tabs_c, l < DEPTH - 1)
    return rmsnorm(x, final_g)
```

```python
import functools
import math

import numpy as np
import jax
import jax.numpy as jnp
from jax import lax
from jax.experimental import pallas as pl
from jax.experimental.pallas import tpu as pltpu

F32 = jnp.float32
BF16 = jnp.bfloat16

D_MODEL = 1024
SEQ = 2048
CTX_LEN = 256
T_ALL = CTX_LEN + SEQ
GRID_W = 64
DEPTH = 2
HEADS = 4
HEAD_DIM = 64
BRANCH = HEADS * HEAD_DIM
DA_QK = 32
SW_WINDOW = 128
HG_CHUNK = 16
ROPE_BASE = 10000.0
NORM_EPS = 1e-6
NEG_BIG = -1e30

TILE = 256
N_TILES = T_ALL // TILE
LANES = 128

SEG_A = 3 * BRANCH
SEG_C = 3 * BRANCH
SEG_B = 4 * BRANCH
SEG_D = 3 * BRANCH
SEG_G = LANES
SEG_O = BRANCH + D_MODEL
OFF_A = 0
OFF_C = OFF_A + SEG_A
OFF_B = OFF_C + SEG_C
OFF_D = OFF_B + SEG_B
OFF_G = OFF_D + SEG_D
OFF_O = OFF_G + SEG_G
PROJ_PAD = OFF_O + SEG_O

VMEM_LIMIT = 56 * 1024 * 1024


def _params(*sem):
    return pltpu.CompilerParams(dimension_semantics=sem, vmem_limit_bytes=VMEM_LIMIT)


def _dot(a, b):
    return jnp.dot(a, b, preferred_element_type=F32)


def _dot_nt(a, b):
    return lax.dot_general(a, b, (((1,), (1,)), ((), ())), preferred_element_type=F32)


def _dot_tn(a, b):
    return lax.dot_general(a, b, (((0,), (0,)), ((), ())), preferred_element_type=F32)


def _split3(x):
    x1 = x.astype(BF16)
    r1 = x - x1.astype(F32)
    x2 = r1.astype(BF16)
    x3 = (r1 - x2.astype(F32)).astype(BF16)
    return x1, x2, x3


def _dot_exact_l(m01, x):
    x1, x2, x3 = _split3(x)
    return _dot(m01, x1) + _dot(m01, x2) + _dot(m01, x3)


def _dot_exact_r(x, m01):
    x1, x2, x3 = _split3(x)
    return _dot(x1, m01) + _dot(x2, m01) + _dot(x3, m01)


def _dot_exact_nt(x, m01):
    x1, x2, x3 = _split3(x)
    return _dot_nt(x1, m01) + _dot_nt(x2, m01) + _dot_nt(x3, m01)


def _sigmoid(z):
    e = jnp.exp(-jnp.abs(z))
    r = 1.0 / (1.0 + e)
    return jnp.where(z >= 0, r, e * r)


def _log_sigmoid(z):
    return jnp.minimum(z, 0.0) - jnp.log(1.0 + jnp.exp(-jnp.abs(z)))


def _iota(shape, dim):
    return lax.broadcasted_iota(jnp.int32, shape, dim)


def _head_sum_matrix():
    r = _iota((BRANCH, BRANCH), 0) // HEAD_DIM
    c = _iota((BRANCH, BRANCH), 1) // HEAD_DIM
    return (r == c).astype(BF16)


def _head_rmsnorm(o, g_row):
    ss = _dot_exact_r(o * o, _head_sum_matrix())
    return o * lax.rsqrt(ss * (1.0 / HEAD_DIM) + NORM_EPS) * g_row


def _mod_kernel(cc_ref, w_ref, b_ref, o_ref):
    cc = cc_ref[...]
    a = (cc * _sigmoid(cc)).astype(BF16)
    o_ref[...] = _dot(a, w_ref[...].astype(BF16)) + b_ref[...]


def _mod_call(cc, w_mod, b_mod):
    rows = cc.shape[0]
    nblk = 3
    return pl.pallas_call(
        _mod_kernel,
        grid=(DEPTH, nblk),
        in_specs=[
            pl.BlockSpec((rows, D_MODEL), lambda l, j: (0, 0)),
            pl.BlockSpec((None, D_MODEL, D_MODEL), lambda l, j: (l, 0, j)),
            pl.BlockSpec((None, 1, D_MODEL), lambda l, j: (l, 0, j)),
        ],
        out_specs=pl.BlockSpec((None, rows, D_MODEL), lambda l, j: (l, 0, j)),
        out_shape=jax.ShapeDtypeStruct((DEPTH, rows, 3 * D_MODEL), F32),
        compiler_params=_params("arbitrary", "arbitrary"),
        name="mod",
    )(cc, w_mod, b_mod.reshape(DEPTH, 1, 3 * D_MODEL))


def _rope(slab, cos, sin_next, sin_prev, off):
    nxt = pltpu.roll(slab, LANES - off, 1)
    prv = pltpu.roll(slab, off, 1)
    return slab * cos + nxt * sin_next + prv * sin_prev


def _inproj_kernel(x_ref, mod_ref, ng_ref, w_ref, b_ref, ta_ref, tc_ref,
                   pa_ref, pc_ref, pb_ref, pd_ref, pg_ref, po_ref, *, bsz):
    row = jnp.where(pl.program_id(1) == 0, bsz, pl.program_id(0))
    x = x_ref[...]
    mrow = mod_ref[pl.ds(row, 1), :]
    shift = mrow[:, 0:D_MODEL]
    scale = mrow[:, D_MODEL:2 * D_MODEL]
    ms = jnp.mean(x * x, axis=-1, keepdims=True)
    h = x * lax.rsqrt(ms + NORM_EPS) * ng_ref[...]
    h = (h * (1.0 + scale) + shift).astype(BF16)

    def proj(off, width):
        return _dot(h, w_ref[:, off:off + width]) + b_ref[:, off:off + width]

    def rope_seg(acc, tab_ref, off, q_scale):
        cos, s_next, s_prev = tab_ref[0], tab_ref[1], tab_ref[2]
        outs = []
        for j in range(4):
            r = _rope(acc[:, j * LANES:(j + 1) * LANES], cos, s_next, s_prev, off)
            outs.append(r * q_scale if j < 2 else r)
        outs.append(acc[:, 4 * LANES:])
        return jnp.concatenate(outs, axis=-1)

    pa_ref[...] = rope_seg(proj(OFF_A, SEG_A), ta_ref, DA_QK // 4, DA_QK ** -0.5).astype(BF16)
    pc_ref[...] = rope_seg(proj(OFF_C, SEG_C), tc_ref, HEAD_DIM // 4, HEAD_DIM ** -0.5).astype(BF16)
    pb_ref[...] = proj(OFF_B, SEG_B)
    accd = proj(OFF_D, SEG_D)
    pd_ref[...] = jnp.concatenate(
        [accd[:, 0:BRANCH], accd[:, BRANCH:2 * BRANCH] * (HEAD_DIM ** -0.5), accd[:, 2 * BRANCH:]],
        axis=-1).astype(BF16)
    pg_ref[...] = proj(OFF_G, SEG_G)
    po_ref[...] = proj(OFF_O, SEG_O)


def _inproj_call(xc, mod_l, norm_g, w_r, b_r, tab_a, tab_c):
    bsz = xc.shape[0]
    rows = mod_l.shape[0]
    widths = [(SEG_A, BF16), (SEG_C, BF16), (SEG_B, F32), (SEG_D, BF16), (SEG_G, F32), (SEG_O, F32)]
    return pl.pallas_call(
        functools.partial(_inproj_kernel, bsz=bsz),
        grid=(bsz, N_TILES),
        in_specs=[
            pl.BlockSpec((None, TILE, D_MODEL), lambda b, t: (b, t, 0)),
            pl.BlockSpec((rows, 3 * D_MODEL), lambda b, t: (0, 0)),
            pl.BlockSpec((1, D_MODEL), lambda b, t: (0, 0)),
            pl.BlockSpec((D_MODEL, PROJ_PAD), lambda b, t: (0, 0)),
            pl.BlockSpec((1, PROJ_PAD), lambda b, t: (0, 0)),
            pl.BlockSpec((3, TILE, LANES), lambda b, t: (0, t, 0)),
            pl.BlockSpec((3, TILE, LANES), lambda b, t: (0, t, 0)),
        ],
        out_specs=[pl.BlockSpec((None, TILE, w), lambda b, t: (b, t, 0)) for w, _ in widths],
        out_shape=[jax.ShapeDtypeStruct((bsz, T_ALL, w), dt) for w, dt in widths],
        compiler_params=_params("arbitrary", "arbitrary"),
        name="inproj",
    )(xc, mod_l, norm_g.reshape(1, D_MODEL), w_r, b_r.reshape(1, PROJ_PAD), tab_a, tab_c)


def _diffattn_kernel(q_ref, k_ref, v_ref, lam_ref, g_ref, o_ref, acc_ref, *, lam_init, q_tile0):
    qb = pl.program_id(1) + q_tile0
    lp = lam_ref[...]
    lam = (jnp.exp(jnp.sum(lp[0:1] * lp[1:2], axis=-1, keepdims=True))
           - jnp.exp(jnp.sum(lp[2:3] * lp[3:4], axis=-1, keepdims=True)) + lam_init)
    q = q_ref[...]
    lane = _iota((1, BRANCH), 1)

    def attend(nk):
        k = k_ref[0:nk, :]
        v = v_ref[0:nk, :]
        acc_ref[...] = jnp.zeros_like(acc_ref)

        def body(hm, carry):
            qm = jnp.where(lane // DA_QK == hm, q, jnp.zeros_like(q))
            s = _dot_nt(qm, k)
            m = jnp.max(s, axis=-1, keepdims=True)
            e = jnp.exp(s - m)
            l = jnp.sum(e, axis=-1, keepdims=True)
            pv = _dot(e.astype(BF16), v)
            sign = jnp.where(hm % 2 == 0, jnp.ones_like(lam), -lam)
            acc_ref[...] += jnp.where(lane // HEAD_DIM == hm // 2, pv * (sign / l), 0.0)
            return carry

        lax.fori_loop(0, 2 * HEADS, body, 0)
        o_ref[...] = _head_rmsnorm(acc_ref[...], g_ref[...]) * (1.0 - lam_init)

    @pl.when(qb == 0)
    def _():
        attend(CTX_LEN)

    @pl.when(qb > 0)
    def _():
        attend(T_ALL)


def _diffattn_call(pa, lam_p, g_row, layer_idx, need_ctx):
    bsz = pa.shape[0]
    q_tile0 = 0 if need_ctx else 1
    lam_init = 0.8 - 0.6 * math.exp(-0.3 * layer_idx)
    return pl.pallas_call(
        functools.partial(_diffattn_kernel, lam_init=lam_init, q_tile0=q_tile0),
        grid=(bsz, N_TILES - q_tile0),
        in_specs=[
            pl.BlockSpec((None, TILE, BRANCH), lambda b, t: (b, t + q_tile0, 0)),
            pl.BlockSpec((None, T_ALL, BRANCH), lambda b, t: (b, 0, 1)),
            pl.BlockSpec((None, T_ALL, BRANCH), lambda b, t: (b, 0, 2)),
            pl.BlockSpec((4, DA_QK), lambda b, t: (0, 0)),
            pl.BlockSpec((1, BRANCH), lambda b, t: (0, 0)),
        ],
        out_specs=pl.BlockSpec((None, TILE, BRANCH), lambda b, t: (b, t + q_tile0, 0)),
        out_shape=jax.ShapeDtypeStruct((bsz, T_ALL, BRANCH), F32),
        scratch_shapes=[pltpu.VMEM((TILE, BRANCH), F32)],
        compiler_params=_params("arbitrary", "arbitrary"),
        name="diffattn",
    )(pa, pa, pa, lam_p, g_row)


BAND = 2 * TILE


def _window_kernel(q_ref, k_ref, v_ref, sink_ref, o_ref, *, q_tile0):
    qb = pl.program_id(1) + q_tile0
    q = q_ref[...]
    lane = _iota((1, BRANCH), 1)
    kc = k_ref[0:CTX_LEN, :]
    vc = v_ref[0:CTX_LEN, :]

    def attend(band):
        if band:
            a = (qb - 1) * TILE
            start = jnp.clip(a - SW_WINDOW, 0, SEQ - BAND)
            row0 = pl.multiple_of(CTX_LEN + start, SW_WINDOW)
            kb = k_ref[pl.ds(row0, BAND), :]
            vb = v_ref[pl.ds(row0, BAND), :]
            qpos = a + _iota((TILE, 1), 0)
            kpos = start + _iota((1, BAND), 1)
            valid = jnp.abs(qpos - kpos) <= SW_WINDOW
        out = jnp.zeros((TILE, BRANCH), F32)
        for h in range(HEADS):
            qm = jnp.where(lane // HEAD_DIM == h, q, jnp.zeros_like(q))
            sink = sink_ref[:, h:h + 1]
            sc = _dot_nt(qm, kc)
            m = jnp.maximum(jnp.max(sc, axis=-1, keepdims=True), sink)
            if band:
                sb = jnp.where(valid, _dot_nt(qm, kb), NEG_BIG)
                m = jnp.maximum(m, jnp.max(sb, axis=-1, keepdims=True))
            ec = jnp.exp(sc - m)
            l = jnp.sum(ec, axis=-1, keepdims=True) + jnp.exp(sink - m)
            o = _dot(ec.astype(BF16), vc)
            if band:
                eb = jnp.exp(sb - m)
                l = l + jnp.sum(eb, axis=-1, keepdims=True)
                o = o + _dot(eb.astype(BF16), vb)
            out = out + jnp.where(lane // HEAD_DIM == h, o * (1.0 / l), 0.0)
        o_ref[...] = out

    @pl.when(qb == 0)
    def _():
        attend(False)

    @pl.when(qb > 0)
    def _():
        attend(True)


def _window_call(pc, sink_row, need_ctx):
    bsz = pc.shape[0]
    q_tile0 = 0 if need_ctx else 1
    return pl.pallas_call(
        functools.partial(_window_kernel, q_tile0=q_tile0),
        grid=(bsz, N_TILES - q_tile0),
        in_specs=[
            pl.BlockSpec((None, TILE, BRANCH), lambda b, t: (b, t + q_tile0, 0)),
            pl.BlockSpec((None, T_ALL, BRANCH), lambda b, t: (b, 0, 1)),
            pl.BlockSpec((None, T_ALL, BRANCH), lambda b, t: (b, 0, 2)),
            pl.BlockSpec((1, HEADS), lambda b, t: (0, 0)),
        ],
        out_specs=pl.BlockSpec((None, TILE, BRANCH), lambda b, t: (b, t + q_tile0, 0)),
        out_shape=jax.ShapeDtypeStruct((bsz, T_ALL, BRANCH), F32),
        compiler_params=_params("arbitrary", "arbitrary"),
        name="window",
    )(pc, pc, pc, sink_row)


N_CHUNK = T_ALL // HG_CHUNK
CTX_CHUNKS = CTX_LEN // HG_CHUNK
CH_PER_TILE = TILE // HG_CHUNK


def _hgrn_kernel(p_ref, lb_ref, g_ref, o_ref, qt_ref, ke_ref, dec_ref, of_ref, st_ref, *, layer_idx):
    lane_r = _iota((TILE, TILE), 0)
    lane_c = _iota((TILE, TILE), 1)
    same_chunk = (lane_r // HG_CHUNK) == (lane_c // HG_CHUNK)
    chunk_ones = same_chunk.astype(BF16)
    head_ones = _head_sum_matrix()
    head_mask = (lane_r // HEAD_DIM) == (lane_c // HEAD_DIM)
    row3 = _iota((CH_PER_TILE, HG_CHUNK, BRANCH), 1)

    if layer_idx > 0:
        lbp = lb_ref[...]
        lbp = lbp - jnp.max(lbp, axis=0, keepdims=True)
        sm = jnp.exp(lbp)
        sm = sm / jnp.sum(sm, axis=0, keepdims=True)
        lb = jnp.sum(sm[1:layer_idx + 1], axis=0, keepdims=True)
        log_lb = jnp.log(lb)
        log_1m = jnp.log(1.0 - lb)

    for direction in (0, 1):
        if direction == 0:
            tri = (same_chunk & (lane_c <= lane_r)).astype(BF16)
        else:
            tri = (same_chunk & (lane_c >= lane_r)).astype(BF16)

        def tile_body(t, carry):
            r0 = pl.multiple_of(t * TILE, TILE)
            rows = pl.ds(r0, TILE)
            q = p_ref[rows, 0:BRANCH] * (HEAD_DIM ** -0.5)
            z = p_ref[rows, (1 + direction) * BRANCH:(2 + direction) * BRANCH]
            v = p_ref[rows, 3 * BRANCH:4 * BRANCH]
            if layer_idx == 0:
                log_f = _log_sigmoid(z)
                kk = _sigmoid(-z)
            else:
                bt = log_1m + _log_sigmoid(z)
                mx = jnp.maximum(log_lb, bt)
                log_f = mx + jnp.log(jnp.exp(log_lb - mx) + jnp.exp(bt - mx))
                kk = (1.0 - lb) * _sigmoid(-z)
            cum = _dot_exact_l(tri, log_f)
            tot = _dot_exact_l(chunk_ones, log_f)
            qt_ref[rows, :] = q * jnp.exp(cum)
            ke_ref[rows, :] = kk * jnp.exp(tot - cum)
            dec_ref[rows, :] = jnp.exp(tot)
            q3 = q.reshape(CH_PER_TILE, HG_CHUNK, BRANCH)
            c3 = cum.reshape(CH_PER_TILE, HG_CHUNK, BRANCH)
            k3 = kk.reshape(CH_PER_TILE, HG_CHUNK, BRANCH)
            v3 = v.reshape(CH_PER_TILE, HG_CHUNK, BRANCH)
            o3 = jnp.zeros((CH_PER_TILE, HG_CHUNK, BRANCH), F32)
            for s in range(HG_CHUNK):
                keep = (row3 >= s) if direction == 0 else (row3 <= s)
                dlt = jnp.where(keep, c3 - c3[:, s:s + 1, :], NEG_BIG)
                xs = (q3 * jnp.exp(dlt) * k3[:, s:s + 1, :]).reshape(TILE, BRANCH)
                hi = xs.astype(BF16)
                lo = (xs - hi.astype(F32)).astype(BF16)
                a_s = _dot(hi, head_ones) + _dot(lo, head_ones)
                o3 = o3 + a_s.reshape(CH_PER_TILE, HG_CHUNK, BRANCH) * v3[:, s:s + 1, :]
            o_intra = o3.reshape(TILE, BRANCH)
            if direction == 0:
                of_ref[rows, :] = o_intra
            else:
                of_ref[rows, :] += o_intra
            return carry

        lax.fori_loop(0, N_TILES, tile_body, 0)

        st_ref[...] = jnp.zeros_like(st_ref)

        def chunk_body(i, carry):
            if direction == 0:
                c = i
            else:
                c = jnp.where(i < CTX_CHUNKS, CTX_CHUNKS - 1 - i, N_CHUNK - 1 + CTX_CHUNKS - i)
            rows = pl.ds(pl.multiple_of(c * HG_CHUNK, HG_CHUNK), HG_CHUNK)
            st = st_ref[...]
            of_ref[rows, :] += _dot_nt(qt_ref[rows, :].astype(BF16), st.astype(BF16))
            ds = _dot_tn(p_ref[rows, 3 * BRANCH:4 * BRANCH].astype(BF16), ke_ref[rows, :].astype(BF16))
            st_ref[...] = st * dec_ref[pl.ds(c * HG_CHUNK, 1), :] + jnp.where(head_mask, ds, 0.0)
            return carry

        lax.fori_loop(0, N_CHUNK, chunk_body, 0)

    o_ref[...] = _head_rmsnorm(of_ref[...], g_ref[...])


def _hgrn_call(pb, hg_lb, g_row, layer_idx):
    bsz = pb.shape[0]
    return pl.pallas_call(
        functools.partial(_hgrn_kernel, layer_idx=layer_idx),
        grid=(bsz,),
        in_specs=[
            pl.BlockSpec((None, T_ALL, SEG_B), lambda b: (b, 0, 0)),
            pl.BlockSpec((DEPTH, BRANCH), lambda b: (0, 0)),
            pl.BlockSpec((1, BRANCH), lambda b: (0, 0)),
        ],
        out_specs=pl.BlockSpec((None, T_ALL, BRANCH), lambda b: (b, 0, 0)),
        out_shape=jax.ShapeDtypeStruct((bsz, T_ALL, BRANCH), F32),
        scratch_shapes=[pltpu.VMEM((T_ALL, BRANCH), F32)] * 4 + [pltpu.VMEM((BRANCH, BRANCH), F32)],
        compiler_params=_params("arbitrary"),
        name="hgrn",
    )(pb, hg_lb, g_row)


def _mlstm_kernel(p_ref, gt_ref, g_ref, o_ref, h_ref, c_ref, n_ref, m_ref):
    rr = _iota((TILE, TILE), 0)
    cc = _iota((TILE, TILE), 1)
    head_mask = (rr // HEAD_DIM) == (cc // HEAD_DIM)
    lane = _iota((1, BRANCH), 1)

    for direction in (0, 1):
        causal = (cc <= rr) if direction == 0 else (cc >= rr)
        tri = causal.astype(BF16)
        c_ref[...] = jnp.zeros_like(c_ref)
        n_ref[...] = jnp.zeros_like(n_ref)
        m_ref[...] = jnp.zeros_like(m_ref)

        def chunk_body(i, carry):
            if direction == 0:
                c = i
            else:
                c = jnp.where(i == 0, 0, N_TILES - i)
            rows = pl.ds(pl.multiple_of(c * TILE, TILE), TILE)
            q = p_ref[rows, 0:BRANCH]
            k = p_ref[rows, BRANCH:2 * BRANCH]
            v = p_ref[rows, 2 * BRANCH:3 * BRANCH]
            qf = q.astype(F32)
            kf = k.astype(F32)
            gates = gt_ref[rows, :]
            lf = _log_sigmoid(gates)
            cum = _dot_exact_l(tri, lf)
            gates_t = gates.T
            cum_t = _dot_exact_nt(lf.T, tri)
            tot = cum[TILE - 1:TILE, :] if direction == 0 else cum[0:1, :]
            cbd = c_ref[...].astype(BF16)
            n0 = n_ref[...]
            m0 = m_ref[...]

            hid = jnp.zeros((TILE, BRANCH), F32)
            wexp = jnp.zeros((TILE, BRANCH), F32)
            sp_row = jnp.zeros((1, BRANCH), F32)
            m_row = jnp.zeros((1, BRANCH), F32)
            for h in range(HEADS):
                icol = 4 * direction + h
                fcol = 8 + 4 * direction + h
                hmask = lane // HEAD_DIM == h
                cumc = cum[:, fcol:fcol + 1]
                igc = gates[:, icol:icol + 1]
                cumr = cum_t[fcol:fcol + 1, :]
                igr = gates_t[icol:icol + 1, :]
                m0h = m0[:, h * HEAD_DIM:h * HEAD_DIM + 1]
                logd = jnp.where(causal, cumc - cumr + igr, NEG_BIG)
                inter = cumc + m0h
                m_t = jnp.maximum(jnp.max(logd, axis=-1, keepdims=True), inter)
                dmat = jnp.exp(logd - m_t)
                g0 = jnp.exp(inter - m_t)
                qm = jnp.where(hmask, q, jnp.zeros_like(q))
                s = _dot_nt(qm, k) * dmat
                num = _dot(s.astype(BF16), v) + g0 * _dot(qm, cbd)
                qn = jnp.sum(jnp.where(hmask, qf * n0, 0.0), axis=-1, keepdims=True)
                den = jnp.sum(s, axis=-1, keepdims=True) + g0 * qn
                hid = hid + jnp.where(hmask, num / jnp.maximum(jnp.abs(den), jnp.exp(-m_t)), 0.0)
                toth = tot[:, fcol:fcol + 1]
                a = toth - cumc + igc
                m_loc = jnp.max(a, axis=0, keepdims=True)
                m_new = jnp.maximum(toth + m0h, m_loc)
                sp = jnp.exp(toth + m0h - m_new)
                sl = jnp.exp(m_loc - m_new)
                wexp = jnp.where(hmask, jnp.exp(a - m_loc) * sl, wexp)
                sp_row = jnp.where(hmask, sp, sp_row)
                m_row = jnp.where(hmask, m_new, m_row)

            if direction == 0:
                h_ref[rows, :] = hid
            else:
                h_ref[rows, :] += hid
            wk = kf * wexp
            dc = _dot_tn(wk.astype(BF16), v)
            c_ref[...] = c_ref[...] * sp_row + jnp.where(head_mask, dc, 0.0)
            n_ref[...] = n0 * sp_row + jnp.sum(wk, axis=0, keepdims=True)
            m_ref[...] = m_row
            return carry

        lax.fori_loop(0, N_TILES, chunk_body, 0)

    o_ref[...] = _head_rmsnorm(h_ref[...], g_ref[...])


def _mlstm_call(pd, pg, g_row):
    bsz = pd.shape[0]
    return pl.pallas_call(
        _mlstm_kernel,
        grid=(bsz,),
        in_specs=[
            pl.BlockSpec((None, T_ALL, SEG_D), lambda b: (b, 0, 0)),
            pl.BlockSpec((None, T_ALL, SEG_G), lambda b: (b, 0, 0)),
            pl.BlockSpec((1, BRANCH), lambda b: (0, 0)),
        ],
        out_specs=pl.BlockSpec((None, T_ALL, BRANCH), lambda b: (b, 0, 0)),
        out_shape=jax.ShapeDtypeStruct((bsz, T_ALL, BRANCH), F32),
        scratch_shapes=[pltpu.VMEM((T_ALL, BRANCH), F32), pltpu.VMEM((BRANCH, BRANCH), F32),
                        pltpu.VMEM((1, BRANCH), F32), pltpu.VMEM((1, BRANCH), F32)],
        compiler_params=_params("arbitrary"),
        name="mlstm",
    )(pd, pg, g_row)


def _outproj_kernel(x_ref, ya_ref, yb_ref, yc_ref, yd_ref, po_ref, mod_ref, w_ref, fg_ref, o_ref,
                    *, bsz, tile0, final_norm):
    t = pl.program_id(1) + tile0
    row = jnp.where(t == 0, bsz, pl.program_id(0))
    gate_mod = mod_ref[pl.ds(row, 1), 2 * D_MODEL:3 * D_MODEL]
    og = po_ref[:, 0:BRANCH]
    gate = po_ref[:, BRANCH:]
    mixed = jnp.concatenate([ya_ref[...], yb_ref[...], yc_ref[...], yd_ref[...] * _sigmoid(og)], axis=-1)
    mixed = (mixed * (gate * _sigmoid(gate))).astype(BF16)
    xn = x_ref[...] + gate_mod * _dot(mixed, w_ref[...])
    if final_norm:
        ms = jnp.mean(xn * xn, axis=-1, keepdims=True)
        xn = xn * lax.rsqrt(ms + NORM_EPS) * fg_ref[...]
    o_ref[...] = xn


def _outproj_call(xc, ya, yb, yc, yd, po, mod_l, w_out_bf, final_g, last):
    bsz = xc.shape[0]
    tile0 = 1 if last else 0
    n_out = N_TILES - tile0
    rows = mod_l.shape[0]

    def tok(width):
        return pl.BlockSpec((None, TILE, width), lambda b, t: (b, t + tile0, 0))

    return pl.pallas_call(
        functools.partial(_outproj_kernel, bsz=bsz, tile0=tile0, final_norm=last),
        grid=(bsz, n_out),
        in_specs=[tok(D_MODEL), tok(BRANCH), tok(BRANCH), tok(BRANCH), tok(BRANCH), tok(SEG_O),
                  pl.BlockSpec((rows, 3 * D_MODEL), lambda b, t: (0, 0)),
                  pl.BlockSpec((D_MODEL, D_MODEL), lambda b, t: (0, 0)),
                  pl.BlockSpec((1, D_MODEL), lambda b, t: (0, 0))],
        out_specs=pl.BlockSpec((None, TILE, D_MODEL), lambda b, t: (b, t, 0)),
        out_shape=jax.ShapeDtypeStruct((bsz, n_out * TILE, D_MODEL), F32),
        compiler_params=_params("arbitrary", "arbitrary"),
        name="outproj",
    )(xc, ya, yb, yc, yd, po, mod_l, w_out_bf, final_g.reshape(1, D_MODEL))


def _rope_tables(dim):
    quarter = dim // 4
    half = dim // 2
    pos = np.arange(SEQ)
    row = (pos // GRID_W).astype(np.float32)
    col = (pos % GRID_W).astype(np.float32)
    inv = (np.float32(ROPE_BASE) ** (-np.arange(0, half, 2, dtype=np.float32) / np.float32(half))).astype(np.float32)
    ang_r = row[:, None] * inv[None, :]
    ang_c = col[:, None] * inv[None, :]
    lane = np.arange(LANES) % dim
    part = lane // quarter
    freq = lane % quarter
    ang = np.where(part[None, :] < 2, ang_r[:, freq], ang_c[:, freq]).astype(np.float32)
    cos = np.cos(ang)
    sin = np.sin(ang)
    first = (part % 2 == 0)[None, :]
    s_next = np.where(first, -sin, 0.0)
    s_prev = np.where(first, 0.0, sin)
    tab = np.stack([cos, s_next, s_prev]).astype(np.float32)
    ident = np.stack([np.ones((CTX_LEN, LANES)), np.zeros((CTX_LEN, LANES)),
                      np.zeros((CTX_LEN, LANES))]).astype(np.float32)
    return jnp.asarray(np.concatenate([ident, tab], axis=1))


def _relayout_in_proj(w, bias):
    def cols(a):
        qa, ka, va = a[..., 0:256], a[..., 256:512], a[..., 512:768]
        seg_b = a[..., 768:1792]
        qc = a[..., 1792:2048]
        kc = a[..., 2048:2176]
        vc = a[..., 2176:2304]
        seg_d = a[..., 2304:3072]
        gates = a[..., 3072:3088]
        seg_o = a[..., 3088:4368]
        rep = lambda kv: jnp.concatenate([kv[..., 0:64], kv[..., 0:64], kv[..., 64:128], kv[..., 64:128]], axis=-1)
        pad = jnp.zeros(a.shape[:-1] + (SEG_G - 16,), a.dtype)
        return jnp.concatenate([qa, ka, va, qc, rep(kc), rep(vc), seg_b, seg_d, gates, pad, seg_o], axis=-1)
    return cols(w).astype(BF16), cols(bias)


def kernel(x, c, ctx, c_ctx, w_mod, b_mod, norm_g, w_in, b_in, diff_lam, diff_g, hg_lb, hg_g,
           sw_sink, ml_g, w_out, final_g):
    bsz = x.shape[0]
    tab_a = _rope_tables(DA_QK)
    tab_c = _rope_tables(HEAD_DIM)
    rows = ((bsz + 1 + 7) // 8) * 8
    cc = jnp.concatenate([c, c_ctx[None, :], jnp.zeros((rows - bsz - 1, D_MODEL), F32)], axis=0)
    mod = _mod_call(cc, w_mod, b_mod)
    xc = jnp.concatenate([ctx, x], axis=1)
    tile4 = lambda g: jnp.tile(g, HEADS).reshape(1, BRANCH)
    out = None
    for l in range(DEPTH):
        last = l == DEPTH - 1
        w_r, b_r = _relayout_in_proj(w_in[l], b_in[l])
        pa, pc, pb, pd, pg, po = _inproj_call(xc, mod[l], norm_g[l], w_r, b_r, tab_a, tab_c)
        ya = _diffattn_call(pa, diff_lam[l], tile4(diff_g[l]), l, not last)
        yb = _hgrn_call(pb, hg_lb, tile4(hg_g[l]), l)
        yc = _window_call(pc, sw_sink[l].reshape(1, HEADS), not last)
        yd = _mlstm_call(pd, pg, tile4(ml_g[l]))
        res = _outproj_call(xc, ya, yb, yc, yd, po, mod[l], w_out[l].astype(BF16), final_g, last)
        if last:
            out = res
        else:
            xc = res
    return out
```

```python
import functools
import math

import numpy as np
import jax
import jax.numpy as jnp
from jax import lax
from jax.experimental import pallas as pl
from jax.experimental.pallas import tpu as pltpu

F32 = jnp.float32
BF16 = jnp.bfloat16

D_MODEL = 1024
SEQ = 2048
CTX_LEN = 256
T_ALL = CTX_LEN + SEQ
GRID_W = 64
DEPTH = 2
HEADS = 4
HEAD_DIM = 64
BRANCH = HEADS * HEAD_DIM
DA_QK = 32
SW_WINDOW = 128
HG_CHUNK = 16
ROPE_BASE = 10000.0
NORM_EPS = 1e-6
NEG_BIG = -1e30
LOG2E = math.log2(math.e)

TILE = 256
N_TILES = T_ALL // TILE
LANES = 128
ONES_ROWS = 16

SEG_A = 3 * BRANCH
SEG_C = 3 * BRANCH
SEG_B = 4 * BRANCH
SEG_D = 3 * BRANCH
SEG_G = LANES
SEG_O = BRANCH + D_MODEL
OFF_A = 0
OFF_C = OFF_A + SEG_A
OFF_B = OFF_C + SEG_C
OFF_D = OFF_B + SEG_B
OFF_G = OFF_D + SEG_D
OFF_O = OFF_G + SEG_G
PROJ_PAD = OFF_O + SEG_O

VMEM_LIMIT = 56 * 1024 * 1024


def _params(*sem):
    return pltpu.CompilerParams(dimension_semantics=sem, vmem_limit_bytes=VMEM_LIMIT)


def _dot(a, b):
    return jnp.dot(a, b, preferred_element_type=F32)


def _dot_nt(a, b):
    return lax.dot_general(a, b, (((1,), (1,)), ((), ())), preferred_element_type=F32)


def _dot_tn(a, b):
    return lax.dot_general(a, b, (((0,), (0,)), ((), ())), preferred_element_type=F32)


def _split3(x):
    x1 = x.astype(BF16)
    r1 = x - x1.astype(F32)
    x2 = r1.astype(BF16)
    x3 = (r1 - x2.astype(F32)).astype(BF16)
    return x1, x2, x3


def _dot_exact_l(m01, x):
    x1, x2, x3 = _split3(x)
    return _dot(m01, x1) + _dot(m01, x2) + _dot(m01, x3)


def _dot_exact_r(x, m01):
    x1, x2, x3 = _split3(x)
    return _dot(x1, m01) + _dot(x2, m01) + _dot(x3, m01)


def _dot_exact_nt(x, m01):
    x1, x2, x3 = _split3(x)
    return _dot_nt(x1, m01) + _dot_nt(x2, m01) + _dot_nt(x3, m01)


def _sigmoid(z):
    e = jnp.exp(-jnp.abs(z))
    r = 1.0 / (1.0 + e)
    return jnp.where(z >= 0, r, e * r)


def _log_sigmoid(z):
    return jnp.minimum(z, 0.0) - jnp.log(1.0 + jnp.exp(-jnp.abs(z)))


def _iota(shape, dim):
    return lax.broadcasted_iota(jnp.int32, shape, dim)


def _head_sum_matrix():
    r = _iota((BRANCH, BRANCH), 0) // HEAD_DIM
    c = _iota((BRANCH, BRANCH), 1) // HEAD_DIM
    return (r == c).astype(BF16)


def _head_rmsnorm(o, g_row):
    ss = _dot_exact_r(o * o, _head_sum_matrix())
    return o * lax.rsqrt(ss * (1.0 / HEAD_DIM) + NORM_EPS) * g_row


def _mod_kernel(cc_ref, w_ref, b_ref, o_ref):
    cc = cc_ref[...]
    a = (cc * _sigmoid(cc)).astype(BF16)
    o_ref[...] = _dot(a, w_ref[...].astype(BF16)) + b_ref[...]


def _mod_call(cc, w_mod, b_mod):
    rows = cc.shape[0]
    nblk = 3
    return pl.pallas_call(
        _mod_kernel,
        grid=(DEPTH, nblk),
        in_specs=[
            pl.BlockSpec((rows, D_MODEL), lambda l, j: (0, 0)),
            pl.BlockSpec((None, D_MODEL, D_MODEL), lambda l, j: (l, 0, j)),
            pl.BlockSpec((None, 1, D_MODEL), lambda l, j: (l, 0, j)),
        ],
        out_specs=pl.BlockSpec((None, rows, D_MODEL), lambda l, j: (l, 0, j)),
        out_shape=jax.ShapeDtypeStruct((DEPTH, rows, 3 * D_MODEL), F32),
        compiler_params=_params("arbitrary", "arbitrary"),
        name="mod",
    )(cc, w_mod, b_mod.reshape(DEPTH, 1, 3 * D_MODEL))


def _rope(slab, cos, sin_next, sin_prev, off):
    nxt = pltpu.roll(slab, LANES - off, 1)
    prv = pltpu.roll(slab, off, 1)
    return slab * cos + nxt * sin_next + prv * sin_prev


def _inproj_kernel(x_ref, mod_ref, ng_ref, w_ref, b_ref, ta_ref, tc_ref,
                   pa_ref, pc_ref, pb_ref, pd_ref, pg_ref, po_ref, vta_ref, *, bsz):
    row = jnp.where(pl.program_id(1) == 0, bsz, pl.program_id(0))
    x = x_ref[...]
    mrow = mod_ref[pl.ds(row, 1), :]
    shift = mrow[:, 0:D_MODEL]
    scale = mrow[:, D_MODEL:2 * D_MODEL]
    ms = jnp.mean(x * x, axis=-1, keepdims=True)
    h = x * lax.rsqrt(ms + NORM_EPS) * ng_ref[...]
    h = (h * (1.0 + scale) + shift).astype(BF16)

    def proj(off, width):
        return _dot(h, w_ref[:, off:off + width]) + b_ref[:, off:off + width]

    def rope_seg(acc, tab_ref, off, q_scale):
        cos, s_next, s_prev = tab_ref[0], tab_ref[1], tab_ref[2]
        outs = []
        for j in range(4):
            r = _rope(acc[:, j * LANES:(j + 1) * LANES], cos, s_next, s_prev, off)
            outs.append(r * q_scale if j < 2 else r)
        outs.append(acc[:, 4 * LANES:])
        return jnp.concatenate(outs, axis=-1)

    acca = rope_seg(proj(OFF_A, SEG_A), ta_ref, DA_QK // 4, DA_QK ** -0.5 * LOG2E)
    pa_ref[...] = acca[:, 0:2 * BRANCH].astype(BF16)
    vta_ref[...] = acca[:, 2 * BRANCH:].T.astype(BF16)
    pc_ref[...] = rope_seg(proj(OFF_C, SEG_C), tc_ref, HEAD_DIM // 4, HEAD_DIM ** -0.5).astype(BF16)
    pb_ref[...] = proj(OFF_B, SEG_B)
    accd = proj(OFF_D, SEG_D)
    pd_ref[...] = jnp.concatenate(
        [accd[:, 0:BRANCH], accd[:, BRANCH:2 * BRANCH] * (HEAD_DIM ** -0.5), accd[:, 2 * BRANCH:]],
        axis=-1).astype(BF16)
    pg_ref[...] = proj(OFF_G, SEG_G)
    po_ref[...] = proj(OFF_O, SEG_O)


def _inproj_call(xc, mod_l, norm_g, w_r, b_r, tab_a, tab_c):
    bsz = xc.shape[0]
    rows = mod_l.shape[0]
    widths = [(2 * BRANCH, BF16), (SEG_C, BF16), (SEG_B, F32), (SEG_D, BF16), (SEG_G, F32), (SEG_O, F32)]
    out_specs = [pl.BlockSpec((None, TILE, w), lambda b, t: (b, t, 0)) for w, _ in widths]
    out_shape = [jax.ShapeDtypeStruct((bsz, T_ALL, w), dt) for w, dt in widths]
    out_specs.append(pl.BlockSpec((None, BRANCH, TILE), lambda b, t: (b, 0, t)))
    out_shape.append(jax.ShapeDtypeStruct((bsz, BRANCH, T_ALL), BF16))
    return pl.pallas_call(
        functools.partial(_inproj_kernel, bsz=bsz),
        grid=(bsz, N_TILES),
        in_specs=[
            pl.BlockSpec((None, TILE, D_MODEL), lambda b, t: (b, t, 0)),
            pl.BlockSpec((rows, 3 * D_MODEL), lambda b, t: (0, 0)),
            pl.BlockSpec((1, D_MODEL), lambda b, t: (0, 0)),
            pl.BlockSpec((D_MODEL, PROJ_PAD), lambda b, t: (0, 0)),
            pl.BlockSpec((1, PROJ_PAD), lambda b, t: (0, 0)),
            pl.BlockSpec((3, TILE, LANES), lambda b, t: (0, t, 0)),
            pl.BlockSpec((3, TILE, LANES), lambda b, t: (0, t, 0)),
        ],
        out_specs=out_specs,
        out_shape=out_shape,
        compiler_params=_params("arbitrary", "arbitrary"),
        name="inproj",
    )(xc, mod_l, norm_g.reshape(1, D_MODEL), w_r, b_r.reshape(1, PROJ_PAD), tab_a, tab_c)


def _diffattn_kernel(q_ref, k_ref, vt_ref, lam_ref, g_ref, o_ref, acc_ref, s_ref, *, lam_init, q_tile0):
    qb = pl.program_id(1) + q_tile0
    lp = lam_ref[...]
    lam = (jnp.exp(jnp.sum(lp[0:1] * lp[1:2], axis=-1, keepdims=True))
           - jnp.exp(jnp.sum(lp[2:3] * lp[3:4], axis=-1, keepdims=True)) + lam_init)
    q = q_ref[...]
    lane = _iota((1, BRANCH), 1)
    n_pairs = 2 * HEADS
    sub = TILE // 8

    def attend(nk):
        n_kt = nk // TILE
        qms = [jnp.where(lane // DA_QK == hm, q, jnp.zeros_like(q)) for hm in range(n_pairs)]

        def logits(hm):
            m8 = None
            half = max(nk // 2, TILE)
            for r0 in range(0, nk, half):
                st = _dot_nt(k_ref[r0:r0 + half, :], qms[hm])
                s_ref[hm % 2, r0:r0 + half, :] = st
                mh = jnp.max(st.reshape(half // 8, 8, TILE), axis=0)
                m8 = mh if m8 is None else jnp.maximum(m8, mh)
            return m8

        ones_rows = jnp.ones((ONES_ROWS, TILE), BF16)

        def value_tile(hm, j, mb, ot):
            h = hm // 2
            st = s_ref[hm % 2, j * TILE:(j + 1) * TILE, :]
            e = jnp.exp2(st.reshape(sub, 8, TILE) - mb[None])
            vt = vt_ref[h * HEAD_DIM:(h + 1) * HEAD_DIM, j * TILE:(j + 1) * TILE]
            vt1 = jnp.concatenate([vt, ones_rows], axis=0)
            return ot + _dot(vt1, e.reshape(TILE, TILE).astype(BF16))

        m8 = logits(0)
        for hm in range(n_pairs):
            mb = jnp.broadcast_to(jnp.max(m8, axis=0, keepdims=True), (8, TILE))
            if hm + 1 < n_pairs:
                m8 = logits(hm + 1)
            ot = jnp.zeros((HEAD_DIM + ONES_ROWS, TILE), F32)
            for j in range(n_kt):
                ot = value_tile(hm, j, mb, ot)
            l = ot[HEAD_DIM:HEAD_DIM + 1, :]
            ot = ot[0:HEAD_DIM, :]
            rows = slice((hm // 2) * HEAD_DIM, (hm // 2 + 1) * HEAD_DIM)
            if hm % 2 == 0:
                acc_ref[rows, :] = ot * (1.0 / l)
            else:
                acc_ref[rows, :] -= ot * (lam / l)
        o_ref[...] = _head_rmsnorm(acc_ref[...].T, g_ref[...]) * (1.0 - lam_init)

    @pl.when(qb == 0)
    def _():
        attend(CTX_LEN)

    @pl.when(qb > 0)
    def _():
        attend(T_ALL)


def _diffattn_call(pa, vta, lam_p, g_row, layer_idx, need_ctx):
    bsz = pa.shape[0]
    q_tile0 = 0 if need_ctx else 1
    lam_init = 0.8 - 0.6 * math.exp(-0.3 * layer_idx)
    return pl.pallas_call(
        functools.partial(_diffattn_kernel, lam_init=lam_init, q_tile0=q_tile0),
        grid=(bsz, N_TILES - q_tile0),
        in_specs=[
            pl.BlockSpec((None, TILE, BRANCH), lambda b, t: (b, t + q_tile0, 0)),
            pl.BlockSpec((None, T_ALL, BRANCH), lambda b, t: (b, 0, 1)),
            pl.BlockSpec((None, BRANCH, T_ALL), lambda b, t: (b, 0, 0)),
            pl.BlockSpec((4, DA_QK), lambda b, t: (0, 0)),
            pl.BlockSpec((1, BRANCH), lambda b, t: (0, 0)),
        ],
        out_specs=pl.BlockSpec((None, TILE, BRANCH), lambda b, t: (b, t + q_tile0, 0)),
        out_shape=jax.ShapeDtypeStruct((bsz, T_ALL, BRANCH), F32),
        scratch_shapes=[pltpu.VMEM((BRANCH, TILE), F32), pltpu.VMEM((2, T_ALL, TILE), F32)],
        compiler_params=_params("arbitrary", "arbitrary"),
        name="diffattn",
    )(pa, pa, vta, lam_p, g_row)


BAND = 2 * TILE


def _window_kernel(q_ref, k_ref, v_ref, sink_ref, o_ref, *, q_tile0):
    qb = pl.program_id(1) + q_tile0
    q = q_ref[...]
    lane = _iota((1, BRANCH), 1)
    kc = k_ref[0:CTX_LEN, :]
    vc = v_ref[0:CTX_LEN, :]

    def attend(band):
        if band:
            a = (qb - 1) * TILE
            start = jnp.clip(a - SW_WINDOW, 0, SEQ - BAND)
            row0 = pl.multiple_of(CTX_LEN + start, SW_WINDOW)
            kb = k_ref[pl.ds(row0, BAND), :]
            vb = v_ref[pl.ds(row0, BAND), :]
            qpos = a + _iota((TILE, 1), 0)
            kpos = start + _iota((1, BAND), 1)
            valid = jnp.abs(qpos - kpos) <= SW_WINDOW
        out = jnp.zeros((TILE, BRANCH), F32)
        for h in range(HEADS):
            qm = jnp.where(lane // HEAD_DIM == h, q, jnp.zeros_like(q))
            sink = sink_ref[:, h:h + 1]
            sc = _dot_nt(qm, kc)
            m = jnp.maximum(jnp.max(sc, axis=-1, keepdims=True), sink)
            if band:
                sb = jnp.where(valid, _dot_nt(qm, kb), NEG_BIG)
                m = jnp.maximum(m, jnp.max(sb, axis=-1, keepdims=True))
            ec = jnp.exp(sc - m)
            l = jnp.sum(ec, axis=-1, keepdims=True) + jnp.exp(sink - m)
            o = _dot(ec.astype(BF16), vc)
            if band:
                eb = jnp.exp(sb - m)
                l = l + jnp.sum(eb, axis=-1, keepdims=True)
                o = o + _dot(eb.astype(BF16), vb)
            out = out + jnp.where(lane // HEAD_DIM == h, o * (1.0 / l), 0.0)
        o_ref[...] = out

    @pl.when(qb == 0)
    def _():
        attend(False)

    @pl.when(qb > 0)
    def _():
        attend(True)


def _window_call(pc, sink_row, need_ctx):
    bsz = pc.shape[0]
    q_tile0 = 0 if need_ctx else 1
    return pl.pallas_call(
        functools.partial(_window_kernel, q_tile0=q_tile0),
        grid=(bsz, N_TILES - q_tile0),
        in_specs=[
            pl.BlockSpec((None, TILE, BRANCH), lambda b, t: (b, t + q_tile0, 0)),
            pl.BlockSpec((None, T_ALL, BRANCH), lambda b, t: (b, 0, 1)),
            pl.BlockSpec((None, T_ALL, BRANCH), lambda b, t: (b, 0, 2)),
            pl.BlockSpec((1, HEADS), lambda b, t: (0, 0)),
        ],
        out_specs=pl.BlockSpec((None, TILE, BRANCH), lambda b, t: (b, t + q_tile0, 0)),
        out_shape=jax.ShapeDtypeStruct((bsz, T_ALL, BRANCH), F32),
        compiler_params=_params("arbitrary", "arbitrary"),
        name="window",
    )(pc, pc, pc, sink_row)


N_CHUNK = T_ALL // HG_CHUNK
CTX_CHUNKS = CTX_LEN // HG_CHUNK
CH_PER_TILE = TILE // HG_CHUNK
HG_UNROLL = 4


def _hgrn_gates(z, lb_terms):
    if lb_terms is None:
        return _log_sigmoid(z) * LOG2E, _sigmoid(-z)
    lb, log_lb, log_1m = lb_terms
    bt = log_1m + _log_sigmoid(z)
    mx = jnp.maximum(log_lb, bt)
    log_f = mx + jnp.log(jnp.exp(log_lb - mx) + jnp.exp(bt - mx))
    return log_f * LOG2E, (1.0 - lb) * _sigmoid(-z)


def _hgrn_intra(direction, q3, c3, k3, v3, head_ones):
    half = HG_CHUNK // 2
    slabs, meta = [], []
    for s in range(HG_CHUNK):
        cs = c3[:, s:s + 1, :]
        ks = k3[:, s:s + 1, :]
        for g in range(2):
            lo_row, hi_row = half * g, half * g + half - 1
            if direction == 0:
                none_valid, all_valid = hi_row < s, lo_row >= s
            else:
                none_valid, all_valid = lo_row > s, hi_row <= s
            if none_valid:
                continue
            d = c3[:, half * g:half * (g + 1), :] - cs
            if not all_valid:
                row = _iota((1, half, 1), 1) + half * g
                d = jnp.where((row >= s) if direction == 0 else (row <= s), d, NEG_BIG)
            x = q3[:, half * g:half * (g + 1), :] * jnp.exp2(d) * ks
            slabs.append(x.reshape(CH_PER_TILE * half, BRANCH).astype(BF16))
            meta.append((s, g))
    a_all = _dot(jnp.concatenate(slabs, axis=0), head_ones)
    o = [jnp.zeros((CH_PER_TILE, half, BRANCH), F32) for _ in range(2)]
    n = CH_PER_TILE * half
    for i, (s, g) in enumerate(meta):
        o[g] = o[g] + a_all[i * n:(i + 1) * n].reshape(CH_PER_TILE, half, BRANCH) * v3[:, s:s + 1, :]
    return jnp.concatenate(o, axis=1).reshape(TILE, BRANCH)


def _hgrn_kernel(p_ref, lb_ref, g_ref, o_ref, qt_ref, ke_ref, vb_ref, dec_ref, of_ref, oi_ref, st_ref,
                 *, layer_idx):
    lane_r = _iota((TILE, TILE), 0)
    lane_c = _iota((TILE, TILE), 1)
    same_chunk = (lane_r // HG_CHUNK) == (lane_c // HG_CHUNK)
    chunk_ones = same_chunk.astype(BF16)
    tris = [(same_chunk & (lane_c <= lane_r)).astype(BF16),
            (same_chunk & (lane_c >= lane_r)).astype(BF16)]
    head_ones = _head_sum_matrix()
    head_mask = (lane_r // HEAD_DIM) == (lane_c // HEAD_DIM)

    lb_terms = None
    if layer_idx > 0:
        lbp = lb_ref[...]
        lbp = lbp - jnp.max(lbp, axis=0, keepdims=True)
        sm = jnp.exp(lbp)
        sm = sm / jnp.sum(sm, axis=0, keepdims=True)
        lb = jnp.sum(sm[1:layer_idx + 1], axis=0, keepdims=True)
        lb_terms = (lb, jnp.log(lb), jnp.log(1.0 - lb))

    def tile_body(t, carry):
        rows = pl.ds(pl.multiple_of(t * TILE, TILE), TILE)
        q = p_ref[rows, 0:BRANCH] * (HEAD_DIM ** -0.5)
        v = p_ref[rows, 3 * BRANCH:4 * BRANCH]
        vb_ref[rows, :] = v.astype(BF16)
        shape3 = (CH_PER_TILE, HG_CHUNK, BRANCH)
        o_intra = None
        for direction in (0, 1):
            z = p_ref[rows, (1 + direction) * BRANCH:(2 + direction) * BRANCH]
            log2_f, kk = _hgrn_gates(z, lb_terms)
            cum = _dot_exact_l(tris[direction], log2_f)
            tot = _dot_exact_l(chunk_ones, log2_f)
            qt_ref[direction, rows, :] = (q * jnp.exp2(cum)).astype(BF16)
            ke_ref[direction, rows, :] = (kk * jnp.exp2(tot - cum)).astype(BF16)
            dec_ref[direction, pl.ds(pl.multiple_of(t * CH_PER_TILE, CH_PER_TILE), CH_PER_TILE), :] = (
                jnp.exp2(tot.reshape(shape3)[:, 0, :]))
            o_d = _hgrn_intra(direction, q.reshape(shape3), cum.reshape(shape3), kk.reshape(shape3),
                              v.reshape(shape3), head_ones)
            o_intra = o_d if o_intra is None else o_intra + o_d
        of_ref[rows, :] = o_intra
        return carry

    lax.fori_loop(0, N_TILES, tile_body, 0)

    st_ref[...] = jnp.zeros_like(st_ref)

    def chunk_step(i, direction):
        if direction == 0:
            c = i
        else:
            c = jnp.where(i < CTX_CHUNKS, CTX_CHUNKS - 1 - i, N_CHUNK - 1 + CTX_CHUNKS - i)
        rows = pl.ds(pl.multiple_of(c * HG_CHUNK, HG_CHUNK), HG_CHUNK)
        st = st_ref[direction]
        oi_ref[direction, rows, :] = _dot_nt(qt_ref[direction, rows, :], st.astype(BF16))
        ds = _dot_tn(vb_ref[rows, :], ke_ref[direction, rows, :])
        st_ref[direction] = st * dec_ref[direction, pl.ds(c, 1), :] + jnp.where(head_mask, ds, 0.0)

    def chunk_body(it, carry):
        for u in range(HG_UNROLL):
            for direction in (0, 1):
                chunk_step(it * HG_UNROLL + u, direction)
        return carry

    lax.fori_loop(0, N_CHUNK // HG_UNROLL, chunk_body, 0)

    o_ref[...] = _head_rmsnorm(of_ref[...] + oi_ref[0] + oi_ref[1], g_ref[...])


def _hgrn_call(pb, hg_lb, g_row, layer_idx):
    bsz = pb.shape[0]
    return pl.pallas_call(
        functools.partial(_hgrn_kernel, layer_idx=layer_idx),
        grid=(bsz,),
        in_specs=[
            pl.BlockSpec((None, T_ALL, SEG_B), lambda b: (b, 0, 0)),
            pl.BlockSpec((DEPTH, BRANCH), lambda b: (0, 0)),
            pl.BlockSpec((1, BRANCH), lambda b: (0, 0)),
        ],
        out_specs=pl.BlockSpec((None, T_ALL, BRANCH), lambda b: (b, 0, 0)),
        out_shape=jax.ShapeDtypeStruct((bsz, T_ALL, BRANCH), F32),
        scratch_shapes=[pltpu.VMEM((2, T_ALL, BRANCH), BF16), pltpu.VMEM((2, T_ALL, BRANCH), BF16),
                        pltpu.VMEM((T_ALL, BRANCH), BF16), pltpu.VMEM((2, N_CHUNK, BRANCH), F32),
                        pltpu.VMEM((T_ALL, BRANCH), F32), pltpu.VMEM((2, T_ALL, BRANCH), F32),
                        pltpu.VMEM((2, BRANCH, BRANCH), F32)],
        compiler_params=_params("arbitrary"),
        name="hgrn",
    )(pb, hg_lb, g_row)


def _mlstm_kernel(p_ref, gt_ref, g_ref, o_ref, h_ref, c_ref, n_ref, m_ref):
    rr = _iota((TILE, TILE), 0)
    cc = _iota((TILE, TILE), 1)
    head_mask = (rr // HEAD_DIM) == (cc // HEAD_DIM)
    lane = _iota((1, BRANCH), 1)

    for direction in (0, 1):
        causal = (cc <= rr) if direction == 0 else (cc >= rr)
        tri = causal.astype(BF16)
        c_ref[...] = jnp.zeros_like(c_ref)
        n_ref[...] = jnp.zeros_like(n_ref)
        m_ref[...] = jnp.zeros_like(m_ref)

        def chunk_body(i, carry):
            if direction == 0:
                c = i
            else:
                c = jnp.where(i == 0, 0, N_TILES - i)
            rows = pl.ds(pl.multiple_of(c * TILE, TILE), TILE)
            q = p_ref[rows, 0:BRANCH]
            k = p_ref[rows, BRANCH:2 * BRANCH]
            v = p_ref[rows, 2 * BRANCH:3 * BRANCH]
            qf = q.astype(F32)
            kf = k.astype(F32)
            gates = gt_ref[rows, :]
            lf = _log_sigmoid(gates)
            cum = _dot_exact_l(tri, lf)
            gates_t = gates.T
            cum_t = _dot_exact_nt(lf.T, tri)
            tot = cum[TILE - 1:TILE, :] if direction == 0 else cum[0:1, :]
            cbd = c_ref[...].astype(BF16)
            n0 = n_ref[...]
            m0 = m_ref[...]

            hid = jnp.zeros((TILE, BRANCH), F32)
            wexp = jnp.zeros((TILE, BRANCH), F32)
            sp_row = jnp.zeros((1, BRANCH), F32)
            m_row = jnp.zeros((1, BRANCH), F32)
            for h in range(HEADS):
                icol = 4 * direction + h
                fcol = 8 + 4 * direction + h
                hmask = lane // HEAD_DIM == h
                cumc = cum[:, fcol:fcol + 1]
                igc = gates[:, icol:icol + 1]
                cumr = cum_t[fcol:fcol + 1, :]
                igr = gates_t[icol:icol + 1, :]
                m0h = m0[:, h * HEAD_DIM:h * HEAD_DIM + 1]
                logd = jnp.where(causal, cumc - cumr + igr, NEG_BIG)
                inter = cumc + m0h
                m_t = jnp.maximum(jnp.max(logd, axis=-1, keepdims=True), inter)
                dmat = jnp.exp(logd - m_t)
                g0 = jnp.exp(inter - m_t)
                qm = jnp.where(hmask, q, jnp.zeros_like(q))
                s = _dot_nt(qm, k) * dmat
                num = _dot(s.astype(BF16), v) + g0 * _dot(qm, cbd)
                qn = jnp.sum(jnp.where(hmask, qf * n0, 0.0), axis=-1, keepdims=True)
                den = jnp.sum(s, axis=-1, keepdims=True) + g0 * qn
                hid = hid + jnp.where(hmask, num / jnp.maximum(jnp.abs(den), jnp.exp(-m_t)), 0.0)
                toth = tot[:, fcol:fcol + 1]
                a = toth - cumc + igc
                m_loc = jnp.max(a, axis=0, keepdims=True)
                m_new = jnp.maximum(toth + m0h, m_loc)
                sp = jnp.exp(toth + m0h - m_new)
                sl = jnp.exp(m_loc - m_new)
                wexp = jnp.where(hmask, jnp.exp(a - m_loc) * sl, wexp)
                sp_row = jnp.where(hmask, sp, sp_row)
                m_row = jnp.where(hmask, m_new, m_row)

            if direction == 0:
                h_ref[rows, :] = hid
            else:
                h_ref[rows, :] += hid
            wk = kf * wexp
            dc = _dot_tn(wk.astype(BF16), v)
            c_ref[...] = c_ref[...] * sp_row + jnp.where(head_mask, dc, 0.0)
            n_ref[...] = n0 * sp_row + jnp.sum(wk, axis=0, keepdims=True)
            m_ref[...] = m_row
            return carry

        lax.fori_loop(0, N_TILES, chunk_body, 0)

    o_ref[...] = _head_rmsnorm(h_ref[...], g_ref[...])


def _mlstm_call(pd, pg, g_row):
    bsz = pd.shape[0]
    return pl.pallas_call(
        _mlstm_kernel,
        grid=(bsz,),
        in_specs=[
            pl.BlockSpec((None, T_ALL, SEG_D), lambda b: (b, 0, 0)),
            pl.BlockSpec((None, T_ALL, SEG_G), lambda b: (b, 0, 0)),
            pl.BlockSpec((1, BRANCH), lambda b: (0, 0)),
        ],
        out_specs=pl.BlockSpec((None, T_ALL, BRANCH), lambda b: (b, 0, 0)),
        out_shape=jax.ShapeDtypeStruct((bsz, T_ALL, BRANCH), F32),
        scratch_shapes=[pltpu.VMEM((T_ALL, BRANCH), F32), pltpu.VMEM((BRANCH, BRANCH), F32),
                        pltpu.VMEM((1, BRANCH), F32), pltpu.VMEM((1, BRANCH), F32)],
        compiler_params=_params("arbitrary"),
        name="mlstm",
    )(pd, pg, g_row)


def _outproj_kernel(x_ref, ya_ref, yb_ref, yc_ref, yd_ref, po_ref, mod_ref, w_ref, fg_ref, o_ref,
                    *, bsz, tile0, final_norm):
    t = pl.program_id(1) + tile0
    row = jnp.where(t == 0, bsz, pl.program_id(0))
    gate_mod = mod_ref[pl.ds(row, 1), 2 * D_MODEL:3 * D_MODEL]
    og = po_ref[:, 0:BRANCH]
    gate = po_ref[:, BRANCH:]
    mixed = jnp.concatenate([ya_ref[...], yb_ref[...], yc_ref[...], yd_ref[...] * _sigmoid(og)], axis=-1)
    mixed = (mixed * (gate * _sigmoid(gate))).astype(BF16)
    xn = x_ref[...] + gate_mod * _dot(mixed, w_ref[...])
    if final_norm:
        ms = jnp.mean(xn * xn, axis=-1, keepdims=True)
        xn = xn * lax.rsqrt(ms + NORM_EPS) * fg_ref[...]
    o_ref[...] = xn


def _outproj_call(xc, ya, yb, yc, yd, po, mod_l, w_out_bf, final_g, last):
    bsz = xc.shape[0]
    tile0 = 1 if last else 0
    n_out = N_TILES - tile0
    rows = mod_l.shape[0]

    def tok(width):
        return pl.BlockSpec((None, TILE, width), lambda b, t: (b, t + tile0, 0))

    return pl.pallas_call(
        functools.partial(_outproj_kernel, bsz=bsz, tile0=tile0, final_norm=last),
        grid=(bsz, n_out),
        in_specs=[tok(D_MODEL), tok(BRANCH), tok(BRANCH), tok(BRANCH), tok(BRANCH), tok(SEG_O),
                  pl.BlockSpec((rows, 3 * D_MODEL), lambda b, t: (0, 0)),
                  pl.BlockSpec((D_MODEL, D_MODEL), lambda b, t: (0, 0)),
                  pl.BlockSpec((1, D_MODEL), lambda b, t: (0, 0))],
        out_specs=pl.BlockSpec((None, TILE, D_MODEL), lambda b, t: (b, t, 0)),
        out_shape=jax.ShapeDtypeStruct((bsz, n_out * TILE, D_MODEL), F32),
        compiler_params=_params("arbitrary", "arbitrary"),
        name="outproj",
    )(xc, ya, yb, yc, yd, po, mod_l, w_out_bf, final_g.reshape(1, D_MODEL))


def _rope_tables(dim):
    quarter = dim // 4
    half = dim // 2
    pos = np.arange(SEQ)
    row = (pos // GRID_W).astype(np.float32)
    col = (pos % GRID_W).astype(np.float32)
    inv = (np.float32(ROPE_BASE) ** (-np.arange(0, half, 2, dtype=np.float32) / np.float32(half))).astype(np.float32)
    ang_r = row[:, None] * inv[None, :]
    ang_c = col[:, None] * inv[None, :]
    lane = np.arange(LANES) % dim
    part = lane // quarter
    freq = lane % quarter
    ang = np.where(part[None, :] < 2, ang_r[:, freq], ang_c[:, freq]).astype(np.float32)
    cos = np.cos(ang)
    sin = np.sin(ang)
    first = (part % 2 == 0)[None, :]
    s_next = np.where(first, -sin, 0.0)
    s_prev = np.where(first, 0.0, sin)
    tab = np.stack([cos, s_next, s_prev]).astype(np.float32)
    ident = np.stack([np.ones((CTX_LEN, LANES)), np.zeros((CTX_LEN, LANES)),
                      np.zeros((CTX_LEN, LANES))]).astype(np.float32)
    return jnp.asarray(np.concatenate([ident, tab], axis=1))


def _relayout_in_proj(w, bias):
    def cols(a):
        qa, ka, va = a[..., 0:256], a[..., 256:512], a[..., 512:768]
        seg_b = a[..., 768:1792]
        qc = a[..., 1792:2048]
        kc = a[..., 2048:2176]
        vc = a[..., 2176:2304]
        seg_d = a[..., 2304:3072]
        gates = a[..., 3072:3088]
        seg_o = a[..., 3088:4368]
        rep = lambda kv: jnp.concatenate([kv[..., 0:64], kv[..., 0:64], kv[..., 64:128], kv[..., 64:128]], axis=-1)
        pad = jnp.zeros(a.shape[:-1] + (SEG_G - 16,), a.dtype)
        return jnp.concatenate([qa, ka, va, qc, rep(kc), rep(vc), seg_b, seg_d, gates, pad, seg_o], axis=-1)
    return cols(w).astype(BF16), cols(bias)


def kernel(x, c, ctx, c_ctx, w_mod, b_mod, norm_g, w_in, b_in, diff_lam, diff_g, hg_lb, hg_g,
           sw_sink, ml_g, w_out, final_g):
    bsz = x.shape[0]
    tab_a = _rope_tables(DA_QK)
    tab_c = _rope_tables(HEAD_DIM)
    rows = ((bsz + 1 + 7) // 8) * 8
    cc = jnp.concatenate([c, c_ctx[None, :], jnp.zeros((rows - bsz - 1, D_MODEL), F32)], axis=0)
    mod = _mod_call(cc, w_mod, b_mod)
    xc = jnp.concatenate([ctx, x], axis=1)
    tile4 = lambda g: jnp.tile(g, HEADS).reshape(1, BRANCH)
    out = None
    for l in range(DEPTH):
        last = l == DEPTH - 1
        w_r, b_r = _relayout_in_proj(w_in[l], b_in[l])
        pa, pc, pb, pd, pg, po, vta = _inproj_call(xc, mod[l], norm_g[l], w_r, b_r, tab_a, tab_c)
        ya = _diffattn_call(pa, vta, diff_lam[l], tile4(diff_g[l]), l, not last)
        yb = _hgrn_call(pb, hg_lb, tile4(hg_g[l]), l)
        yc = _window_call(pc, sw_sink[l].reshape(1, HEADS), not last)
        yd = _mlstm_call(pd, pg, tile4(ml_g[l]))
        res = _outproj_call(xc, ya, yb, yc, yd, po, mod[l], w_out[l].astype(BF16), final_g, last)
        if last:
            out = res
        else:
            xc = res
    return out
```

```python
import functools
import math

import numpy as np
import jax
import jax.numpy as jnp
from jax import lax
from jax.experimental import pallas as pl
from jax.experimental.pallas import tpu as pltpu

F32 = jnp.float32
BF16 = jnp.bfloat16

D_MODEL = 1024
SEQ = 2048
CTX_LEN = 256
T_ALL = CTX_LEN + SEQ
GRID_W = 64
DEPTH = 2
HEADS = 4
HEAD_DIM = 64
BRANCH = HEADS * HEAD_DIM
DA_QK = 32
SW_WINDOW = 128
HG_CHUNK = 16
ROPE_BASE = 10000.0
NORM_EPS = 1e-6
NEG_BIG = -1e30
LOG2E = math.log2(math.e)

TILE = 256
N_TILES = T_ALL // TILE
LANES = 128
ONES_ROWS = 16

SEG_A = 3 * BRANCH
SEG_C = 3 * BRANCH
SEG_B = 4 * BRANCH
SEG_D = 3 * BRANCH
SEG_G = LANES
SEG_O = BRANCH + D_MODEL
OFF_A = 0
OFF_C = OFF_A + SEG_A
OFF_B = OFF_C + SEG_C
OFF_D = OFF_B + SEG_B
OFF_G = OFF_D + SEG_D
OFF_O = OFF_G + SEG_G
PROJ_PAD = OFF_O + SEG_O

VMEM_LIMIT = 56 * 1024 * 1024


def _params(*sem):
    return pltpu.CompilerParams(dimension_semantics=sem, vmem_limit_bytes=VMEM_LIMIT)


def _dot(a, b):
    return jnp.dot(a, b, preferred_element_type=F32)


def _dot_nt(a, b):
    return lax.dot_general(a, b, (((1,), (1,)), ((), ())), preferred_element_type=F32)


def _dot_tn(a, b):
    return lax.dot_general(a, b, (((0,), (0,)), ((), ())), preferred_element_type=F32)


def _split3(x):
    x1 = x.astype(BF16)
    r1 = x - x1.astype(F32)
    x2 = r1.astype(BF16)
    x3 = (r1 - x2.astype(F32)).astype(BF16)
    return x1, x2, x3


def _dot_exact_l(m01, x):
    x1, x2, x3 = _split3(x)
    return _dot(m01, x1) + _dot(m01, x2) + _dot(m01, x3)


def _dot_exact_r(x, m01):
    x1, x2, x3 = _split3(x)
    return _dot(x1, m01) + _dot(x2, m01) + _dot(x3, m01)


def _dot_exact_nt(x, m01):
    x1, x2, x3 = _split3(x)
    return _dot_nt(x1, m01) + _dot_nt(x2, m01) + _dot_nt(x3, m01)


def _sigmoid(z):
    e = jnp.exp(-jnp.abs(z))
    r = 1.0 / (1.0 + e)
    return jnp.where(z >= 0, r, e * r)


def _log_sigmoid(z):
    return jnp.minimum(z, 0.0) - jnp.log(1.0 + jnp.exp(-jnp.abs(z)))


def _iota(shape, dim):
    return lax.broadcasted_iota(jnp.int32, shape, dim)


def _head_sum_matrix():
    r = _iota((BRANCH, BRANCH), 0) // HEAD_DIM
    c = _iota((BRANCH, BRANCH), 1) // HEAD_DIM
    return (r == c).astype(BF16)


def _head_rmsnorm(o, g_row):
    ss = _dot_exact_r(o * o, _head_sum_matrix())
    return o * lax.rsqrt(ss * (1.0 / HEAD_DIM) + NORM_EPS) * g_row


def _mod_kernel(cc_ref, w_ref, b_ref, o_ref):
    cc = cc_ref[...]
    a = (cc * _sigmoid(cc)).astype(BF16)
    o_ref[...] = _dot(a, w_ref[...].astype(BF16)) + b_ref[...]


def _mod_call(cc, w_mod, b_mod):
    rows = cc.shape[0]
    nblk = 3
    return pl.pallas_call(
        _mod_kernel,
        grid=(DEPTH, nblk),
        in_specs=[
            pl.BlockSpec((rows, D_MODEL), lambda l, j: (0, 0)),
            pl.BlockSpec((None, D_MODEL, D_MODEL), lambda l, j: (l, 0, j)),
            pl.BlockSpec((None, 1, D_MODEL), lambda l, j: (l, 0, j)),
        ],
        out_specs=pl.BlockSpec((None, rows, D_MODEL), lambda l, j: (l, 0, j)),
        out_shape=jax.ShapeDtypeStruct((DEPTH, rows, 3 * D_MODEL), F32),
        compiler_params=_params("arbitrary", "arbitrary"),
        name="mod",
    )(cc, w_mod, b_mod.reshape(DEPTH, 1, 3 * D_MODEL))


def _rope(slab, cos, sin_next, sin_prev, off):
    nxt = pltpu.roll(slab, LANES - off, 1)
    prv = pltpu.roll(slab, off, 1)
    return slab * cos + nxt * sin_next + prv * sin_prev


def _inproj_kernel(x_ref, mod_ref, ng_ref, w_ref, b_ref, ta_ref, tc_ref,
                   pa_ref, pc_ref, pb_ref, pd_ref, pg_ref, po_ref, vta_ref, vtc_ref, vtd_ref, pgt_ref,
                   *, bsz):
    row = jnp.where(pl.program_id(1) == 0, bsz, pl.program_id(0))
    x = x_ref[...]
    mrow = mod_ref[pl.ds(row, 1), :]
    shift = mrow[:, 0:D_MODEL]
    scale = mrow[:, D_MODEL:2 * D_MODEL]
    ms = jnp.mean(x * x, axis=-1, keepdims=True)
    h = x * lax.rsqrt(ms + NORM_EPS) * ng_ref[...]
    h = (h * (1.0 + scale) + shift).astype(BF16)

    def proj(off, width):
        return _dot(h, w_ref[:, off:off + width]) + b_ref[:, off:off + width]

    def rope_seg(acc, tab_ref, off, q_scale):
        cos, s_next, s_prev = tab_ref[0], tab_ref[1], tab_ref[2]
        outs = []
        for j in range(4):
            r = _rope(acc[:, j * LANES:(j + 1) * LANES], cos, s_next, s_prev, off)
            outs.append(r * q_scale if j < 2 else r)
        outs.append(acc[:, 4 * LANES:])
        return jnp.concatenate(outs, axis=-1)

    acca = rope_seg(proj(OFF_A, SEG_A), ta_ref, DA_QK // 4, DA_QK ** -0.5 * LOG2E)
    pa_ref[...] = acca[:, 0:2 * BRANCH].astype(BF16)
    vta_ref[...] = acca[:, 2 * BRANCH:].T.astype(BF16)
    accc = rope_seg(proj(OFF_C, SEG_C), tc_ref, HEAD_DIM // 4, HEAD_DIM ** -0.5 * LOG2E)
    pc_ref[...] = accc[:, 0:2 * BRANCH].astype(BF16)
    vtc_ref[...] = accc[:, 2 * BRANCH:].T.astype(BF16)
    pb_ref[...] = proj(OFF_B, SEG_B)
    accd = proj(OFF_D, SEG_D)
    pd_ref[...] = jnp.concatenate(
        [accd[:, 0:BRANCH], accd[:, BRANCH:2 * BRANCH] * (HEAD_DIM ** -0.5)], axis=-1).astype(BF16)
    vtd_ref[...] = accd[:, 2 * BRANCH:].T.astype(BF16)
    gates = proj(OFF_G, SEG_G)
    pg_ref[...] = gates
    pgt_ref[...] = gates.T
    po_ref[...] = proj(OFF_O, SEG_O)


def _inproj_call(xc, mod_l, norm_g, w_r, b_r, tab_a, tab_c):
    bsz = xc.shape[0]
    rows = mod_l.shape[0]
    widths = [(2 * BRANCH, BF16), (2 * BRANCH, BF16), (SEG_B, F32), (2 * BRANCH, BF16), (SEG_G, F32),
              (SEG_O, F32)]
    out_specs = [pl.BlockSpec((None, TILE, w), lambda b, t: (b, t, 0)) for w, _ in widths]
    out_shape = [jax.ShapeDtypeStruct((bsz, T_ALL, w), dt) for w, dt in widths]
    for rows_t, dt in ((BRANCH, BF16), (BRANCH, BF16), (BRANCH, BF16), (SEG_G, F32)):
        out_specs.append(pl.BlockSpec((None, rows_t, TILE), lambda b, t: (b, 0, t)))
        out_shape.append(jax.ShapeDtypeStruct((bsz, rows_t, T_ALL), dt))
    return pl.pallas_call(
        functools.partial(_inproj_kernel, bsz=bsz),
        grid=(bsz, N_TILES),
        in_specs=[
            pl.BlockSpec((None, TILE, D_MODEL), lambda b, t: (b, t, 0)),
            pl.BlockSpec((rows, 3 * D_MODEL), lambda b, t: (0, 0)),
            pl.BlockSpec((1, D_MODEL), lambda b, t: (0, 0)),
            pl.BlockSpec((D_MODEL, PROJ_PAD), lambda b, t: (0, 0)),
            pl.BlockSpec((1, PROJ_PAD), lambda b, t: (0, 0)),
            pl.BlockSpec((3, TILE, LANES), lambda b, t: (0, t, 0)),
            pl.BlockSpec((3, TILE, LANES), lambda b, t: (0, t, 0)),
        ],
        out_specs=out_specs,
        out_shape=out_shape,
        compiler_params=_params("arbitrary", "arbitrary"),
        name="inproj",
    )(xc, mod_l, norm_g.reshape(1, D_MODEL), w_r, b_r.reshape(1, PROJ_PAD), tab_a, tab_c)


def _diffattn_kernel(q_ref, k_ref, vt_ref, lam_ref, g_ref, o_ref, acc_ref, s_ref, *, lam_init, q_tile0):
    qb = pl.program_id(1) + q_tile0
    lp = lam_ref[...]
    lam = (jnp.exp(jnp.sum(lp[0:1] * lp[1:2], axis=-1, keepdims=True))
           - jnp.exp(jnp.sum(lp[2:3] * lp[3:4], axis=-1, keepdims=True)) + lam_init)
    q = q_ref[...]
    lane = _iota((1, BRANCH), 1)
    n_pairs = 2 * HEADS
    sub = TILE // 8

    def attend(nk):
        n_kt = nk // TILE
        qms = [jnp.where(lane // DA_QK == hm, q, jnp.zeros_like(q)) for hm in range(n_pairs)]

        def logits(hm):
            m8 = None
            half = max(nk // 2, TILE)
            for r0 in range(0, nk, half):
                st = _dot_nt(k_ref[r0:r0 + half, :], qms[hm])
                s_ref[hm % 2, r0:r0 + half, :] = st
                mh = jnp.max(st.reshape(half // 8, 8, TILE), axis=0)
                m8 = mh if m8 is None else jnp.maximum(m8, mh)
            return m8

        ones_rows = jnp.ones((ONES_ROWS, TILE), BF16)

        def value_tile(hm, j, mb, ot):
            h = hm // 2
            st = s_ref[hm % 2, j * TILE:(j + 1) * TILE, :]
            e = jnp.exp2(st.reshape(sub, 8, TILE) - mb[None])
            vt = vt_ref[h * HEAD_DIM:(h + 1) * HEAD_DIM, j * TILE:(j + 1) * TILE]
            vt1 = jnp.concatenate([vt, ones_rows], axis=0)
            return ot + _dot(vt1, e.reshape(TILE, TILE).astype(BF16))

        m8 = logits(0)
        for hm in range(n_pairs):
            mb = jnp.broadcast_to(jnp.max(m8, axis=0, keepdims=True), (8, TILE))
            if hm + 1 < n_pairs:
                m8 = logits(hm + 1)
            ot = jnp.zeros((HEAD_DIM + ONES_ROWS, TILE), F32)
            for j in range(n_kt):
                ot = value_tile(hm, j, mb, ot)
            l = ot[HEAD_DIM:HEAD_DIM + 1, :]
            ot = ot[0:HEAD_DIM, :]
            rows = slice((hm // 2) * HEAD_DIM, (hm // 2 + 1) * HEAD_DIM)
            if hm % 2 == 0:
                acc_ref[rows, :] = ot * (1.0 / l)
            else:
                acc_ref[rows, :] -= ot * (lam / l)
        o_ref[...] = _head_rmsnorm(acc_ref[...].T, g_ref[...]) * (1.0 - lam_init)

    @pl.when(qb == 0)
    def _():
        attend(CTX_LEN)

    @pl.when(qb > 0)
    def _():
        attend(T_ALL)


def _diffattn_call(pa, vta, lam_p, g_row, layer_idx, need_ctx):
    bsz = pa.shape[0]
    q_tile0 = 0 if need_ctx else 1
    lam_init = 0.8 - 0.6 * math.exp(-0.3 * layer_idx)
    return pl.pallas_call(
        functools.partial(_diffattn_kernel, lam_init=lam_init, q_tile0=q_tile0),
        grid=(bsz, N_TILES - q_tile0),
        in_specs=[
            pl.BlockSpec((None, TILE, BRANCH), lambda b, t: (b, t + q_tile0, 0)),
            pl.BlockSpec((None, T_ALL, BRANCH), lambda b, t: (b, 0, 1)),
            pl.BlockSpec((None, BRANCH, T_ALL), lambda b, t: (b, 0, 0)),
            pl.BlockSpec((4, DA_QK), lambda b, t: (0, 0)),
            pl.BlockSpec((1, BRANCH), lambda b, t: (0, 0)),
        ],
        out_specs=pl.BlockSpec((None, TILE, BRANCH), lambda b, t: (b, t, 0)),
        out_shape=jax.ShapeDtypeStruct((bsz, (N_TILES - q_tile0) * TILE, BRANCH), F32),
        scratch_shapes=[pltpu.VMEM((BRANCH, TILE), F32), pltpu.VMEM((2, T_ALL, TILE), F32)],
        compiler_params=_params("arbitrary", "arbitrary"),
        name="diffattn",
    )(pa, pa, vta, lam_p, g_row)


BAND = 2 * TILE


def _window_kernel(q_ref, k_ref, vt_ref, sink_ref, o_ref, acc_ref, s_ref, *, q_tile0):
    qb = pl.program_id(1) + q_tile0
    q = q_ref[...]
    lane = _iota((1, BRANCH), 1)
    ones_rows = jnp.ones((ONES_ROWS, TILE), BF16)

    def attend(band):
        if band:
            a = (qb - 1) * TILE
            start = jnp.clip(a - SW_WINDOW, 0, SEQ - BAND)
            row0 = pl.multiple_of(CTX_LEN + start, SW_WINDOW)
            kb = k_ref[pl.ds(row0, BAND), :]
            kpos = start + _iota((BAND, 1), 0)
            qpos = a + _iota((1, TILE), 1)
            valid = jnp.abs(qpos - kpos) <= SW_WINDOW
        sinks, maxes = [], []
        for h in range(HEADS):
            qm = jnp.where(lane // HEAD_DIM == h, q, jnp.zeros_like(q))
            sink = sink_ref[:, h:h + 1] * LOG2E
            sc = _dot_nt(k_ref[0:CTX_LEN, :], qm)
            s_ref[h, 0:CTX_LEN, :] = sc
            m = jnp.maximum(jnp.max(sc, axis=0, keepdims=True), sink)
            if band:
                sb = jnp.where(valid, _dot_nt(kb, qm), NEG_BIG)
                s_ref[h, CTX_LEN:CTX_LEN + BAND, :] = sb
                m = jnp.maximum(m, jnp.max(sb, axis=0, keepdims=True))
            sinks.append(sink)
            maxes.append(m)
        for h in range(HEADS):
            m = maxes[h]
            vt1 = jnp.concatenate([vt_ref[h * HEAD_DIM:(h + 1) * HEAD_DIM, 0:CTX_LEN], ones_rows], axis=0)
            pv = _dot(vt1, jnp.exp2(s_ref[h, 0:CTX_LEN, :] - m).astype(BF16))
            if band:
                vtb = vt_ref[h * HEAD_DIM:(h + 1) * HEAD_DIM, pl.ds(row0, BAND)]
                ones_b = jnp.ones((ONES_ROWS, BAND), BF16)
                eb = jnp.exp2(s_ref[h, CTX_LEN:CTX_LEN + BAND, :] - m).astype(BF16)
                pv = pv + _dot(jnp.concatenate([vtb, ones_b], axis=0), eb)
            l = pv[HEAD_DIM:HEAD_DIM + 1, :] + jnp.exp2(sinks[h] - m)
            acc_ref[h * HEAD_DIM:(h + 1) * HEAD_DIM, :] = pv[0:HEAD_DIM, :] * (1.0 / l)
        o_ref[...] = acc_ref[...].T

    @pl.when(qb == 0)
    def _():
        attend(False)

    @pl.when(qb > 0)
    def _():
        attend(True)


def _window_call(pc, vtc, sink_row, need_ctx):
    bsz = pc.shape[0]
    q_tile0 = 0 if need_ctx else 1
    n_q = N_TILES - q_tile0
    return pl.pallas_call(
        functools.partial(_window_kernel, q_tile0=q_tile0),
        grid=(bsz, n_q),
        in_specs=[
            pl.BlockSpec((None, TILE, BRANCH), lambda b, t: (b, t + q_tile0, 0)),
            pl.BlockSpec((None, T_ALL, BRANCH), lambda b, t: (b, 0, 1)),
            pl.BlockSpec((None, BRANCH, T_ALL), lambda b, t: (b, 0, 0)),
            pl.BlockSpec((1, HEADS), lambda b, t: (0, 0)),
        ],
        out_specs=pl.BlockSpec((None, TILE, BRANCH), lambda b, t: (b, t, 0)),
        out_shape=jax.ShapeDtypeStruct((bsz, n_q * TILE, BRANCH), F32),
        scratch_shapes=[pltpu.VMEM((BRANCH, TILE), F32), pltpu.VMEM((HEADS, CTX_LEN + BAND, TILE), F32)],
        compiler_params=_params("arbitrary", "arbitrary"),
        name="window",
    )(pc, pc, vtc, sink_row)


N_CHUNK = T_ALL // HG_CHUNK
CTX_CHUNKS = CTX_LEN // HG_CHUNK
CH_PER_TILE = TILE // HG_CHUNK
HG_UNROLL = 4


def _hgrn_gates(z, lb_terms):
    if lb_terms is None:
        return _log_sigmoid(z) * LOG2E, _sigmoid(-z)
    lb, log_lb, log_1m = lb_terms
    bt = log_1m + _log_sigmoid(z)
    mx = jnp.maximum(log_lb, bt)
    log_f = mx + jnp.log(jnp.exp(log_lb - mx) + jnp.exp(bt - mx))
    return log_f * LOG2E, (1.0 - lb) * _sigmoid(-z)


def _hgrn_intra(direction, q3, c3, k3, v3, head_ones):
    half = HG_CHUNK // 2
    slabs, meta = [], []
    for s in range(HG_CHUNK):
        cs = c3[:, s:s + 1, :]
        ks = k3[:, s:s + 1, :]
        for g in range(2):
            lo_row, hi_row = half * g, half * g + half - 1
            if direction == 0:
                none_valid, all_valid = hi_row < s, lo_row >= s
            else:
                none_valid, all_valid = lo_row > s, hi_row <= s
            if none_valid:
                continue
            d = c3[:, half * g:half * (g + 1), :] - cs
            if not all_valid:
                row = _iota((1, half, 1), 1) + half * g
                d = jnp.where((row >= s) if direction == 0 else (row <= s), d, NEG_BIG)
            x = q3[:, half * g:half * (g + 1), :] * jnp.exp2(d) * ks
            slabs.append(x.reshape(CH_PER_TILE * half, BRANCH).astype(BF16))
            meta.append((s, g))
    a_all = _dot(jnp.concatenate(slabs, axis=0), head_ones)
    o = [jnp.zeros((CH_PER_TILE, half, BRANCH), F32) for _ in range(2)]
    n = CH_PER_TILE * half
    for i, (s, g) in enumerate(meta):
        o[g] = o[g] + a_all[i * n:(i + 1) * n].reshape(CH_PER_TILE, half, BRANCH) * v3[:, s:s + 1, :]
    return jnp.concatenate(o, axis=1).reshape(TILE, BRANCH)


def _hgrn_kernel(p_ref, lb_ref, g_ref, o_ref, qt_ref, ke_ref, vb_ref, dec_ref, of_ref, oi_ref, st_ref,
                 *, layer_idx):
    lane_r = _iota((TILE, TILE), 0)
    lane_c = _iota((TILE, TILE), 1)
    same_chunk = (lane_r // HG_CHUNK) == (lane_c // HG_CHUNK)
    chunk_ones = same_chunk.astype(BF16)
    tris = [(same_chunk & (lane_c <= lane_r)).astype(BF16),
            (same_chunk & (lane_c >= lane_r)).astype(BF16)]
    head_ones = _head_sum_matrix()
    head_mask = (lane_r // HEAD_DIM) == (lane_c // HEAD_DIM)

    lb_terms = None
    if layer_idx > 0:
        lbp = lb_ref[...]
        lbp = lbp - jnp.max(lbp, axis=0, keepdims=True)
        sm = jnp.exp(lbp)
        sm = sm / jnp.sum(sm, axis=0, keepdims=True)
        lb = jnp.sum(sm[1:layer_idx + 1], axis=0, keepdims=True)
        lb_terms = (lb, jnp.log(lb), jnp.log(1.0 - lb))

    def tile_body(t, carry):
        rows = pl.ds(pl.multiple_of(t * TILE, TILE), TILE)
        q = p_ref[rows, 0:BRANCH] * (HEAD_DIM ** -0.5)
        v = p_ref[rows, 3 * BRANCH:4 * BRANCH]
        vb_ref[rows, :] = v.astype(BF16)
        shape3 = (CH_PER_TILE, HG_CHUNK, BRANCH)
        o_intra = None
        for direction in (0, 1):
            z = p_ref[rows, (1 + direction) * BRANCH:(2 + direction) * BRANCH]
            log2_f, kk = _hgrn_gates(z, lb_terms)
            cum = _dot_exact_l(tris[direction], log2_f)
            tot = _dot_exact_l(chunk_ones, log2_f)
            qt_ref[direction, rows, :] = (q * jnp.exp2(cum)).astype(BF16)
            ke_ref[direction, rows, :] = (kk * jnp.exp2(tot - cum)).astype(BF16)
            dec_ref[direction, pl.ds(pl.multiple_of(t * CH_PER_TILE, CH_PER_TILE), CH_PER_TILE), :] = (
                jnp.exp2(tot.reshape(shape3)[:, 0, :]))
            o_d = _hgrn_intra(direction, q.reshape(shape3), cum.reshape(shape3), kk.reshape(shape3),
                              v.reshape(shape3), head_ones)
            o_intra = o_d if o_intra is None else o_intra + o_d
        of_ref[rows, :] = o_intra
        return carry

    lax.fori_loop(0, N_TILES, tile_body, 0)

    st_ref[...] = jnp.zeros_like(st_ref)

    def chunk_step(i, direction):
        if direction == 0:
            c = i
        else:
            c = jnp.where(i < CTX_CHUNKS, CTX_CHUNKS - 1 - i, N_CHUNK - 1 + CTX_CHUNKS - i)
        rows = pl.ds(pl.multiple_of(c * HG_CHUNK, HG_CHUNK), HG_CHUNK)
        st = st_ref[direction]
        oi_ref[direction, rows, :] = _dot_nt(qt_ref[direction, rows, :], st.astype(BF16))
        ds = _dot_tn(vb_ref[rows, :], ke_ref[direction, rows, :])
        st_ref[direction] = st * dec_ref[direction, pl.ds(c, 1), :] + jnp.where(head_mask, ds, 0.0)

    def chunk_body(it, carry):
        for u in range(HG_UNROLL):
            for direction in (0, 1):
                chunk_step(it * HG_UNROLL + u, direction)
        return carry

    lax.fori_loop(0, N_CHUNK // HG_UNROLL, chunk_body, 0)

    o_ref[...] = _head_rmsnorm(of_ref[...] + oi_ref[0] + oi_ref[1], g_ref[...])


def _hgrn_call(pb, hg_lb, g_row, layer_idx):
    bsz = pb.shape[0]
    return pl.pallas_call(
        functools.partial(_hgrn_kernel, layer_idx=layer_idx),
        grid=(bsz,),
        in_specs=[
            pl.BlockSpec((None, T_ALL, SEG_B), lambda b: (b, 0, 0)),
            pl.BlockSpec((DEPTH, BRANCH), lambda b: (0, 0)),
            pl.BlockSpec((1, BRANCH), lambda b: (0, 0)),
        ],
        out_specs=pl.BlockSpec((None, T_ALL, BRANCH), lambda b: (b, 0, 0)),
        out_shape=jax.ShapeDtypeStruct((bsz, T_ALL, BRANCH), F32),
        scratch_shapes=[pltpu.VMEM((2, T_ALL, BRANCH), BF16), pltpu.VMEM((2, T_ALL, BRANCH), BF16),
                        pltpu.VMEM((T_ALL, BRANCH), BF16), pltpu.VMEM((2, N_CHUNK, BRANCH), F32),
                        pltpu.VMEM((T_ALL, BRANCH), F32), pltpu.VMEM((2, T_ALL, BRANCH), F32),
                        pltpu.VMEM((2, BRANCH, BRANCH), F32)],
        compiler_params=_params("arbitrary"),
        name="hgrn",
    )(pb, hg_lb, g_row)


def _mlstm_logits(direction, c, p_ref, vt_ref, g_ref, gt_ref, ct_ref, n_ref, m_ref, consts):
    tri, valid, _, lane, row16, lane16 = consts
    rows = pl.ds(pl.multiple_of(c * TILE, TILE), TILE)
    q = p_ref[rows, 0:BRANCH]
    k = p_ref[rows, BRANCH:2 * BRANCH]
    g = g_ref[rows, :]
    gt = gt_ref[:, rows]
    cum = _dot_exact_l(tri, _log_sigmoid(g) * LOG2E)
    cum_t = _dot_exact_nt(_log_sigmoid(gt[8:16, :]) * LOG2E, tri)
    ig_t = gt[0:8, :] * LOG2E
    ct = ct_ref[direction]
    n0 = n_ref[direction]
    n_hi = n0.astype(BF16).astype(F32)
    n_lo = n0 - n_hi
    n_rows = (jnp.where((row16 < HEADS) & (lane16 == row16), n_hi, 0.0)
              + jnp.where((row16 >= HEADS) & (lane16 == row16 - HEADS), n_lo, 0.0)).astype(BF16)
    inter_all = _dot_nt(jnp.concatenate([ct.astype(BF16), n_rows], axis=0), q)
    heads = []
    for h in range(HEADS):
        r = HEADS * direction + h
        cumr = cum_t[r:r + 1, :]
        ucol = g[:, r:r + 1] * LOG2E - cum[:, 2 * HEADS + r:2 * HEADS + r + 1]
        toth = cumr[:, TILE - 1:TILE] if direction == 0 else cumr[:, 0:1]
        m0h = m_ref[direction, :, h:h + 1]
        logd = jnp.where(valid, cumr + ucol, NEG_BIG)
        inter = cumr + m0h
        m_t = jnp.maximum(jnp.max(logd, axis=0, keepdims=True), inter)
        qm = jnp.where(lane // HEAD_DIM == h, q, jnp.zeros_like(q))
        heads.append(dict(qk=_dot_nt(k, qm), logd=logd, m_t=m_t, g0=jnp.exp2(inter - m_t),
                          a_row=toth - cumr + ig_t[r:r + 1, :], carry=toth + m0h))
    return dict(direction=direction, rows=rows, k=k, ct=ct, n0=n0, inter_all=inter_all, heads=heads)


def _mlstm_outputs(cx, vt_ref, ht_ref, ct_ref, n_ref, m_ref, consts):
    _, _, head_mask, lane, row16, _ = consts
    direction, rows, k, inter_all = cx["direction"], cx["rows"], cx["k"], cx["inter_all"]
    vt = vt_ref[:, rows]
    ones_rows = jnp.ones((ONES_ROWS, TILE), BF16)
    w_rows, sp_row = [], jnp.zeros((1, BRANCH), F32)
    for h, hd in enumerate(cx["heads"]):
        m_t, g0 = hd["m_t"], hd["g0"]
        s_t = hd["qk"] * jnp.exp2(hd["logd"] - m_t)
        vt1 = jnp.concatenate([vt[h * HEAD_DIM:(h + 1) * HEAD_DIM, :], ones_rows], axis=0)
        pv = _dot(vt1, s_t.astype(BF16))
        num = pv[0:HEAD_DIM, :] + g0 * inter_all[h * HEAD_DIM:(h + 1) * HEAD_DIM, :]
        den = pv[HEAD_DIM:HEAD_DIM + 1, :] + g0 * (inter_all[BRANCH + h:BRANCH + h + 1, :]
                                                   + inter_all[BRANCH + HEADS + h:BRANCH + HEADS + h + 1, :])
        ht_ref[direction, h * HEAD_DIM:(h + 1) * HEAD_DIM, rows] = (
            num / jnp.maximum(jnp.abs(den), jnp.exp2(-m_t)))
        a_row = hd["a_row"]
        m_loc = jnp.max(a_row, axis=1, keepdims=True)
        m_new = jnp.maximum(hd["carry"], m_loc)
        sp = jnp.exp2(hd["carry"] - m_new)
        w_rows.append(jnp.exp2(a_row - m_loc) * jnp.exp2(m_loc - m_new))
        sp_row = sp_row + jnp.where(lane // HEAD_DIM == h, sp, 0.0)
        m_ref[direction, :, h:h + 1] = m_new

    w_block = jnp.concatenate([jnp.broadcast_to(w, (HEAD_DIM, TILE)) for w in w_rows], axis=0)
    vtw = (vt.astype(F32) * w_block).astype(BF16)
    w16 = jnp.zeros((ONES_ROWS, TILE), F32)
    for h in range(HEADS):
        w_hi = w_rows[h].astype(BF16).astype(F32)
        w16 = w16 + jnp.where(row16 == h, w_hi, 0.0) + jnp.where(row16 == HEADS + h, w_rows[h] - w_hi, 0.0)
    dall = _dot(jnp.concatenate([vtw, w16.astype(BF16)], axis=0), k)
    ct_ref[direction] = cx["ct"] * sp_row + jnp.where(head_mask, dall[0:BRANCH, :], 0.0)
    dn = jnp.zeros((1, BRANCH), F32)
    for h in range(HEADS):
        dn = dn + jnp.where(lane // HEAD_DIM == h,
                            dall[BRANCH + h:BRANCH + h + 1, :] + dall[BRANCH + HEADS + h:BRANCH + HEADS + h + 1, :], 0.0)
    n_ref[direction] = cx["n0"] * sp_row + dn


def _mlstm_kernel(p_ref, vt_ref, g_ref, gt_ref, gain_ref, o_ref, ht_ref, ct_ref, n_ref, m_ref):
    rr = _iota((TILE, TILE), 0)
    cc = _iota((TILE, TILE), 1)
    head_mask = (rr // HEAD_DIM) == (cc // HEAD_DIM)
    lane = _iota((1, BRANCH), 1)
    row16 = _iota((ONES_ROWS, BRANCH), 0)
    lane16 = _iota((ONES_ROWS, BRANCH), 1) // HEAD_DIM
    consts = []
    for direction in (0, 1):
        tri = ((cc <= rr) if direction == 0 else (cc >= rr)).astype(BF16)
        valid = (rr <= cc) if direction == 0 else (rr >= cc)
        consts.append((tri, valid, head_mask, lane, row16, lane16))
    ct_ref[...] = jnp.zeros_like(ct_ref)
    n_ref[...] = jnp.zeros_like(n_ref)
    m_ref[...] = jnp.zeros_like(m_ref)

    def body(i, carry):
        chunk = (i, jnp.where(i == 0, 0, N_TILES - i))
        cxs = [_mlstm_logits(d, chunk[d], p_ref, vt_ref, g_ref, gt_ref, ct_ref, n_ref, m_ref, consts[d])
               for d in (0, 1)]
        for d in (0, 1):
            _mlstm_outputs(cxs[d], vt_ref, ht_ref, ct_ref, n_ref, m_ref, consts[d])
        return carry

    lax.fori_loop(0, N_TILES, body, 0)

    def out_body(t, carry):
        rows = pl.ds(pl.multiple_of(t * TILE, TILE), TILE)
        o_ref[rows, :] = _head_rmsnorm((ht_ref[0, :, rows] + ht_ref[1, :, rows]).T, gain_ref[...])
        return carry

    lax.fori_loop(0, N_TILES, out_body, 0)


def _mlstm_call(pd, vtd, pg, pgt, g_row):
    bsz = pd.shape[0]
    return pl.pallas_call(
        _mlstm_kernel,
        grid=(bsz,),
        in_specs=[
            pl.BlockSpec((None, T_ALL, 2 * BRANCH), lambda b: (b, 0, 0)),
            pl.BlockSpec((None, BRANCH, T_ALL), lambda b: (b, 0, 0)),
            pl.BlockSpec((None, T_ALL, SEG_G), lambda b: (b, 0, 0)),
            pl.BlockSpec((None, SEG_G, T_ALL), lambda b: (b, 0, 0)),
            pl.BlockSpec((1, BRANCH), lambda b: (0, 0)),
        ],
        out_specs=pl.BlockSpec((None, T_ALL, BRANCH), lambda b: (b, 0, 0)),
        out_shape=jax.ShapeDtypeStruct((bsz, T_ALL, BRANCH), F32),
        scratch_shapes=[pltpu.VMEM((2, BRANCH, T_ALL), F32), pltpu.VMEM((2, BRANCH, BRANCH), F32),
                        pltpu.VMEM((2, 1, BRANCH), F32), pltpu.VMEM((2, 1, LANES), F32)],
        compiler_params=_params("arbitrary"),
        name="mlstm",
    )(pd, vtd, pg, pgt, g_row)


def _outproj_kernel(x_ref, ya_ref, yb_ref, yc_ref, yd_ref, po_ref, mod_ref, w_ref, fg_ref, o_ref,
                    *, bsz, tile0, final_norm):
    t = pl.program_id(1) + tile0
    row = jnp.where(t == 0, bsz, pl.program_id(0))
    gate_mod = mod_ref[pl.ds(row, 1), 2 * D_MODEL:3 * D_MODEL]
    og = po_ref[:, 0:BRANCH]
    gate = po_ref[:, BRANCH:]
    mixed = jnp.concatenate([ya_ref[...], yb_ref[...], yc_ref[...], yd_ref[...] * _sigmoid(og)], axis=-1)
    mixed = (mixed * (gate * _sigmoid(gate))).astype(BF16)
    xn = x_ref[...] + gate_mod * _dot(mixed, w_ref[...])
    if final_norm:
        ms = jnp.mean(xn * xn, axis=-1, keepdims=True)
        xn = xn * lax.rsqrt(ms + NORM_EPS) * fg_ref[...]
    o_ref[...] = xn


def _outproj_call(xc, ya, yb, yc, yd, po, mod_l, w_out_bf, final_g, last):
    bsz = xc.shape[0]
    tile0 = 1 if last else 0
    n_out = N_TILES - tile0
    rows = mod_l.shape[0]

    def tok(width, arr):
        off = tile0 if arr.shape[1] == T_ALL else 0
        return pl.BlockSpec((None, TILE, width), lambda b, t: (b, t + off, 0))

    return pl.pallas_call(
        functools.partial(_outproj_kernel, bsz=bsz, tile0=tile0, final_norm=last),
        grid=(bsz, n_out),
        in_specs=[tok(D_MODEL, xc), tok(BRANCH, ya), tok(BRANCH, yb), tok(BRANCH, yc), tok(BRANCH, yd),
                  tok(SEG_O, po),
                  pl.BlockSpec((rows, 3 * D_MODEL), lambda b, t: (0, 0)),
                  pl.BlockSpec((D_MODEL, D_MODEL), lambda b, t: (0, 0)),
                  pl.BlockSpec((1, D_MODEL), lambda b, t: (0, 0))],
        out_specs=pl.BlockSpec((None, TILE, D_MODEL), lambda b, t: (b, t, 0)),
        out_shape=jax.ShapeDtypeStruct((bsz, n_out * TILE, D_MODEL), F32),
        compiler_params=_params("arbitrary", "arbitrary"),
        name="outproj",
    )(xc, ya, yb, yc, yd, po, mod_l, w_out_bf, final_g.reshape(1, D_MODEL))


def _rope_tables(dim):
    quarter = dim // 4
    half = dim // 2
    pos = np.arange(SEQ)
    row = (pos // GRID_W).astype(np.float32)
    col = (pos % GRID_W).astype(np.float32)
    inv = (np.float32(ROPE_BASE) ** (-np.arange(0, half, 2, dtype=np.float32) / np.float32(half))).astype(np.float32)
    ang_r = row[:, None] * inv[None, :]
    ang_c = col[:, None] * inv[None, :]
    lane = np.arange(LANES) % dim
    part = lane // quarter
    freq = lane % quarter
    ang = np.where(part[None, :] < 2, ang_r[:, freq], ang_c[:, freq]).astype(np.float32)
    cos = np.cos(ang)
    sin = np.sin(ang)
    first = (part % 2 == 0)[None, :]
    s_next = np.where(first, -sin, 0.0)
    s_prev = np.where(first, 0.0, sin)
    tab = np.stack([cos, s_next, s_prev]).astype(np.float32)
    ident = np.stack([np.ones((CTX_LEN, LANES)), np.zeros((CTX_LEN, LANES)),
                      np.zeros((CTX_LEN, LANES))]).astype(np.float32)
    return jnp.asarray(np.concatenate([ident, tab], axis=1))


def _relayout_in_proj(w, bias):
    def cols(a):
        qa, ka, va = a[..., 0:256], a[..., 256:512], a[..., 512:768]
        seg_b = a[..., 768:1792]
        qc = a[..., 1792:2048]
        kc = a[..., 2048:2176]
        vc = a[..., 2176:2304]
        seg_d = a[..., 2304:3072]
        gates = a[..., 3072:3088]
        seg_o = a[..., 3088:4368]
        rep = lambda kv: jnp.concatenate([kv[..., 0:64], kv[..., 0:64], kv[..., 64:128], kv[..., 64:128]], axis=-1)
        pad = jnp.zeros(a.shape[:-1] + (SEG_G - 16,), a.dtype)
        return jnp.concatenate([qa, ka, va, qc, rep(kc), rep(vc), seg_b, seg_d, gates, pad, seg_o], axis=-1)
    return cols(w).astype(BF16), cols(bias)


def kernel(x, c, ctx, c_ctx, w_mod, b_mod, norm_g, w_in, b_in, diff_lam, diff_g, hg_lb, hg_g,
           sw_sink, ml_g, w_out, final_g):
    bsz = x.shape[0]
    tab_a = _rope_tables(DA_QK)
    tab_c = _rope_tables(HEAD_DIM)
    rows = ((bsz + 1 + 7) // 8) * 8
    cc = jnp.concatenate([c, c_ctx[None, :], jnp.zeros((rows - bsz - 1, D_MODEL), F32)], axis=0)
    mod = _mod_call(cc, w_mod, b_mod)
    xc = jnp.concatenate([ctx, x], axis=1)
    tile4 = lambda g: jnp.tile(g, HEADS).reshape(1, BRANCH)
    out = None
    for l in range(DEPTH):
        last = l == DEPTH - 1
        w_r, b_r = _relayout_in_proj(w_in[l], b_in[l])
        pa, pc, pb, pd, pg, po, vta, vtc, vtd, pgt = _inproj_call(xc, mod[l], norm_g[l], w_r, b_r, tab_a, tab_c)
        ya = _diffattn_call(pa, vta, diff_lam[l], tile4(diff_g[l]), l, not last)
        yb = _hgrn_call(pb, hg_lb, tile4(hg_g[l]), l)
        yc = _window_call(pc, vtc, sw_sink[l].reshape(1, HEADS), not last)
        yd = _mlstm_call(pd, vtd, pg, pgt, tile4(ml_g[l]))
        res = _outproj_call(xc, ya, yb, yc, yd, po, mod[l], w_out[l].astype(BF16), final_g, last)
        if last:
            out = res
        else:
            xc = res
    return out
```

```python
import functools
import math

import numpy as np
import jax
import jax.numpy as jnp
from jax import lax
from jax.experimental import pallas as pl
from jax.experimental.pallas import tpu as pltpu

F32 = jnp.float32
BF16 = jnp.bfloat16

D_MODEL = 1024
SEQ = 2048
CTX_LEN = 256
T_ALL = CTX_LEN + SEQ
GRID_W = 64
DEPTH = 2
HEADS = 4
HEAD_DIM = 64
BRANCH = HEADS * HEAD_DIM
DA_QK = 32
SW_WINDOW = 128
HG_CHUNK = 16
ROPE_BASE = 10000.0
NORM_EPS = 1e-6
NEG_BIG = -1e30
LOG2E = math.log2(math.e)

TILE = 256
N_TILES = T_ALL // TILE
LANES = 128
ONES_ROWS = 16

SEG_A = 3 * BRANCH
SEG_C = 3 * BRANCH
SEG_B = 4 * BRANCH
SEG_D = 3 * BRANCH
SEG_G = LANES
SEG_O = BRANCH + D_MODEL
OFF_A = 0
OFF_C = OFF_A + SEG_A
OFF_B = OFF_C + SEG_C
OFF_D = OFF_B + SEG_B
OFF_G = OFF_D + SEG_D
OFF_O = OFF_G + SEG_G
PROJ_PAD = OFF_O + SEG_O

VMEM_LIMIT = 56 * 1024 * 1024


def _params(*sem):
    return pltpu.CompilerParams(dimension_semantics=sem, vmem_limit_bytes=VMEM_LIMIT)


def _dot(a, b):
    return jnp.dot(a, b, preferred_element_type=F32)


def _dot_nt(a, b):
    return lax.dot_general(a, b, (((1,), (1,)), ((), ())), preferred_element_type=F32)


def _dot_tn(a, b):
    return lax.dot_general(a, b, (((0,), (0,)), ((), ())), preferred_element_type=F32)


def _split3(x):
    x1 = x.astype(BF16)
    r1 = x - x1.astype(F32)
    x2 = r1.astype(BF16)
    x3 = (r1 - x2.astype(F32)).astype(BF16)
    return x1, x2, x3


def _dot_exact_l(m01, x):
    x1, x2, x3 = _split3(x)
    return _dot(m01, x1) + _dot(m01, x2) + _dot(m01, x3)


def _dot_exact_r(x, m01):
    x1, x2, x3 = _split3(x)
    return _dot(x1, m01) + _dot(x2, m01) + _dot(x3, m01)


def _dot_exact_nt(x, m01):
    x1, x2, x3 = _split3(x)
    return _dot_nt(x1, m01) + _dot_nt(x2, m01) + _dot_nt(x3, m01)


def _sigmoid(z):
    e = jnp.exp(-jnp.abs(z))
    r = 1.0 / (1.0 + e)
    return jnp.where(z >= 0, r, e * r)


def _log_sigmoid(z):
    return jnp.minimum(z, 0.0) - jnp.log(1.0 + jnp.exp(-jnp.abs(z)))


def _iota(shape, dim):
    return lax.broadcasted_iota(jnp.int32, shape, dim)


def _head_sum_matrix():
    r = _iota((BRANCH, BRANCH), 0) // HEAD_DIM
    c = _iota((BRANCH, BRANCH), 1) // HEAD_DIM
    return (r == c).astype(BF16)


def _head_rmsnorm(o, g_row):
    ss = _dot_exact_r(o * o, _head_sum_matrix())
    return o * lax.rsqrt(ss * (1.0 / HEAD_DIM) + NORM_EPS) * g_row


def _mod_kernel(cc_ref, w_ref, b_ref, o_ref):
    cc = cc_ref[...]
    a = (cc * _sigmoid(cc)).astype(BF16)
    o_ref[...] = _dot(a, w_ref[...].astype(BF16)) + b_ref[...]


def _mod_call(cc, w_mod, b_mod):
    rows = cc.shape[0]
    nblk = 3
    return pl.pallas_call(
        _mod_kernel,
        grid=(DEPTH, nblk),
        in_specs=[
            pl.BlockSpec((rows, D_MODEL), lambda l, j: (0, 0)),
            pl.BlockSpec((None, D_MODEL, D_MODEL), lambda l, j: (l, 0, j)),
            pl.BlockSpec((None, 1, D_MODEL), lambda l, j: (l, 0, j)),
        ],
        out_specs=pl.BlockSpec((None, rows, D_MODEL), lambda l, j: (l, 0, j)),
        out_shape=jax.ShapeDtypeStruct((DEPTH, rows, 3 * D_MODEL), F32),
        compiler_params=_params("arbitrary", "arbitrary"),
        name="mod",
    )(cc, w_mod, b_mod.reshape(DEPTH, 1, 3 * D_MODEL))


def _rope(slab, cos, sin_next, sin_prev, off):
    nxt = pltpu.roll(slab, LANES - off, 1)
    prv = pltpu.roll(slab, off, 1)
    return slab * cos + nxt * sin_next + prv * sin_prev


def _inproj_kernel(x_ref, ctx_ref, mod_ref, ng_ref, w_ref, b_ref, ta_ref, tc_ref,
                   pa_ref, pc_ref, pb_ref, pd_ref, pg_ref, po_ref, vta_ref, vtc_ref, vtd_ref, pgt_ref,
                   *, bsz):
    is_ctx = pl.program_id(1) == 0
    row = jnp.where(is_ctx, bsz, pl.program_id(0))
    x = jnp.where(is_ctx, ctx_ref[...], x_ref[...])
    mrow = mod_ref[pl.ds(row, 1), :]
    shift = mrow[:, 0:D_MODEL]
    scale = mrow[:, D_MODEL:2 * D_MODEL]
    ms = jnp.mean(x * x, axis=-1, keepdims=True)
    h = x * lax.rsqrt(ms + NORM_EPS) * ng_ref[...]
    h = (h * (1.0 + scale) + shift).astype(BF16)

    def proj(off, width):
        return _dot(h, w_ref[:, off:off + width]) + b_ref[:, off:off + width]

    def rope_seg(acc, tab_ref, off, q_scale):
        cos, s_next, s_prev = tab_ref[0], tab_ref[1], tab_ref[2]
        outs = []
        for j in range(4):
            r = _rope(acc[:, j * LANES:(j + 1) * LANES], cos, s_next, s_prev, off)
            outs.append(r * q_scale if j < 2 else r)
        outs.append(acc[:, 4 * LANES:])
        return jnp.concatenate(outs, axis=-1)

    acco = proj(OFF_O, SEG_O)
    po_ref[...] = jnp.concatenate(
        [_sigmoid(acco[:, 0:BRANCH]), acco[:, BRANCH:] * _sigmoid(acco[:, BRANCH:])], axis=-1).astype(BF16)
    acca = rope_seg(proj(OFF_A, SEG_A), ta_ref, DA_QK // 4, DA_QK ** -0.5 * LOG2E)
    pa_ref[...] = acca[:, 0:2 * BRANCH].astype(BF16)
    vta_ref[...] = acca[:, 2 * BRANCH:].T.astype(BF16)
    accc = rope_seg(proj(OFF_C, SEG_C), tc_ref, HEAD_DIM // 4, HEAD_DIM ** -0.5 * LOG2E)
    pc_ref[...] = accc[:, 0:2 * BRANCH].astype(BF16)
    vtc_ref[...] = accc[:, 2 * BRANCH:].T.astype(BF16)
    accd = proj(OFF_D, SEG_D)
    pd_ref[...] = jnp.concatenate(
        [accd[:, 0:BRANCH], accd[:, BRANCH:2 * BRANCH] * (HEAD_DIM ** -0.5)], axis=-1).astype(BF16)
    vtd_ref[...] = accd[:, 2 * BRANCH:].T.astype(BF16)
    gates = proj(OFF_G, SEG_G)
    pg_ref[...] = gates
    pgt_ref[...] = gates.T
    pb_ref[...] = proj(OFF_B, SEG_B)


def _inproj_call(x, ctx, mod_l, norm_g, w_r, b_r, tab_a, tab_c):
    bsz = x.shape[0]
    rows = mod_l.shape[0]
    widths = [(2 * BRANCH, BF16), (2 * BRANCH, BF16), (SEG_B, F32), (2 * BRANCH, BF16), (SEG_G, F32),
              (SEG_O, BF16)]
    out_specs = [pl.BlockSpec((None, TILE, w), lambda b, t: (b, t, 0)) for w, _ in widths]
    out_shape = [jax.ShapeDtypeStruct((bsz, T_ALL, w), dt) for w, dt in widths]
    for rows_t, dt in ((BRANCH, BF16), (BRANCH, BF16), (BRANCH, BF16), (SEG_G, F32)):
        out_specs.append(pl.BlockSpec((None, rows_t, TILE), lambda b, t: (b, 0, t)))
        out_shape.append(jax.ShapeDtypeStruct((bsz, rows_t, T_ALL), dt))
    return pl.pallas_call(
        functools.partial(_inproj_kernel, bsz=bsz),
        grid=(bsz, N_TILES),
        in_specs=[
            pl.BlockSpec((None, TILE, D_MODEL), lambda b, t: (b, jnp.maximum(t - 1, 0), 0)),
            pl.BlockSpec((None, CTX_LEN, D_MODEL), lambda b, t: (b, 0, 0)),
            pl.BlockSpec((rows, 3 * D_MODEL), lambda b, t: (0, 0)),
            pl.BlockSpec((1, D_MODEL), lambda b, t: (0, 0)),
            pl.BlockSpec((D_MODEL, PROJ_PAD), lambda b, t: (0, 0)),
            pl.BlockSpec((1, PROJ_PAD), lambda b, t: (0, 0)),
            pl.BlockSpec((3, TILE, LANES), lambda b, t: (0, t, 0)),
            pl.BlockSpec((3, TILE, LANES), lambda b, t: (0, t, 0)),
        ],
        out_specs=out_specs,
        out_shape=out_shape,
        compiler_params=_params("arbitrary", "arbitrary"),
        name="inproj",
    )(x, ctx, mod_l, norm_g.reshape(1, D_MODEL), w_r, b_r.reshape(1, PROJ_PAD), tab_a, tab_c)


def _diffattn_kernel(q_ref, k_ref, vt_ref, lam_ref, g_ref, o_ref, acc_ref, s_ref, *, lam_init, q_tile0):
    qb = pl.program_id(1) + q_tile0
    lp = lam_ref[...]
    lam = (jnp.exp(jnp.sum(lp[0:1] * lp[1:2], axis=-1, keepdims=True))
           - jnp.exp(jnp.sum(lp[2:3] * lp[3:4], axis=-1, keepdims=True)) + lam_init)
    q = q_ref[...]
    lane = _iota((1, BRANCH), 1)
    n_pairs = 2 * HEADS
    sub = TILE // 8

    def attend(nk):
        n_kt = nk // TILE
        qms = [jnp.where(lane // DA_QK == hm, q, jnp.zeros_like(q)) for hm in range(n_pairs)]

        def logits(hm):
            m8 = None
            half = max(nk // 2, TILE)
            for r0 in range(0, nk, half):
                st = _dot_nt(k_ref[r0:r0 + half, :], qms[hm])
                s_ref[hm % 2, r0:r0 + half, :] = st
                mh = jnp.max(st.reshape(half // 8, 8, TILE), axis=0)
                m8 = mh if m8 is None else jnp.maximum(m8, mh)
            return m8

        ones_rows = jnp.ones((ONES_ROWS, TILE), BF16)

        def value_tile(hm, j, mb, ot):
            h = hm // 2
            st = s_ref[hm % 2, j * TILE:(j + 1) * TILE, :]
            e = jnp.exp2(st.reshape(sub, 8, TILE) - mb[None])
            vt = vt_ref[h * HEAD_DIM:(h + 1) * HEAD_DIM, j * TILE:(j + 1) * TILE]
            vt1 = jnp.concatenate([vt, ones_rows], axis=0)
            return ot + _dot(vt1, e.reshape(TILE, TILE).astype(BF16))

        m8 = logits(0)
        for hm in range(n_pairs):
            mb = jnp.broadcast_to(jnp.max(m8, axis=0, keepdims=True), (8, TILE))
            if hm + 1 < n_pairs:
                m8 = logits(hm + 1)
            ot = jnp.zeros((HEAD_DIM + ONES_ROWS, TILE), F32)
            for j in range(n_kt):
                ot = value_tile(hm, j, mb, ot)
            l = ot[HEAD_DIM:HEAD_DIM + 1, :]
            ot = ot[0:HEAD_DIM, :]
            rows = slice((hm // 2) * HEAD_DIM, (hm // 2 + 1) * HEAD_DIM)
            if hm % 2 == 0:
                acc_ref[rows, :] = ot * (1.0 / l)
            else:
                acc_ref[rows, :] -= ot * (lam / l)
        o_ref[...] = (_head_rmsnorm(acc_ref[...].T, g_ref[...]) * (1.0 - lam_init)).astype(BF16)

    @pl.when(qb == 0)
    def _():
        attend(CTX_LEN)

    @pl.when(qb > 0)
    def _():
        attend(T_ALL)


def _diffattn_call(pa, vta, lam_p, g_row, layer_idx, need_ctx):
    bsz = pa.shape[0]
    q_tile0 = 0 if need_ctx else 1
    lam_init = 0.8 - 0.6 * math.exp(-0.3 * layer_idx)
    return pl.pallas_call(
        functools.partial(_diffattn_kernel, lam_init=lam_init, q_tile0=q_tile0),
        grid=(bsz, N_TILES - q_tile0),
        in_specs=[
            pl.BlockSpec((None, TILE, BRANCH), lambda b, t: (b, t + q_tile0, 0)),
            pl.BlockSpec((None, T_ALL, BRANCH), lambda b, t: (b, 0, 1)),
            pl.BlockSpec((None, BRANCH, T_ALL), lambda b, t: (b, 0, 0)),
            pl.BlockSpec((4, DA_QK), lambda b, t: (0, 0)),
            pl.BlockSpec((1, BRANCH), lambda b, t: (0, 0)),
        ],
        out_specs=pl.BlockSpec((None, TILE, BRANCH), lambda b, t: (b, t, 0)),
        out_shape=jax.ShapeDtypeStruct((bsz, (N_TILES - q_tile0) * TILE, BRANCH), BF16),
        scratch_shapes=[pltpu.VMEM((BRANCH, TILE), F32), pltpu.VMEM((2, T_ALL, TILE), F32)],
        compiler_params=_params("arbitrary", "arbitrary"),
        name="diffattn",
    )(pa, pa, vta, lam_p, g_row)


BAND = 2 * TILE


def _window_kernel(q_ref, k_ref, vt_ref, sink_ref, o_ref, acc_ref, s_ref, *, q_tile0):
    qb = pl.program_id(1) + q_tile0
    q = q_ref[...]
    lane = _iota((1, BRANCH), 1)
    ones_rows = jnp.ones((ONES_ROWS, TILE), BF16)

    def attend(band):
        if band:
            a = (qb - 1) * TILE
            start = jnp.clip(a - SW_WINDOW, 0, SEQ - BAND)
            row0 = pl.multiple_of(CTX_LEN + start, SW_WINDOW)
            kb = k_ref[pl.ds(row0, BAND), :]
            kpos = start + _iota((BAND, 1), 0)
            qpos = a + _iota((1, TILE), 1)
            valid = jnp.abs(qpos - kpos) <= SW_WINDOW
        sinks, maxes = [], []
        for h in range(HEADS):
            qm = jnp.where(lane // HEAD_DIM == h, q, jnp.zeros_like(q))
            sink = sink_ref[:, h:h + 1] * LOG2E
            sc = _dot_nt(k_ref[0:CTX_LEN, :], qm)
            s_ref[h, 0:CTX_LEN, :] = sc
            m = jnp.maximum(jnp.max(sc, axis=0, keepdims=True), sink)
            if band:
                sb = jnp.where(valid, _dot_nt(kb, qm), NEG_BIG)
                s_ref[h, CTX_LEN:CTX_LEN + BAND, :] = sb
                m = jnp.maximum(m, jnp.max(sb, axis=0, keepdims=True))
            sinks.append(sink)
            maxes.append(m)
        for h in range(HEADS):
            m = maxes[h]
            vt1 = jnp.concatenate([vt_ref[h * HEAD_DIM:(h + 1) * HEAD_DIM, 0:CTX_LEN], ones_rows], axis=0)
            pv = _dot(vt1, jnp.exp2(s_ref[h, 0:CTX_LEN, :] - m).astype(BF16))
            if band:
                vtb = vt_ref[h * HEAD_DIM:(h + 1) * HEAD_DIM, pl.ds(row0, BAND)]
                ones_b = jnp.ones((ONES_ROWS, BAND), BF16)
                eb = jnp.exp2(s_ref[h, CTX_LEN:CTX_LEN + BAND, :] - m).astype(BF16)
                pv = pv + _dot(jnp.concatenate([vtb, ones_b], axis=0), eb)
            l = pv[HEAD_DIM:HEAD_DIM + 1, :] + jnp.exp2(sinks[h] - m)
            acc_ref[h * HEAD_DIM:(h + 1) * HEAD_DIM, :] = pv[0:HEAD_DIM, :] * (1.0 / l)
        o_ref[...] = acc_ref[...].T.astype(BF16)

    @pl.when(qb == 0)
    def _():
        attend(False)

    @pl.when(qb > 0)
    def _():
        attend(True)


def _window_call(pc, vtc, sink_row, need_ctx):
    bsz = pc.shape[0]
    q_tile0 = 0 if need_ctx else 1
    n_q = N_TILES - q_tile0
    return pl.pallas_call(
        functools.partial(_window_kernel, q_tile0=q_tile0),
        grid=(bsz, n_q),
        in_specs=[
            pl.BlockSpec((None, TILE, BRANCH), lambda b, t: (b, t + q_tile0, 0)),
            pl.BlockSpec((None, T_ALL, BRANCH), lambda b, t: (b, 0, 1)),
            pl.BlockSpec((None, BRANCH, T_ALL), lambda b, t: (b, 0, 0)),
            pl.BlockSpec((1, HEADS), lambda b, t: (0, 0)),
        ],
        out_specs=pl.BlockSpec((None, TILE, BRANCH), lambda b, t: (b, t, 0)),
        out_shape=jax.ShapeDtypeStruct((bsz, n_q * TILE, BRANCH), BF16),
        scratch_shapes=[pltpu.VMEM((BRANCH, TILE), F32), pltpu.VMEM((HEADS, CTX_LEN + BAND, TILE), F32)],
        compiler_params=_params("arbitrary", "arbitrary"),
        name="window",
    )(pc, pc, vtc, sink_row)


N_CHUNK = T_ALL // HG_CHUNK
CTX_CHUNKS = CTX_LEN // HG_CHUNK
CH_PER_TILE = TILE // HG_CHUNK
HG_UNROLL = 4


def _hgrn_gates(z, lb_terms):
    log2_ksig = _log_sigmoid(-z) * LOG2E
    if lb_terms is None:
        return _log_sigmoid(z) * LOG2E, _sigmoid(-z), log2_ksig
    lb, log_lb, log_1m = lb_terms
    bt = log_1m + _log_sigmoid(z)
    mx = jnp.maximum(log_lb, bt)
    log_f = mx + jnp.log(jnp.exp(log_lb - mx) + jnp.exp(bt - mx))
    return log_f * LOG2E, (1.0 - lb) * _sigmoid(-z), log_1m * LOG2E + log2_ksig


def _hgrn_intra(direction, q3, c3, u3, v3, head_ones):
    half = HG_CHUNK // 2
    slabs, meta = [], []
    for s in range(HG_CHUNK):
        us = jnp.broadcast_to(u3[:, s:s + 1, :], (CH_PER_TILE, half, BRANCH))
        for g in range(2):
            lo_row, hi_row = half * g, half * g + half - 1
            if direction == 0:
                none_valid, all_valid = hi_row < s, lo_row >= s
            else:
                none_valid, all_valid = lo_row > s, hi_row <= s
            if none_valid:
                continue
            d = c3[:, half * g:half * (g + 1), :] - us
            if not all_valid:
                row = _iota((1, half, 1), 1) + half * g
                d = jnp.where((row >= s) if direction == 0 else (row <= s), d, NEG_BIG)
            x = q3[:, half * g:half * (g + 1), :] * jnp.exp2(d)
            slabs.append(x.reshape(CH_PER_TILE * half, BRANCH).astype(BF16))
            meta.append((s, g))
    a_all = _dot(jnp.concatenate(slabs, axis=0), head_ones)
    o = [jnp.zeros((CH_PER_TILE, half, BRANCH), F32) for _ in range(2)]
    n = CH_PER_TILE * half
    vs = None
    for i, (s, g) in enumerate(meta):
        if i == 0 or meta[i - 1][0] != s:
            vs = jnp.broadcast_to(v3[:, s:s + 1, :], (CH_PER_TILE, half, BRANCH))
        o[g] = o[g] + a_all[i * n:(i + 1) * n].reshape(CH_PER_TILE, half, BRANCH) * vs
    return jnp.concatenate(o, axis=1).reshape(TILE, BRANCH)


def _hgrn_kernel(p_ref, lb_ref, g_ref, o_ref, qt_ref, ke_ref, vb_ref, dec_ref, of_ref, oi_ref, st_ref,
                 *, layer_idx):
    lane_r = _iota((TILE, TILE), 0)
    lane_c = _iota((TILE, TILE), 1)
    same_chunk = (lane_r // HG_CHUNK) == (lane_c // HG_CHUNK)
    chunk_ones = same_chunk.astype(BF16)
    tris = [(same_chunk & (lane_c <= lane_r)).astype(BF16),
            (same_chunk & (lane_c >= lane_r)).astype(BF16)]
    head_ones = _head_sum_matrix()
    head_mask = (lane_r // HEAD_DIM) == (lane_c // HEAD_DIM)

    lb_terms = None
    if layer_idx > 0:
        lbp = lb_ref[...]
        lbp = lbp - jnp.max(lbp, axis=0, keepdims=True)
        sm = jnp.exp(lbp)
        sm = sm / jnp.sum(sm, axis=0, keepdims=True)
        lb = jnp.sum(sm[1:layer_idx + 1], axis=0, keepdims=True)
        lb_terms = (lb, jnp.log(lb), jnp.log(1.0 - lb))

    def tile_body(t, carry):
        rows = pl.ds(pl.multiple_of(t * TILE, TILE), TILE)
        q = p_ref[rows, 0:BRANCH] * (HEAD_DIM ** -0.5)
        v = p_ref[rows, 3 * BRANCH:4 * BRANCH]
        vb_ref[rows, :] = v.astype(BF16)
        shape3 = (CH_PER_TILE, HG_CHUNK, BRANCH)
        o_intra = None
        for direction in (0, 1):
            z = p_ref[rows, (1 + direction) * BRANCH:(2 + direction) * BRANCH]
            log2_f, kk, log2_k = _hgrn_gates(z, lb_terms)
            cum = _dot_exact_l(tris[direction], log2_f)
            tot = _dot_exact_l(chunk_ones, log2_f)
            qt_ref[direction, rows, :] = (q * jnp.exp2(cum)).astype(BF16)
            ke_ref[direction, rows, :] = (kk * jnp.exp2(tot - cum)).astype(BF16)
            dec_ref[direction, pl.ds(pl.multiple_of(t * CH_PER_TILE, CH_PER_TILE), CH_PER_TILE), :] = (
                jnp.exp2(tot.reshape(shape3)[:, 0, :]))
            o_d = _hgrn_intra(direction, q.reshape(shape3), cum.reshape(shape3),
                              (cum - log2_k).reshape(shape3), v.reshape(shape3), head_ones)
            o_intra = o_d if o_intra is None else o_intra + o_d
        of_ref[rows, :] = o_intra
        return carry

    lax.fori_loop(0, N_TILES, tile_body, 0)

    st_ref[...] = jnp.zeros_like(st_ref)

    def chunk_step(i, direction):
        if direction == 0:
            c = i
        else:
            c = jnp.where(i < CTX_CHUNKS, CTX_CHUNKS - 1 - i, N_CHUNK - 1 + CTX_CHUNKS - i)
        rows = pl.ds(pl.multiple_of(c * HG_CHUNK, HG_CHUNK), HG_CHUNK)
        st = st_ref[direction]
        oi_ref[direction, rows, :] = _dot_nt(qt_ref[direction, rows, :], st.astype(BF16))
        ds = _dot_tn(vb_ref[rows, :], ke_ref[direction, rows, :])
        st_ref[direction] = st * dec_ref[direction, pl.ds(c, 1), :] + jnp.where(head_mask, ds, 0.0)

    def chunk_body(it, carry):
        for u in range(HG_UNROLL):
            for direction in (0, 1):
                chunk_step(it * HG_UNROLL + u, direction)
        return carry

    lax.fori_loop(0, N_CHUNK // HG_UNROLL, chunk_body, 0)

    o_ref[...] = _head_rmsnorm(of_ref[...] + oi_ref[0] + oi_ref[1], g_ref[...]).astype(BF16)


def _hgrn_call(pb, hg_lb, g_row, layer_idx):
    bsz = pb.shape[0]
    return pl.pallas_call(
        functools.partial(_hgrn_kernel, layer_idx=layer_idx),
        grid=(bsz,),
        in_specs=[
            pl.BlockSpec((None, T_ALL, SEG_B), lambda b: (b, 0, 0)),
            pl.BlockSpec((DEPTH, BRANCH), lambda b: (0, 0)),
            pl.BlockSpec((1, BRANCH), lambda b: (0, 0)),
        ],
        out_specs=pl.BlockSpec((None, T_ALL, BRANCH), lambda b: (b, 0, 0)),
        out_shape=jax.ShapeDtypeStruct((bsz, T_ALL, BRANCH), BF16),
        scratch_shapes=[pltpu.VMEM((2, T_ALL, BRANCH), BF16), pltpu.VMEM((2, T_ALL, BRANCH), BF16),
                        pltpu.VMEM((T_ALL, BRANCH), BF16), pltpu.VMEM((2, N_CHUNK, BRANCH), F32),
                        pltpu.VMEM((T_ALL, BRANCH), F32), pltpu.VMEM((2, T_ALL, BRANCH), F32),
                        pltpu.VMEM((2, BRANCH, BRANCH), F32)],
        compiler_params=_params("arbitrary"),
        name="hgrn",
    )(pb, hg_lb, g_row)


def _mlstm_logits(direction, c, p_ref, vt_ref, g_ref, gt_ref, ct_ref, n_ref, m_ref, consts):
    tri, valid, _, lane, row16, lane16 = consts
    rows = pl.ds(pl.multiple_of(c * TILE, TILE), TILE)
    q = p_ref[rows, 0:BRANCH]
    k = p_ref[rows, BRANCH:2 * BRANCH]
    g = g_ref[rows, :]
    gt = gt_ref[:, rows]
    cum = _dot_exact_l(tri, _log_sigmoid(g) * LOG2E)
    cum_t = _dot_exact_nt(_log_sigmoid(gt[8:16, :]) * LOG2E, tri)
    ig_t = gt[0:8, :] * LOG2E
    ct = ct_ref[direction]
    n0 = n_ref[direction]
    n_hi = n0.astype(BF16).astype(F32)
    n_lo = n0 - n_hi
    n_rows = (jnp.where((row16 < HEADS) & (lane16 == row16), n_hi, 0.0)
              + jnp.where((row16 >= HEADS) & (lane16 == row16 - HEADS), n_lo, 0.0)).astype(BF16)
    inter_all = _dot_nt(jnp.concatenate([ct.astype(BF16), n_rows], axis=0), q)
    heads = []
    for h in range(HEADS):
        r = HEADS * direction + h
        cumr = cum_t[r:r + 1, :]
        ucol = g[:, r:r + 1] * LOG2E - cum[:, 2 * HEADS + r:2 * HEADS + r + 1]
        toth = cumr[:, TILE - 1:TILE] if direction == 0 else cumr[:, 0:1]
        m0h = m_ref[direction, :, h:h + 1]
        logd = jnp.where(valid, cumr + ucol, NEG_BIG)
        inter = cumr + m0h
        m_t = jnp.maximum(jnp.max(logd, axis=0, keepdims=True), inter)
        qm = jnp.where(lane // HEAD_DIM == h, q, jnp.zeros_like(q))
        heads.append(dict(qk=_dot_nt(k, qm), logd=logd, m_t=m_t, g0=jnp.exp2(inter - m_t),
                          a_row=toth - cumr + ig_t[r:r + 1, :], carry=toth + m0h))
    return dict(direction=direction, rows=rows, k=k, ct=ct, n0=n0, inter_all=inter_all, heads=heads)


def _mlstm_outputs(cx, vt_ref, ht_ref, ct_ref, n_ref, m_ref, consts):
    _, _, head_mask, lane, row16, _ = consts
    direction, rows, k, inter_all = cx["direction"], cx["rows"], cx["k"], cx["inter_all"]
    vt = vt_ref[:, rows]
    ones_rows = jnp.ones((ONES_ROWS, TILE), BF16)
    w_rows, sp_row = [], jnp.zeros((1, BRANCH), F32)
    for h, hd in enumerate(cx["heads"]):
        m_t, g0 = hd["m_t"], hd["g0"]
        s_t = hd["qk"] * jnp.exp2(hd["logd"] - m_t)
        vt1 = jnp.concatenate([vt[h * HEAD_DIM:(h + 1) * HEAD_DIM, :], ones_rows], axis=0)
        pv = _dot(vt1, s_t.astype(BF16))
        num = pv[0:HEAD_DIM, :] + g0 * inter_all[h * HEAD_DIM:(h + 1) * HEAD_DIM, :]
        den = pv[HEAD_DIM:HEAD_DIM + 1, :] + g0 * (inter_all[BRANCH + h:BRANCH + h + 1, :]
                                                   + inter_all[BRANCH + HEADS + h:BRANCH + HEADS + h + 1, :])
        ht_ref[direction, h * HEAD_DIM:(h + 1) * HEAD_DIM, rows] = (
            num / jnp.maximum(jnp.abs(den), jnp.exp2(-m_t)))
        a_row = hd["a_row"]
        m_loc = jnp.max(a_row, axis=1, keepdims=True)
        m_new = jnp.maximum(hd["carry"], m_loc)
        sp = jnp.exp2(hd["carry"] - m_new)
        w_rows.append(jnp.exp2(a_row - m_loc) * jnp.exp2(m_loc - m_new))
        sp_row = sp_row + jnp.where(lane // HEAD_DIM == h, sp, 0.0)
        m_ref[direction, :, h:h + 1] = m_new

    w_block = jnp.concatenate([jnp.broadcast_to(w, (HEAD_DIM, TILE)) for w in w_rows], axis=0)
    vtw = (vt.astype(F32) * w_block).astype(BF16)
    w16 = jnp.zeros((ONES_ROWS, TILE), F32)
    for h in range(HEADS):
        w_hi = w_rows[h].astype(BF16).astype(F32)
        w16 = w16 + jnp.where(row16 == h, w_hi, 0.0) + jnp.where(row16 == HEADS + h, w_rows[h] - w_hi, 0.0)
    dall = _dot(jnp.concatenate([vtw, w16.astype(BF16)], axis=0), k)
    ct_ref[direction] = cx["ct"] * sp_row + jnp.where(head_mask, dall[0:BRANCH, :], 0.0)
    dn = jnp.zeros((1, BRANCH), F32)
    for h in range(HEADS):
        dn = dn + jnp.where(lane // HEAD_DIM == h,
                            dall[BRANCH + h:BRANCH + h + 1, :] + dall[BRANCH + HEADS + h:BRANCH + HEADS + h + 1, :], 0.0)
    n_ref[direction] = cx["n0"] * sp_row + dn


def _mlstm_kernel(p_ref, vt_ref, g_ref, gt_ref, gain_ref, o_ref, ht_ref, ct_ref, n_ref, m_ref):
    rr = _iota((TILE, TILE), 0)
    cc = _iota((TILE, TILE), 1)
    head_mask = (rr // HEAD_DIM) == (cc // HEAD_DIM)
    lane = _iota((1, BRANCH), 1)
    row16 = _iota((ONES_ROWS, BRANCH), 0)
    lane16 = _iota((ONES_ROWS, BRANCH), 1) // HEAD_DIM
    consts = []
    for direction in (0, 1):
        tri = ((cc <= rr) if direction == 0 else (cc >= rr)).astype(BF16)
        valid = (rr <= cc) if direction == 0 else (rr >= cc)
        consts.append((tri, valid, head_mask, lane, row16, lane16))
    ct_ref[...] = jnp.zeros_like(ct_ref)
    n_ref[...] = jnp.zeros_like(n_ref)
    m_ref[...] = jnp.zeros_like(m_ref)

    def body(i, carry):
        chunk = (i, jnp.where(i == 0, 0, N_TILES - i))
        cxs = [_mlstm_logits(d, chunk[d], p_ref, vt_ref, g_ref, gt_ref, ct_ref, n_ref, m_ref, consts[d])
               for d in (0, 1)]
        for d in (0, 1):
            _mlstm_outputs(cxs[d], vt_ref, ht_ref, ct_ref, n_ref, m_ref, consts[d])
        return carry

    lax.fori_loop(0, N_TILES, body, 0)

    def out_body(t, carry):
        rows = pl.ds(pl.multiple_of(t * TILE, TILE), TILE)
        o_ref[rows, :] = _head_rmsnorm((ht_ref[0, :, rows] + ht_ref[1, :, rows]).T,
                                       gain_ref[...]).astype(BF16)
        return carry

    lax.fori_loop(0, N_TILES, out_body, 0)


def _mlstm_call(pd, vtd, pg, pgt, g_row):
    bsz = pd.shape[0]
    return pl.pallas_call(
        _mlstm_kernel,
        grid=(bsz,),
        in_specs=[
            pl.BlockSpec((None, T_ALL, 2 * BRANCH), lambda b: (b, 0, 0)),
            pl.BlockSpec((None, BRANCH, T_ALL), lambda b: (b, 0, 0)),
            pl.BlockSpec((None, T_ALL, SEG_G), lambda b: (b, 0, 0)),
            pl.BlockSpec((None, SEG_G, T_ALL), lambda b: (b, 0, 0)),
            pl.BlockSpec((1, BRANCH), lambda b: (0, 0)),
        ],
        out_specs=pl.BlockSpec((None, T_ALL, BRANCH), lambda b: (b, 0, 0)),
        out_shape=jax.ShapeDtypeStruct((bsz, T_ALL, BRANCH), BF16),
        scratch_shapes=[pltpu.VMEM((2, BRANCH, T_ALL), F32), pltpu.VMEM((2, BRANCH, BRANCH), F32),
                        pltpu.VMEM((2, 1, BRANCH), F32), pltpu.VMEM((2, 1, LANES), F32)],
        compiler_params=_params("arbitrary"),
        name="mlstm",
    )(pd, vtd, pg, pgt, g_row)


def _outproj_kernel(x_ref, ctx_ref, ya_ref, yb_ref, yc_ref, yd_ref, po_ref, mod_ref, w_ref, fg_ref,
                    *out_refs, bsz, last):
    t = pl.program_id(1)
    is_ctx = jnp.logical_and(t == 0, not last)
    row = jnp.where(is_ctx, bsz, pl.program_id(0))
    gate_mod = mod_ref[pl.ds(row, 1), 2 * D_MODEL:3 * D_MODEL]
    po = po_ref[...].astype(F32)
    yd = yd_ref[...].astype(F32) * po[:, 0:BRANCH]
    mixed = jnp.concatenate([ya_ref[...].astype(F32), yb_ref[...].astype(F32), yc_ref[...].astype(F32), yd],
                            axis=-1)
    mixed = (mixed * po[:, BRANCH:]).astype(BF16)
    delta = gate_mod * _dot(mixed, w_ref[...])
    if last:
        xn = x_ref[...] + delta
        ms = jnp.mean(xn * xn, axis=-1, keepdims=True)
        out_refs[0][...] = xn * lax.rsqrt(ms + NORM_EPS) * fg_ref[...]
    else:
        x_out_ref, ctx_out_ref = out_refs

        @pl.when(t == 0)
        def _():
            ctx_out_ref[...] = ctx_ref[...] + delta

        @pl.when(t > 0)
        def _():
            x_out_ref[...] = x_ref[...] + delta


def _outproj_call(x, ctx, ya, yb, yc, yd, po, mod_l, w_out_bf, final_g, last):
    bsz = x.shape[0]
    tile0 = 1 if last else 0
    rows = mod_l.shape[0]

    def tok(width, arr):
        off = tile0 if arr.shape[1] == T_ALL else 0
        return pl.BlockSpec((None, TILE, width), lambda b, t: (b, t + off, 0))

    lat_spec = pl.BlockSpec((None, TILE, D_MODEL), lambda b, t: (b, jnp.maximum(t + tile0 - 1, 0), 0))
    ctx_spec = pl.BlockSpec((None, CTX_LEN, D_MODEL), lambda b, t: (b, 0, 0))
    lat_shape = jax.ShapeDtypeStruct((bsz, SEQ, D_MODEL), F32)
    ctx_shape = jax.ShapeDtypeStruct((bsz, CTX_LEN, D_MODEL), F32)
    return pl.pallas_call(
        functools.partial(_outproj_kernel, bsz=bsz, last=last),
        grid=(bsz, N_TILES - tile0),
        in_specs=[lat_spec, ctx_spec, tok(BRANCH, ya), tok(BRANCH, yb), tok(BRANCH, yc), tok(BRANCH, yd),
                  tok(SEG_O, po),
                  pl.BlockSpec((rows, 3 * D_MODEL), lambda b, t: (0, 0)),
                  pl.BlockSpec((D_MODEL, D_MODEL), lambda b, t: (0, 0)),
                  pl.BlockSpec((1, D_MODEL), lambda b, t: (0, 0))],
        out_specs=lat_spec if last else [lat_spec, ctx_spec],
        out_shape=lat_shape if last else [lat_shape, ctx_shape],
        compiler_params=_params("arbitrary", "arbitrary"),
        name="outproj",
    )(x, ctx, ya, yb, yc, yd, po, mod_l, w_out_bf, final_g.reshape(1, D_MODEL))


def _rope_tables(dim):
    quarter = dim // 4
    half = dim // 2
    pos = np.arange(SEQ)
    row = (pos // GRID_W).astype(np.float32)
    col = (pos % GRID_W).astype(np.float32)
    inv = (np.float32(ROPE_BASE) ** (-np.arange(0, half, 2, dtype=np.float32) / np.float32(half))).astype(np.float32)
    ang_r = row[:, None] * inv[None, :]
    ang_c = col[:, None] * inv[None, :]
    lane = np.arange(LANES) % dim
    part = lane // quarter
    freq = lane % quarter
    ang = np.where(part[None, :] < 2, ang_r[:, freq], ang_c[:, freq]).astype(np.float32)
    cos = np.cos(ang)
    sin = np.sin(ang)
    first = (part % 2 == 0)[None, :]
    s_next = np.where(first, -sin, 0.0)
    s_prev = np.where(first, 0.0, sin)
    tab = np.stack([cos, s_next, s_prev]).astype(np.float32)
    ident = np.stack([np.ones((CTX_LEN, LANES)), np.zeros((CTX_LEN, LANES)),
                      np.zeros((CTX_LEN, LANES))]).astype(np.float32)
    return jnp.asarray(np.concatenate([ident, tab], axis=1))


def _relayout_in_proj(w, bias):
    def cols(a):
        qa, ka, va = a[..., 0:256], a[..., 256:512], a[..., 512:768]
        seg_b = a[..., 768:1792]
        qc = a[..., 1792:2048]
        kc = a[..., 2048:2176]
        vc = a[..., 2176:2304]
        seg_d = a[..., 2304:3072]
        gates = a[..., 3072:3088]
        seg_o = a[..., 3088:4368]
        rep = lambda kv: jnp.concatenate([kv[..., 0:64], kv[..., 0:64], kv[..., 64:128], kv[..., 64:128]], axis=-1)
        pad = jnp.zeros(a.shape[:-1] + (SEG_G - 16,), a.dtype)
        return jnp.concatenate([qa, ka, va, qc, rep(kc), rep(vc), seg_b, seg_d, gates, pad, seg_o], axis=-1)
    return cols(w), cols(bias)


def kernel(x, c, ctx, c_ctx, w_mod, b_mod, norm_g, w_in, b_in, diff_lam, diff_g, hg_lb, hg_g,
           sw_sink, ml_g, w_out, final_g):
    bsz = x.shape[0]
    tab_a = _rope_tables(DA_QK)
    tab_c = _rope_tables(HEAD_DIM)
    rows = ((bsz + 1 + 7) // 8) * 8
    cc = jnp.concatenate([c, c_ctx[None, :], jnp.zeros((rows - bsz - 1, D_MODEL), F32)], axis=0)
    mod = _mod_call(cc, w_mod, b_mod)
    tile4 = lambda g: jnp.tile(g, HEADS).reshape(1, BRANCH)
    w_in_bf = w_in.astype(BF16)
    w_out_bf = w_out.astype(BF16)
    for l in range(DEPTH):
        last = l == DEPTH - 1
        w_r, b_r = _relayout_in_proj(w_in_bf[l], b_in[l])
        pa, pc, pb, pd, pg, po, vta, vtc, vtd, pgt = _inproj_call(x, ctx, mod[l], norm_g[l], w_r, b_r,
                                                                 tab_a, tab_c)
        ya = _diffattn_call(pa, vta, diff_lam[l], tile4(diff_g[l]), l, not last)
        yb = _hgrn_call(pb, hg_lb, tile4(hg_g[l]), l)
        yc = _window_call(pc, vtc, sw_sink[l].reshape(1, HEADS), not last)
        yd = _mlstm_call(pd, vtd, pg, pgt, tile4(ml_g[l]))
        res = _outproj_call(x, ctx, ya, yb, yc, yd, po, mod[l], w_out_bf[l], final_g, last)
        if last:
            return res
        x, ctx = res
```

```python
import functools
import math

import numpy as np
import jax
import jax.numpy as jnp
from jax import lax
from jax.experimental import pallas as pl
from jax.experimental.pallas import tpu as pltpu

F32 = jnp.float32
BF16 = jnp.bfloat16

D_MODEL = 1024
SEQ = 2048
CTX_LEN = 256
T_ALL = CTX_LEN + SEQ
GRID_W = 64
DEPTH = 2
HEADS = 4
HEAD_DIM = 64
BRANCH = HEADS * HEAD_DIM
DA_QK = 32
SW_WINDOW = 128
HG_CHUNK = 16
ROPE_BASE = 10000.0
NORM_EPS = 1e-6
NEG_BIG = -1e30
LOG2E = math.log2(math.e)

TILE = 256
N_TILES = T_ALL // TILE
LANES = 128
ONES_ROWS = 16

SEG_A = 3 * BRANCH
SEG_C = 3 * BRANCH
SEG_B = 4 * BRANCH
SEG_D = 3 * BRANCH
SEG_G = LANES
SEG_O = BRANCH + D_MODEL
OFF_A = 0
OFF_C = OFF_A + SEG_A
OFF_B = OFF_C + SEG_C
OFF_D = OFF_B + SEG_B
OFF_G = OFF_D + SEG_D
OFF_O = OFF_G + SEG_G
PROJ_PAD = OFF_O + SEG_O

VMEM_LIMIT = 56 * 1024 * 1024


def _params(*sem):
    return pltpu.CompilerParams(dimension_semantics=sem, vmem_limit_bytes=VMEM_LIMIT)


def _dot(a, b):
    return jnp.dot(a, b, preferred_element_type=F32)


def _dot_nt(a, b):
    return lax.dot_general(a, b, (((1,), (1,)), ((), ())), preferred_element_type=F32)


def _dot_tn(a, b):
    return lax.dot_general(a, b, (((0,), (0,)), ((), ())), preferred_element_type=F32)


def _split3(x):
    x1 = x.astype(BF16)
    r1 = x - x1.astype(F32)
    x2 = r1.astype(BF16)
    x3 = (r1 - x2.astype(F32)).astype(BF16)
    return x1, x2, x3


def _dot_exact_l(m01, x):
    x1, x2, x3 = _split3(x)
    return _dot(m01, x1) + _dot(m01, x2) + _dot(m01, x3)


def _dot_exact_r(x, m01):
    x1, x2, x3 = _split3(x)
    return _dot(x1, m01) + _dot(x2, m01) + _dot(x3, m01)


def _dot_exact_nt(x, m01):
    x1, x2, x3 = _split3(x)
    return _dot_nt(x1, m01) + _dot_nt(x2, m01) + _dot_nt(x3, m01)


def _sigmoid(z):
    e = jnp.exp(-jnp.abs(z))
    r = 1.0 / (1.0 + e)
    return jnp.where(z >= 0, r, e * r)


def _log_sigmoid(z):
    return jnp.minimum(z, 0.0) - jnp.log(1.0 + jnp.exp(-jnp.abs(z)))


def _iota(shape, dim):
    return lax.broadcasted_iota(jnp.int32, shape, dim)


def _head_sum_matrix():
    r = _iota((BRANCH, BRANCH), 0) // HEAD_DIM
    c = _iota((BRANCH, BRANCH), 1) // HEAD_DIM
    return (r == c).astype(BF16)


def _head_rmsnorm(o, g_row):
    ss = _dot_exact_r(o * o, _head_sum_matrix())
    return o * lax.rsqrt(ss * (1.0 / HEAD_DIM) + NORM_EPS) * g_row


def _mod_kernel(cc_ref, w_ref, b_ref, o_ref):
    cc = cc_ref[...]
    a = (cc * _sigmoid(cc)).astype(BF16)
    o_ref[...] = _dot(a, w_ref[...].astype(BF16)) + b_ref[...]


def _mod_call(cc, w_mod, b_mod):
    rows = cc.shape[0]
    nblk = 3
    return pl.pallas_call(
        _mod_kernel,
        grid=(DEPTH, nblk),
        in_specs=[
            pl.BlockSpec((rows, D_MODEL), lambda l, j: (0, 0)),
            pl.BlockSpec((None, D_MODEL, D_MODEL), lambda l, j: (l, 0, j)),
            pl.BlockSpec((None, 1, D_MODEL), lambda l, j: (l, 0, j)),
        ],
        out_specs=pl.BlockSpec((None, rows, D_MODEL), lambda l, j: (l, 0, j)),
        out_shape=jax.ShapeDtypeStruct((DEPTH, rows, 3 * D_MODEL), F32),
        compiler_params=_params("arbitrary", "arbitrary"),
        name="mod",
    )(cc, w_mod, b_mod.reshape(DEPTH, 1, 3 * D_MODEL))


def _rope(slab, cos, sin_next, sin_prev, off):
    nxt = pltpu.roll(slab, LANES - off, 1)
    prv = pltpu.roll(slab, off, 1)
    return slab * cos + nxt * sin_next + prv * sin_prev


def _inproj_kernel(x_ref, ctx_ref, mod_ref, ng_ref, w_ref, b_ref, ta_ref, tc_ref,
                   pa_ref, pc_ref, pb_ref, pd_ref, pg_ref, po_ref, vta_ref, vtc_ref, vtd_ref, pgt_ref,
                   *, bsz):
    is_ctx = pl.program_id(1) == 0
    row = jnp.where(is_ctx, bsz, pl.program_id(0))
    x = jnp.where(is_ctx, ctx_ref[...], x_ref[...])
    mrow = mod_ref[pl.ds(row, 1), :]
    shift = mrow[:, 0:D_MODEL]
    scale = mrow[:, D_MODEL:2 * D_MODEL]
    ms = jnp.mean(x * x, axis=-1, keepdims=True)
    h = x * lax.rsqrt(ms + NORM_EPS) * ng_ref[...]
    h = (h * (1.0 + scale) + shift).astype(BF16)

    def proj(off, width):
        return _dot(h, w_ref[:, off:off + width]) + b_ref[:, off:off + width]

    def rope_seg(acc, tab_ref, off, q_scale):
        cos, s_next, s_prev = tab_ref[0], tab_ref[1], tab_ref[2]
        outs = []
        for j in range(4):
            r = _rope(acc[:, j * LANES:(j + 1) * LANES], cos, s_next, s_prev, off)
            outs.append(r * q_scale if j < 2 else r)
        outs.append(acc[:, 4 * LANES:])
        return jnp.concatenate(outs, axis=-1)

    acco = proj(OFF_O, SEG_O)
    po_ref[...] = jnp.concatenate(
        [_sigmoid(acco[:, 0:BRANCH]), acco[:, BRANCH:] * _sigmoid(acco[:, BRANCH:])], axis=-1).astype(BF16)
    acca = rope_seg(proj(OFF_A, SEG_A), ta_ref, DA_QK // 4, DA_QK ** -0.5 * LOG2E)
    pa_ref[...] = acca[:, 0:2 * BRANCH].astype(BF16)
    vta_ref[...] = acca[:, 2 * BRANCH:].T.astype(BF16)
    accc = rope_seg(proj(OFF_C, SEG_C), tc_ref, HEAD_DIM // 4, HEAD_DIM ** -0.5 * LOG2E)
    pc_ref[...] = accc[:, 0:2 * BRANCH].astype(BF16)
    vtc_ref[...] = accc[:, 2 * BRANCH:].T.astype(BF16)
    accd = proj(OFF_D, SEG_D)
    pd_ref[...] = jnp.concatenate(
        [accd[:, 0:BRANCH], accd[:, BRANCH:2 * BRANCH] * (HEAD_DIM ** -0.5)], axis=-1).astype(BF16)
    vtd_ref[...] = accd[:, 2 * BRANCH:].T.astype(BF16)
    gates = proj(OFF_G, SEG_G)
    pg_ref[...] = gates
    pgt_ref[...] = gates.T
    pb_ref[...] = proj(OFF_B, SEG_B)


def _inproj_call(x, ctx, mod_l, norm_g, w_r, b_r, tab_a, tab_c):
    bsz = x.shape[0]
    rows = mod_l.shape[0]
    widths = [(2 * BRANCH, BF16), (2 * BRANCH, BF16), (SEG_B, F32), (2 * BRANCH, BF16), (SEG_G, F32),
              (SEG_O, BF16)]
    out_specs = [pl.BlockSpec((None, TILE, w), lambda b, t: (b, t, 0)) for w, _ in widths]
    out_shape = [jax.ShapeDtypeStruct((bsz, T_ALL, w), dt) for w, dt in widths]
    for rows_t, dt in ((BRANCH, BF16), (BRANCH, BF16), (BRANCH, BF16), (SEG_G, F32)):
        out_specs.append(pl.BlockSpec((None, rows_t, TILE), lambda b, t: (b, 0, t)))
        out_shape.append(jax.ShapeDtypeStruct((bsz, rows_t, T_ALL), dt))
    return pl.pallas_call(
        functools.partial(_inproj_kernel, bsz=bsz),
        grid=(bsz, N_TILES),
        in_specs=[
            pl.BlockSpec((None, TILE, D_MODEL), lambda b, t: (b, jnp.maximum(t - 1, 0), 0)),
            pl.BlockSpec((None, CTX_LEN, D_MODEL), lambda b, t: (b, 0, 0)),
            pl.BlockSpec((rows, 3 * D_MODEL), lambda b, t: (0, 0)),
            pl.BlockSpec((1, D_MODEL), lambda b, t: (0, 0)),
            pl.BlockSpec((D_MODEL, PROJ_PAD), lambda b, t: (0, 0)),
            pl.BlockSpec((1, PROJ_PAD), lambda b, t: (0, 0)),
            pl.BlockSpec((3, TILE, LANES), lambda b, t: (0, t, 0)),
            pl.BlockSpec((3, TILE, LANES), lambda b, t: (0, t, 0)),
        ],
        out_specs=out_specs,
        out_shape=out_shape,
        compiler_params=_params("arbitrary", "arbitrary"),
        name="inproj",
    )(x, ctx, mod_l, norm_g.reshape(1, D_MODEL), w_r, b_r.reshape(1, PROJ_PAD), tab_a, tab_c)


def _diffattn_kernel(q_ref, k_ref, vt_ref, lam_ref, g_ref, o_ref, acc_ref, s_ref, *, lam_init, q_tile0):
    qb = pl.program_id(1) + q_tile0
    lp = lam_ref[...]
    lam = (jnp.exp(jnp.sum(lp[0:1] * lp[1:2], axis=-1, keepdims=True))
           - jnp.exp(jnp.sum(lp[2:3] * lp[3:4], axis=-1, keepdims=True)) + lam_init)
    q = q_ref[...]
    lane = _iota((1, BRANCH), 1)
    n_pairs = 2 * HEADS
    sub = TILE // 8

    def attend(nk):
        n_kt = nk // TILE
        qms = [jnp.where(lane // DA_QK == hm, q, jnp.zeros_like(q)) for hm in range(n_pairs)]

        def logits(hm):
            m8 = None
            half = max(nk // 2, TILE)
            for r0 in range(0, nk, half):
                st = _dot_nt(k_ref[r0:r0 + half, :], qms[hm])
                s_ref[hm % 2, r0:r0 + half, :] = st
                mh = jnp.max(st.reshape(half // 8, 8, TILE), axis=0)
                m8 = mh if m8 is None else jnp.maximum(m8, mh)
            return m8

        ones_rows = jnp.ones((ONES_ROWS, TILE), BF16)

        def value_tile(hm, j, mb, ot):
            h = hm // 2
            st = s_ref[hm % 2, j * TILE:(j + 1) * TILE, :]
            e = jnp.exp2(st.reshape(sub, 8, TILE) - mb[None])
            vt = vt_ref[h * HEAD_DIM:(h + 1) * HEAD_DIM, j * TILE:(j + 1) * TILE]
            vt1 = jnp.concatenate([vt, ones_rows], axis=0)
            return ot + _dot(vt1, e.reshape(TILE, TILE).astype(BF16))

        m8 = logits(0)
        for hm in range(n_pairs):
            mb = jnp.broadcast_to(jnp.max(m8, axis=0, keepdims=True), (8, TILE))
            if hm + 1 < n_pairs:
                m8 = logits(hm + 1)
            ot = jnp.zeros((HEAD_DIM + ONES_ROWS, TILE), F32)
            for j in range(n_kt):
                ot = value_tile(hm, j, mb, ot)
            l = ot[HEAD_DIM:HEAD_DIM + 1, :]
            ot = ot[0:HEAD_DIM, :]
            rows = slice((hm // 2) * HEAD_DIM, (hm // 2 + 1) * HEAD_DIM)
            if hm % 2 == 0:
                acc_ref[rows, :] = ot * (1.0 / l)
            else:
                acc_ref[rows, :] -= ot * (lam / l)
        o_ref[...] = (_head_rmsnorm(acc_ref[...].T, g_ref[...]) * (1.0 - lam_init)).astype(BF16)

    @pl.when(qb == 0)
    def _():
        attend(CTX_LEN)

    @pl.when(qb > 0)
    def _():
        attend(T_ALL)


def _diffattn_call(pa, vta, lam_p, g_row, layer_idx, need_ctx):
    bsz = pa.shape[0]
    q_tile0 = 0 if need_ctx else 1
    lam_init = 0.8 - 0.6 * math.exp(-0.3 * layer_idx)
    return pl.pallas_call(
        functools.partial(_diffattn_kernel, lam_init=lam_init, q_tile0=q_tile0),
        grid=(bsz, N_TILES - q_tile0),
        in_specs=[
            pl.BlockSpec((None, TILE, BRANCH), lambda b, t: (b, t + q_tile0, 0)),
            pl.BlockSpec((None, T_ALL, BRANCH), lambda b, t: (b, 0, 1)),
            pl.BlockSpec((None, BRANCH, T_ALL), lambda b, t: (b, 0, 0)),
            pl.BlockSpec((4, DA_QK), lambda b, t: (0, 0)),
            pl.BlockSpec((1, BRANCH), lambda b, t: (0, 0)),
        ],
        out_specs=pl.BlockSpec((None, TILE, BRANCH), lambda b, t: (b, t, 0)),
        out_shape=jax.ShapeDtypeStruct((bsz, (N_TILES - q_tile0) * TILE, BRANCH), BF16),
        scratch_shapes=[pltpu.VMEM((BRANCH, TILE), F32), pltpu.VMEM((2, T_ALL, TILE), F32)],
        compiler_params=_params("arbitrary", "arbitrary"),
        name="diffattn",
    )(pa, pa, vta, lam_p, g_row)


BAND = 2 * TILE


def _window_kernel(q_ref, k_ref, vt_ref, sink_ref, o_ref, acc_ref, s_ref, *, q_tile0):
    qb = pl.program_id(1) + q_tile0
    q = q_ref[...]
    lane = _iota((1, BRANCH), 1)
    ones_rows = jnp.ones((ONES_ROWS, TILE), BF16)

    def attend(band):
        if band:
            a = (qb - 1) * TILE
            start = jnp.clip(a - SW_WINDOW, 0, SEQ - BAND)
            row0 = pl.multiple_of(CTX_LEN + start, SW_WINDOW)
            kb = k_ref[pl.ds(row0, BAND), :]
            kpos = start + _iota((BAND, 1), 0)
            qpos = a + _iota((1, TILE), 1)
            valid = jnp.abs(qpos - kpos) <= SW_WINDOW
        sinks, maxes = [], []
        for h in range(HEADS):
            qm = jnp.where(lane // HEAD_DIM == h, q, jnp.zeros_like(q))
            sink = sink_ref[:, h:h + 1] * LOG2E
            sc = _dot_nt(k_ref[0:CTX_LEN, :], qm)
            s_ref[h, 0:CTX_LEN, :] = sc
            m = jnp.maximum(jnp.max(sc, axis=0, keepdims=True), sink)
            if band:
                sb = jnp.where(valid, _dot_nt(kb, qm), NEG_BIG)
                s_ref[h, CTX_LEN:CTX_LEN + BAND, :] = sb
                m = jnp.maximum(m, jnp.max(sb, axis=0, keepdims=True))
            sinks.append(sink)
            maxes.append(m)
        for h in range(HEADS):
            m = maxes[h]
            vt1 = jnp.concatenate([vt_ref[h * HEAD_DIM:(h + 1) * HEAD_DIM, 0:CTX_LEN], ones_rows], axis=0)
            pv = _dot(vt1, jnp.exp2(s_ref[h, 0:CTX_LEN, :] - m).astype(BF16))
            if band:
                vtb = vt_ref[h * HEAD_DIM:(h + 1) * HEAD_DIM, pl.ds(row0, BAND)]
                ones_b = jnp.ones((ONES_ROWS, BAND), BF16)
                eb = jnp.exp2(s_ref[h, CTX_LEN:CTX_LEN + BAND, :] - m).astype(BF16)
                pv = pv + _dot(jnp.concatenate([vtb, ones_b], axis=0), eb)
            l = pv[HEAD_DIM:HEAD_DIM + 1, :] + jnp.exp2(sinks[h] - m)
            acc_ref[h * HEAD_DIM:(h + 1) * HEAD_DIM, :] = pv[0:HEAD_DIM, :] * (1.0 / l)
        o_ref[...] = acc_ref[...].T.astype(BF16)

    @pl.when(qb == 0)
    def _():
        attend(False)

    @pl.when(qb > 0)
    def _():
        attend(True)


def _window_call(pc, vtc, sink_row, need_ctx):
    bsz = pc.shape[0]
    q_tile0 = 0 if need_ctx else 1
    n_q = N_TILES - q_tile0
    return pl.pallas_call(
        functools.partial(_window_kernel, q_tile0=q_tile0),
        grid=(bsz, n_q),
        in_specs=[
            pl.BlockSpec((None, TILE, BRANCH), lambda b, t: (b, t + q_tile0, 0)),
            pl.BlockSpec((None, T_ALL, BRANCH), lambda b, t: (b, 0, 1)),
            pl.BlockSpec((None, BRANCH, T_ALL), lambda b, t: (b, 0, 0)),
            pl.BlockSpec((1, HEADS), lambda b, t: (0, 0)),
        ],
        out_specs=pl.BlockSpec((None, TILE, BRANCH), lambda b, t: (b, t, 0)),
        out_shape=jax.ShapeDtypeStruct((bsz, n_q * TILE, BRANCH), BF16),
        scratch_shapes=[pltpu.VMEM((BRANCH, TILE), F32), pltpu.VMEM((HEADS, CTX_LEN + BAND, TILE), F32)],
        compiler_params=_params("arbitrary", "arbitrary"),
        name="window",
    )(pc, pc, vtc, sink_row)


CH_PER_TILE = TILE // HG_CHUNK
HG_LEVELS = 2
HG_STATE = HG_CHUNK * 2 ** HG_LEVELS
ST_PER_TILE = TILE // HG_STATE
HG_UNROLL = 4


def _hgrn_gates(z, lb_terms):
    log2_ksig = _log_sigmoid(-z) * LOG2E
    if lb_terms is None:
        return _log_sigmoid(z) * LOG2E, _sigmoid(-z), log2_ksig
    lb, log_lb, log_1m = lb_terms
    bt = log_1m + _log_sigmoid(z)
    mx = jnp.maximum(log_lb, bt)
    log_f = mx + jnp.log(jnp.exp(log_lb - mx) + jnp.exp(bt - mx))
    return log_f * LOG2E, (1.0 - lb) * _sigmoid(-z), log_1m * LOG2E + log2_ksig


def _hgrn_intra(direction, q3, c3, u3, v3, head_ones):
    half = HG_CHUNK // 2
    slabs, meta = [], []
    for s in range(HG_CHUNK):
        us = jnp.broadcast_to(u3[:, s:s + 1, :], (CH_PER_TILE, half, BRANCH))
        for g in range(2):
            lo_row, hi_row = half * g, half * g + half - 1
            if direction == 0:
                none_valid, all_valid = hi_row < s, lo_row >= s
            else:
                none_valid, all_valid = lo_row > s, hi_row <= s
            if none_valid:
                continue
            d = c3[:, half * g:half * (g + 1), :] - us
            if not all_valid:
                row = _iota((1, half, 1), 1) + half * g
                d = jnp.where((row >= s) if direction == 0 else (row <= s), d, NEG_BIG)
            x = q3[:, half * g:half * (g + 1), :] * jnp.exp2(d)
            slabs.append(x.reshape(CH_PER_TILE * half, BRANCH).astype(BF16))
            meta.append((s, g))
    a_all = _dot(jnp.concatenate(slabs, axis=0), head_ones)
    o = [jnp.zeros((CH_PER_TILE, half, BRANCH), F32) for _ in range(2)]
    n = CH_PER_TILE * half
    vs = None
    for i, (s, g) in enumerate(meta):
        if i == 0 or meta[i - 1][0] != s:
            vs = jnp.broadcast_to(v3[:, s:s + 1, :], (CH_PER_TILE, half, BRANCH))
        o[g] = o[g] + a_all[i * n:(i + 1) * n].reshape(CH_PER_TILE, half, BRANCH) * vs
    return jnp.concatenate(o, axis=1).reshape(TILE, BRANCH)


def _hgrn_kernel(p_ref, lb_ref, g_ref, o_ref, qt_ref, ke_ref, vb_ref, dec_ref, of_ref, oi_ref, st_ref,
                 *, layer_idx):
    rr = _iota((TILE, TILE), 0)
    cc = _iota((TILE, TILE), 1)
    same_chunk = (rr // HG_CHUNK) == (cc // HG_CHUNK)
    chunk_ones = same_chunk.astype(BF16)
    tris = [(same_chunk & (cc <= rr)).astype(BF16),
            (same_chunk & (cc >= rr)).astype(BF16)]
    head_ones = _head_sum_matrix()
    head_mask = (rr // HEAD_DIM) == (cc // HEAD_DIM)
    lane = _iota((1, BRANCH), 1)
    row_in = _iota((TILE, 1), 0)

    def shift_rows(a, n):
        n = n % TILE
        return jnp.concatenate([a[TILE - n:, :], a[:TILE - n, :]], axis=0)

    lb_terms = None
    if layer_idx > 0:
        lbp = lb_ref[...]
        lbp = lbp - jnp.max(lbp, axis=0, keepdims=True)
        sm = jnp.exp(lbp)
        sm = sm / jnp.sum(sm, axis=0, keepdims=True)
        lb = jnp.sum(sm[1:layer_idx + 1], axis=0, keepdims=True)
        lb_terms = (lb, jnp.log(lb), jnp.log(1.0 - lb))

    def tile_body(t, carry):
        rows = pl.ds(pl.multiple_of(t * TILE, TILE), TILE)
        q = p_ref[rows, 0:BRANCH] * (HEAD_DIM ** -0.5)
        v = p_ref[rows, 3 * BRANCH:4 * BRANCH]
        vbf = v.astype(BF16)
        vb_ref[rows, :] = vbf
        v_heads = jnp.concatenate([jnp.where(lane // HEAD_DIM == h, vbf, jnp.zeros_like(vbf))
                                   for h in range(HEADS)], axis=0)
        shape3 = (CH_PER_TILE, HG_CHUNK, BRANCH)
        o_intra = None
        for direction in (0, 1):
            z = p_ref[rows, (1 + direction) * BRANCH:(2 + direction) * BRANCH]
            log2_f, kk, log2_k = _hgrn_gates(z, lb_terms)
            cum = _dot_exact_l(tris[direction], log2_f)
            tot = _dot_exact_l(chunk_ones, log2_f)
            o_d = _hgrn_intra(direction, q.reshape(shape3), cum.reshape(shape3),
                              (cum - log2_k).reshape(shape3), v.reshape(shape3), head_ones)
            a_heads = [jnp.zeros((TILE, TILE), F32) for _ in range(HEADS)]
            g = HG_CHUNK
            for _ in range(HG_LEVELS):
                qt = (q * jnp.exp2(cum)).astype(BF16)
                ke = (kk * jnp.exp2(tot - cum)).astype(BF16)
                later_r = ((rr % (2 * g)) >= g) if direction == 0 else ((rr % (2 * g)) < g)
                later_c = ((cc % (2 * g)) >= g) if direction == 0 else ((cc % (2 * g)) < g)
                pair = ((rr // (2 * g)) == (cc // (2 * g))) & later_r & jnp.logical_not(later_c)
                for h in range(HEADS):
                    qh = jnp.where(lane // HEAD_DIM == h, qt, jnp.zeros_like(qt))
                    a_heads[h] = jnp.where(pair, _dot_nt(qh, ke), a_heads[h])
                later_row = ((row_in % (2 * g)) >= g) if direction == 0 else ((row_in % (2 * g)) < g)
                sign = 1 if direction == 0 else -1
                tot_other = jnp.where(later_row, shift_rows(tot, sign * g), shift_rows(tot, -sign * g))
                cum = cum + jnp.where(later_row, tot_other, 0.0)
                tot = tot + tot_other
                g *= 2
            if HG_LEVELS:
                a_cat = jnp.concatenate([a.astype(BF16) for a in a_heads], axis=1)
                o_d = o_d + _dot(a_cat, v_heads)
            qt_ref[direction, rows, :] = (q * jnp.exp2(cum)).astype(BF16)
            ke_ref[direction, rows, :] = (kk * jnp.exp2(tot - cum)).astype(BF16)
            dec_ref[direction, t] = jnp.exp2(tot.reshape(ST_PER_TILE, HG_STATE, BRANCH)[:, 0, :])
            o_intra = o_d if o_intra is None else o_intra + o_d
        of_ref[rows, :] = o_intra
        return carry

    lax.fori_loop(0, N_TILES, tile_body, 0)

    st_ref[...] = jnp.zeros_like(st_ref)
    n_ctx, n_all = CTX_LEN // HG_STATE, T_ALL // HG_STATE

    def state_step(i, direction):
        if direction == 0:
            c = i
        else:
            c = jnp.where(i < n_ctx, n_ctx - 1 - i, n_all - 1 + n_ctx - i)
        rows = pl.ds(pl.multiple_of(c * HG_STATE, HG_STATE), HG_STATE)
        st = st_ref[direction]
        oi_ref[direction, rows, :] = _dot_nt(qt_ref[direction, rows, :], st.astype(BF16))
        ds = _dot_tn(vb_ref[rows, :], ke_ref[direction, rows, :])
        dec = dec_ref[direction, c // ST_PER_TILE, pl.ds(c % ST_PER_TILE, 1), :]
        st_ref[direction] = st * dec + jnp.where(head_mask, ds, 0.0)

    def state_body(it, carry):
        for u in range(HG_UNROLL):
            for direction in (0, 1):
                state_step(it * HG_UNROLL + u, direction)
        return carry

    lax.fori_loop(0, n_all // HG_UNROLL, state_body, 0)

    o_ref[...] = _head_rmsnorm(of_ref[...] + oi_ref[0] + oi_ref[1], g_ref[...]).astype(BF16)


def _hgrn_call(pb, hg_lb, g_row, layer_idx):
    bsz = pb.shape[0]
    return pl.pallas_call(
        functools.partial(_hgrn_kernel, layer_idx=layer_idx),
        grid=(bsz,),
        in_specs=[
            pl.BlockSpec((None, T_ALL, SEG_B), lambda b: (b, 0, 0)),
            pl.BlockSpec((DEPTH, BRANCH), lambda b: (0, 0)),
            pl.BlockSpec((1, BRANCH), lambda b: (0, 0)),
        ],
        out_specs=pl.BlockSpec((None, T_ALL, BRANCH), lambda b: (b, 0, 0)),
        out_shape=jax.ShapeDtypeStruct((bsz, T_ALL, BRANCH), BF16),
        scratch_shapes=[pltpu.VMEM((2, T_ALL, BRANCH), BF16), pltpu.VMEM((2, T_ALL, BRANCH), BF16),
                        pltpu.VMEM((T_ALL, BRANCH), BF16), pltpu.VMEM((2, N_TILES, ST_PER_TILE, BRANCH), F32),
                        pltpu.VMEM((T_ALL, BRANCH), F32), pltpu.VMEM((2, T_ALL, BRANCH), F32),
                        pltpu.VMEM((2, BRANCH, BRANCH), F32)],
        compiler_params=_params("arbitrary"),
        name="hgrn",
    )(pb, hg_lb, g_row)


def _mlstm_logits(direction, c, p_ref, vt_ref, g_ref, gt_ref, ct_ref, n_ref, m_ref, consts):
    tri, valid, _, lane, row16, lane16 = consts
    rows = pl.ds(pl.multiple_of(c * TILE, TILE), TILE)
    q = p_ref[rows, 0:BRANCH]
    k = p_ref[rows, BRANCH:2 * BRANCH]
    g = g_ref[rows, :]
    gt = gt_ref[:, rows]
    cum = _dot_exact_l(tri, _log_sigmoid(g) * LOG2E)
    cum_t = _dot_exact_nt(_log_sigmoid(gt[8:16, :]) * LOG2E, tri)
    ig_t = gt[0:8, :] * LOG2E
    ct = ct_ref[direction]
    n0 = n_ref[direction]
    n_hi = n0.astype(BF16).astype(F32)
    n_lo = n0 - n_hi
    n_rows = (jnp.where((row16 < HEADS) & (lane16 == row16), n_hi, 0.0)
              + jnp.where((row16 >= HEADS) & (lane16 == row16 - HEADS), n_lo, 0.0)).astype(BF16)
    inter_all = _dot_nt(jnp.concatenate([ct.astype(BF16), n_rows], axis=0), q)
    heads = []
    for h in range(HEADS):
        r = HEADS * direction + h
        cumr = cum_t[r:r + 1, :]
        ucol = g[:, r:r + 1] * LOG2E - cum[:, 2 * HEADS + r:2 * HEADS + r + 1]
        toth = cumr[:, TILE - 1:TILE] if direction == 0 else cumr[:, 0:1]
        m0h = m_ref[direction, :, h:h + 1]
        logd = jnp.where(valid, cumr + ucol, NEG_BIG)
        inter = cumr + m0h
        m_t = jnp.maximum(jnp.max(logd, axis=0, keepdims=True), inter)
        qm = jnp.where(lane // HEAD_DIM == h, q, jnp.zeros_like(q))
        heads.append(dict(qk=_dot_nt(k, qm), logd=logd, m_t=m_t, g0=jnp.exp2(inter - m_t),
                          a_row=toth - cumr + ig_t[r:r + 1, :], carry=toth + m0h))
    return dict(direction=direction, rows=rows, k=k, ct=ct, n0=n0, inter_all=inter_all, heads=heads)


def _mlstm_outputs(cx, vt_ref, ht_ref, ct_ref, n_ref, m_ref, consts):
    _, _, head_mask, lane, row16, _ = consts
    direction, rows, k, inter_all = cx["direction"], cx["rows"], cx["k"], cx["inter_all"]
    vt = vt_ref[:, rows]
    ones_rows = jnp.ones((ONES_ROWS, TILE), BF16)
    w_rows, sp_row = [], jnp.zeros((1, BRANCH), F32)
    for h, hd in enumerate(cx["heads"]):
        m_t, g0 = hd["m_t"], hd["g0"]
        s_t = hd["qk"] * jnp.exp2(hd["logd"] - m_t)
        vt1 = jnp.concatenate([vt[h * HEAD_DIM:(h + 1) * HEAD_DIM, :], ones_rows], axis=0)
        pv = _dot(vt1, s_t.astype(BF16))
        num = pv[0:HEAD_DIM, :] + g0 * inter_all[h * HEAD_DIM:(h + 1) * HEAD_DIM, :]
        den = pv[HEAD_DIM:HEAD_DIM + 1, :] + g0 * (inter_all[BRANCH + h:BRANCH + h + 1, :]
                                                   + inter_all[BRANCH + HEADS + h:BRANCH + HEADS + h + 1, :])
        ht_ref[direction, h * HEAD_DIM:(h + 1) * HEAD_DIM, rows] = (
            num / jnp.maximum(jnp.abs(den), jnp.exp2(-m_t)))
        a_row = hd["a_row"]
        m_loc = jnp.max(a_row, axis=1, keepdims=True)
        m_new = jnp.maximum(hd["carry"], m_loc)
        sp = jnp.exp2(hd["carry"] - m_new)
        w_rows.append(jnp.exp2(a_row - m_loc) * jnp.exp2(m_loc - m_new))
        sp_row = sp_row + jnp.where(lane // HEAD_DIM == h, sp, 0.0)
        m_ref[direction, :, h:h + 1] = m_new

    w_block = jnp.concatenate([jnp.broadcast_to(w, (HEAD_DIM, TILE)) for w in w_rows], axis=0)
    vtw = (vt.astype(F32) * w_block).astype(BF16)
    w16 = jnp.zeros((ONES_ROWS, TILE), F32)
    for h in range(HEADS):
        w_hi = w_rows[h].astype(BF16).astype(F32)
        w16 = w16 + jnp.where(row16 == h, w_hi, 0.0) + jnp.where(row16 == HEADS + h, w_rows[h] - w_hi, 0.0)
    dall = _dot(jnp.concatenate([vtw, w16.astype(BF16)], axis=0), k)
    ct_ref[direction] = cx["ct"] * sp_row + jnp.where(head_mask, dall[0:BRANCH, :], 0.0)
    dn = jnp.zeros((1, BRANCH), F32)
    for h in range(HEADS):
        dn = dn + jnp.where(lane // HEAD_DIM == h,
                            dall[BRANCH + h:BRANCH + h + 1, :] + dall[BRANCH + HEADS + h:BRANCH + HEADS + h + 1, :], 0.0)
    n_ref[direction] = cx["n0"] * sp_row + dn


def _mlstm_kernel(p_ref, vt_ref, g_ref, gt_ref, gain_ref, o_ref, ht_ref, ct_ref, n_ref, m_ref):
    rr = _iota((TILE, TILE), 0)
    cc = _iota((TILE, TILE), 1)
    head_mask = (rr // HEAD_DIM) == (cc // HEAD_DIM)
    lane = _iota((1, BRANCH), 1)
    row16 = _iota((ONES_ROWS, BRANCH), 0)
    lane16 = _iota((ONES_ROWS, BRANCH), 1) // HEAD_DIM
    consts = []
    for direction in (0, 1):
        tri = ((cc <= rr) if direction == 0 else (cc >= rr)).astype(BF16)
        valid = (rr <= cc) if direction == 0 else (rr >= cc)
        consts.append((tri, valid, head_mask, lane, row16, lane16))
    ct_ref[...] = jnp.zeros_like(ct_ref)
    n_ref[...] = jnp.zeros_like(n_ref)
    m_ref[...] = jnp.zeros_like(m_ref)

    def body(i, carry):
        chunk = (i, jnp.where(i == 0, 0, N_TILES - i))
        cxs = [_mlstm_logits(d, chunk[d], p_ref, vt_ref, g_ref, gt_ref, ct_ref, n_ref, m_ref, consts[d])
               for d in (0, 1)]
        for d in (0, 1):
            _mlstm_outputs(cxs[d], vt_ref, ht_ref, ct_ref, n_ref, m_ref, consts[d])
        return carry

    lax.fori_loop(0, N_TILES, body, 0)

    def out_body(t, carry):
        rows = pl.ds(pl.multiple_of(t * TILE, TILE), TILE)
        o_ref[rows, :] = _head_rmsnorm((ht_ref[0, :, rows] + ht_ref[1, :, rows]).T,
                                       gain_ref[...]).astype(BF16)
        return carry

    lax.fori_loop(0, N_TILES, out_body, 0)


def _mlstm_call(pd, vtd, pg, pgt, g_row):
    bsz = pd.shape[0]
    return pl.pallas_call(
        _mlstm_kernel,
        grid=(bsz,),
        in_specs=[
            pl.BlockSpec((None, T_ALL, 2 * BRANCH), lambda b: (b, 0, 0)),
            pl.BlockSpec((None, BRANCH, T_ALL), lambda b: (b, 0, 0)),
            pl.BlockSpec((None, T_ALL, SEG_G), lambda b: (b, 0, 0)),
            pl.BlockSpec((None, SEG_G, T_ALL), lambda b: (b, 0, 0)),
            pl.BlockSpec((1, BRANCH), lambda b: (0, 0)),
        ],
        out_specs=pl.BlockSpec((None, T_ALL, BRANCH), lambda b: (b, 0, 0)),
        out_shape=jax.ShapeDtypeStruct((bsz, T_ALL, BRANCH), BF16),
        scratch_shapes=[pltpu.VMEM((2, BRANCH, T_ALL), F32), pltpu.VMEM((2, BRANCH, BRANCH), F32),
                        pltpu.VMEM((2, 1, BRANCH), F32), pltpu.VMEM((2, 1, LANES), F32)],
        compiler_params=_params("arbitrary"),
        name="mlstm",
    )(pd, vtd, pg, pgt, g_row)


def _outproj_kernel(x_ref, ctx_ref, ya_ref, yb_ref, yc_ref, yd_ref, po_ref, mod_ref, w_ref, fg_ref,
                    *out_refs, bsz, last):
    t = pl.program_id(1)
    is_ctx = jnp.logical_and(t == 0, not last)
    row = jnp.where(is_ctx, bsz, pl.program_id(0))
    gate_mod = mod_ref[pl.ds(row, 1), 2 * D_MODEL:3 * D_MODEL]
    po = po_ref[...].astype(F32)
    yd = yd_ref[...].astype(F32) * po[:, 0:BRANCH]
    mixed = jnp.concatenate([ya_ref[...].astype(F32), yb_ref[...].astype(F32), yc_ref[...].astype(F32), yd],
                            axis=-1)
    mixed = (mixed * po[:, BRANCH:]).astype(BF16)
    delta = gate_mod * _dot(mixed, w_ref[...])
    if last:
        xn = x_ref[...] + delta
        ms = jnp.mean(xn * xn, axis=-1, keepdims=True)
        out_refs[0][...] = xn * lax.rsqrt(ms + NORM_EPS) * fg_ref[...]
    else:
        x_out_ref, ctx_out_ref = out_refs

        @pl.when(t == 0)
        def _():
            ctx_out_ref[...] = ctx_ref[...] + delta

        @pl.when(t > 0)
        def _():
            x_out_ref[...] = x_ref[...] + delta


def _outproj_call(x, ctx, ya, yb, yc, yd, po, mod_l, w_out_bf, final_g, last):
    bsz = x.shape[0]
    tile0 = 1 if last else 0
    rows = mod_l.shape[0]

    def tok(width, arr):
        off = tile0 if arr.shape[1] == T_ALL else 0
        return pl.BlockSpec((None, TILE, width), lambda b, t: (b, t + off, 0))

    lat_spec = pl.BlockSpec((None, TILE, D_MODEL), lambda b, t: (b, jnp.maximum(t + tile0 - 1, 0), 0))
    ctx_spec = pl.BlockSpec((None, CTX_LEN, D_MODEL), lambda b, t: (b, 0, 0))
    lat_shape = jax.ShapeDtypeStruct((bsz, SEQ, D_MODEL), F32)
    ctx_shape = jax.ShapeDtypeStruct((bsz, CTX_LEN, D_MODEL), F32)
    return pl.pallas_call(
        functools.partial(_outproj_kernel, bsz=bsz, last=last),
        grid=(bsz, N_TILES - tile0),
        in_specs=[lat_spec, ctx_spec, tok(BRANCH, ya), tok(BRANCH, yb), tok(BRANCH, yc), tok(BRANCH, yd),
                  tok(SEG_O, po),
                  pl.BlockSpec((rows, 3 * D_MODEL), lambda b, t: (0, 0)),
                  pl.BlockSpec((D_MODEL, D_MODEL), lambda b, t: (0, 0)),
                  pl.BlockSpec((1, D_MODEL), lambda b, t: (0, 0))],
        out_specs=lat_spec if last else [lat_spec, ctx_spec],
        out_shape=lat_shape if last else [lat_shape, ctx_shape],
        compiler_params=_params("arbitrary", "arbitrary"),
        name="outproj",
    )(x, ctx, ya, yb, yc, yd, po, mod_l, w_out_bf, final_g.reshape(1, D_MODEL))


def _rope_tables(dim):
    quarter = dim // 4
    half = dim // 2
    pos = np.arange(SEQ)
    row = (pos // GRID_W).astype(np.float32)
    col = (pos % GRID_W).astype(np.float32)
    inv = (np.float32(ROPE_BASE) ** (-np.arange(0, half, 2, dtype=np.float32) / np.float32(half))).astype(np.float32)
    ang_r = row[:, None] * inv[None, :]
    ang_c = col[:, None] * inv[None, :]
    lane = np.arange(LANES) % dim
    part = lane // quarter
    freq = lane % quarter
    ang = np.where(part[None, :] < 2, ang_r[:, freq], ang_c[:, freq]).astype(np.float32)
    cos = np.cos(ang)
    sin = np.sin(ang)
    first = (part % 2 == 0)[None, :]
    s_next = np.where(first, -sin, 0.0)
    s_prev = np.where(first, 0.0, sin)
    tab = np.stack([cos, s_next, s_prev]).astype(np.float32)
    ident = np.stack([np.ones((CTX_LEN, LANES)), np.zeros((CTX_LEN, LANES)),
                      np.zeros((CTX_LEN, LANES))]).astype(np.float32)
    return jnp.asarray(np.concatenate([ident, tab], axis=1))


def _relayout_in_proj(w, bias):
    def cols(a):
        qa, ka, va = a[..., 0:256], a[..., 256:512], a[..., 512:768]
        seg_b = a[..., 768:1792]
        qc = a[..., 1792:2048]
        kc = a[..., 2048:2176]
        vc = a[..., 2176:2304]
        seg_d = a[..., 2304:3072]
        gates = a[..., 3072:3088]
        seg_o = a[..., 3088:4368]
        rep = lambda kv: jnp.concatenate([kv[..., 0:64], kv[..., 0:64], kv[..., 64:128], kv[..., 64:128]], axis=-1)
        pad = jnp.zeros(a.shape[:-1] + (SEG_G - 16,), a.dtype)
        return jnp.concatenate([qa, ka, va, qc, rep(kc), rep(vc), seg_b, seg_d, gates, pad, seg_o], axis=-1)
    return cols(w), cols(bias)


def kernel(x, c, ctx, c_ctx, w_mod, b_mod, norm_g, w_in, b_in, diff_lam, diff_g, hg_lb, hg_g,
           sw_sink, ml_g, w_out, final_g):
    bsz = x.shape[0]
    tab_a = _rope_tables(DA_QK)
    tab_c = _rope_tables(HEAD_DIM)
    rows = ((bsz + 1 + 7) // 8) * 8
    cc = jnp.concatenate([c, c_ctx[None, :], jnp.zeros((rows - bsz - 1, D_MODEL), F32)], axis=0)
    mod = _mod_call(cc, w_mod, b_mod)
    tile4 = lambda g: jnp.tile(g, HEADS).reshape(1, BRANCH)
    w_in_bf = w_in.astype(BF16)
    w_out_bf = w_out.astype(BF16)
    for l in range(DEPTH):
        last = l == DEPTH - 1
        w_r, b_r = _relayout_in_proj(w_in_bf[l], b_in[l])
        pa, pc, pb, pd, pg, po, vta, vtc, vtd, pgt = _inproj_call(x, ctx, mod[l], norm_g[l], w_r, b_r,
                                                                 tab_a, tab_c)
        ya = _diffattn_call(pa, vta, diff_lam[l], tile4(diff_g[l]), l, not last)
        yb = _hgrn_call(pb, hg_lb, tile4(hg_g[l]), l)
        yc = _window_call(pc, vtc, sw_sink[l].reshape(1, HEADS), not last)
        yd = _mlstm_call(pd, vtd, pg, pgt, tile4(ml_g[l]))
        res = _outproj_call(x, ctx, ya, yb, yc, yd, po, mod[l], w_out_bf[l], final_g, last)
        if last:
            return res
        x, ctx = res
```

```python
import functools
import math

import numpy as np
import jax
import jax.numpy as jnp
from jax import lax
from jax.experimental import pallas as pl
from jax.experimental.pallas import tpu as pltpu

F32 = jnp.float32
BF16 = jnp.bfloat16

D_MODEL = 1024
SEQ = 2048
CTX_LEN = 256
T_ALL = CTX_LEN + SEQ
GRID_W = 64
DEPTH = 2
HEADS = 4
HEAD_DIM = 64
BRANCH = HEADS * HEAD_DIM
DA_QK = 32
SW_WINDOW = 128
HG_CHUNK = 16
ROPE_BASE = 10000.0
NORM_EPS = 1e-6
NEG_BIG = -1e30
LOG2E = math.log2(math.e)

TILE = 256
N_TILES = T_ALL // TILE
LANES = 128
ONES_ROWS = 16

SEG_A = 3 * BRANCH
KV_WIDTH = 2 * HEAD_DIM
SEG_C = BRANCH + 2 * KV_WIDTH
SEG_B = 4 * BRANCH
SEG_D = 3 * BRANCH
SEG_G = LANES
SEG_O = BRANCH + D_MODEL
OFF_A = 0
OFF_C = OFF_A + SEG_A
OFF_B = OFF_C + SEG_C
OFF_D = OFF_B + SEG_B
OFF_G = OFF_D + SEG_D
OFF_O = OFF_G + SEG_G
PROJ_PAD = OFF_O + SEG_O

VMEM_LIMIT = 56 * 1024 * 1024


def _params(*sem):
    return pltpu.CompilerParams(dimension_semantics=sem, vmem_limit_bytes=VMEM_LIMIT)


def _dot(a, b):
    return jnp.dot(a, b, preferred_element_type=F32)


def _dot_nt(a, b):
    return lax.dot_general(a, b, (((1,), (1,)), ((), ())), preferred_element_type=F32)


def _dot_tn(a, b):
    return lax.dot_general(a, b, (((0,), (0,)), ((), ())), preferred_element_type=F32)


def _split3(x):
    x1 = x.astype(BF16)
    r1 = x - x1.astype(F32)
    x2 = r1.astype(BF16)
    x3 = (r1 - x2.astype(F32)).astype(BF16)
    return x1, x2, x3


def _dot_exact_l(m01, x):
    x1, x2, x3 = _split3(x)
    return _dot(m01, x1) + _dot(m01, x2) + _dot(m01, x3)


def _dot_exact_r(x, m01):
    x1, x2, x3 = _split3(x)
    return _dot(x1, m01) + _dot(x2, m01) + _dot(x3, m01)


def _dot_exact_nt(x, m01):
    x1, x2, x3 = _split3(x)
    return _dot_nt(x1, m01) + _dot_nt(x2, m01) + _dot_nt(x3, m01)


def _sigmoid(z):
    e = jnp.exp(-jnp.abs(z))
    r = 1.0 / (1.0 + e)
    return jnp.where(z >= 0, r, e * r)


def _log_sigmoid(z):
    return jnp.minimum(z, 0.0) - jnp.log(1.0 + jnp.exp(-jnp.abs(z)))


def _iota(shape, dim):
    return lax.broadcasted_iota(jnp.int32, shape, dim)


def _head_sum_matrix():
    r = _iota((BRANCH, BRANCH), 0) // HEAD_DIM
    c = _iota((BRANCH, BRANCH), 1) // HEAD_DIM
    return (r == c).astype(BF16)


def _head_rmsnorm(o, g_row):
    ss = _dot_exact_r(o * o, _head_sum_matrix())
    return o * lax.rsqrt(ss * (1.0 / HEAD_DIM) + NORM_EPS) * g_row


def _mod_kernel(cc_ref, w_ref, b_ref, o_ref):
    cc = cc_ref[...]
    a = (cc * _sigmoid(cc)).astype(BF16)
    o_ref[...] = _dot(a, w_ref[...].astype(BF16)) + b_ref[...]


def _mod_call(cc, w_mod, b_mod):
    rows = cc.shape[0]
    nblk = 3
    return pl.pallas_call(
        _mod_kernel,
        grid=(DEPTH, nblk),
        in_specs=[
            pl.BlockSpec((rows, D_MODEL), lambda l, j: (0, 0)),
            pl.BlockSpec((None, D_MODEL, D_MODEL), lambda l, j: (l, 0, j)),
            pl.BlockSpec((None, 1, D_MODEL), lambda l, j: (l, 0, j)),
        ],
        out_specs=pl.BlockSpec((None, rows, D_MODEL), lambda l, j: (l, 0, j)),
        out_shape=jax.ShapeDtypeStruct((DEPTH, rows, 3 * D_MODEL), F32),
        compiler_params=_params("arbitrary", "arbitrary"),
        name="mod",
    )(cc, w_mod, b_mod.reshape(DEPTH, 1, 3 * D_MODEL))


def _rope(slab, cos, sin_next, sin_prev, off):
    nxt = pltpu.roll(slab, LANES - off, 1)
    prv = pltpu.roll(slab, off, 1)
    return slab * cos + nxt * sin_next + prv * sin_prev


def _inproj_kernel(x_ref, ctx_ref, mod_ref, ng_ref, w_ref, b_ref, ta_ref, tc_ref,
                   pa_ref, pc_ref, pb_ref, pd_ref, pg_ref, po_ref, vta_ref, vtc_ref, vtd_ref, pgt_ref,
                   *, bsz):
    is_ctx = pl.program_id(1) == 0
    row = jnp.where(is_ctx, bsz, pl.program_id(0))
    x = jnp.where(is_ctx, ctx_ref[...], x_ref[...])
    mrow = mod_ref[pl.ds(row, 1), :]
    shift = mrow[:, 0:D_MODEL]
    scale = mrow[:, D_MODEL:2 * D_MODEL]
    ms = jnp.mean(x * x, axis=-1, keepdims=True)
    h = x * lax.rsqrt(ms + NORM_EPS) * ng_ref[...]
    h = (h * (1.0 + scale) + shift).astype(BF16)

    def proj(off, width):
        return _dot(h, w_ref[:, off:off + width]) + b_ref[:, off:off + width]

    def rope_seg(acc, tab_ref, off, q_scale, k_slabs):
        cos, s_next, s_prev = tab_ref[0], tab_ref[1], tab_ref[2]
        outs = []
        for j in range(2 + k_slabs):
            r = _rope(acc[:, j * LANES:(j + 1) * LANES], cos, s_next, s_prev, off)
            outs.append(r * q_scale if j < 2 else r)
        outs.append(acc[:, (2 + k_slabs) * LANES:])
        return jnp.concatenate(outs, axis=-1)

    acco = proj(OFF_O, SEG_O)
    po_ref[...] = jnp.concatenate(
        [_sigmoid(acco[:, 0:BRANCH]), acco[:, BRANCH:] * _sigmoid(acco[:, BRANCH:])], axis=-1).astype(BF16)
    acca = rope_seg(proj(OFF_A, SEG_A), ta_ref, DA_QK // 4, DA_QK ** -0.5 * LOG2E, 2)
    pa_ref[...] = acca[:, 0:2 * BRANCH].astype(BF16)
    vta_ref[...] = acca[:, 2 * BRANCH:].T.astype(BF16)
    accc = rope_seg(proj(OFF_C, SEG_C), tc_ref, HEAD_DIM // 4, HEAD_DIM ** -0.5 * LOG2E, 1)
    pc_ref[...] = accc[:, 0:BRANCH + KV_WIDTH].astype(BF16)
    vtc_ref[...] = accc[:, BRANCH + KV_WIDTH:].T.astype(BF16)
    accd = proj(OFF_D, SEG_D)
    pd_ref[...] = jnp.concatenate(
        [accd[:, 0:BRANCH], accd[:, BRANCH:2 * BRANCH] * (HEAD_DIM ** -0.5)], axis=-1).astype(BF16)
    vtd_ref[...] = accd[:, 2 * BRANCH:].T.astype(BF16)
    gates = proj(OFF_G, SEG_G)
    pg_ref[...] = gates
    pgt_ref[...] = gates.T
    pb_ref[...] = proj(OFF_B, SEG_B)


def _inproj_call(x, ctx, mod_l, norm_g, w_r, b_r, tab_a, tab_c):
    bsz = x.shape[0]
    rows = mod_l.shape[0]
    widths = [(2 * BRANCH, BF16), (BRANCH + KV_WIDTH, BF16), (SEG_B, F32), (2 * BRANCH, BF16), (SEG_G, F32),
              (SEG_O, BF16)]
    out_specs = [pl.BlockSpec((None, TILE, w), lambda b, t: (b, t, 0)) for w, _ in widths]
    out_shape = [jax.ShapeDtypeStruct((bsz, T_ALL, w), dt) for w, dt in widths]
    for rows_t, dt in ((BRANCH, BF16), (KV_WIDTH, BF16), (BRANCH, BF16), (SEG_G, F32)):
        out_specs.append(pl.BlockSpec((None, rows_t, TILE), lambda b, t: (b, 0, t)))
        out_shape.append(jax.ShapeDtypeStruct((bsz, rows_t, T_ALL), dt))
    return pl.pallas_call(
        functools.partial(_inproj_kernel, bsz=bsz),
        grid=(bsz, N_TILES),
        in_specs=[
            pl.BlockSpec((None, TILE, D_MODEL), lambda b, t: (b, jnp.maximum(t - 1, 0), 0)),
            pl.BlockSpec((None, CTX_LEN, D_MODEL), lambda b, t: (b, 0, 0)),
            pl.BlockSpec((rows, 3 * D_MODEL), lambda b, t: (0, 0)),
            pl.BlockSpec((1, D_MODEL), lambda b, t: (0, 0)),
            pl.BlockSpec((D_MODEL, PROJ_PAD), lambda b, t: (0, 0)),
            pl.BlockSpec((1, PROJ_PAD), lambda b, t: (0, 0)),
            pl.BlockSpec((3, TILE, LANES), lambda b, t: (0, t, 0)),
            pl.BlockSpec((3, TILE, LANES), lambda b, t: (0, t, 0)),
        ],
        out_specs=out_specs,
        out_shape=out_shape,
        compiler_params=_params("arbitrary", "arbitrary"),
        name="inproj",
    )(x, ctx, mod_l, norm_g.reshape(1, D_MODEL), w_r, b_r.reshape(1, PROJ_PAD), tab_a, tab_c)


def _diffattn_kernel(q_ref, qn_ref, k_ref, vt_ref, lam_ref, g_ref, o_ref, acc_ref, s_ref, m8_ref,
                     *, lam_init, q_tile0):
    step = pl.program_id(1)
    qb = step + q_tile0
    lp = lam_ref[...]
    lam = (jnp.exp(jnp.sum(lp[0:1] * lp[1:2], axis=-1, keepdims=True))
           - jnp.exp(jnp.sum(lp[2:3] * lp[3:4], axis=-1, keepdims=True)) + lam_init)
    q = q_ref[...]
    lane = _iota((1, BRANCH), 1)
    n_pairs = 2 * HEADS
    sub = TILE // 8
    ones_rows = jnp.ones((ONES_ROWS, TILE), BF16)

    def pair_q(qv, hm):
        return jnp.where(lane // DA_QK == hm, qv, jnp.zeros_like(qv))

    def logits(qm, nk, buf):
        m8 = None
        half = max(nk // 2, TILE)
        for r0 in range(0, nk, half):
            st = _dot_nt(k_ref[r0:r0 + half, :], qm)
            s_ref[buf, r0:r0 + half, :] = st
            mh = jnp.max(st.reshape(half // 8, 8, TILE), axis=0)
            m8 = mh if m8 is None else jnp.maximum(m8, mh)
        return m8

    def value_tile(hm, j, mb, ot):
        h = hm // 2
        st = s_ref[hm % 2, j * TILE:(j + 1) * TILE, :]
        e = jnp.exp2(st.reshape(sub, 8, TILE) - mb[None])
        vt = vt_ref[h * HEAD_DIM:(h + 1) * HEAD_DIM, j * TILE:(j + 1) * TILE]
        vt1 = jnp.concatenate([vt, ones_rows], axis=0)
        return ot + _dot(vt1, e.reshape(TILE, TILE).astype(BF16))

    def attend(nk, own_first_logits):
        n_kt = nk // TILE
        m8 = logits(pair_q(q, 0), nk, 0) if own_first_logits else m8_ref[...]
        for hm in range(n_pairs):
            mb = jnp.broadcast_to(jnp.max(m8, axis=0, keepdims=True), (8, TILE))
            if hm + 1 < n_pairs:
                m8 = logits(pair_q(q, hm + 1), nk, (hm + 1) % 2)
            else:
                m8_ref[...] = logits(pair_q(qn_ref[...], 0), T_ALL, 0)
            ot = jnp.zeros((HEAD_DIM + ONES_ROWS, TILE), F32)
            for j in range(n_kt):
                ot = value_tile(hm, j, mb, ot)
            l = ot[HEAD_DIM:HEAD_DIM + 1, :]
            ot = ot[0:HEAD_DIM, :]
            rows = slice((hm // 2) * HEAD_DIM, (hm // 2 + 1) * HEAD_DIM)
            if hm % 2 == 0:
                acc_ref[rows, :] = ot * (1.0 / l)
            else:
                acc_ref[rows, :] -= ot * (lam / l)
        o_ref[...] = (_head_rmsnorm(acc_ref[...].T, g_ref[...]) * (1.0 - lam_init)).astype(BF16)

    if q_tile0 == 0:
        @pl.when(qb == 0)
        def _():
            attend(CTX_LEN, True)
    else:
        @pl.when(step == 0)
        def _():
            m8_ref[...] = logits(pair_q(q, 0), T_ALL, 0)

    @pl.when(qb > 0)
    def _():
        attend(T_ALL, False)


def _diffattn_call(pa, vta, lam_p, g_row, layer_idx, need_ctx):
    bsz = pa.shape[0]
    q_tile0 = 0 if need_ctx else 1
    lam_init = 0.8 - 0.6 * math.exp(-0.3 * layer_idx)
    return pl.pallas_call(
        functools.partial(_diffattn_kernel, lam_init=lam_init, q_tile0=q_tile0),
        grid=(bsz, N_TILES - q_tile0),
        in_specs=[
            pl.BlockSpec((None, TILE, BRANCH), lambda b, t: (b, t + q_tile0, 0)),
            pl.BlockSpec((None, TILE, BRANCH), lambda b, t: (b, jnp.minimum(t + q_tile0 + 1, N_TILES - 1), 0)),
            pl.BlockSpec((None, T_ALL, BRANCH), lambda b, t: (b, 0, 1)),
            pl.BlockSpec((None, BRANCH, T_ALL), lambda b, t: (b, 0, 0)),
            pl.BlockSpec((4, DA_QK), lambda b, t: (0, 0)),
            pl.BlockSpec((1, BRANCH), lambda b, t: (0, 0)),
        ],
        out_specs=pl.BlockSpec((None, TILE, BRANCH), lambda b, t: (b, t, 0)),
        out_shape=jax.ShapeDtypeStruct((bsz, (N_TILES - q_tile0) * TILE, BRANCH), BF16),
        scratch_shapes=[pltpu.VMEM((BRANCH, TILE), F32), pltpu.VMEM((2, T_ALL, TILE), F32),
                        pltpu.VMEM((8, TILE), F32)],
        compiler_params=_params("arbitrary", "arbitrary"),
        name="diffattn",
    )(pa, pa, pa, vta, lam_p, g_row)


BAND = 2 * TILE


def _window_kernel(q_ref, k_ref, vt_ref, sink_ref, o_ref, acc_ref, s_ref, *, q_tile0):
    qb = pl.program_id(1) + q_tile0
    lane = _iota((1, KV_WIDTH), 1)
    ones_rows = jnp.ones((ONES_ROWS, TILE), BF16)
    group = HEADS // (KV_WIDTH // HEAD_DIM)

    def attend(band):
        if band:
            a = (qb - 1) * TILE
            start = jnp.clip(a - SW_WINDOW, 0, SEQ - BAND)
            row0 = pl.multiple_of(CTX_LEN + start, SW_WINDOW)
            kb = k_ref[pl.ds(row0, BAND), :]
            kpos = start + _iota((BAND, 1), 0)
            qpos = a + _iota((1, TILE), 1)
            valid = jnp.abs(qpos - kpos) <= SW_WINDOW
        sinks, maxes = [], []
        for h in range(HEADS):
            kvh, g = h // group, h % group
            qg = q_ref[:, g * KV_WIDTH:(g + 1) * KV_WIDTH]
            qm = jnp.where(lane // HEAD_DIM == kvh, qg, jnp.zeros_like(qg))
            sink = sink_ref[:, h:h + 1] * LOG2E
            sc = _dot_nt(k_ref[0:CTX_LEN, :], qm)
            s_ref[h, 0:CTX_LEN, :] = sc
            m = jnp.maximum(jnp.max(sc, axis=0, keepdims=True), sink)
            if band:
                sb = jnp.where(valid, _dot_nt(kb, qm), NEG_BIG)
                s_ref[h, CTX_LEN:CTX_LEN + BAND, :] = sb
                m = jnp.maximum(m, jnp.max(sb, axis=0, keepdims=True))
            sinks.append(sink)
            maxes.append(m)
        for h in range(HEADS):
            m = maxes[h]
            kv_rows = slice((h // group) * HEAD_DIM, (h // group + 1) * HEAD_DIM)
            vt1 = jnp.concatenate([vt_ref[kv_rows, 0:CTX_LEN], ones_rows], axis=0)
            pv = _dot(vt1, jnp.exp2(s_ref[h, 0:CTX_LEN, :] - m).astype(BF16))
            if band:
                vtb = vt_ref[kv_rows, pl.ds(row0, BAND)]
                ones_b = jnp.ones((ONES_ROWS, BAND), BF16)
                eb = jnp.exp2(s_ref[h, CTX_LEN:CTX_LEN + BAND, :] - m).astype(BF16)
                pv = pv + _dot(jnp.concatenate([vtb, ones_b], axis=0), eb)
            l = pv[HEAD_DIM:HEAD_DIM + 1, :] + jnp.exp2(sinks[h] - m)
            acc_ref[h * HEAD_DIM:(h + 1) * HEAD_DIM, :] = pv[0:HEAD_DIM, :] * (1.0 / l)
        o_ref[...] = acc_ref[...].T.astype(BF16)

    @pl.when(qb == 0)
    def _():
        attend(False)

    @pl.when(qb > 0)
    def _():
        attend(True)


def _window_call(pc, vtc, sink_row, need_ctx):
    bsz = pc.shape[0]
    q_tile0 = 0 if need_ctx else 1
    n_q = N_TILES - q_tile0
    return pl.pallas_call(
        functools.partial(_window_kernel, q_tile0=q_tile0),
        grid=(bsz, n_q),
        in_specs=[
            pl.BlockSpec((None, TILE, BRANCH), lambda b, t: (b, t + q_tile0, 0)),
            pl.BlockSpec((None, T_ALL, KV_WIDTH), lambda b, t: (b, 0, BRANCH // KV_WIDTH)),
            pl.BlockSpec((None, KV_WIDTH, T_ALL), lambda b, t: (b, 0, 0)),
            pl.BlockSpec((1, HEADS), lambda b, t: (0, 0)),
        ],
        out_specs=pl.BlockSpec((None, TILE, BRANCH), lambda b, t: (b, t, 0)),
        out_shape=jax.ShapeDtypeStruct((bsz, n_q * TILE, BRANCH), BF16),
        scratch_shapes=[pltpu.VMEM((BRANCH, TILE), F32), pltpu.VMEM((HEADS, CTX_LEN + BAND, TILE), F32)],
        compiler_params=_params("arbitrary", "arbitrary"),
        name="window",
    )(pc, pc, vtc, sink_row)


CH_PER_TILE = TILE // HG_CHUNK
HG_LEVELS = 2
HG_STATE = HG_CHUNK * 2 ** HG_LEVELS
ST_PER_TILE = TILE // HG_STATE
HG_UNROLL = 4


def _hgrn_gates(z, lb_terms):
    log2_ksig = _log_sigmoid(-z) * LOG2E
    if lb_terms is None:
        return _log_sigmoid(z) * LOG2E, _sigmoid(-z), log2_ksig
    lb, log_lb, log_1m = lb_terms
    bt = log_1m + _log_sigmoid(z)
    mx = jnp.maximum(log_lb, bt)
    log_f = mx + jnp.log(jnp.exp(log_lb - mx) + jnp.exp(bt - mx))
    return log_f * LOG2E, (1.0 - lb) * _sigmoid(-z), log_1m * LOG2E + log2_ksig


def _hgrn_intra(direction, q3, c3, u3, v3, head_ones):
    half = HG_CHUNK // 2
    slabs, meta = [], []
    for s in range(HG_CHUNK):
        us = jnp.broadcast_to(u3[:, s:s + 1, :], (CH_PER_TILE, half, BRANCH))
        for g in range(2):
            lo_row, hi_row = half * g, half * g + half - 1
            if direction == 0:
                none_valid, all_valid = hi_row < s, lo_row >= s
            else:
                none_valid, all_valid = lo_row > s, hi_row <= s
            if none_valid:
                continue
            d = c3[:, half * g:half * (g + 1), :] - us
            if not all_valid:
                row = _iota((1, half, 1), 1) + half * g
                d = jnp.where((row >= s) if direction == 0 else (row <= s), d, NEG_BIG)
            x = q3[:, half * g:half * (g + 1), :] * jnp.exp2(d)
            slabs.append(x.reshape(CH_PER_TILE * half, BRANCH).astype(BF16))
            meta.append((s, g))
    a_all = _dot(jnp.concatenate(slabs, axis=0), head_ones)
    o = [jnp.zeros((CH_PER_TILE, half, BRANCH), F32) for _ in range(2)]
    n = CH_PER_TILE * half
    vs = None
    for i, (s, g) in enumerate(meta):
        if i == 0 or meta[i - 1][0] != s:
            vs = jnp.broadcast_to(v3[:, s:s + 1, :], (CH_PER_TILE, half, BRANCH))
        o[g] = o[g] + a_all[i * n:(i + 1) * n].reshape(CH_PER_TILE, half, BRANCH) * vs
    return jnp.concatenate(o, axis=1).reshape(TILE, BRANCH)


def _hgrn_kernel(p_ref, lb_ref, g_ref, o_ref, qt_ref, ke_ref, vb_ref, dec_ref, of_ref, oi_ref, st_ref,
                 *, layer_idx):
    rr = _iota((TILE, TILE), 0)
    cc = _iota((TILE, TILE), 1)
    same_chunk = (rr // HG_CHUNK) == (cc // HG_CHUNK)
    chunk_ones = same_chunk.astype(BF16)
    tris = [(same_chunk & (cc <= rr)).astype(BF16),
            (same_chunk & (cc >= rr)).astype(BF16)]
    head_ones = _head_sum_matrix()
    head_mask = (rr // HEAD_DIM) == (cc // HEAD_DIM)
    lane = _iota((1, BRANCH), 1)
    row_in = _iota((TILE, 1), 0)

    def shift_rows(a, n):
        n = n % TILE
        return jnp.concatenate([a[TILE - n:, :], a[:TILE - n, :]], axis=0)

    lb_terms = None
    if layer_idx > 0:
        lbp = lb_ref[...]
        lbp = lbp - jnp.max(lbp, axis=0, keepdims=True)
        sm = jnp.exp(lbp)
        sm = sm / jnp.sum(sm, axis=0, keepdims=True)
        lb = jnp.sum(sm[1:layer_idx + 1], axis=0, keepdims=True)
        lb_terms = (lb, jnp.log(lb), jnp.log(1.0 - lb))

    def tile_body(t, carry):
        rows = pl.ds(pl.multiple_of(t * TILE, TILE), TILE)
        q = p_ref[rows, 0:BRANCH] * (HEAD_DIM ** -0.5)
        v = p_ref[rows, 3 * BRANCH:4 * BRANCH]
        vbf = v.astype(BF16)
        vb_ref[rows, :] = vbf
        v_heads = jnp.concatenate([jnp.where(lane // HEAD_DIM == h, vbf, jnp.zeros_like(vbf))
                                   for h in range(HEADS)], axis=0)
        shape3 = (CH_PER_TILE, HG_CHUNK, BRANCH)
        o_intra = None
        for direction in (0, 1):
            z = p_ref[rows, (1 + direction) * BRANCH:(2 + direction) * BRANCH]
            log2_f, kk, log2_k = _hgrn_gates(z, lb_terms)
            cum = _dot_exact_l(tris[direction], log2_f)
            tot = _dot_exact_l(chunk_ones, log2_f)
            o_d = _hgrn_intra(direction, q.reshape(shape3), cum.reshape(shape3),
                              (cum - log2_k).reshape(shape3), v.reshape(shape3), head_ones)
            a_heads = [jnp.zeros((TILE, TILE), F32) for _ in range(HEADS)]
            g = HG_CHUNK
            for _ in range(HG_LEVELS):
                qt = (q * jnp.exp2(cum)).astype(BF16)
                ke = (kk * jnp.exp2(tot - cum)).astype(BF16)
                later_r = ((rr % (2 * g)) >= g) if direction == 0 else ((rr % (2 * g)) < g)
                later_c = ((cc % (2 * g)) >= g) if direction == 0 else ((cc % (2 * g)) < g)
                pair = ((rr // (2 * g)) == (cc // (2 * g))) & later_r & jnp.logical_not(later_c)
                for h in range(HEADS):
                    qh = jnp.where(lane // HEAD_DIM == h, qt, jnp.zeros_like(qt))
                    a_heads[h] = jnp.where(pair, _dot_nt(qh, ke), a_heads[h])
                later_row = ((row_in % (2 * g)) >= g) if direction == 0 else ((row_in % (2 * g)) < g)
                sign = 1 if direction == 0 else -1
                tot_other = jnp.where(later_row, shift_rows(tot, sign * g), shift_rows(tot, -sign * g))
                cum = cum + jnp.where(later_row, tot_other, 0.0)
                tot = tot + tot_other
                g *= 2
            if HG_LEVELS:
                a_cat = jnp.concatenate([a.astype(BF16) for a in a_heads], axis=1)
                o_d = o_d + _dot(a_cat, v_heads)
            qt_ref[direction, rows, :] = (q * jnp.exp2(cum)).astype(BF16)
            ke_ref[direction, rows, :] = (kk * jnp.exp2(tot - cum)).astype(BF16)
            dec_ref[direction, t] = jnp.exp2(tot.reshape(ST_PER_TILE, HG_STATE, BRANCH)[:, 0, :])
            o_intra = o_d if o_intra is None else o_intra + o_d
        of_ref[rows, :] = o_intra
        return carry

    lax.fori_loop(0, N_TILES, tile_body, 0)

    st_ref[...] = jnp.zeros_like(st_ref)
    n_ctx, n_all = CTX_LEN // HG_STATE, T_ALL // HG_STATE

    def state_step(i, direction):
        if direction == 0:
            c = i
        else:
            c = jnp.where(i < n_ctx, n_ctx - 1 - i, n_all - 1 + n_ctx - i)
        rows = pl.ds(pl.multiple_of(c * HG_STATE, HG_STATE), HG_STATE)
        st = st_ref[direction]
        oi_ref[direction, rows, :] = _dot_nt(qt_ref[direction, rows, :], st.astype(BF16))
        ds = _dot_tn(vb_ref[rows, :], ke_ref[direction, rows, :])
        dec = dec_ref[direction, c // ST_PER_TILE, pl.ds(c % ST_PER_TILE, 1), :]
        st_ref[direction] = st * dec + jnp.where(head_mask, ds, 0.0)

    def state_body(it, carry):
        for u in range(HG_UNROLL):
            for direction in (0, 1):
                state_step(it * HG_UNROLL + u, direction)
        return carry

    lax.fori_loop(0, n_all // HG_UNROLL, state_body, 0)

    o_ref[...] = _head_rmsnorm(of_ref[...] + oi_ref[0] + oi_ref[1], g_ref[...]).astype(BF16)


def _hgrn_call(pb, hg_lb, g_row, layer_idx):
    bsz = pb.shape[0]
    return pl.pallas_call(
        functools.partial(_hgrn_kernel, layer_idx=layer_idx),
        grid=(bsz,),
        in_specs=[
            pl.BlockSpec((None, T_ALL, SEG_B), lambda b: (b, 0, 0)),
            pl.BlockSpec((DEPTH, BRANCH), lambda b: (0, 0)),
            pl.BlockSpec((1, BRANCH), lambda b: (0, 0)),
        ],
        out_specs=pl.BlockSpec((None, T_ALL, BRANCH), lambda b: (b, 0, 0)),
        out_shape=jax.ShapeDtypeStruct((bsz, T_ALL, BRANCH), BF16),
        scratch_shapes=[pltpu.VMEM((2, T_ALL, BRANCH), BF16), pltpu.VMEM((2, T_ALL, BRANCH), BF16),
                        pltpu.VMEM((T_ALL, BRANCH), BF16), pltpu.VMEM((2, N_TILES, ST_PER_TILE, BRANCH), F32),
                        pltpu.VMEM((T_ALL, BRANCH), F32), pltpu.VMEM((2, T_ALL, BRANCH), F32),
                        pltpu.VMEM((2, BRANCH, BRANCH), F32)],
        compiler_params=_params("arbitrary"),
        name="hgrn",
    )(pb, hg_lb, g_row)


ML_UNROLL = 3


def _mlstm_logits(direction, c, p_ref, vt_ref, g_ref, gt_ref, ct_ref, n_ref, m_ref, consts):
    tri, valid, _, lane, row16, lane16 = consts
    rows = pl.ds(pl.multiple_of(c * TILE, TILE), TILE)
    q = p_ref[rows, 0:BRANCH]
    k = p_ref[rows, BRANCH:2 * BRANCH]
    g = g_ref[rows, :]
    gt = gt_ref[:, rows]
    cum = _dot_exact_l(tri, _log_sigmoid(g) * LOG2E)
    cum_t = _dot_exact_nt(_log_sigmoid(gt[8:16, :]) * LOG2E, tri)
    ig_t = gt[0:8, :] * LOG2E
    ct = ct_ref[direction]
    n0 = n_ref[direction]
    n_hi = n0.astype(BF16).astype(F32)
    n_lo = n0 - n_hi
    n_rows = (jnp.where((row16 < HEADS) & (lane16 == row16), n_hi, 0.0)
              + jnp.where((row16 >= HEADS) & (lane16 == row16 - HEADS), n_lo, 0.0)).astype(BF16)
    inter_all = _dot_nt(jnp.concatenate([ct.astype(BF16), n_rows], axis=0), q)
    heads = []
    for h in range(HEADS):
        r = HEADS * direction + h
        cumr = cum_t[r:r + 1, :]
        ucol = g[:, r:r + 1] * LOG2E - cum[:, 2 * HEADS + r:2 * HEADS + r + 1]
        toth = cumr[:, TILE - 1:TILE] if direction == 0 else cumr[:, 0:1]
        m0h = m_ref[direction, :, h:h + 1]
        logd = jnp.where(valid, cumr + ucol, NEG_BIG)
        inter = cumr + m0h
        m_t = jnp.maximum(jnp.max(logd, axis=0, keepdims=True), inter)
        qm = jnp.where(lane // HEAD_DIM == h, q, jnp.zeros_like(q))
        heads.append(dict(qk=_dot_nt(k, qm), logd=logd, m_t=m_t, g0=jnp.exp2(inter - m_t),
                          a_row=toth - cumr + ig_t[r:r + 1, :], carry=toth + m0h))
    return dict(direction=direction, rows=rows, k=k, ct=ct, n0=n0, inter_all=inter_all, heads=heads)


def _mlstm_outputs(cx, vt_ref, ht_ref, ct_ref, n_ref, m_ref, consts):
    _, _, head_mask, lane, row16, _ = consts
    direction, rows, k, inter_all = cx["direction"], cx["rows"], cx["k"], cx["inter_all"]
    vt = vt_ref[:, rows]
    ones_rows = jnp.ones((ONES_ROWS, TILE), BF16)
    w_rows, sp_row = [], jnp.zeros((1, BRANCH), F32)
    for h, hd in enumerate(cx["heads"]):
        m_t, g0 = hd["m_t"], hd["g0"]
        s_t = hd["qk"] * jnp.exp2(hd["logd"] - m_t)
        vt1 = jnp.concatenate([vt[h * HEAD_DIM:(h + 1) * HEAD_DIM, :], ones_rows], axis=0)
        pv = _dot(vt1, s_t.astype(BF16))
        num = pv[0:HEAD_DIM, :] + g0 * inter_all[h * HEAD_DIM:(h + 1) * HEAD_DIM, :]
        den = pv[HEAD_DIM:HEAD_DIM + 1, :] + g0 * (inter_all[BRANCH + h:BRANCH + h + 1, :]
                                                   + inter_all[BRANCH + HEADS + h:BRANCH + HEADS + h + 1, :])
        ht_ref[direction, h * HEAD_DIM:(h + 1) * HEAD_DIM, rows] = (
            num / jnp.maximum(jnp.abs(den), jnp.exp2(-m_t)))
        a_row = hd["a_row"]
        m_loc = jnp.max(a_row, axis=1, keepdims=True)
        m_new = jnp.maximum(hd["carry"], m_loc)
        sp = jnp.exp2(hd["carry"] - m_new)
        w_rows.append(jnp.exp2(a_row - m_loc) * jnp.exp2(m_loc - m_new))
        sp_row = sp_row + jnp.where(lane // HEAD_DIM == h, sp, 0.0)
        m_ref[direction, :, h:h + 1] = m_new

    w_block = jnp.concatenate([jnp.broadcast_to(w, (HEAD_DIM, TILE)) for w in w_rows], axis=0)
    vtw = (vt.astype(F32) * w_block).astype(BF16)
    w16 = jnp.zeros((ONES_ROWS, TILE), F32)
    for h in range(HEADS):
        w_hi = w_rows[h].astype(BF16).astype(F32)
        w16 = w16 + jnp.where(row16 == h, w_hi, 0.0) + jnp.where(row16 == HEADS + h, w_rows[h] - w_hi, 0.0)
    dall = _dot(jnp.concatenate([vtw, w16.astype(BF16)], axis=0), k)
    ct_ref[direction] = cx["ct"] * sp_row + jnp.where(head_mask, dall[0:BRANCH, :], 0.0)
    dn = jnp.zeros((1, BRANCH), F32)
    for h in range(HEADS):
        dn = dn + jnp.where(lane // HEAD_DIM == h,
                            dall[BRANCH + h:BRANCH + h + 1, :] + dall[BRANCH + HEADS + h:BRANCH + HEADS + h + 1, :], 0.0)
    n_ref[direction] = cx["n0"] * sp_row + dn


def _mlstm_kernel(p_ref, vt_ref, g_ref, gt_ref, gain_ref, o_ref, ht_ref, ct_ref, n_ref, m_ref):
    rr = _iota((TILE, TILE), 0)
    cc = _iota((TILE, TILE), 1)
    head_mask = (rr // HEAD_DIM) == (cc // HEAD_DIM)
    lane = _iota((1, BRANCH), 1)
    row16 = _iota((ONES_ROWS, BRANCH), 0)
    lane16 = _iota((ONES_ROWS, BRANCH), 1) // HEAD_DIM
    consts = []
    for direction in (0, 1):
        tri = ((cc <= rr) if direction == 0 else (cc >= rr)).astype(BF16)
        valid = (rr <= cc) if direction == 0 else (rr >= cc)
        consts.append((tri, valid, head_mask, lane, row16, lane16))
    ct_ref[...] = jnp.zeros_like(ct_ref)
    n_ref[...] = jnp.zeros_like(n_ref)
    m_ref[...] = jnp.zeros_like(m_ref)

    def body(it, carry):
        for u in range(ML_UNROLL):
            i = it * ML_UNROLL + u
            chunk = (i, jnp.where(i == 0, 0, N_TILES - i))
            cxs = [_mlstm_logits(d, chunk[d], p_ref, vt_ref, g_ref, gt_ref, ct_ref, n_ref, m_ref, consts[d])
                   for d in (0, 1)]
            for d in (0, 1):
                _mlstm_outputs(cxs[d], vt_ref, ht_ref, ct_ref, n_ref, m_ref, consts[d])
        return carry

    lax.fori_loop(0, N_TILES // ML_UNROLL, body, 0)

    def out_body(t, carry):
        rows = pl.ds(pl.multiple_of(t * TILE, TILE), TILE)
        o_ref[rows, :] = _head_rmsnorm((ht_ref[0, :, rows] + ht_ref[1, :, rows]).T,
                                       gain_ref[...]).astype(BF16)
        return carry

    lax.fori_loop(0, N_TILES, out_body, 0)


def _mlstm_call(pd, vtd, pg, pgt, g_row):
    bsz = pd.shape[0]
    return pl.pallas_call(
        _mlstm_kernel,
        grid=(bsz,),
        in_specs=[
            pl.BlockSpec((None, T_ALL, 2 * BRANCH), lambda b: (b, 0, 0)),
            pl.BlockSpec((None, BRANCH, T_ALL), lambda b: (b, 0, 0)),
            pl.BlockSpec((None, T_ALL, SEG_G), lambda b: (b, 0, 0)),
            pl.BlockSpec((None, SEG_G, T_ALL), lambda b: (b, 0, 0)),
            pl.BlockSpec((1, BRANCH), lambda b: (0, 0)),
        ],
        out_specs=pl.BlockSpec((None, T_ALL, BRANCH), lambda b: (b, 0, 0)),
        out_shape=jax.ShapeDtypeStruct((bsz, T_ALL, BRANCH), BF16),
        scratch_shapes=[pltpu.VMEM((2, BRANCH, T_ALL), F32), pltpu.VMEM((2, BRANCH, BRANCH), F32),
                        pltpu.VMEM((2, 1, BRANCH), F32), pltpu.VMEM((2, 1, LANES), F32)],
        compiler_params=_params("arbitrary"),
        name="mlstm",
    )(pd, vtd, pg, pgt, g_row)


def _outproj_kernel(x_ref, ctx_ref, ya_ref, yb_ref, yc_ref, yd_ref, po_ref, mod_ref, w_ref, fg_ref,
                    *out_refs, bsz, last):
    t = pl.program_id(1)
    is_ctx = jnp.logical_and(t == 0, not last)
    row = jnp.where(is_ctx, bsz, pl.program_id(0))
    gate_mod = mod_ref[pl.ds(row, 1), 2 * D_MODEL:3 * D_MODEL]
    po = po_ref[...].astype(F32)
    yd = yd_ref[...].astype(F32) * po[:, 0:BRANCH]
    mixed = jnp.concatenate([ya_ref[...].astype(F32), yb_ref[...].astype(F32), yc_ref[...].astype(F32), yd],
                            axis=-1)
    mixed = (mixed * po[:, BRANCH:]).astype(BF16)
    delta = gate_mod * _dot(mixed, w_ref[...])
    if last:
        xn = x_ref[...] + delta
        ms = jnp.mean(xn * xn, axis=-1, keepdims=True)
        out_refs[0][...] = xn * lax.rsqrt(ms + NORM_EPS) * fg_ref[...]
    else:
        x_out_ref, ctx_out_ref = out_refs

        @pl.when(t == 0)
        def _():
            ctx_out_ref[...] = ctx_ref[...] + delta

        @pl.when(t > 0)
        def _():
            x_out_ref[...] = x_ref[...] + delta


def _outproj_call(x, ctx, ya, yb, yc, yd, po, mod_l, w_out_bf, final_g, last):
    bsz = x.shape[0]
    tile0 = 1 if last else 0
    rows = mod_l.shape[0]

    def tok(width, arr):
        off = tile0 if arr.shape[1] == T_ALL else 0
        return pl.BlockSpec((None, TILE, width), lambda b, t: (b, t + off, 0))

    lat_spec = pl.BlockSpec((None, TILE, D_MODEL), lambda b, t: (b, jnp.maximum(t + tile0 - 1, 0), 0))
    ctx_spec = pl.BlockSpec((None, CTX_LEN, D_MODEL), lambda b, t: (b, 0, 0))
    lat_shape = jax.ShapeDtypeStruct((bsz, SEQ, D_MODEL), F32)
    ctx_shape = jax.ShapeDtypeStruct((bsz, CTX_LEN, D_MODEL), F32)
    return pl.pallas_call(
        functools.partial(_outproj_kernel, bsz=bsz, last=last),
        grid=(bsz, N_TILES - tile0),
        in_specs=[lat_spec, ctx_spec, tok(BRANCH, ya), tok(BRANCH, yb), tok(BRANCH, yc), tok(BRANCH, yd),
                  tok(SEG_O, po),
                  pl.BlockSpec((rows, 3 * D_MODEL), lambda b, t: (0, 0)),
                  pl.BlockSpec((D_MODEL, D_MODEL), lambda b, t: (0, 0)),
                  pl.BlockSpec((1, D_MODEL), lambda b, t: (0, 0))],
        out_specs=lat_spec if last else [lat_spec, ctx_spec],
        out_shape=lat_shape if last else [lat_shape, ctx_shape],
        compiler_params=_params("arbitrary", "arbitrary"),
        name="outproj",
    )(x, ctx, ya, yb, yc, yd, po, mod_l, w_out_bf, final_g.reshape(1, D_MODEL))


def _rope_tables(dim):
    quarter = dim // 4
    half = dim // 2
    pos = np.arange(SEQ)
    row = (pos // GRID_W).astype(np.float32)
    col = (pos % GRID_W).astype(np.float32)
    inv = (np.float32(ROPE_BASE) ** (-np.arange(0, half, 2, dtype=np.float32) / np.float32(half))).astype(np.float32)
    ang_r = row[:, None] * inv[None, :]
    ang_c = col[:, None] * inv[None, :]
    lane = np.arange(LANES) % dim
    part = lane // quarter
    freq = lane % quarter
    ang = np.where(part[None, :] < 2, ang_r[:, freq], ang_c[:, freq]).astype(np.float32)
    cos = np.cos(ang)
    sin = np.sin(ang)
    first = (part % 2 == 0)[None, :]
    s_next = np.where(first, -sin, 0.0)
    s_prev = np.where(first, 0.0, sin)
    tab = np.stack([cos, s_next, s_prev]).astype(np.float32)
    ident = np.stack([np.ones((CTX_LEN, LANES)), np.zeros((CTX_LEN, LANES)),
                      np.zeros((CTX_LEN, LANES))]).astype(np.float32)
    return jnp.asarray(np.concatenate([ident, tab], axis=1))


def _relayout_in_proj(w, bias):
    def cols(a):
        seg_a = a[..., 0:768]
        seg_b = a[..., 768:1792]
        qc = a[..., 1792:2048]
        qc = jnp.concatenate([qc[..., 0:64], qc[..., 128:192], qc[..., 64:128], qc[..., 192:256]], axis=-1)
        kvc = a[..., 2048:2304]
        seg_d = a[..., 2304:3072]
        gates = a[..., 3072:3088]
        seg_o = a[..., 3088:4368]
        pad = jnp.zeros(a.shape[:-1] + (SEG_G - 16,), a.dtype)
        return jnp.concatenate([seg_a, qc, kvc, seg_b, seg_d, gates, pad, seg_o], axis=-1)
    return cols(w), cols(bias)


def kernel(x, c, ctx, c_ctx, w_mod, b_mod, norm_g, w_in, b_in, diff_lam, diff_g, hg_lb, hg_g,
           sw_sink, ml_g, w_out, final_g):
    bsz = x.shape[0]
    tab_a = _rope_tables(DA_QK)
    tab_c = _rope_tables(HEAD_DIM)
    rows = ((bsz + 1 + 7) // 8) * 8
    cc = jnp.concatenate([c, c_ctx[None, :], jnp.zeros((rows - bsz - 1, D_MODEL), F32)], axis=0)
    mod = _mod_call(cc, w_mod, b_mod)
    tile4 = lambda g: jnp.tile(g, HEADS).reshape(1, BRANCH)
    w_in_bf = w_in.astype(BF16)
    w_out_bf = w_out.astype(BF16)
    for l in range(DEPTH):
        last = l == DEPTH - 1
        w_r, b_r = _relayout_in_proj(w_in_bf[l], b_in[l])
        pa, pc, pb, pd, pg, po, vta, vtc, vtd, pgt = _inproj_call(x, ctx, mod[l], norm_g[l], w_r, b_r,
                                                                 tab_a, tab_c)
        ya = _diffattn_call(pa, vta, diff_lam[l], tile4(diff_g[l]), l, not last)
        yb = _hgrn_call(pb, hg_lb, tile4(hg_g[l]), l)
        yc = _window_call(pc, vtc, sw_sink[l].reshape(1, HEADS), not last)
        yd = _mlstm_call(pd, vtd, pg, pgt, tile4(ml_g[l]))
        res = _outproj_call(x, ctx, ya, yb, yc, yd, po, mod[l], w_out_bf[l], final_g, last)
        if last:
            return res
        x, ctx = res
```

```python
import functools
import math

import numpy as np
import jax
import jax.numpy as jnp
from jax import lax
from jax.experimental import pallas as pl
from jax.experimental.pallas import tpu as pltpu

F32 = jnp.float32
BF16 = jnp.bfloat16

D_MODEL = 1024
SEQ = 2048
CTX_LEN = 256
T_ALL = CTX_LEN + SEQ
GRID_W = 64
DEPTH = 2
HEADS = 4
HEAD_DIM = 64
BRANCH = HEADS * HEAD_DIM
DA_QK = 32
SW_WINDOW = 128
HG_CHUNK = 16
ROPE_BASE = 10000.0
NORM_EPS = 1e-6
NEG_BIG = -1e30
LOG2E = math.log2(math.e)

TILE = 256
N_TILES = T_ALL // TILE
LANES = 128
ONES_ROWS = 16

SEG_A = 3 * BRANCH
KV_WIDTH = 2 * HEAD_DIM
SEG_C = BRANCH + 2 * KV_WIDTH
SEG_B = 4 * BRANCH
SEG_D = 3 * BRANCH
SEG_G = LANES
SEG_O = BRANCH + D_MODEL
OFF_A = 0
OFF_C = OFF_A + SEG_A
OFF_B = OFF_C + SEG_C
OFF_D = OFF_B + SEG_B
OFF_G = OFF_D + SEG_D
OFF_O = OFF_G + SEG_G
PROJ_PAD = OFF_O + SEG_O

VMEM_LIMIT = 56 * 1024 * 1024


def _params(*sem):
    return pltpu.CompilerParams(dimension_semantics=sem, vmem_limit_bytes=VMEM_LIMIT)


def _dot(a, b):
    return jnp.dot(a, b, preferred_element_type=F32)


def _dot_nt(a, b):
    return lax.dot_general(a, b, (((1,), (1,)), ((), ())), preferred_element_type=F32)


def _dot_tn(a, b):
    return lax.dot_general(a, b, (((0,), (0,)), ((), ())), preferred_element_type=F32)


def _split3(x):
    x1 = x.astype(BF16)
    r1 = x - x1.astype(F32)
    x2 = r1.astype(BF16)
    x3 = (r1 - x2.astype(F32)).astype(BF16)
    return x1, x2, x3


def _dot_exact_l(m01, x):
    x1, x2, x3 = _split3(x)
    return _dot(m01, x1) + _dot(m01, x2) + _dot(m01, x3)


def _dot_exact_r(x, m01):
    x1, x2, x3 = _split3(x)
    return _dot(x1, m01) + _dot(x2, m01) + _dot(x3, m01)


def _dot_exact_nt(x, m01):
    x1, x2, x3 = _split3(x)
    return _dot_nt(x1, m01) + _dot_nt(x2, m01) + _dot_nt(x3, m01)


def _sigmoid(z):
    e = jnp.exp(-jnp.abs(z))
    r = 1.0 / (1.0 + e)
    return jnp.where(z >= 0, r, e * r)


def _log_sigmoid(z):
    return jnp.minimum(z, 0.0) - jnp.log(1.0 + jnp.exp(-jnp.abs(z)))


def _iota(shape, dim):
    return lax.broadcasted_iota(jnp.int32, shape, dim)


def _head_sum_matrix():
    r = _iota((BRANCH, BRANCH), 0) // HEAD_DIM
    c = _iota((BRANCH, BRANCH), 1) // HEAD_DIM
    return (r == c).astype(BF16)


def _head_rmsnorm(o, g_row):
    ss = _dot_exact_r(o * o, _head_sum_matrix())
    return o * lax.rsqrt(ss * (1.0 / HEAD_DIM) + NORM_EPS) * g_row


def _mod_kernel(cc_ref, w_ref, b_ref, o_ref):
    cc = cc_ref[...]
    a = (cc * _sigmoid(cc)).astype(BF16)
    o_ref[...] = _dot(a, w_ref[...].astype(BF16)) + b_ref[...]


def _mod_call(cc, w_mod, b_mod):
    rows = cc.shape[0]
    nblk = 3
    return pl.pallas_call(
        _mod_kernel,
        grid=(DEPTH, nblk),
        in_specs=[
            pl.BlockSpec((rows, D_MODEL), lambda l, j: (0, 0)),
            pl.BlockSpec((None, D_MODEL, D_MODEL), lambda l, j: (l, 0, j)),
            pl.BlockSpec((None, 1, D_MODEL), lambda l, j: (l, 0, j)),
        ],
        out_specs=pl.BlockSpec((None, rows, D_MODEL), lambda l, j: (l, 0, j)),
        out_shape=jax.ShapeDtypeStruct((DEPTH, rows, 3 * D_MODEL), F32),
        compiler_params=_params("arbitrary", "arbitrary"),
        name="mod",
    )(cc, w_mod, b_mod.reshape(DEPTH, 1, 3 * D_MODEL))


def _rope(slab, cos, sin_next, sin_prev, off):
    nxt = pltpu.roll(slab, LANES - off, 1)
    prv = pltpu.roll(slab, off, 1)
    return slab * cos + nxt * sin_next + prv * sin_prev


def _inproj_kernel(x_ref, ctx_ref, mod_ref, ng_ref, w_ref, b_ref, ta_ref, tc_ref,
                   pa_ref, pc_ref, pb_ref, pd_ref, pg_ref, po_ref, vta_ref, vtc_ref, vtd_ref, pgt_ref,
                   *, bsz):
    is_ctx = pl.program_id(1) == 0
    row = jnp.where(is_ctx, bsz, pl.program_id(0))
    x = jnp.where(is_ctx, ctx_ref[...], x_ref[...])
    mrow = mod_ref[pl.ds(row, 1), :]
    shift = mrow[:, 0:D_MODEL]
    scale = mrow[:, D_MODEL:2 * D_MODEL]
    ms = jnp.mean(x * x, axis=-1, keepdims=True)
    h = x * lax.rsqrt(ms + NORM_EPS) * ng_ref[...]
    h = (h * (1.0 + scale) + shift).astype(BF16)

    def proj(off, width):
        return _dot(h, w_ref[:, off:off + width]) + b_ref[:, off:off + width]

    def rope_seg(acc, tab_ref, off, q_scale, k_slabs):
        cos, s_next, s_prev = tab_ref[0], tab_ref[1], tab_ref[2]
        outs = []
        for j in range(2 + k_slabs):
            r = _rope(acc[:, j * LANES:(j + 1) * LANES], cos, s_next, s_prev, off)
            outs.append(r * q_scale if j < 2 else r)
        outs.append(acc[:, (2 + k_slabs) * LANES:])
        return jnp.concatenate(outs, axis=-1)

    acco = proj(OFF_O, SEG_O)
    po_ref[...] = jnp.concatenate(
        [_sigmoid(acco[:, 0:BRANCH]), acco[:, BRANCH:] * _sigmoid(acco[:, BRANCH:])], axis=-1).astype(BF16)
    acca = rope_seg(proj(OFF_A, SEG_A), ta_ref, DA_QK // 4, DA_QK ** -0.5 * LOG2E, 2)
    pa_ref[...] = acca[:, 0:2 * BRANCH].astype(BF16)
    vta_ref[...] = acca[:, 2 * BRANCH:].T.astype(BF16)
    accc = rope_seg(proj(OFF_C, SEG_C), tc_ref, HEAD_DIM // 4, HEAD_DIM ** -0.5 * LOG2E, 1)
    pc_ref[...] = accc[:, 0:BRANCH + KV_WIDTH].astype(BF16)
    vtc_ref[...] = accc[:, BRANCH + KV_WIDTH:].T.astype(BF16)
    accd = proj(OFF_D, SEG_D)
    pd_ref[...] = jnp.concatenate(
        [accd[:, 0:BRANCH], accd[:, BRANCH:2 * BRANCH] * (HEAD_DIM ** -0.5)], axis=-1).astype(BF16)
    vtd_ref[...] = accd[:, 2 * BRANCH:].T.astype(BF16)
    gates = proj(OFF_G, SEG_G)
    pg_ref[...] = gates
    pgt_ref[...] = gates.T
    pb_ref[...] = proj(OFF_B, SEG_B)


def _inproj_call(x, ctx, mod_l, norm_g, w_r, b_r, tab_a, tab_c):
    bsz = x.shape[0]
    rows = mod_l.shape[0]
    widths = [(2 * BRANCH, BF16), (BRANCH + KV_WIDTH, BF16), (SEG_B, F32), (2 * BRANCH, BF16), (SEG_G, F32),
              (SEG_O, BF16)]
    out_specs = [pl.BlockSpec((None, TILE, w), lambda b, t: (b, t, 0)) for w, _ in widths]
    out_shape = [jax.ShapeDtypeStruct((bsz, T_ALL, w), dt) for w, dt in widths]
    for rows_t, dt in ((BRANCH, BF16), (KV_WIDTH, BF16), (BRANCH, BF16), (SEG_G, F32)):
        out_specs.append(pl.BlockSpec((None, rows_t, TILE), lambda b, t: (b, 0, t)))
        out_shape.append(jax.ShapeDtypeStruct((bsz, rows_t, T_ALL), dt))
    return pl.pallas_call(
        functools.partial(_inproj_kernel, bsz=bsz),
        grid=(bsz, N_TILES),
        in_specs=[
            pl.BlockSpec((None, TILE, D_MODEL), lambda b, t: (b, jnp.maximum(t - 1, 0), 0)),
            pl.BlockSpec((None, CTX_LEN, D_MODEL), lambda b, t: (b, 0, 0)),
            pl.BlockSpec((rows, 3 * D_MODEL), lambda b, t: (0, 0)),
            pl.BlockSpec((1, D_MODEL), lambda b, t: (0, 0)),
            pl.BlockSpec((D_MODEL, PROJ_PAD), lambda b, t: (0, 0)),
            pl.BlockSpec((1, PROJ_PAD), lambda b, t: (0, 0)),
            pl.BlockSpec((3, TILE, LANES), lambda b, t: (0, t, 0)),
            pl.BlockSpec((3, TILE, LANES), lambda b, t: (0, t, 0)),
        ],
        out_specs=out_specs,
        out_shape=out_shape,
        compiler_params=_params("arbitrary", "arbitrary"),
        name="inproj",
    )(x, ctx, mod_l, norm_g.reshape(1, D_MODEL), w_r, b_r.reshape(1, PROJ_PAD), tab_a, tab_c)


def _diffattn_kernel(q_ref, qn_ref, k_ref, vt_ref, lam_ref, g_ref, o_ref, acc_ref, s_ref, m8_ref,
                     *, lam_init, q_tile0):
    step = pl.program_id(1)
    qb = step + q_tile0
    lp = lam_ref[...]
    lam = (jnp.exp(jnp.sum(lp[0:1] * lp[1:2], axis=-1, keepdims=True))
           - jnp.exp(jnp.sum(lp[2:3] * lp[3:4], axis=-1, keepdims=True)) + lam_init)
    q = q_ref[...]
    lane = _iota((1, BRANCH), 1)
    n_pairs = 2 * HEADS
    sub = TILE // 8
    ones_rows = jnp.ones((ONES_ROWS, TILE), BF16)

    def pair_q(qv, hm):
        return jnp.where(lane // DA_QK == hm, qv, jnp.zeros_like(qv))

    def logits(qm, nk, buf):
        m8 = None
        half = max(nk // 2, TILE)
        for r0 in range(0, nk, half):
            st = _dot_nt(k_ref[r0:r0 + half, :], qm)
            s_ref[buf, r0:r0 + half, :] = st
            mh = jnp.max(st.reshape(half // 8, 8, TILE), axis=0)
            m8 = mh if m8 is None else jnp.maximum(m8, mh)
        return m8

    def value_tile(hm, j, mb, ot):
        h = hm // 2
        st = s_ref[hm % 2, j * TILE:(j + 1) * TILE, :]
        e = jnp.exp2(st.reshape(sub, 8, TILE) - mb[None])
        vt = vt_ref[h * HEAD_DIM:(h + 1) * HEAD_DIM, j * TILE:(j + 1) * TILE]
        vt1 = jnp.concatenate([vt, ones_rows], axis=0)
        return ot + _dot(vt1, e.reshape(TILE, TILE).astype(BF16))

    def attend(nk, own_first_logits):
        n_kt = nk // TILE
        m8 = logits(pair_q(q, 0), nk, 0) if own_first_logits else m8_ref[...]
        for hm in range(n_pairs):
            mb = jnp.broadcast_to(jnp.max(m8, axis=0, keepdims=True), (8, TILE))
            if hm + 1 < n_pairs:
                m8 = logits(pair_q(q, hm + 1), nk, (hm + 1) % 2)
            else:
                m8_ref[...] = logits(pair_q(qn_ref[...], 0), T_ALL, 0)
            ot = jnp.zeros((HEAD_DIM + ONES_ROWS, TILE), F32)
            for j in range(n_kt):
                ot = value_tile(hm, j, mb, ot)
            l = ot[HEAD_DIM:HEAD_DIM + 1, :]
            ot = ot[0:HEAD_DIM, :]
            rows = slice((hm // 2) * HEAD_DIM, (hm // 2 + 1) * HEAD_DIM)
            if hm % 2 == 0:
                acc_ref[rows, :] = ot * (1.0 / l)
            else:
                acc_ref[rows, :] -= ot * (lam / l)
        o_ref[...] = (_head_rmsnorm(acc_ref[...].T, g_ref[...]) * (1.0 - lam_init)).astype(BF16)

    if q_tile0 == 0:
        @pl.when(qb == 0)
        def _():
            attend(CTX_LEN, True)
    else:
        @pl.when(step == 0)
        def _():
            m8_ref[...] = logits(pair_q(q, 0), T_ALL, 0)

    @pl.when(qb > 0)
    def _():
        attend(T_ALL, False)


def _diffattn_call(pa, vta, lam_p, g_row, layer_idx, need_ctx):
    bsz = pa.shape[0]
    q_tile0 = 0 if need_ctx else 1
    lam_init = 0.8 - 0.6 * math.exp(-0.3 * layer_idx)
    return pl.pallas_call(
        functools.partial(_diffattn_kernel, lam_init=lam_init, q_tile0=q_tile0),
        grid=(bsz, N_TILES - q_tile0),
        in_specs=[
            pl.BlockSpec((None, TILE, BRANCH), lambda b, t: (b, t + q_tile0, 0)),
            pl.BlockSpec((None, TILE, BRANCH), lambda b, t: (b, jnp.minimum(t + q_tile0 + 1, N_TILES - 1), 0)),
            pl.BlockSpec((None, T_ALL, BRANCH), lambda b, t: (b, 0, 1)),
            pl.BlockSpec((None, BRANCH, T_ALL), lambda b, t: (b, 0, 0)),
            pl.BlockSpec((4, DA_QK), lambda b, t: (0, 0)),
            pl.BlockSpec((1, BRANCH), lambda b, t: (0, 0)),
        ],
        out_specs=pl.BlockSpec((None, TILE, BRANCH), lambda b, t: (b, t, 0)),
        out_shape=jax.ShapeDtypeStruct((bsz, (N_TILES - q_tile0) * TILE, BRANCH), BF16),
        scratch_shapes=[pltpu.VMEM((BRANCH, TILE), F32), pltpu.VMEM((2, T_ALL, TILE), F32),
                        pltpu.VMEM((8, TILE), F32)],
        compiler_params=_params("arbitrary", "arbitrary"),
        name="diffattn",
    )(pa, pa, pa, vta, lam_p, g_row)


BAND = 2 * TILE


def _window_kernel(q_ref, k_ref, vt_ref, sink_ref, o_ref, acc_ref, s_ref, *, q_tile0):
    qb = pl.program_id(1) + q_tile0
    lane = _iota((1, KV_WIDTH), 1)
    ones_rows = jnp.ones((ONES_ROWS, TILE), BF16)
    group = HEADS // (KV_WIDTH // HEAD_DIM)

    def attend(band):
        if band:
            a = (qb - 1) * TILE
            start = jnp.clip(a - SW_WINDOW, 0, SEQ - BAND)
            row0 = pl.multiple_of(CTX_LEN + start, SW_WINDOW)
            kb = k_ref[pl.ds(row0, BAND), :]
            kpos = start + _iota((BAND, 1), 0)
            qpos = a + _iota((1, TILE), 1)
            valid = jnp.abs(qpos - kpos) <= SW_WINDOW
        sinks, maxes = [], []
        for h in range(HEADS):
            kvh, g = h // group, h % group
            qg = q_ref[:, g * KV_WIDTH:(g + 1) * KV_WIDTH]
            qm = jnp.where(lane // HEAD_DIM == kvh, qg, jnp.zeros_like(qg))
            sink = sink_ref[:, h:h + 1] * LOG2E
            sc = _dot_nt(k_ref[0:CTX_LEN, :], qm)
            s_ref[h, 0:CTX_LEN, :] = sc
            m = jnp.maximum(jnp.max(sc, axis=0, keepdims=True), sink)
            if band:
                sb = jnp.where(valid, _dot_nt(kb, qm), NEG_BIG)
                s_ref[h, CTX_LEN:CTX_LEN + BAND, :] = sb
                m = jnp.maximum(m, jnp.max(sb, axis=0, keepdims=True))
            sinks.append(sink)
            maxes.append(m)
        for h in range(HEADS):
            m = maxes[h]
            kv_rows = slice((h // group) * HEAD_DIM, (h // group + 1) * HEAD_DIM)
            vt1 = jnp.concatenate([vt_ref[kv_rows, 0:CTX_LEN], ones_rows], axis=0)
            pv = _dot(vt1, jnp.exp2(s_ref[h, 0:CTX_LEN, :] - m).astype(BF16))
            if band:
                vtb = vt_ref[kv_rows, pl.ds(row0, BAND)]
                ones_b = jnp.ones((ONES_ROWS, BAND), BF16)
                eb = jnp.exp2(s_ref[h, CTX_LEN:CTX_LEN + BAND, :] - m).astype(BF16)
                pv = pv + _dot(jnp.concatenate([vtb, ones_b], axis=0), eb)
            l = pv[HEAD_DIM:HEAD_DIM + 1, :] + jnp.exp2(sinks[h] - m)
            acc_ref[h * HEAD_DIM:(h + 1) * HEAD_DIM, :] = pv[0:HEAD_DIM, :] * (1.0 / l)
        o_ref[...] = acc_ref[...].T.astype(BF16)

    @pl.when(qb == 0)
    def _():
        attend(False)

    @pl.when(qb > 0)
    def _():
        attend(True)


def _window_call(pc, vtc, sink_row, need_ctx):
    bsz = pc.shape[0]
    q_tile0 = 0 if need_ctx else 1
    n_q = N_TILES - q_tile0
    return pl.pallas_call(
        functools.partial(_window_kernel, q_tile0=q_tile0),
        grid=(bsz, n_q),
        in_specs=[
            pl.BlockSpec((None, TILE, BRANCH), lambda b, t: (b, t + q_tile0, 0)),
            pl.BlockSpec((None, T_ALL, KV_WIDTH), lambda b, t: (b, 0, BRANCH // KV_WIDTH)),
            pl.BlockSpec((None, KV_WIDTH, T_ALL), lambda b, t: (b, 0, 0)),
            pl.BlockSpec((1, HEADS), lambda b, t: (0, 0)),
        ],
        out_specs=pl.BlockSpec((None, TILE, BRANCH), lambda b, t: (b, t, 0)),
        out_shape=jax.ShapeDtypeStruct((bsz, n_q * TILE, BRANCH), BF16),
        scratch_shapes=[pltpu.VMEM((BRANCH, TILE), F32), pltpu.VMEM((HEADS, CTX_LEN + BAND, TILE), F32)],
        compiler_params=_params("arbitrary", "arbitrary"),
        name="window",
    )(pc, pc, vtc, sink_row)


CH_PER_TILE = TILE // HG_CHUNK
HG_LEVELS = 2
HG_STATE = HG_CHUNK * 2 ** HG_LEVELS
ST_PER_TILE = TILE // HG_STATE
HG_UNROLL = 4
HG_FAST_LIMIT = 120.0


def _hgrn_gates(z, lb_terms):
    log2_ksig = _log_sigmoid(-z) * LOG2E
    if lb_terms is None:
        return _log_sigmoid(z) * LOG2E, _sigmoid(-z), log2_ksig
    lb, log_lb, log_1m = lb_terms
    bt = log_1m + _log_sigmoid(z)
    mx = jnp.maximum(log_lb, bt)
    log_f = mx + jnp.log(jnp.exp(log_lb - mx) + jnp.exp(bt - mx))
    return log_f * LOG2E, (1.0 - lb) * _sigmoid(-z), log_1m * LOG2E + log2_ksig


def _hgrn_intra(direction, q3, c3, u3, v3, head_ones):
    half = HG_CHUNK // 2
    slabs, meta = [], []
    for s in range(HG_CHUNK):
        us = jnp.broadcast_to(u3[:, s:s + 1, :], (CH_PER_TILE, half, BRANCH))
        for g in range(2):
            lo_row, hi_row = half * g, half * g + half - 1
            if direction == 0:
                none_valid, all_valid = hi_row < s, lo_row >= s
            else:
                none_valid, all_valid = lo_row > s, hi_row <= s
            if none_valid:
                continue
            d = c3[:, half * g:half * (g + 1), :] - us
            if not all_valid:
                row = _iota((1, half, 1), 1) + half * g
                d = jnp.where((row >= s) if direction == 0 else (row <= s), d, NEG_BIG)
            x = q3[:, half * g:half * (g + 1), :] * jnp.exp2(d)
            slabs.append(x.reshape(CH_PER_TILE * half, BRANCH).astype(BF16))
            meta.append((s, g))
    a_all = _dot(jnp.concatenate(slabs, axis=0), head_ones)
    o = [jnp.zeros((CH_PER_TILE, half, BRANCH), F32) for _ in range(2)]
    n = CH_PER_TILE * half
    vs = None
    for i, (s, g) in enumerate(meta):
        if i == 0 or meta[i - 1][0] != s:
            vs = jnp.broadcast_to(v3[:, s:s + 1, :], (CH_PER_TILE, half, BRANCH))
        o[g] = o[g] + a_all[i * n:(i + 1) * n].reshape(CH_PER_TILE, half, BRANCH) * vs
    return jnp.concatenate(o, axis=1).reshape(TILE, BRANCH)


def _hgrn_kernel(p_ref, lb_ref, g_ref, o_ref, qt_ref, ke_ref, vb_ref, dec_ref, of_ref, oi_ref, st_ref,
                 *, layer_idx):
    rr = _iota((TILE, TILE), 0)
    cc = _iota((TILE, TILE), 1)
    same_chunk = (rr // HG_CHUNK) == (cc // HG_CHUNK)
    chunk_ones = same_chunk.astype(BF16)
    within = [same_chunk & (cc <= rr), same_chunk & (cc >= rr)]
    tris = [w.astype(BF16) for w in within]
    head_ones = _head_sum_matrix()
    head_mask = (rr // HEAD_DIM) == (cc // HEAD_DIM)
    lane = _iota((1, BRANCH), 1)
    row_in = _iota((TILE, 1), 0)

    def shift_rows(a, n):
        n = n % TILE
        return jnp.concatenate([a[TILE - n:, :], a[:TILE - n, :]], axis=0)

    lb_terms = None
    if layer_idx > 0:
        lbp = lb_ref[...]
        lbp = lbp - jnp.max(lbp, axis=0, keepdims=True)
        sm = jnp.exp(lbp)
        sm = sm / jnp.sum(sm, axis=0, keepdims=True)
        lb = jnp.sum(sm[1:layer_idx + 1], axis=0, keepdims=True)
        lb_terms = (lb, jnp.log(lb), jnp.log(1.0 - lb))

    def tile_body(t, carry):
        rows = pl.ds(pl.multiple_of(t * TILE, TILE), TILE)
        q = p_ref[rows, 0:BRANCH] * (HEAD_DIM ** -0.5)
        v = p_ref[rows, 3 * BRANCH:4 * BRANCH]
        vbf = v.astype(BF16)
        vb_ref[rows, :] = vbf
        v_heads = jnp.concatenate([jnp.where(lane // HEAD_DIM == h, vbf, jnp.zeros_like(vbf))
                                   for h in range(HEADS)], axis=0)
        shape3 = (CH_PER_TILE, HG_CHUNK, BRANCH)
        o_intra = None
        direct_args = []
        for direction in (0, 1):
            z = p_ref[rows, (1 + direction) * BRANCH:(2 + direction) * BRANCH]
            log2_f, kk, log2_k = _hgrn_gates(z, lb_terms)
            cum = _dot_exact_l(tris[direction], log2_f)
            tot = _dot_exact_l(chunk_ones, log2_f)
            qt16 = (q * jnp.exp2(cum)).astype(BF16)
            cum16 = cum
            low = jnp.min(cum16, axis=(0, 1), keepdims=True)
            k_back = (kk * jnp.exp2(jnp.minimum(-cum16, HG_FAST_LIMIT))).astype(BF16)
            keep = within[direction] & (low >= -HG_FAST_LIMIT)
            a_heads = [jnp.where(keep, _dot_nt(jnp.where(lane // HEAD_DIM == h, qt16, jnp.zeros_like(qt16)),
                                               k_back), 0.0) for h in range(HEADS)]
            direct_args.append((low, cum16, log2_k))
            g = HG_CHUNK
            for _ in range(HG_LEVELS):
                qt = (q * jnp.exp2(cum)).astype(BF16)
                ke = (kk * jnp.exp2(tot - cum)).astype(BF16)
                later_r = ((rr % (2 * g)) >= g) if direction == 0 else ((rr % (2 * g)) < g)
                later_c = ((cc % (2 * g)) >= g) if direction == 0 else ((cc % (2 * g)) < g)
                pair = ((rr // (2 * g)) == (cc // (2 * g))) & later_r & jnp.logical_not(later_c)
                for h in range(HEADS):
                    qh = jnp.where(lane // HEAD_DIM == h, qt, jnp.zeros_like(qt))
                    a_heads[h] = jnp.where(pair, _dot_nt(qh, ke), a_heads[h])
                later_row = ((row_in % (2 * g)) >= g) if direction == 0 else ((row_in % (2 * g)) < g)
                sign = 1 if direction == 0 else -1
                tot_other = jnp.where(later_row, shift_rows(tot, sign * g), shift_rows(tot, -sign * g))
                cum = cum + jnp.where(later_row, tot_other, 0.0)
                tot = tot + tot_other
                g *= 2
            a_cat = jnp.concatenate([a.astype(BF16) for a in a_heads], axis=1)
            o_d = _dot(a_cat, v_heads)
            qt_ref[direction, rows, :] = (q * jnp.exp2(cum)).astype(BF16)
            ke_ref[direction, rows, :] = (kk * jnp.exp2(tot - cum)).astype(BF16)
            dec_ref[direction, t] = jnp.exp2(tot.reshape(ST_PER_TILE, HG_STATE, BRANCH)[:, 0, :])
            o_intra = o_d if o_intra is None else o_intra + o_d
        of_ref[rows, :] = o_intra

        for direction, (low, cum16, log2_k) in enumerate(direct_args):
            @pl.when(low[0, 0] < -HG_FAST_LIMIT)
            def _():
                of_ref[rows, :] += _hgrn_intra(direction, q.reshape(shape3), cum16.reshape(shape3),
                                               (cum16 - log2_k).reshape(shape3), v.reshape(shape3), head_ones)
        return carry

    lax.fori_loop(0, N_TILES, tile_body, 0)

    st_ref[...] = jnp.zeros_like(st_ref)
    n_ctx, n_all = CTX_LEN // HG_STATE, T_ALL // HG_STATE

    def state_step(i, direction):
        if direction == 0:
            c = i
        else:
            c = jnp.where(i < n_ctx, n_ctx - 1 - i, n_all - 1 + n_ctx - i)
        rows = pl.ds(pl.multiple_of(c * HG_STATE, HG_STATE), HG_STATE)
        st = st_ref[direction]
        oi_ref[direction, rows, :] = _dot_nt(qt_ref[direction, rows, :], st.astype(BF16))
        ds = _dot_tn(vb_ref[rows, :], ke_ref[direction, rows, :])
        dec = dec_ref[direction, c // ST_PER_TILE, pl.ds(c % ST_PER_TILE, 1), :]
        st_ref[direction] = st * dec + jnp.where(head_mask, ds, 0.0)

    def state_body(it, carry):
        for u in range(HG_UNROLL):
            for direction in (0, 1):
                state_step(it * HG_UNROLL + u, direction)
        return carry

    lax.fori_loop(0, n_all // HG_UNROLL, state_body, 0)

    o_ref[...] = _head_rmsnorm(of_ref[...] + oi_ref[0] + oi_ref[1], g_ref[...]).astype(BF16)


def _hgrn_call(pb, hg_lb, g_row, layer_idx):
    bsz = pb.shape[0]
    return pl.pallas_call(
        functools.partial(_hgrn_kernel, layer_idx=layer_idx),
        grid=(bsz,),
        in_specs=[
            pl.BlockSpec((None, T_ALL, SEG_B), lambda b: (b, 0, 0)),
            pl.BlockSpec((DEPTH, BRANCH), lambda b: (0, 0)),
            pl.BlockSpec((1, BRANCH), lambda b: (0, 0)),
        ],
        out_specs=pl.BlockSpec((None, T_ALL, BRANCH), lambda b: (b, 0, 0)),
        out_shape=jax.ShapeDtypeStruct((bsz, T_ALL, BRANCH), BF16),
        scratch_shapes=[pltpu.VMEM((2, T_ALL, BRANCH), BF16), pltpu.VMEM((2, T_ALL, BRANCH), BF16),
                        pltpu.VMEM((T_ALL, BRANCH), BF16), pltpu.VMEM((2, N_TILES, ST_PER_TILE, BRANCH), F32),
                        pltpu.VMEM((T_ALL, BRANCH), F32), pltpu.VMEM((2, T_ALL, BRANCH), F32),
                        pltpu.VMEM((2, BRANCH, BRANCH), F32)],
        compiler_params=_params("arbitrary"),
        name="hgrn",
    )(pb, hg_lb, g_row)


ML_UNROLL = 3


def _mlstm_logits(direction, c, p_ref, vt_ref, g_ref, gt_ref, ct_ref, n_ref, m_ref, consts):
    tri, valid, _, lane, row16, lane16 = consts
    rows = pl.ds(pl.multiple_of(c * TILE, TILE), TILE)
    q = p_ref[rows, 0:BRANCH]
    k = p_ref[rows, BRANCH:2 * BRANCH]
    g = g_ref[rows, :]
    gt = gt_ref[:, rows]
    cum = _dot_exact_l(tri, _log_sigmoid(g) * LOG2E)
    cum_t = _dot_exact_nt(_log_sigmoid(gt[8:16, :]) * LOG2E, tri)
    ig_t = gt[0:8, :] * LOG2E
    ct = ct_ref[direction]
    n0 = n_ref[direction]
    n_hi = n0.astype(BF16).astype(F32)
    n_lo = n0 - n_hi
    n_rows = (jnp.where((row16 < HEADS) & (lane16 == row16), n_hi, 0.0)
              + jnp.where((row16 >= HEADS) & (lane16 == row16 - HEADS), n_lo, 0.0)).astype(BF16)
    inter_all = _dot_nt(jnp.concatenate([ct.astype(BF16), n_rows], axis=0), q)
    heads = []
    for h in range(HEADS):
        r = HEADS * direction + h
        cumr = cum_t[r:r + 1, :]
        ucol = g[:, r:r + 1] * LOG2E - cum[:, 2 * HEADS + r:2 * HEADS + r + 1]
        toth = cumr[:, TILE - 1:TILE] if direction == 0 else cumr[:, 0:1]
        m0h = m_ref[direction, :, h:h + 1]
        logd = jnp.where(valid, cumr + ucol, NEG_BIG)
        inter = cumr + m0h
        m_t = jnp.maximum(jnp.max(logd, axis=0, keepdims=True), inter)
        qm = jnp.where(lane // HEAD_DIM == h, q, jnp.zeros_like(q))
        heads.append(dict(qk=_dot_nt(k, qm), logd=logd, m_t=m_t, g0=jnp.exp2(inter - m_t),
                          a_row=toth - cumr + ig_t[r:r + 1, :], carry=toth + m0h))
    return dict(direction=direction, rows=rows, k=k, ct=ct, n0=n0, inter_all=inter_all, heads=heads)


def _mlstm_outputs(cx, vt_ref, ht_ref, ct_ref, n_ref, m_ref, consts):
    _, _, head_mask, lane, row16, _ = consts
    direction, rows, k, inter_all = cx["direction"], cx["rows"], cx["k"], cx["inter_all"]
    vt = vt_ref[:, rows]
    ones_rows = jnp.ones((ONES_ROWS, TILE), BF16)
    w_rows, sp_row = [], jnp.zeros((1, BRANCH), F32)
    for h, hd in enumerate(cx["heads"]):
        m_t, g0 = hd["m_t"], hd["g0"]
        s_t = hd["qk"] * jnp.exp2(hd["logd"] - m_t)
        vt1 = jnp.concatenate([vt[h * HEAD_DIM:(h + 1) * HEAD_DIM, :], ones_rows], axis=0)
        pv = _dot(vt1, s_t.astype(BF16))
        num = pv[0:HEAD_DIM, :] + g0 * inter_all[h * HEAD_DIM:(h + 1) * HEAD_DIM, :]
        den = pv[HEAD_DIM:HEAD_DIM + 1, :] + g0 * (inter_all[BRANCH + h:BRANCH + h + 1, :]
                                                   + inter_all[BRANCH + HEADS + h:BRANCH + HEADS + h + 1, :])
        ht_ref[direction, h * HEAD_DIM:(h + 1) * HEAD_DIM, rows] = (
            num / jnp.maximum(jnp.abs(den), jnp.exp2(-m_t)))
        a_row = hd["a_row"]
        m_loc = jnp.max(a_row, axis=1, keepdims=True)
        m_new = jnp.maximum(hd["carry"], m_loc)
        sp = jnp.exp2(hd["carry"] - m_new)
        w_rows.append(jnp.exp2(a_row - m_loc) * jnp.exp2(m_loc - m_new))
        sp_row = sp_row + jnp.where(lane // HEAD_DIM == h, sp, 0.0)
        m_ref[direction, :, h:h + 1] = m_new

    w_block = jnp.concatenate([jnp.broadcast_to(w, (HEAD_DIM, TILE)) for w in w_rows], axis=0)
    vtw = (vt.astype(F32) * w_block).astype(BF16)
    w16 = jnp.zeros((ONES_ROWS, TILE), F32)
    for h in range(HEADS):
        w_hi = w_rows[h].astype(BF16).astype(F32)
        w16 = w16 + jnp.where(row16 == h, w_hi, 0.0) + jnp.where(row16 == HEADS + h, w_rows[h] - w_hi, 0.0)
    dall = _dot(jnp.concatenate([vtw, w16.astype(BF16)], axis=0), k)
    ct_ref[direction] = cx["ct"] * sp_row + jnp.where(head_mask, dall[0:BRANCH, :], 0.0)
    dn = jnp.zeros((1, BRANCH), F32)
    for h in range(HEADS):
        dn = dn + jnp.where(lane // HEAD_DIM == h,
                            dall[BRANCH + h:BRANCH + h + 1, :] + dall[BRANCH + HEADS + h:BRANCH + HEADS + h + 1, :], 0.0)
    n_ref[direction] = cx["n0"] * sp_row + dn


def _mlstm_kernel(p_ref, vt_ref, g_ref, gt_ref, gain_ref, o_ref, ht_ref, ct_ref, n_ref, m_ref):
    rr = _iota((TILE, TILE), 0)
    cc = _iota((TILE, TILE), 1)
    head_mask = (rr // HEAD_DIM) == (cc // HEAD_DIM)
    lane = _iota((1, BRANCH), 1)
    row16 = _iota((ONES_ROWS, BRANCH), 0)
    lane16 = _iota((ONES_ROWS, BRANCH), 1) // HEAD_DIM
    consts = []
    for direction in (0, 1):
        tri = ((cc <= rr) if direction == 0 else (cc >= rr)).astype(BF16)
        valid = (rr <= cc) if direction == 0 else (rr >= cc)
        consts.append((tri, valid, head_mask, lane, row16, lane16))
    ct_ref[...] = jnp.zeros_like(ct_ref)
    n_ref[...] = jnp.zeros_like(n_ref)
    m_ref[...] = jnp.zeros_like(m_ref)

    def body(it, carry):
        for u in range(ML_UNROLL):
            i = it * ML_UNROLL + u
            chunk = (i, jnp.where(i == 0, 0, N_TILES - i))
            cxs = [_mlstm_logits(d, chunk[d], p_ref, vt_ref, g_ref, gt_ref, ct_ref, n_ref, m_ref, consts[d])
                   for d in (0, 1)]
            for d in (0, 1):
                _mlstm_outputs(cxs[d], vt_ref, ht_ref, ct_ref, n_ref, m_ref, consts[d])
        return carry

    lax.fori_loop(0, N_TILES // ML_UNROLL, body, 0)

    def out_body(t, carry):
        rows = pl.ds(pl.multiple_of(t * TILE, TILE), TILE)
        o_ref[rows, :] = _head_rmsnorm((ht_ref[0, :, rows] + ht_ref[1, :, rows]).T,
                                       gain_ref[...]).astype(BF16)
        return carry

    lax.fori_loop(0, N_TILES, out_body, 0)


def _mlstm_call(pd, vtd, pg, pgt, g_row):
    bsz = pd.shape[0]
    return pl.pallas_call(
        _mlstm_kernel,
        grid=(bsz,),
        in_specs=[
            pl.BlockSpec((None, T_ALL, 2 * BRANCH), lambda b: (b, 0, 0)),
            pl.BlockSpec((None, BRANCH, T_ALL), lambda b: (b, 0, 0)),
            pl.BlockSpec((None, T_ALL, SEG_G), lambda b: (b, 0, 0)),
            pl.BlockSpec((None, SEG_G, T_ALL), lambda b: (b, 0, 0)),
            pl.BlockSpec((1, BRANCH), lambda b: (0, 0)),
        ],
        out_specs=pl.BlockSpec((None, T_ALL, BRANCH), lambda b: (b, 0, 0)),
        out_shape=jax.ShapeDtypeStruct((bsz, T_ALL, BRANCH), BF16),
        scratch_shapes=[pltpu.VMEM((2, BRANCH, T_ALL), F32), pltpu.VMEM((2, BRANCH, BRANCH), F32),
                        pltpu.VMEM((2, 1, BRANCH), F32), pltpu.VMEM((2, 1, LANES), F32)],
        compiler_params=_params("arbitrary"),
        name="mlstm",
    )(pd, vtd, pg, pgt, g_row)


def _outproj_kernel(x_ref, ctx_ref, ya_ref, yb_ref, yc_ref, yd_ref, po_ref, mod_ref, w_ref, fg_ref,
                    *out_refs, bsz, last):
    t = pl.program_id(1)
    is_ctx = jnp.logical_and(t == 0, not last)
    row = jnp.where(is_ctx, bsz, pl.program_id(0))
    gate_mod = mod_ref[pl.ds(row, 1), 2 * D_MODEL:3 * D_MODEL]
    po = po_ref[...].astype(F32)
    yd = yd_ref[...].astype(F32) * po[:, 0:BRANCH]
    mixed = jnp.concatenate([ya_ref[...].astype(F32), yb_ref[...].astype(F32), yc_ref[...].astype(F32), yd],
                            axis=-1)
    mixed = (mixed * po[:, BRANCH:]).astype(BF16)
    delta = gate_mod * _dot(mixed, w_ref[...])
    if last:
        xn = x_ref[...] + delta
        ms = jnp.mean(xn * xn, axis=-1, keepdims=True)
        out_refs[0][...] = xn * lax.rsqrt(ms + NORM_EPS) * fg_ref[...]
    else:
        x_out_ref, ctx_out_ref = out_refs

        @pl.when(t == 0)
        def _():
            ctx_out_ref[...] = ctx_ref[...] + delta

        @pl.when(t > 0)
        def _():
            x_out_ref[...] = x_ref[...] + delta


def _outproj_call(x, ctx, ya, yb, yc, yd, po, mod_l, w_out_bf, final_g, last):
    bsz = x.shape[0]
    tile0 = 1 if last else 0
    rows = mod_l.shape[0]

    def tok(width, arr):
        off = tile0 if arr.shape[1] == T_ALL else 0
        return pl.BlockSpec((None, TILE, width), lambda b, t: (b, t + off, 0))

    lat_spec = pl.BlockSpec((None, TILE, D_MODEL), lambda b, t: (b, jnp.maximum(t + tile0 - 1, 0), 0))
    ctx_spec = pl.BlockSpec((None, CTX_LEN, D_MODEL), lambda b, t: (b, 0, 0))
    lat_shape = jax.ShapeDtypeStruct((bsz, SEQ, D_MODEL), F32)
    ctx_shape = jax.ShapeDtypeStruct((bsz, CTX_LEN, D_MODEL), F32)
    return pl.pallas_call(
        functools.partial(_outproj_kernel, bsz=bsz, last=last),
        grid=(bsz, N_TILES - tile0),
        in_specs=[lat_spec, ctx_spec, tok(BRANCH, ya), tok(BRANCH, yb), tok(BRANCH, yc), tok(BRANCH, yd),
                  tok(SEG_O, po),
                  pl.BlockSpec((rows, 3 * D_MODEL), lambda b, t: (0, 0)),
                  pl.BlockSpec((D_MODEL, D_MODEL), lambda b, t: (0, 0)),
                  pl.BlockSpec((1, D_MODEL), lambda b, t: (0, 0))],
        out_specs=lat_spec if last else [lat_spec, ctx_spec],
        out_shape=lat_shape if last else [lat_shape, ctx_shape],
        compiler_params=_params("arbitrary", "arbitrary"),
        name="outproj",
    )(x, ctx, ya, yb, yc, yd, po, mod_l, w_out_bf, final_g.reshape(1, D_MODEL))


def _rope_tables(dim):
    quarter = dim // 4
    half = dim // 2
    pos = np.arange(SEQ)
    row = (pos // GRID_W).astype(np.float32)
    col = (pos % GRID_W).astype(np.float32)
    inv = (np.float32(ROPE_BASE) ** (-np.arange(0, half, 2, dtype=np.float32) / np.float32(half))).astype(np.float32)
    ang_r = row[:, None] * inv[None, :]
    ang_c = col[:, None] * inv[None, :]
    lane = np.arange(LANES) % dim
    part = lane // quarter
    freq = lane % quarter
    ang = np.where(part[None, :] < 2, ang_r[:, freq], ang_c[:, freq]).astype(np.float32)
    cos = np.cos(ang)
    sin = np.sin(ang)
    first = (part % 2 == 0)[None, :]
    s_next = np.where(first, -sin, 0.0)
    s_prev = np.where(first, 0.0, sin)
    tab = np.stack([cos, s_next, s_prev]).astype(np.float32)
    ident = np.stack([np.ones((CTX_LEN, LANES)), np.zeros((CTX_LEN, LANES)),
                      np.zeros((CTX_LEN, LANES))]).astype(np.float32)
    return jnp.asarray(np.concatenate([ident, tab], axis=1))


def _relayout_in_proj(w, bias):
    def cols(a):
        seg_a = a[..., 0:768]
        seg_b = a[..., 768:1792]
        qc = a[..., 1792:2048]
        qc = jnp.concatenate([qc[..., 0:64], qc[..., 128:192], qc[..., 64:128], qc[..., 192:256]], axis=-1)
        kvc = a[..., 2048:2304]
        seg_d = a[..., 2304:3072]
        gates = a[..., 3072:3088]
        seg_o = a[..., 3088:4368]
        pad = jnp.zeros(a.shape[:-1] + (SEG_G - 16,), a.dtype)
        return jnp.concatenate([seg_a, qc, kvc, seg_b, seg_d, gates, pad, seg_o], axis=-1)
    return cols(w), cols(bias)


def kernel(x, c, ctx, c_ctx, w_mod, b_mod, norm_g, w_in, b_in, diff_lam, diff_g, hg_lb, hg_g,
           sw_sink, ml_g, w_out, final_g):
    bsz = x.shape[0]
    tab_a = _rope_tables(DA_QK)
    tab_c = _rope_tables(HEAD_DIM)
    rows = ((bsz + 1 + 7) // 8) * 8
    cc = jnp.concatenate([c, c_ctx[None, :], jnp.zeros((rows - bsz - 1, D_MODEL), F32)], axis=0)
    mod = _mod_call(cc, w_mod, b_mod)
    tile4 = lambda g: jnp.tile(g, HEADS).reshape(1, BRANCH)
    w_in_bf = w_in.astype(BF16)
    w_out_bf = w_out.astype(BF16)
    for l in range(DEPTH):
        last = l == DEPTH - 1
        w_r, b_r = _relayout_in_proj(w_in_bf[l], b_in[l])
        pa, pc, pb, pd, pg, po, vta, vtc, vtd, pgt = _inproj_call(x, ctx, mod[l], norm_g[l], w_r, b_r,
                                                                 tab_a, tab_c)
        ya = _diffattn_call(pa, vta, diff_lam[l], tile4(diff_g[l]), l, not last)
        yb = _hgrn_call(pb, hg_lb, tile4(hg_g[l]), l)
        yc = _window_call(pc, vtc, sw_sink[l].reshape(1, HEADS), not last)
        yd = _mlstm_call(pd, vtd, pg, pgt, tile4(ml_g[l]))
        res = _outproj_call(x, ctx, ya, yb, yc, yd, po, mod[l], w_out_bf[l], final_g, last)
        if last:
            return res
        x, ctx = res
```

```python
import functools
import math

import numpy as np
import jax
import jax.numpy as jnp
from jax import lax
from jax.experimental import pallas as pl
from jax.experimental.pallas import tpu as pltpu

F32 = jnp.float32
BF16 = jnp.bfloat16

D_MODEL = 1024
SEQ = 2048
CTX_LEN = 256
T_ALL = CTX_LEN + SEQ
GRID_W = 64
DEPTH = 2
HEADS = 4
HEAD_DIM = 64
BRANCH = HEADS * HEAD_DIM
DA_QK = 32
SW_WINDOW = 128
HG_CHUNK = 16
ROPE_BASE = 10000.0
NORM_EPS = 1e-6
NEG_BIG = -1e30
LOG2E = math.log2(math.e)

TILE = 256
N_TILES = T_ALL // TILE
LANES = 128
ONES_ROWS = 16

SEG_A = 3 * BRANCH
KV_WIDTH = 2 * HEAD_DIM
SEG_C = BRANCH + 2 * KV_WIDTH
SEG_B = 4 * BRANCH
SEG_D = 3 * BRANCH
SEG_G = LANES
SEG_O = BRANCH + D_MODEL
OFF_A = 0
OFF_C = OFF_A + SEG_A
OFF_B = OFF_C + SEG_C
OFF_D = OFF_B + SEG_B
OFF_G = OFF_D + SEG_D
OFF_O = OFF_G + SEG_G
PROJ_PAD = OFF_O + SEG_O

VMEM_LIMIT = 56 * 1024 * 1024


def _params(*sem):
    return pltpu.CompilerParams(dimension_semantics=sem, vmem_limit_bytes=VMEM_LIMIT)


def _dot(a, b):
    return jnp.dot(a, b, preferred_element_type=F32)


def _dot_nt(a, b):
    return lax.dot_general(a, b, (((1,), (1,)), ((), ())), preferred_element_type=F32)


def _dot_tn(a, b):
    return lax.dot_general(a, b, (((0,), (0,)), ((), ())), preferred_element_type=F32)


def _split3(x):
    x1 = x.astype(BF16)
    r1 = x - x1.astype(F32)
    x2 = r1.astype(BF16)
    x3 = (r1 - x2.astype(F32)).astype(BF16)
    return x1, x2, x3


def _dot_exact_l(m01, x):
    x1, x2, x3 = _split3(x)
    return _dot(m01, x1) + _dot(m01, x2) + _dot(m01, x3)


def _dot_exact_r(x, m01):
    x1, x2, x3 = _split3(x)
    return _dot(x1, m01) + _dot(x2, m01) + _dot(x3, m01)


def _dot_exact_nt(x, m01):
    x1, x2, x3 = _split3(x)
    return _dot_nt(x1, m01) + _dot_nt(x2, m01) + _dot_nt(x3, m01)


def _sigmoid(z):
    e = jnp.exp(-jnp.abs(z))
    r = 1.0 / (1.0 + e)
    return jnp.where(z >= 0, r, e * r)


def _log_sigmoid(z):
    return jnp.minimum(z, 0.0) - jnp.log(1.0 + jnp.exp(-jnp.abs(z)))


def _iota(shape, dim):
    return lax.broadcasted_iota(jnp.int32, shape, dim)


def _head_sum_matrix():
    r = _iota((BRANCH, BRANCH), 0) // HEAD_DIM
    c = _iota((BRANCH, BRANCH), 1) // HEAD_DIM
    return (r == c).astype(BF16)


def _head_rmsnorm(o, g_row):
    ss = _dot_exact_r(o * o, _head_sum_matrix())
    return o * lax.rsqrt(ss * (1.0 / HEAD_DIM) + NORM_EPS) * g_row


def _mod_kernel(cc_ref, w_ref, b_ref, o_ref):
    cc = cc_ref[...]
    a = (cc * _sigmoid(cc)).astype(BF16)
    o_ref[...] = _dot(a, w_ref[...].astype(BF16)) + b_ref[...]


def _mod_call(cc, w_mod, b_mod):
    rows = cc.shape[0]
    nblk = 3
    return pl.pallas_call(
        _mod_kernel,
        grid=(DEPTH, nblk),
        in_specs=[
            pl.BlockSpec((rows, D_MODEL), lambda l, j: (0, 0)),
            pl.BlockSpec((None, D_MODEL, D_MODEL), lambda l, j: (l, 0, j)),
            pl.BlockSpec((None, 1, D_MODEL), lambda l, j: (l, 0, j)),
        ],
        out_specs=pl.BlockSpec((None, rows, D_MODEL), lambda l, j: (l, 0, j)),
        out_shape=jax.ShapeDtypeStruct((DEPTH, rows, 3 * D_MODEL), F32),
        compiler_params=_params("arbitrary", "arbitrary"),
        name="mod",
    )(cc, w_mod, b_mod.reshape(DEPTH, 1, 3 * D_MODEL))


def _rope(slab, cos, sin_next, sin_prev, off):
    nxt = pltpu.roll(slab, LANES - off, 1)
    prv = pltpu.roll(slab, off, 1)
    return slab * cos + nxt * sin_next + prv * sin_prev


def _inproj_kernel(x_ref, ctx_ref, mod_ref, ng_ref, w_ref, b_ref, ta_ref, tc_ref,
                   pa_ref, pc_ref, pb_ref, pd_ref, pg_ref, po_ref, vta_ref, vtc_ref, vtd_ref, pgt_ref,
                   *, bsz):
    is_ctx = pl.program_id(1) == 0
    row = jnp.where(is_ctx, bsz, pl.program_id(0))
    x = jnp.where(is_ctx, ctx_ref[...], x_ref[...])
    mrow = mod_ref[pl.ds(row, 1), :]
    shift = mrow[:, 0:D_MODEL]
    scale = mrow[:, D_MODEL:2 * D_MODEL]
    ms = jnp.mean(x * x, axis=-1, keepdims=True)
    h = x * lax.rsqrt(ms + NORM_EPS) * ng_ref[...]
    h = (h * (1.0 + scale) + shift).astype(BF16)

    def proj(off, width):
        return _dot(h, w_ref[:, off:off + width]) + b_ref[:, off:off + width]

    def rope_seg(acc, tab_ref, off, q_scale, k_slabs):
        cos, s_next, s_prev = tab_ref[0], tab_ref[1], tab_ref[2]
        outs = []
        for j in range(2 + k_slabs):
            r = _rope(acc[:, j * LANES:(j + 1) * LANES], cos, s_next, s_prev, off)
            outs.append(r * q_scale if j < 2 else r)
        outs.append(acc[:, (2 + k_slabs) * LANES:])
        return jnp.concatenate(outs, axis=-1)

    acco = proj(OFF_O, SEG_O)
    po_ref[...] = jnp.concatenate(
        [_sigmoid(acco[:, 0:BRANCH]), acco[:, BRANCH:] * _sigmoid(acco[:, BRANCH:])], axis=-1).astype(BF16)
    acca = rope_seg(proj(OFF_A, SEG_A), ta_ref, DA_QK // 4, DA_QK ** -0.5 * LOG2E, 2)
    pa_ref[...] = acca[:, 0:2 * BRANCH].astype(BF16)
    vta_ref[...] = acca[:, 2 * BRANCH:].T.astype(BF16)
    accc = rope_seg(proj(OFF_C, SEG_C), tc_ref, HEAD_DIM // 4, HEAD_DIM ** -0.5 * LOG2E, 1)
    pc_ref[...] = accc[:, 0:BRANCH + KV_WIDTH].astype(BF16)
    vtc_ref[...] = accc[:, BRANCH + KV_WIDTH:].T.astype(BF16)
    accd = proj(OFF_D, SEG_D)
    pd_ref[...] = jnp.concatenate(
        [accd[:, 0:BRANCH], accd[:, BRANCH:2 * BRANCH] * (HEAD_DIM ** -0.5)], axis=-1).astype(BF16)
    vtd_ref[...] = accd[:, 2 * BRANCH:].T.astype(BF16)
    gates = proj(OFF_G, SEG_G)
    pg_ref[...] = gates
    pgt_ref[...] = gates.T
    pb_ref[...] = proj(OFF_B, SEG_B)


def _inproj_call(x, ctx, mod_l, norm_g, w_r, b_r, tab_a, tab_c):
    bsz = x.shape[0]
    rows = mod_l.shape[0]
    widths = [(2 * BRANCH, BF16), (BRANCH + KV_WIDTH, BF16), (SEG_B, F32), (2 * BRANCH, BF16), (SEG_G, F32),
              (SEG_O, BF16)]
    out_specs = [pl.BlockSpec((None, TILE, w), lambda b, t: (b, t, 0)) for w, _ in widths]
    out_shape = [jax.ShapeDtypeStruct((bsz, T_ALL, w), dt) for w, dt in widths]
    for rows_t, dt in ((BRANCH, BF16), (KV_WIDTH, BF16), (BRANCH, BF16), (SEG_G, F32)):
        out_specs.append(pl.BlockSpec((None, rows_t, TILE), lambda b, t: (b, 0, t)))
        out_shape.append(jax.ShapeDtypeStruct((bsz, rows_t, T_ALL), dt))
    return pl.pallas_call(
        functools.partial(_inproj_kernel, bsz=bsz),
        grid=(bsz, N_TILES),
        in_specs=[
            pl.BlockSpec((None, TILE, D_MODEL), lambda b, t: (b, jnp.maximum(t - 1, 0), 0)),
            pl.BlockSpec((None, CTX_LEN, D_MODEL), lambda b, t: (b, 0, 0)),
            pl.BlockSpec((rows, 3 * D_MODEL), lambda b, t: (0, 0)),
            pl.BlockSpec((1, D_MODEL), lambda b, t: (0, 0)),
            pl.BlockSpec((D_MODEL, PROJ_PAD), lambda b, t: (0, 0)),
            pl.BlockSpec((1, PROJ_PAD), lambda b, t: (0, 0)),
            pl.BlockSpec((3, TILE, LANES), lambda b, t: (0, t, 0)),
            pl.BlockSpec((3, TILE, LANES), lambda b, t: (0, t, 0)),
        ],
        out_specs=out_specs,
        out_shape=out_shape,
        compiler_params=_params("arbitrary", "arbitrary"),
        name="inproj",
    )(x, ctx, mod_l, norm_g.reshape(1, D_MODEL), w_r, b_r.reshape(1, PROJ_PAD), tab_a, tab_c)


def _diffattn_kernel(q_ref, qn_ref, k_ref, vt_ref, lam_ref, g_ref, o_ref, acc_ref, s_ref, m8_ref,
                     *, lam_init, q_tile0):
    step = pl.program_id(1)
    qb = step + q_tile0
    lp = lam_ref[...]
    lam = (jnp.exp(jnp.sum(lp[0:1] * lp[1:2], axis=-1, keepdims=True))
           - jnp.exp(jnp.sum(lp[2:3] * lp[3:4], axis=-1, keepdims=True)) + lam_init)
    q = q_ref[...]
    lane = _iota((1, BRANCH), 1)
    n_pairs = 2 * HEADS
    sub = TILE // 8
    ones_rows = jnp.ones((ONES_ROWS, TILE), BF16)

    def pair_q(qv, hm):
        return jnp.where(lane // DA_QK == hm, qv, jnp.zeros_like(qv))

    def logits(qm, nk, buf):
        m8 = None
        half = max(nk // 2, TILE)
        for r0 in range(0, nk, half):
            st = _dot_nt(k_ref[r0:r0 + half, :], qm)
            s_ref[buf, r0:r0 + half, :] = st
            mh = jnp.max(st.reshape(half // 8, 8, TILE), axis=0)
            m8 = mh if m8 is None else jnp.maximum(m8, mh)
        return m8

    def value_tile(hm, j, mb, ot):
        h = hm // 2
        st = s_ref[hm % 2, j * TILE:(j + 1) * TILE, :]
        e = jnp.exp2(st.reshape(sub, 8, TILE) - mb[None])
        vt = vt_ref[h * HEAD_DIM:(h + 1) * HEAD_DIM, j * TILE:(j + 1) * TILE]
        vt1 = jnp.concatenate([vt, ones_rows], axis=0)
        return ot + _dot(vt1, e.reshape(TILE, TILE).astype(BF16))

    def attend(nk, own_first_logits):
        n_kt = nk // TILE
        m8 = logits(pair_q(q, 0), nk, 0) if own_first_logits else m8_ref[...]
        for hm in range(n_pairs):
            mb = jnp.broadcast_to(jnp.max(m8, axis=0, keepdims=True), (8, TILE))
            if hm + 1 < n_pairs:
                m8 = logits(pair_q(q, hm + 1), nk, (hm + 1) % 2)
            else:
                m8_ref[...] = logits(pair_q(qn_ref[...], 0), T_ALL, 0)
            ot = jnp.zeros((HEAD_DIM + ONES_ROWS, TILE), F32)
            for j in range(n_kt):
                ot = value_tile(hm, j, mb, ot)
            l = ot[HEAD_DIM:HEAD_DIM + 1, :]
            ot = ot[0:HEAD_DIM, :]
            rows = slice((hm // 2) * HEAD_DIM, (hm // 2 + 1) * HEAD_DIM)
            if hm % 2 == 0:
                acc_ref[rows, :] = ot * (1.0 / l)
            else:
                acc_ref[rows, :] -= ot * (lam / l)
        o_ref[...] = (_head_rmsnorm(acc_ref[...].T, g_ref[...]) * (1.0 - lam_init)).astype(BF16)

    if q_tile0 == 0:
        @pl.when(qb == 0)
        def _():
            attend(CTX_LEN, True)
    else:
        @pl.when(step == 0)
        def _():
            m8_ref[...] = logits(pair_q(q, 0), T_ALL, 0)

    @pl.when(qb > 0)
    def _():
        attend(T_ALL, False)


def _diffattn_call(pa, vta, lam_p, g_row, layer_idx, need_ctx):
    bsz = pa.shape[0]
    q_tile0 = 0 if need_ctx else 1
    lam_init = 0.8 - 0.6 * math.exp(-0.3 * layer_idx)
    return pl.pallas_call(
        functools.partial(_diffattn_kernel, lam_init=lam_init, q_tile0=q_tile0),
        grid=(bsz, N_TILES - q_tile0),
        in_specs=[
            pl.BlockSpec((None, TILE, BRANCH), lambda b, t: (b, t + q_tile0, 0)),
            pl.BlockSpec((None, TILE, BRANCH), lambda b, t: (b, jnp.minimum(t + q_tile0 + 1, N_TILES - 1), 0)),
            pl.BlockSpec((None, T_ALL, BRANCH), lambda b, t: (b, 0, 1)),
            pl.BlockSpec((None, BRANCH, T_ALL), lambda b, t: (b, 0, 0)),
            pl.BlockSpec((4, DA_QK), lambda b, t: (0, 0)),
            pl.BlockSpec((1, BRANCH), lambda b, t: (0, 0)),
        ],
        out_specs=pl.BlockSpec((None, TILE, BRANCH), lambda b, t: (b, t, 0)),
        out_shape=jax.ShapeDtypeStruct((bsz, (N_TILES - q_tile0) * TILE, BRANCH), BF16),
        scratch_shapes=[pltpu.VMEM((BRANCH, TILE), F32), pltpu.VMEM((2, T_ALL, TILE), F32),
                        pltpu.VMEM((8, TILE), F32)],
        compiler_params=_params("arbitrary", "arbitrary"),
        name="diffattn",
    )(pa, pa, pa, vta, lam_p, g_row)


BAND = 2 * TILE


def _window_kernel(q_ref, k_ref, vt_ref, sink_ref, o_ref, acc_ref, s_ref, *, q_tile0):
    qb = pl.program_id(1) + q_tile0
    lane = _iota((1, KV_WIDTH), 1)
    ones_rows = jnp.ones((ONES_ROWS, TILE), BF16)
    group = HEADS // (KV_WIDTH // HEAD_DIM)

    def attend(band):
        if band:
            a = (qb - 1) * TILE
            start = jnp.clip(a - SW_WINDOW, 0, SEQ - BAND)
            row0 = pl.multiple_of(CTX_LEN + start, SW_WINDOW)
            kb = k_ref[pl.ds(row0, BAND), :]
            kpos = start + _iota((BAND, 1), 0)
            qpos = a + _iota((1, TILE), 1)
            valid = jnp.abs(qpos - kpos) <= SW_WINDOW
        sinks, maxes = [], []
        for h in range(HEADS):
            kvh, g = h // group, h % group
            qg = q_ref[:, g * KV_WIDTH:(g + 1) * KV_WIDTH]
            qm = jnp.where(lane // HEAD_DIM == kvh, qg, jnp.zeros_like(qg))
            sink = sink_ref[:, h:h + 1] * LOG2E
            sc = _dot_nt(k_ref[0:CTX_LEN, :], qm)
            s_ref[h, 0:CTX_LEN, :] = sc
            m = jnp.maximum(jnp.max(sc, axis=0, keepdims=True), sink)
            if band:
                sb = jnp.where(valid, _dot_nt(kb, qm), NEG_BIG)
                s_ref[h, CTX_LEN:CTX_LEN + BAND, :] = sb
                m = jnp.maximum(m, jnp.max(sb, axis=0, keepdims=True))
            sinks.append(sink)
            maxes.append(m)
        for h in range(HEADS):
            m = maxes[h]
            kv_rows = slice((h // group) * HEAD_DIM, (h // group + 1) * HEAD_DIM)
            vt1 = jnp.concatenate([vt_ref[kv_rows, 0:CTX_LEN], ones_rows], axis=0)
            pv = _dot(vt1, jnp.exp2(s_ref[h, 0:CTX_LEN, :] - m).astype(BF16))
            if band:
                vtb = vt_ref[kv_rows, pl.ds(row0, BAND)]
                ones_b = jnp.ones((ONES_ROWS, BAND), BF16)
                eb = jnp.exp2(s_ref[h, CTX_LEN:CTX_LEN + BAND, :] - m).astype(BF16)
                pv = pv + _dot(jnp.concatenate([vtb, ones_b], axis=0), eb)
            l = pv[HEAD_DIM:HEAD_DIM + 1, :] + jnp.exp2(sinks[h] - m)
            acc_ref[h * HEAD_DIM:(h + 1) * HEAD_DIM, :] = pv[0:HEAD_DIM, :] * (1.0 / l)
        o_ref[...] = acc_ref[...].T.astype(BF16)

    @pl.when(qb == 0)
    def _():
        attend(False)

    @pl.when(qb > 0)
    def _():
        attend(True)


def _window_call(pc, vtc, sink_row, need_ctx):
    bsz = pc.shape[0]
    q_tile0 = 0 if need_ctx else 1
    n_q = N_TILES - q_tile0
    return pl.pallas_call(
        functools.partial(_window_kernel, q_tile0=q_tile0),
        grid=(bsz, n_q),
        in_specs=[
            pl.BlockSpec((None, TILE, BRANCH), lambda b, t: (b, t + q_tile0, 0)),
            pl.BlockSpec((None, T_ALL, KV_WIDTH), lambda b, t: (b, 0, BRANCH // KV_WIDTH)),
            pl.BlockSpec((None, KV_WIDTH, T_ALL), lambda b, t: (b, 0, 0)),
            pl.BlockSpec((1, HEADS), lambda b, t: (0, 0)),
        ],
        out_specs=pl.BlockSpec((None, TILE, BRANCH), lambda b, t: (b, t, 0)),
        out_shape=jax.ShapeDtypeStruct((bsz, n_q * TILE, BRANCH), BF16),
        scratch_shapes=[pltpu.VMEM((BRANCH, TILE), F32), pltpu.VMEM((HEADS, CTX_LEN + BAND, TILE), F32)],
        compiler_params=_params("arbitrary", "arbitrary"),
        name="window",
    )(pc, pc, vtc, sink_row)


CH_PER_TILE = TILE // HG_CHUNK
HG_STATE = 64
ST_PER_TILE = TILE // HG_STATE
HG_UNROLL = 4
HG_FAST_LIMIT = 180.0
HG_FAST_SHIFT = 60.0


def _hgrn_gates(z, lb_terms):
    log2_ksig = _log_sigmoid(-z) * LOG2E
    if lb_terms is None:
        return _log_sigmoid(z) * LOG2E, _sigmoid(-z), log2_ksig
    lb, log_lb, log_1m = lb_terms
    bt = log_1m + _log_sigmoid(z)
    mx = jnp.maximum(log_lb, bt)
    log_f = mx + jnp.log(jnp.exp(log_lb - mx) + jnp.exp(bt - mx))
    return log_f * LOG2E, (1.0 - lb) * _sigmoid(-z), log_1m * LOG2E + log2_ksig


def _hgrn_intra(direction, q3, c3, u3, v3, head_ones):
    half = HG_CHUNK // 2
    slabs, meta = [], []
    for s in range(HG_CHUNK):
        us = jnp.broadcast_to(u3[:, s:s + 1, :], (CH_PER_TILE, half, BRANCH))
        for g in range(2):
            lo_row, hi_row = half * g, half * g + half - 1
            if direction == 0:
                none_valid, all_valid = hi_row < s, lo_row >= s
            else:
                none_valid, all_valid = lo_row > s, hi_row <= s
            if none_valid:
                continue
            d = c3[:, half * g:half * (g + 1), :] - us
            if not all_valid:
                row = _iota((1, half, 1), 1) + half * g
                d = jnp.where((row >= s) if direction == 0 else (row <= s), d, NEG_BIG)
            x = q3[:, half * g:half * (g + 1), :] * jnp.exp2(d)
            slabs.append(x.reshape(CH_PER_TILE * half, BRANCH).astype(BF16))
            meta.append((s, g))
    a_all = _dot(jnp.concatenate(slabs, axis=0), head_ones)
    o = [jnp.zeros((CH_PER_TILE, half, BRANCH), F32) for _ in range(2)]
    n = CH_PER_TILE * half
    vs = None
    for i, (s, g) in enumerate(meta):
        if i == 0 or meta[i - 1][0] != s:
            vs = jnp.broadcast_to(v3[:, s:s + 1, :], (CH_PER_TILE, half, BRANCH))
        o[g] = o[g] + a_all[i * n:(i + 1) * n].reshape(CH_PER_TILE, half, BRANCH) * vs
    return jnp.concatenate(o, axis=1).reshape(TILE, BRANCH)


def _hgrn_safe_block(direction, q, kk, log2_f, log2_k, v, v_heads, masks):
    rr, cc, lane, row_in, tri16, ones16, head_ones = masks
    shape3 = (CH_PER_TILE, HG_CHUNK, BRANCH)
    cum = _dot_exact_l(tri16[direction], log2_f)
    tot = _dot_exact_l(ones16, log2_f)
    o = _hgrn_intra(direction, q.reshape(shape3), cum.reshape(shape3), (cum - log2_k).reshape(shape3),
                    v.reshape(shape3), head_ones)

    def shift_rows(a, n):
        n = n % TILE
        return jnp.concatenate([a[TILE - n:, :], a[:TILE - n, :]], axis=0)

    a_heads = [jnp.zeros((TILE, TILE), F32) for _ in range(HEADS)]
    g = HG_CHUNK
    while g < HG_STATE:
        qt = (q * jnp.exp2(cum)).astype(BF16)
        ke = (kk * jnp.exp2(tot - cum)).astype(BF16)
        later_r = ((rr % (2 * g)) >= g) if direction == 0 else ((rr % (2 * g)) < g)
        later_c = ((cc % (2 * g)) >= g) if direction == 0 else ((cc % (2 * g)) < g)
        pair = ((rr // (2 * g)) == (cc // (2 * g))) & later_r & jnp.logical_not(later_c)
        for h in range(HEADS):
            qh = jnp.where(lane // HEAD_DIM == h, qt, jnp.zeros_like(qt))
            a_heads[h] = jnp.where(pair, _dot_nt(qh, ke), a_heads[h])
        later_row = ((row_in % (2 * g)) >= g) if direction == 0 else ((row_in % (2 * g)) < g)
        sign = 1 if direction == 0 else -1
        tot_other = jnp.where(later_row, shift_rows(tot, sign * g), shift_rows(tot, -sign * g))
        cum = cum + jnp.where(later_row, tot_other, 0.0)
        tot = tot + tot_other
        g *= 2
    a_cat = jnp.concatenate([a.astype(BF16) for a in a_heads], axis=1)
    return o + _dot(a_cat, v_heads)


def _hgrn_kernel(p_ref, lb_ref, g_ref, o_ref, qt_ref, ke_ref, vb_ref, dec_ref, of_ref, oi_ref, st_ref,
                 *, layer_idx):
    rr = _iota((TILE, TILE), 0)
    cc = _iota((TILE, TILE), 1)
    lane = _iota((1, BRANCH), 1)
    row_in = _iota((TILE, 1), 0)
    same16 = (rr // HG_CHUNK) == (cc // HG_CHUNK)
    same64 = (rr // HG_STATE) == (cc // HG_STATE)
    within = [same64 & (cc <= rr), same64 & (cc >= rr)]
    tri64 = [w.astype(BF16) for w in within]
    ones64 = same64.astype(BF16)
    head_mask = (rr // HEAD_DIM) == (cc // HEAD_DIM)
    safe_masks = (rr, cc, lane, row_in, [(same16 & (cc <= rr)).astype(BF16), (same16 & (cc >= rr)).astype(BF16)],
                  same16.astype(BF16), _head_sum_matrix())

    lb_terms = None
    if layer_idx > 0:
        lbp = lb_ref[...]
        lbp = lbp - jnp.max(lbp, axis=0, keepdims=True)
        sm = jnp.exp(lbp)
        sm = sm / jnp.sum(sm, axis=0, keepdims=True)
        lb = jnp.sum(sm[1:layer_idx + 1], axis=0, keepdims=True)
        lb_terms = (lb, jnp.log(lb), jnp.log(1.0 - lb))

    def tile_body(t, carry):
        rows = pl.ds(pl.multiple_of(t * TILE, TILE), TILE)
        q = p_ref[rows, 0:BRANCH] * (HEAD_DIM ** -0.5)
        v = p_ref[rows, 3 * BRANCH:4 * BRANCH]
        vbf = v.astype(BF16)
        vb_ref[rows, :] = vbf
        v_heads = jnp.concatenate([jnp.where(lane // HEAD_DIM == h, vbf, jnp.zeros_like(vbf))
                                   for h in range(HEADS)], axis=0)
        o_intra = None
        fallback = []
        for direction in (0, 1):
            z = p_ref[rows, (1 + direction) * BRANCH:(2 + direction) * BRANCH]
            log2_f, kk, log2_k = _hgrn_gates(z, lb_terms)
            cum = _dot_exact_l(tri64[direction], log2_f)
            tot = _dot_exact_l(ones64, log2_f)
            low = jnp.min(cum, axis=(0, 1), keepdims=True)
            q_fast = (q * jnp.exp2(cum + HG_FAST_SHIFT)).astype(BF16)
            k_fast = (kk * jnp.exp2(jnp.minimum(-cum, HG_FAST_LIMIT) - HG_FAST_SHIFT)).astype(BF16)
            keep = within[direction] & (low >= -HG_FAST_LIMIT)
            a_heads = [jnp.where(keep, _dot_nt(jnp.where(lane // HEAD_DIM == h, q_fast, jnp.zeros_like(q_fast)),
                                               k_fast), 0.0) for h in range(HEADS)]
            a_cat = jnp.concatenate([a.astype(BF16) for a in a_heads], axis=1)
            o_d = _dot(a_cat, v_heads)
            qt_ref[direction, rows, :] = (q * jnp.exp2(cum)).astype(BF16)
            ke_ref[direction, rows, :] = (kk * jnp.exp2(tot - cum)).astype(BF16)
            dec_ref[direction, t] = jnp.exp2(tot.reshape(ST_PER_TILE, HG_STATE, BRANCH)[:, 0, :])
            o_intra = o_d if o_intra is None else o_intra + o_d
            fallback.append((low, kk, log2_f, log2_k))
        of_ref[rows, :] = o_intra

        for direction, (low, kk, log2_f, log2_k) in enumerate(fallback):
            @pl.when(low[0, 0] < -HG_FAST_LIMIT)
            def _():
                of_ref[rows, :] += _hgrn_safe_block(direction, q, kk, log2_f, log2_k, v, v_heads, safe_masks)
        return carry

    lax.fori_loop(0, N_TILES, tile_body, 0)

    st_ref[...] = jnp.zeros_like(st_ref)
    n_ctx, n_all = CTX_LEN // HG_STATE, T_ALL // HG_STATE

    def state_step(i, direction):
        if direction == 0:
            c = i
        else:
            c = jnp.where(i < n_ctx, n_ctx - 1 - i, n_all - 1 + n_ctx - i)
        rows = pl.ds(pl.multiple_of(c * HG_STATE, HG_STATE), HG_STATE)
        st = st_ref[direction]
        oi_ref[direction, rows, :] = _dot_nt(qt_ref[direction, rows, :], st.astype(BF16))
        ds = _dot_tn(vb_ref[rows, :], ke_ref[direction, rows, :])
        dec = dec_ref[direction, c // ST_PER_TILE, pl.ds(c % ST_PER_TILE, 1), :]
        st_ref[direction] = st * dec + jnp.where(head_mask, ds, 0.0)

    def state_body(it, carry):
        for u in range(HG_UNROLL):
            for direction in (0, 1):
                state_step(it * HG_UNROLL + u, direction)
        return carry

    lax.fori_loop(0, n_all // HG_UNROLL, state_body, 0)

    o_ref[...] = _head_rmsnorm(of_ref[...] + oi_ref[0] + oi_ref[1], g_ref[...]).astype(BF16)


def _hgrn_call(pb, hg_lb, g_row, layer_idx):
    bsz = pb.shape[0]
    return pl.pallas_call(
        functools.partial(_hgrn_kernel, layer_idx=layer_idx),
        grid=(bsz,),
        in_specs=[
            pl.BlockSpec((None, T_ALL, SEG_B), lambda b: (b, 0, 0)),
            pl.BlockSpec((DEPTH, BRANCH), lambda b: (0, 0)),
            pl.BlockSpec((1, BRANCH), lambda b: (0, 0)),
        ],
        out_specs=pl.BlockSpec((None, T_ALL, BRANCH), lambda b: (b, 0, 0)),
        out_shape=jax.ShapeDtypeStruct((bsz, T_ALL, BRANCH), BF16),
        scratch_shapes=[pltpu.VMEM((2, T_ALL, BRANCH), BF16), pltpu.VMEM((2, T_ALL, BRANCH), BF16),
                        pltpu.VMEM((T_ALL, BRANCH), BF16), pltpu.VMEM((2, N_TILES, ST_PER_TILE, BRANCH), F32),
                        pltpu.VMEM((T_ALL, BRANCH), F32), pltpu.VMEM((2, T_ALL, BRANCH), F32),
                        pltpu.VMEM((2, BRANCH, BRANCH), F32)],
        compiler_params=_params("arbitrary"),
        name="hgrn",
    )(pb, hg_lb, g_row)


ML_UNROLL = 3


def _mlstm_logits(direction, c, p_ref, vt_ref, g_ref, gt_ref, ct_ref, n_ref, m_ref, consts):
    tri, valid, _, lane, row16, lane16 = consts
    rows = pl.ds(pl.multiple_of(c * TILE, TILE), TILE)
    q = p_ref[rows, 0:BRANCH]
    k = p_ref[rows, BRANCH:2 * BRANCH]
    g = g_ref[rows, :]
    gt = gt_ref[:, rows]
    cum = _dot_exact_l(tri, _log_sigmoid(g) * LOG2E)
    cum_t = _dot_exact_nt(_log_sigmoid(gt[8:16, :]) * LOG2E, tri)
    ig_t = gt[0:8, :] * LOG2E
    ct = ct_ref[direction]
    n0 = n_ref[direction]
    n_hi = n0.astype(BF16).astype(F32)
    n_lo = n0 - n_hi
    n_rows = (jnp.where((row16 < HEADS) & (lane16 == row16), n_hi, 0.0)
              + jnp.where((row16 >= HEADS) & (lane16 == row16 - HEADS), n_lo, 0.0)).astype(BF16)
    inter_all = _dot_nt(jnp.concatenate([ct.astype(BF16), n_rows], axis=0), q)
    heads = []
    for h in range(HEADS):
        r = HEADS * direction + h
        cumr = cum_t[r:r + 1, :]
        ucol = g[:, r:r + 1] * LOG2E - cum[:, 2 * HEADS + r:2 * HEADS + r + 1]
        toth = cumr[:, TILE - 1:TILE] if direction == 0 else cumr[:, 0:1]
        m0h = m_ref[direction, :, h:h + 1]
        logd = jnp.where(valid, cumr + ucol, NEG_BIG)
        inter = cumr + m0h
        m_t = jnp.maximum(jnp.max(logd, axis=0, keepdims=True), inter)
        qm = jnp.where(lane // HEAD_DIM == h, q, jnp.zeros_like(q))
        heads.append(dict(qk=_dot_nt(k, qm), logd=logd, m_t=m_t, g0=jnp.exp2(inter - m_t),
                          a_row=toth - cumr + ig_t[r:r + 1, :], carry=toth + m0h))
    return dict(direction=direction, rows=rows, k=k, ct=ct, n0=n0, inter_all=inter_all, heads=heads)


def _mlstm_outputs(cx, vt_ref, ht_ref, ct_ref, n_ref, m_ref, consts):
    _, _, head_mask, lane, row16, _ = consts
    direction, rows, k, inter_all = cx["direction"], cx["rows"], cx["k"], cx["inter_all"]
    vt = vt_ref[:, rows]
    ones_rows = jnp.ones((ONES_ROWS, TILE), BF16)
    w_rows, sp_row = [], jnp.zeros((1, BRANCH), F32)
    for h, hd in enumerate(cx["heads"]):
        m_t, g0 = hd["m_t"], hd["g0"]
        s_t = hd["qk"] * jnp.exp2(hd["logd"] - m_t)
        vt1 = jnp.concatenate([vt[h * HEAD_DIM:(h + 1) * HEAD_DIM, :], ones_rows], axis=0)
        pv = _dot(vt1, s_t.astype(BF16))
        num = pv[0:HEAD_DIM, :] + g0 * inter_all[h * HEAD_DIM:(h + 1) * HEAD_DIM, :]
        den = pv[HEAD_DIM:HEAD_DIM + 1, :] + g0 * (inter_all[BRANCH + h:BRANCH + h + 1, :]
                                                   + inter_all[BRANCH + HEADS + h:BRANCH + HEADS + h + 1, :])
        ht_ref[direction, h * HEAD_DIM:(h + 1) * HEAD_DIM, rows] = (
            num / jnp.maximum(jnp.abs(den), jnp.exp2(-m_t)))
        a_row = hd["a_row"]
        m_loc = jnp.max(a_row, axis=1, keepdims=True)
        m_new = jnp.maximum(hd["carry"], m_loc)
        sp = jnp.exp2(hd["carry"] - m_new)
        w_rows.append(jnp.exp2(a_row - m_loc) * jnp.exp2(m_loc - m_new))
        sp_row = sp_row + jnp.where(lane // HEAD_DIM == h, sp, 0.0)
        m_ref[direction, :, h:h + 1] = m_new

    w_block = jnp.concatenate([jnp.broadcast_to(w, (HEAD_DIM, TILE)) for w in w_rows], axis=0)
    vtw = (vt.astype(F32) * w_block).astype(BF16)
    w16 = jnp.zeros((ONES_ROWS, TILE), F32)
    for h in range(HEADS):
        w_hi = w_rows[h].astype(BF16).astype(F32)
        w16 = w16 + jnp.where(row16 == h, w_hi, 0.0) + jnp.where(row16 == HEADS + h, w_rows[h] - w_hi, 0.0)
    dall = _dot(jnp.concatenate([vtw, w16.astype(BF16)], axis=0), k)
    ct_ref[direction] = cx["ct"] * sp_row + jnp.where(head_mask, dall[0:BRANCH, :], 0.0)
    dn = jnp.zeros((1, BRANCH), F32)
    for h in range(HEADS):
        dn = dn + jnp.where(lane // HEAD_DIM == h,
                            dall[BRANCH + h:BRANCH + h + 1, :] + dall[BRANCH + HEADS + h:BRANCH + HEADS + h + 1, :], 0.0)
    n_ref[direction] = cx["n0"] * sp_row + dn


def _mlstm_kernel(p_ref, vt_ref, g_ref, gt_ref, gain_ref, o_ref, ht_ref, ct_ref, n_ref, m_ref):
    rr = _iota((TILE, TILE), 0)
    cc = _iota((TILE, TILE), 1)
    head_mask = (rr // HEAD_DIM) == (cc // HEAD_DIM)
    lane = _iota((1, BRANCH), 1)
    row16 = _iota((ONES_ROWS, BRANCH), 0)
    lane16 = _iota((ONES_ROWS, BRANCH), 1) // HEAD_DIM
    consts = []
    for direction in (0, 1):
        tri = ((cc <= rr) if direction == 0 else (cc >= rr)).astype(BF16)
        valid = (rr <= cc) if direction == 0 else (rr >= cc)
        consts.append((tri, valid, head_mask, lane, row16, lane16))
    ct_ref[...] = jnp.zeros_like(ct_ref)
    n_ref[...] = jnp.zeros_like(n_ref)
    m_ref[...] = jnp.zeros_like(m_ref)

    def body(it, carry):
        for u in range(ML_UNROLL):
            i = it * ML_UNROLL + u
            chunk = (i, jnp.where(i == 0, 0, N_TILES - i))
            cxs = [_mlstm_logits(d, chunk[d], p_ref, vt_ref, g_ref, gt_ref, ct_ref, n_ref, m_ref, consts[d])
                   for d in (0, 1)]
            for d in (0, 1):
                _mlstm_outputs(cxs[d], vt_ref, ht_ref, ct_ref, n_ref, m_ref, consts[d])
        return carry

    lax.fori_loop(0, N_TILES // ML_UNROLL, body, 0)

    def out_body(t, carry):
        rows = pl.ds(pl.multiple_of(t * TILE, TILE), TILE)
        o_ref[rows, :] = _head_rmsnorm((ht_ref[0, :, rows] + ht_ref[1, :, rows]).T,
                                       gain_ref[...]).astype(BF16)
        return carry

    lax.fori_loop(0, N_TILES, out_body, 0)


def _mlstm_call(pd, vtd, pg, pgt, g_row):
    bsz = pd.shape[0]
    return pl.pallas_call(
        _mlstm_kernel,
        grid=(bsz,),
        in_specs=[
            pl.BlockSpec((None, T_ALL, 2 * BRANCH), lambda b: (b, 0, 0)),
            pl.BlockSpec((None, BRANCH, T_ALL), lambda b: (b, 0, 0)),
            pl.BlockSpec((None, T_ALL, SEG_G), lambda b: (b, 0, 0)),
            pl.BlockSpec((None, SEG_G, T_ALL), lambda b: (b, 0, 0)),
            pl.BlockSpec((1, BRANCH), lambda b: (0, 0)),
        ],
        out_specs=pl.BlockSpec((None, T_ALL, BRANCH), lambda b: (b, 0, 0)),
        out_shape=jax.ShapeDtypeStruct((bsz, T_ALL, BRANCH), BF16),
        scratch_shapes=[pltpu.VMEM((2, BRANCH, T_ALL), F32), pltpu.VMEM((2, BRANCH, BRANCH), F32),
                        pltpu.VMEM((2, 1, BRANCH), F32), pltpu.VMEM((2, 1, LANES), F32)],
        compiler_params=_params("arbitrary"),
        name="mlstm",
    )(pd, vtd, pg, pgt, g_row)


def _outproj_kernel(x_ref, ctx_ref, ya_ref, yb_ref, yc_ref, yd_ref, po_ref, mod_ref, w_ref, fg_ref,
                    *out_refs, bsz, last):
    t = pl.program_id(1)
    is_ctx = jnp.logical_and(t == 0, not last)
    row = jnp.where(is_ctx, bsz, pl.program_id(0))
    gate_mod = mod_ref[pl.ds(row, 1), 2 * D_MODEL:3 * D_MODEL]
    po = po_ref[...].astype(F32)
    yd = yd_ref[...].astype(F32) * po[:, 0:BRANCH]
    mixed = jnp.concatenate([ya_ref[...].astype(F32), yb_ref[...].astype(F32), yc_ref[...].astype(F32), yd],
                            axis=-1)
    mixed = (mixed * po[:, BRANCH:]).astype(BF16)
    delta = gate_mod * _dot(mixed, w_ref[...])
    if last:
        xn = x_ref[...] + delta
        ms = jnp.mean(xn * xn, axis=-1, keepdims=True)
        out_refs[0][...] = xn * lax.rsqrt(ms + NORM_EPS) * fg_ref[...]
    else:
        x_out_ref, ctx_out_ref = out_refs

        @pl.when(t == 0)
        def _():
            ctx_out_ref[...] = ctx_ref[...] + delta

        @pl.when(t > 0)
        def _():
            x_out_ref[...] = x_ref[...] + delta


def _outproj_call(x, ctx, ya, yb, yc, yd, po, mod_l, w_out_bf, final_g, last):
    bsz = x.shape[0]
    tile0 = 1 if last else 0
    rows = mod_l.shape[0]

    def tok(width, arr):
        off = tile0 if arr.shape[1] == T_ALL else 0
        return pl.BlockSpec((None, TILE, width), lambda b, t: (b, t + off, 0))

    lat_spec = pl.BlockSpec((None, TILE, D_MODEL), lambda b, t: (b, jnp.maximum(t + tile0 - 1, 0), 0))
    ctx_spec = pl.BlockSpec((None, CTX_LEN, D_MODEL), lambda b, t: (b, 0, 0))
    lat_shape = jax.ShapeDtypeStruct((bsz, SEQ, D_MODEL), F32)
    ctx_shape = jax.ShapeDtypeStruct((bsz, CTX_LEN, D_MODEL), F32)
    return pl.pallas_call(
        functools.partial(_outproj_kernel, bsz=bsz, last=last),
        grid=(bsz, N_TILES - tile0),
        in_specs=[lat_spec, ctx_spec, tok(BRANCH, ya), tok(BRANCH, yb), tok(BRANCH, yc), tok(BRANCH, yd),
                  tok(SEG_O, po),
                  pl.BlockSpec((rows, 3 * D_MODEL), lambda b, t: (0, 0)),
                  pl.BlockSpec((D_MODEL, D_MODEL), lambda b, t: (0, 0)),
                  pl.BlockSpec((1, D_MODEL), lambda b, t: (0, 0))],
        out_specs=lat_spec if last else [lat_spec, ctx_spec],
        out_shape=lat_shape if last else [lat_shape, ctx_shape],
        compiler_params=_params("arbitrary", "arbitrary"),
        name="outproj",
    )(x, ctx, ya, yb, yc, yd, po, mod_l, w_out_bf, final_g.reshape(1, D_MODEL))


def _rope_tables(dim):
    quarter = dim // 4
    half = dim // 2
    pos = np.arange(SEQ)
    row = (pos // GRID_W).astype(np.float32)
    col = (pos % GRID_W).astype(np.float32)
    inv = (np.float32(ROPE_BASE) ** (-np.arange(0, half, 2, dtype=np.float32) / np.float32(half))).astype(np.float32)
    ang_r = row[:, None] * inv[None, :]
    ang_c = col[:, None] * inv[None, :]
    lane = np.arange(LANES) % dim
    part = lane // quarter
    freq = lane % quarter
    ang = np.where(part[None, :] < 2, ang_r[:, freq], ang_c[:, freq]).astype(np.float32)
    cos = np.cos(ang)
    sin = np.sin(ang)
    first = (part % 2 == 0)[None, :]
    s_next = np.where(first, -sin, 0.0)
    s_prev = np.where(first, 0.0, sin)
    tab = np.stack([cos, s_next, s_prev]).astype(np.float32)
    ident = np.stack([np.ones((CTX_LEN, LANES)), np.zeros((CTX_LEN, LANES)),
                      np.zeros((CTX_LEN, LANES))]).astype(np.float32)
    return jnp.asarray(np.concatenate([ident, tab], axis=1))


def _relayout_in_proj(w, bias):
    def cols(a):
        seg_a = a[..., 0:768]
        seg_b = a[..., 768:1792]
        qc = a[..., 1792:2048]
        qc = jnp.concatenate([qc[..., 0:64], qc[..., 128:192], qc[..., 64:128], qc[..., 192:256]], axis=-1)
        kvc = a[..., 2048:2304]
        seg_d = a[..., 2304:3072]
        gates = a[..., 3072:3088]
        seg_o = a[..., 3088:4368]
        pad = jnp.zeros(a.shape[:-1] + (SEG_G - 16,), a.dtype)
        return jnp.concatenate([seg_a, qc, kvc, seg_b, seg_d, gates, pad, seg_o], axis=-1)
    return cols(w), cols(bias)


def kernel(x, c, ctx, c_ctx, w_mod, b_mod, norm_g, w_in, b_in, diff_lam, diff_g, hg_lb, hg_g,
           sw_sink, ml_g, w_out, final_g):
    bsz = x.shape[0]
    tab_a = _rope_tables(DA_QK)
    tab_c = _rope_tables(HEAD_DIM)
    rows = ((bsz + 1 + 7) // 8) * 8
    cc = jnp.concatenate([c, c_ctx[None, :], jnp.zeros((rows - bsz - 1, D_MODEL), F32)], axis=0)
    mod = _mod_call(cc, w_mod, b_mod)
    tile4 = lambda g: jnp.tile(g, HEADS).reshape(1, BRANCH)
    w_in_bf = w_in.astype(BF16)
    w_out_bf = w_out.astype(BF16)
    for l in range(DEPTH):
        last = l == DEPTH - 1
        w_r, b_r = _relayout_in_proj(w_in_bf[l], b_in[l])
        pa, pc, pb, pd, pg, po, vta, vtc, vtd, pgt = _inproj_call(x, ctx, mod[l], norm_g[l], w_r, b_r,
                                                                 tab_a, tab_c)
        ya = _diffattn_call(pa, vta, diff_lam[l], tile4(diff_g[l]), l, not last)
        yb = _hgrn_call(pb, hg_lb, tile4(hg_g[l]), l)
        yc = _window_call(pc, vtc, sw_sink[l].reshape(1, HEADS), not last)
        yd = _mlstm_call(pd, vtd, pg, pgt, tile4(ml_g[l]))
        res = _outproj_call(x, ctx, ya, yb, yc, yd, po, mod[l], w_out_bf[l], final_g, last)
        if last:
            return res
        x, ctx = res
```

```python
import functools
import math

import numpy as np
import jax
import jax.numpy as jnp
from jax import lax
from jax.experimental import pallas as pl
from jax.experimental.pallas import tpu as pltpu

F32 = jnp.float32
BF16 = jnp.bfloat16

D_MODEL = 1024
SEQ = 2048
CTX_LEN = 256
T_ALL = CTX_LEN + SEQ
GRID_W = 64
DEPTH = 2
HEADS = 4
HEAD_DIM = 64
BRANCH = HEADS * HEAD_DIM
DA_QK = 32
SW_WINDOW = 128
HG_CHUNK = 16
ROPE_BASE = 10000.0
NORM_EPS = 1e-6
NEG_BIG = -1e30
LOG2E = math.log2(math.e)

TILE = 256
N_TILES = T_ALL // TILE
LANES = 128
ONES_ROWS = 16

SEG_A = 3 * BRANCH
KV_WIDTH = 2 * HEAD_DIM
SEG_C = BRANCH + 2 * KV_WIDTH
SEG_B = 4 * BRANCH
SEG_D = 3 * BRANCH
SEG_G = LANES
SEG_O = BRANCH + D_MODEL
OFF_A = 0
OFF_C = OFF_A + SEG_A
OFF_B = OFF_C + SEG_C
OFF_D = OFF_B + SEG_B
OFF_G = OFF_D + SEG_D
OFF_O = OFF_G + SEG_G
PROJ_PAD = OFF_O + SEG_O

VMEM_LIMIT = 56 * 1024 * 1024


def _params(*sem):
    return pltpu.CompilerParams(dimension_semantics=sem, vmem_limit_bytes=VMEM_LIMIT)


def _dot(a, b):
    return jnp.dot(a, b, preferred_element_type=F32)


def _dot_nt(a, b):
    return lax.dot_general(a, b, (((1,), (1,)), ((), ())), preferred_element_type=F32)


def _dot_tn(a, b):
    return lax.dot_general(a, b, (((0,), (0,)), ((), ())), preferred_element_type=F32)


def _split3(x):
    x1 = x.astype(BF16)
    r1 = x - x1.astype(F32)
    x2 = r1.astype(BF16)
    x3 = (r1 - x2.astype(F32)).astype(BF16)
    return x1, x2, x3


def _dot_exact_l(m01, x):
    x1, x2, x3 = _split3(x)
    return _dot(m01, x1) + _dot(m01, x2) + _dot(m01, x3)


def _dot_exact_r(x, m01):
    x1, x2, x3 = _split3(x)
    return _dot(x1, m01) + _dot(x2, m01) + _dot(x3, m01)


def _dot_exact_nt(x, m01):
    x1, x2, x3 = _split3(x)
    return _dot_nt(x1, m01) + _dot_nt(x2, m01) + _dot_nt(x3, m01)


def _sigmoid(z):
    e = jnp.exp(-jnp.abs(z))
    r = 1.0 / (1.0 + e)
    return jnp.where(z >= 0, r, e * r)


def _log_sigmoid(z):
    return jnp.minimum(z, 0.0) - jnp.log(1.0 + jnp.exp(-jnp.abs(z)))


def _iota(shape, dim):
    return lax.broadcasted_iota(jnp.int32, shape, dim)


def _head_sum_matrix():
    r = _iota((BRANCH, BRANCH), 0) // HEAD_DIM
    c = _iota((BRANCH, BRANCH), 1) // HEAD_DIM
    return (r == c).astype(BF16)


def _head_rmsnorm(o, g_row):
    ss = _dot_exact_r(o * o, _head_sum_matrix())
    return o * lax.rsqrt(ss * (1.0 / HEAD_DIM) + NORM_EPS) * g_row


def _mod_kernel(cc_ref, w_ref, b_ref, o_ref):
    cc = cc_ref[...]
    a = (cc * _sigmoid(cc)).astype(BF16)
    o_ref[...] = _dot(a, w_ref[...].astype(BF16)) + b_ref[...]


def _mod_call(cc, w_mod, b_mod):
    rows = cc.shape[0]
    nblk = 3
    return pl.pallas_call(
        _mod_kernel,
        grid=(DEPTH, nblk),
        in_specs=[
            pl.BlockSpec((rows, D_MODEL), lambda l, j: (0, 0)),
            pl.BlockSpec((None, D_MODEL, D_MODEL), lambda l, j: (l, 0, j)),
            pl.BlockSpec((None, 1, D_MODEL), lambda l, j: (l, 0, j)),
        ],
        out_specs=pl.BlockSpec((None, rows, D_MODEL), lambda l, j: (l, 0, j)),
        out_shape=jax.ShapeDtypeStruct((DEPTH, rows, 3 * D_MODEL), F32),
        compiler_params=_params("arbitrary", "arbitrary"),
        name="mod",
    )(cc, w_mod, b_mod.reshape(DEPTH, 1, 3 * D_MODEL))


def _rope(slab, cos, sin_next, sin_prev, off):
    nxt = pltpu.roll(slab, LANES - off, 1)
    prv = pltpu.roll(slab, off, 1)
    return slab * cos + nxt * sin_next + prv * sin_prev


def _inproj_kernel(x_ref, ctx_ref, mod_ref, ng_ref, w_ref, b_ref, ta_ref, tc_ref,
                   pa_ref, pc_ref, pb_ref, pd_ref, pg_ref, po_ref, vta_ref, vtc_ref, vtd_ref, pgt_ref,
                   *, bsz):
    is_ctx = pl.program_id(1) == 0
    row = jnp.where(is_ctx, bsz, pl.program_id(0))
    x = jnp.where(is_ctx, ctx_ref[...], x_ref[...])
    mrow = mod_ref[pl.ds(row, 1), :]
    shift = mrow[:, 0:D_MODEL]
    scale = mrow[:, D_MODEL:2 * D_MODEL]
    ms = jnp.mean(x * x, axis=-1, keepdims=True)
    h = x * lax.rsqrt(ms + NORM_EPS) * ng_ref[...]
    h = (h * (1.0 + scale) + shift).astype(BF16)

    def proj(off, width):
        return _dot(h, w_ref[:, off:off + width]) + b_ref[:, off:off + width]

    def rope_seg(acc, tab_ref, off, q_scale, k_slabs):
        cos, s_next, s_prev = tab_ref[0], tab_ref[1], tab_ref[2]
        outs = []
        for j in range(2 + k_slabs):
            r = _rope(acc[:, j * LANES:(j + 1) * LANES], cos, s_next, s_prev, off)
            outs.append(r * q_scale if j < 2 else r)
        outs.append(acc[:, (2 + k_slabs) * LANES:])
        return jnp.concatenate(outs, axis=-1)

    acco = proj(OFF_O, SEG_O)
    po_ref[...] = jnp.concatenate(
        [_sigmoid(acco[:, 0:BRANCH]), acco[:, BRANCH:] * _sigmoid(acco[:, BRANCH:])], axis=-1).astype(BF16)
    acca = rope_seg(proj(OFF_A, SEG_A), ta_ref, DA_QK // 4, DA_QK ** -0.5 * LOG2E, 2)
    pa_ref[...] = acca[:, 0:2 * BRANCH].astype(BF16)
    vta_ref[...] = acca[:, 2 * BRANCH:].T.astype(BF16)
    accc = rope_seg(proj(OFF_C, SEG_C), tc_ref, HEAD_DIM // 4, HEAD_DIM ** -0.5 * LOG2E, 1)
    pc_ref[...] = accc[:, 0:BRANCH + KV_WIDTH].astype(BF16)
    vtc_ref[...] = accc[:, BRANCH + KV_WIDTH:].T.astype(BF16)
    accd = proj(OFF_D, SEG_D)
    pd_ref[...] = jnp.concatenate(
        [accd[:, 0:BRANCH], accd[:, BRANCH:2 * BRANCH] * (HEAD_DIM ** -0.5)], axis=-1).astype(BF16)
    vtd_ref[...] = accd[:, 2 * BRANCH:].T.astype(BF16)
    gates = proj(OFF_G, SEG_G)
    pg_ref[...] = gates
    pgt_ref[...] = gates.T
    pb_ref[...] = proj(OFF_B, SEG_B)


def _inproj_call(x, ctx, mod, norm_g, w_r, b_r, tab_a, tab_c, layer):
    bsz = x.shape[0]
    rows = mod.shape[1]
    widths = [(2 * BRANCH, BF16), (BRANCH + KV_WIDTH, BF16), (SEG_B, F32), (2 * BRANCH, BF16), (SEG_G, F32),
              (SEG_O, BF16)]
    out_specs = [pl.BlockSpec((None, TILE, w), lambda b, t: (b, t, 0)) for w, _ in widths]
    out_shape = [jax.ShapeDtypeStruct((bsz, T_ALL, w), dt) for w, dt in widths]
    for rows_t, dt in ((BRANCH, BF16), (KV_WIDTH, BF16), (BRANCH, BF16), (SEG_G, F32)):
        out_specs.append(pl.BlockSpec((None, rows_t, TILE), lambda b, t: (b, 0, t)))
        out_shape.append(jax.ShapeDtypeStruct((bsz, rows_t, T_ALL), dt))
    return pl.pallas_call(
        functools.partial(_inproj_kernel, bsz=bsz),
        grid=(bsz, N_TILES),
        in_specs=[
            pl.BlockSpec((None, TILE, D_MODEL), lambda b, t: (b, jnp.maximum(t - 1, 0), 0)),
            pl.BlockSpec((None, CTX_LEN, D_MODEL), lambda b, t: (b, 0, 0)),
            pl.BlockSpec((None, rows, 3 * D_MODEL), lambda b, t: (layer, 0, 0)),
            pl.BlockSpec((None, 1, D_MODEL), lambda b, t: (layer, 0, 0)),
            pl.BlockSpec((None, D_MODEL, PROJ_PAD), lambda b, t: (layer, 0, 0)),
            pl.BlockSpec((None, 1, PROJ_PAD), lambda b, t: (layer, 0, 0)),
            pl.BlockSpec((3, TILE, LANES), lambda b, t: (0, t, 0)),
            pl.BlockSpec((3, TILE, LANES), lambda b, t: (0, t, 0)),
        ],
        out_specs=out_specs,
        out_shape=out_shape,
        compiler_params=_params("arbitrary", "arbitrary"),
        name="inproj",
    )(x, ctx, mod, norm_g.reshape(-1, 1, D_MODEL), w_r, b_r.reshape(-1, 1, PROJ_PAD), tab_a, tab_c)


def _diffattn_kernel(q_ref, qn_ref, k_ref, vt_ref, lam_ref, g_ref, o_ref, acc_ref, s_ref, m8_ref,
                     *, lam_init, q_tile0):
    step = pl.program_id(1)
    qb = step + q_tile0
    lp = lam_ref[...]
    lam = (jnp.exp(jnp.sum(lp[0:1] * lp[1:2], axis=-1, keepdims=True))
           - jnp.exp(jnp.sum(lp[2:3] * lp[3:4], axis=-1, keepdims=True)) + lam_init)
    q = q_ref[...]
    lane = _iota((1, BRANCH), 1)
    n_pairs = 2 * HEADS
    sub = TILE // 8
    ones_rows = jnp.ones((ONES_ROWS, TILE), BF16)

    def pair_q(qv, hm):
        return jnp.where(lane // DA_QK == hm, qv, jnp.zeros_like(qv))

    def logits(qm, nk, buf):
        m8 = None
        half = max(nk // 2, TILE)
        for r0 in range(0, nk, half):
            st = _dot_nt(k_ref[r0:r0 + half, :], qm)
            s_ref[buf, r0:r0 + half, :] = st
            mh = jnp.max(st.reshape(half // 8, 8, TILE), axis=0)
            m8 = mh if m8 is None else jnp.maximum(m8, mh)
        return m8

    def value_tile(hm, j, mb, ot):
        h = hm // 2
        st = s_ref[hm % 2, j * TILE:(j + 1) * TILE, :]
        e = jnp.exp2(st.reshape(sub, 8, TILE) - mb[None])
        vt = vt_ref[h * HEAD_DIM:(h + 1) * HEAD_DIM, j * TILE:(j + 1) * TILE]
        vt1 = jnp.concatenate([vt, ones_rows], axis=0)
        return ot + _dot(vt1, e.reshape(TILE, TILE).astype(BF16))

    def attend(nk, own_first_logits):
        n_kt = nk // TILE
        m8 = logits(pair_q(q, 0), nk, 0) if own_first_logits else m8_ref[...]
        for hm in range(n_pairs):
            mb = jnp.broadcast_to(jnp.max(m8, axis=0, keepdims=True), (8, TILE))
            if hm + 1 < n_pairs:
                m8 = logits(pair_q(q, hm + 1), nk, (hm + 1) % 2)
            else:
                m8_ref[...] = logits(pair_q(qn_ref[...], 0), T_ALL, 0)
            ot = jnp.zeros((HEAD_DIM + ONES_ROWS, TILE), F32)
            for j in range(n_kt):
                ot = value_tile(hm, j, mb, ot)
            l = ot[HEAD_DIM:HEAD_DIM + 1, :]
            ot = ot[0:HEAD_DIM, :]
            rows = slice((hm // 2) * HEAD_DIM, (hm // 2 + 1) * HEAD_DIM)
            if hm % 2 == 0:
                acc_ref[rows, :] = ot * (1.0 / l)
            else:
                acc_ref[rows, :] -= ot * (lam / l)
        o_ref[...] = (_head_rmsnorm(acc_ref[...].T, g_ref[...]) * (1.0 - lam_init)).astype(BF16)

    if q_tile0 == 0:
        @pl.when(qb == 0)
        def _():
            attend(CTX_LEN, True)
    else:
        @pl.when(step == 0)
        def _():
            m8_ref[...] = logits(pair_q(q, 0), T_ALL, 0)

    @pl.when(qb > 0)
    def _():
        attend(T_ALL, False)


def _diffattn_call(pa, vta, lam_p, g_row, layer_idx, need_ctx):
    bsz = pa.shape[0]
    q_tile0 = 0 if need_ctx else 1
    lam_init = 0.8 - 0.6 * math.exp(-0.3 * layer_idx)
    return pl.pallas_call(
        functools.partial(_diffattn_kernel, lam_init=lam_init, q_tile0=q_tile0),
        grid=(bsz, N_TILES - q_tile0),
        in_specs=[
            pl.BlockSpec((None, TILE, BRANCH), lambda b, t: (b, t + q_tile0, 0)),
            pl.BlockSpec((None, TILE, BRANCH), lambda b, t: (b, jnp.minimum(t + q_tile0 + 1, N_TILES - 1), 0)),
            pl.BlockSpec((None, T_ALL, BRANCH), lambda b, t: (b, 0, 1)),
            pl.BlockSpec((None, BRANCH, T_ALL), lambda b, t: (b, 0, 0)),
            pl.BlockSpec((4, DA_QK), lambda b, t: (0, 0)),
            pl.BlockSpec((1, BRANCH), lambda b, t: (0, 0)),
        ],
        out_specs=pl.BlockSpec((None, TILE, BRANCH), lambda b, t: (b, t, 0)),
        out_shape=jax.ShapeDtypeStruct((bsz, (N_TILES - q_tile0) * TILE, BRANCH), BF16),
        scratch_shapes=[pltpu.VMEM((BRANCH, TILE), F32), pltpu.VMEM((2, T_ALL, TILE), F32),
                        pltpu.VMEM((8, TILE), F32)],
        compiler_params=_params("arbitrary", "arbitrary"),
        name="diffattn",
    )(pa, pa, pa, vta, lam_p, g_row)


BAND = 2 * TILE


def _window_kernel(q_ref, k_ref, vt_ref, sink_ref, o_ref, acc_ref, s_ref, *, q_tile0):
    qb = pl.program_id(1) + q_tile0
    lane = _iota((1, KV_WIDTH), 1)
    ones_rows = jnp.ones((ONES_ROWS, TILE), BF16)
    group = HEADS // (KV_WIDTH // HEAD_DIM)

    def attend(band):
        if band:
            a = (qb - 1) * TILE
            start = jnp.clip(a - SW_WINDOW, 0, SEQ - BAND)
            row0 = pl.multiple_of(CTX_LEN + start, SW_WINDOW)
            kb = k_ref[pl.ds(row0, BAND), :]
            kpos = start + _iota((BAND, 1), 0)
            qpos = a + _iota((1, TILE), 1)
            valid = jnp.abs(qpos - kpos) <= SW_WINDOW
        sinks, maxes = [], []
        for h in range(HEADS):
            kvh, g = h // group, h % group
            qg = q_ref[:, g * KV_WIDTH:(g + 1) * KV_WIDTH]
            qm = jnp.where(lane // HEAD_DIM == kvh, qg, jnp.zeros_like(qg))
            sink = sink_ref[:, h:h + 1] * LOG2E
            sc = _dot_nt(k_ref[0:CTX_LEN, :], qm)
            s_ref[h, 0:CTX_LEN, :] = sc
            m = jnp.maximum(jnp.max(sc, axis=0, keepdims=True), sink)
            if band:
                sb = jnp.where(valid, _dot_nt(kb, qm), NEG_BIG)
                s_ref[h, CTX_LEN:CTX_LEN + BAND, :] = sb
                m = jnp.maximum(m, jnp.max(sb, axis=0, keepdims=True))
            sinks.append(sink)
            maxes.append(m)
        for h in range(HEADS):
            m = maxes[h]
            kv_rows = slice((h // group) * HEAD_DIM, (h // group + 1) * HEAD_DIM)
            vt1 = jnp.concatenate([vt_ref[kv_rows, 0:CTX_LEN], ones_rows], axis=0)
            pv = _dot(vt1, jnp.exp2(s_ref[h, 0:CTX_LEN, :] - m).astype(BF16))
            if band:
                vtb = vt_ref[kv_rows, pl.ds(row0, BAND)]
                ones_b = jnp.ones((ONES_ROWS, BAND), BF16)
                eb = jnp.exp2(s_ref[h, CTX_LEN:CTX_LEN + BAND, :] - m).astype(BF16)
                pv = pv + _dot(jnp.concatenate([vtb, ones_b], axis=0), eb)
            l = pv[HEAD_DIM:HEAD_DIM + 1, :] + jnp.exp2(sinks[h] - m)
            acc_ref[h * HEAD_DIM:(h + 1) * HEAD_DIM, :] = pv[0:HEAD_DIM, :] * (1.0 / l)
        o_ref[...] = acc_ref[...].T.astype(BF16)

    @pl.when(qb == 0)
    def _():
        attend(False)

    @pl.when(qb > 0)
    def _():
        attend(True)


def _window_call(pc, vtc, sink_row, need_ctx):
    bsz = pc.shape[0]
    q_tile0 = 0 if need_ctx else 1
    n_q = N_TILES - q_tile0
    return pl.pallas_call(
        functools.partial(_window_kernel, q_tile0=q_tile0),
        grid=(bsz, n_q),
        in_specs=[
            pl.BlockSpec((None, TILE, BRANCH), lambda b, t: (b, t + q_tile0, 0)),
            pl.BlockSpec((None, T_ALL, KV_WIDTH), lambda b, t: (b, 0, BRANCH // KV_WIDTH)),
            pl.BlockSpec((None, KV_WIDTH, T_ALL), lambda b, t: (b, 0, 0)),
            pl.BlockSpec((1, HEADS), lambda b, t: (0, 0)),
        ],
        out_specs=pl.BlockSpec((None, TILE, BRANCH), lambda b, t: (b, t, 0)),
        out_shape=jax.ShapeDtypeStruct((bsz, n_q * TILE, BRANCH), BF16),
        scratch_shapes=[pltpu.VMEM((BRANCH, TILE), F32), pltpu.VMEM((HEADS, CTX_LEN + BAND, TILE), F32)],
        compiler_params=_params("arbitrary", "arbitrary"),
        name="window",
    )(pc, pc, vtc, sink_row)


CH_PER_TILE = TILE // HG_CHUNK
HG_STATE = 64
ST_PER_TILE = TILE // HG_STATE
HG_UNROLL = 4
HG_FAST_LIMIT = 180.0
HG_FAST_SHIFT = 60.0


def _hgrn_gates(z, lb_terms):
    log2_ksig = _log_sigmoid(-z) * LOG2E
    if lb_terms is None:
        return _log_sigmoid(z) * LOG2E, _sigmoid(-z), log2_ksig
    lb, log_lb, log_1m = lb_terms
    bt = log_1m + _log_sigmoid(z)
    mx = jnp.maximum(log_lb, bt)
    log_f = mx + jnp.log(jnp.exp(log_lb - mx) + jnp.exp(bt - mx))
    return log_f * LOG2E, (1.0 - lb) * _sigmoid(-z), log_1m * LOG2E + log2_ksig


def _hgrn_intra(direction, q3, c3, u3, v3, head_ones):
    half = HG_CHUNK // 2
    slabs, meta = [], []
    for s in range(HG_CHUNK):
        us = jnp.broadcast_to(u3[:, s:s + 1, :], (CH_PER_TILE, half, BRANCH))
        for g in range(2):
            lo_row, hi_row = half * g, half * g + half - 1
            if direction == 0:
                none_valid, all_valid = hi_row < s, lo_row >= s
            else:
                none_valid, all_valid = lo_row > s, hi_row <= s
            if none_valid:
                continue
            d = c3[:, half * g:half * (g + 1), :] - us
            if not all_valid:
                row = _iota((1, half, 1), 1) + half * g
                d = jnp.where((row >= s) if direction == 0 else (row <= s), d, NEG_BIG)
            x = q3[:, half * g:half * (g + 1), :] * jnp.exp2(d)
            slabs.append(x.reshape(CH_PER_TILE * half, BRANCH).astype(BF16))
            meta.append((s, g))
    a_all = _dot(jnp.concatenate(slabs, axis=0), head_ones)
    o = [jnp.zeros((CH_PER_TILE, half, BRANCH), F32) for _ in range(2)]
    n = CH_PER_TILE * half
    vs = None
    for i, (s, g) in enumerate(meta):
        if i == 0 or meta[i - 1][0] != s:
            vs = jnp.broadcast_to(v3[:, s:s + 1, :], (CH_PER_TILE, half, BRANCH))
        o[g] = o[g] + a_all[i * n:(i + 1) * n].reshape(CH_PER_TILE, half, BRANCH) * vs
    return jnp.concatenate(o, axis=1).reshape(TILE, BRANCH)


def _hgrn_safe_block(direction, q, kk, log2_f, log2_k, v, v_heads, masks):
    rr, cc, lane, row_in, tri16, ones16, head_ones = masks
    shape3 = (CH_PER_TILE, HG_CHUNK, BRANCH)
    cum = _dot_exact_l(tri16[direction], log2_f)
    tot = _dot_exact_l(ones16, log2_f)
    o = _hgrn_intra(direction, q.reshape(shape3), cum.reshape(shape3), (cum - log2_k).reshape(shape3),
                    v.reshape(shape3), head_ones)

    def shift_rows(a, n):
        n = n % TILE
        return jnp.concatenate([a[TILE - n:, :], a[:TILE - n, :]], axis=0)

    a_heads = [jnp.zeros((TILE, TILE), F32) for _ in range(HEADS)]
    g = HG_CHUNK
    while g < HG_STATE:
        qt = (q * jnp.exp2(cum)).astype(BF16)
        ke = (kk * jnp.exp2(tot - cum)).astype(BF16)
        later_r = ((rr % (2 * g)) >= g) if direction == 0 else ((rr % (2 * g)) < g)
        later_c = ((cc % (2 * g)) >= g) if direction == 0 else ((cc % (2 * g)) < g)
        pair = ((rr // (2 * g)) == (cc // (2 * g))) & later_r & jnp.logical_not(later_c)
        for h in range(HEADS):
            qh = jnp.where(lane // HEAD_DIM == h, qt, jnp.zeros_like(qt))
            a_heads[h] = jnp.where(pair, _dot_nt(qh, ke), a_heads[h])
        later_row = ((row_in % (2 * g)) >= g) if direction == 0 else ((row_in % (2 * g)) < g)
        sign = 1 if direction == 0 else -1
        tot_other = jnp.where(later_row, shift_rows(tot, sign * g), shift_rows(tot, -sign * g))
        cum = cum + jnp.where(later_row, tot_other, 0.0)
        tot = tot + tot_other
        g *= 2
    a_cat = jnp.concatenate([a.astype(BF16) for a in a_heads], axis=1)
    return o + _dot(a_cat, v_heads)


def _hgrn_kernel(p_ref, lb_ref, g_ref, o_ref, qt_ref, ke_ref, vb_ref, dec_ref, of_ref, oi_ref, st_ref,
                 *, layer_idx):
    rr = _iota((TILE, TILE), 0)
    cc = _iota((TILE, TILE), 1)
    lane = _iota((1, BRANCH), 1)
    row_in = _iota((TILE, 1), 0)
    same16 = (rr // HG_CHUNK) == (cc // HG_CHUNK)
    same64 = (rr // HG_STATE) == (cc // HG_STATE)
    within = [same64 & (cc <= rr), same64 & (cc >= rr)]
    tri64 = [w.astype(BF16) for w in within]
    ones64 = same64.astype(BF16)
    head_mask = (rr // HEAD_DIM) == (cc // HEAD_DIM)
    safe_masks = (rr, cc, lane, row_in, [(same16 & (cc <= rr)).astype(BF16), (same16 & (cc >= rr)).astype(BF16)],
                  same16.astype(BF16), _head_sum_matrix())

    lb_terms = None
    if layer_idx > 0:
        lbp = lb_ref[...]
        lbp = lbp - jnp.max(lbp, axis=0, keepdims=True)
        sm = jnp.exp(lbp)
        sm = sm / jnp.sum(sm, axis=0, keepdims=True)
        lb = jnp.sum(sm[1:layer_idx + 1], axis=0, keepdims=True)
        lb_terms = (lb, jnp.log(lb), jnp.log(1.0 - lb))

    def tile_body(t, carry):
        rows = pl.ds(pl.multiple_of(t * TILE, TILE), TILE)
        q = p_ref[rows, 0:BRANCH] * (HEAD_DIM ** -0.5)
        v = p_ref[rows, 3 * BRANCH:4 * BRANCH]
        vbf = v.astype(BF16)
        vb_ref[rows, :] = vbf
        v_heads = jnp.concatenate([jnp.where(lane // HEAD_DIM == h, vbf, jnp.zeros_like(vbf))
                                   for h in range(HEADS)], axis=0)
        o_intra = None
        fallback = []
        for direction in (0, 1):
            z = p_ref[rows, (1 + direction) * BRANCH:(2 + direction) * BRANCH]
            log2_f, kk, log2_k = _hgrn_gates(z, lb_terms)
            cum = _dot_exact_l(tri64[direction], log2_f)
            tot = _dot_exact_l(ones64, log2_f)
            low = jnp.min(cum, axis=(0, 1), keepdims=True)
            q_fast = (q * jnp.exp2(cum + HG_FAST_SHIFT)).astype(BF16)
            k_fast = (kk * jnp.exp2(jnp.minimum(-cum, HG_FAST_LIMIT) - HG_FAST_SHIFT)).astype(BF16)
            keep = within[direction] & (low >= -HG_FAST_LIMIT)
            a_heads = [jnp.where(keep, _dot_nt(jnp.where(lane // HEAD_DIM == h, q_fast, jnp.zeros_like(q_fast)),
                                               k_fast), 0.0) for h in range(HEADS)]
            a_cat = jnp.concatenate([a.astype(BF16) for a in a_heads], axis=1)
            o_d = _dot(a_cat, v_heads)
            qt_ref[direction, rows, :] = (q * jnp.exp2(cum)).astype(BF16)
            ke_ref[direction, rows, :] = (kk * jnp.exp2(tot - cum)).astype(BF16)
            dec_ref[direction, t] = jnp.exp2(tot.reshape(ST_PER_TILE, HG_STATE, BRANCH)[:, 0, :])
            o_intra = o_d if o_intra is None else o_intra + o_d
            fallback.append((low, kk, log2_f, log2_k))
        of_ref[rows, :] = o_intra

        for direction, (low, kk, log2_f, log2_k) in enumerate(fallback):
            @pl.when(low[0, 0] < -HG_FAST_LIMIT)
            def _():
                of_ref[rows, :] += _hgrn_safe_block(direction, q, kk, log2_f, log2_k, v, v_heads, safe_masks)
        return carry

    lax.fori_loop(0, N_TILES, tile_body, 0)

    st_ref[...] = jnp.zeros_like(st_ref)
    n_ctx, n_all = CTX_LEN // HG_STATE, T_ALL // HG_STATE

    def state_step(i, direction):
        if direction == 0:
            c = i
        else:
            c = jnp.where(i < n_ctx, n_ctx - 1 - i, n_all - 1 + n_ctx - i)
        rows = pl.ds(pl.multiple_of(c * HG_STATE, HG_STATE), HG_STATE)
        st = st_ref[direction]
        oi_ref[direction, rows, :] = _dot_nt(qt_ref[direction, rows, :], st.astype(BF16))
        ds = _dot_tn(vb_ref[rows, :], ke_ref[direction, rows, :])
        dec = dec_ref[direction, c // ST_PER_TILE, pl.ds(c % ST_PER_TILE, 1), :]
        st_ref[direction] = st * dec + jnp.where(head_mask, ds, 0.0)

    def state_body(it, carry):
        for u in range(HG_UNROLL):
            for direction in (0, 1):
                state_step(it * HG_UNROLL + u, direction)
        return carry

    lax.fori_loop(0, n_all // HG_UNROLL, state_body, 0)

    o_ref[...] = _head_rmsnorm(of_ref[...] + oi_ref[0] + oi_ref[1], g_ref[...]).astype(BF16)


def _hgrn_call(pb, hg_lb, g_row, layer_idx):
    bsz = pb.shape[0]
    return pl.pallas_call(
        functools.partial(_hgrn_kernel, layer_idx=layer_idx),
        grid=(bsz,),
        in_specs=[
            pl.BlockSpec((None, T_ALL, SEG_B), lambda b: (b, 0, 0)),
            pl.BlockSpec((DEPTH, BRANCH), lambda b: (0, 0)),
            pl.BlockSpec((1, BRANCH), lambda b: (0, 0)),
        ],
        out_specs=pl.BlockSpec((None, T_ALL, BRANCH), lambda b: (b, 0, 0)),
        out_shape=jax.ShapeDtypeStruct((bsz, T_ALL, BRANCH), BF16),
        scratch_shapes=[pltpu.VMEM((2, T_ALL, BRANCH), BF16), pltpu.VMEM((2, T_ALL, BRANCH), BF16),
                        pltpu.VMEM((T_ALL, BRANCH), BF16), pltpu.VMEM((2, N_TILES, ST_PER_TILE, BRANCH), F32),
                        pltpu.VMEM((T_ALL, BRANCH), F32), pltpu.VMEM((2, T_ALL, BRANCH), F32),
                        pltpu.VMEM((2, BRANCH, BRANCH), F32)],
        compiler_params=_params("arbitrary"),
        name="hgrn",
    )(pb, hg_lb, g_row)


ML_CHUNK = 256
ML_UNROLL = 3


def _mlstm_logits(direction, c, p_ref, vt_ref, g_ref, gt_ref, ct_ref, n_ref, m_ref, consts):
    tri, valid, _, lane, row16, lane16 = consts
    rows = pl.ds(pl.multiple_of(c * ML_CHUNK, ML_CHUNK), ML_CHUNK)
    q = p_ref[rows, 0:BRANCH]
    k = p_ref[rows, BRANCH:2 * BRANCH]
    g = g_ref[rows, :]
    gt = gt_ref[:, rows]
    cum = _dot_exact_l(tri, _log_sigmoid(g) * LOG2E)
    cum_t = _dot_exact_nt(_log_sigmoid(gt[8:16, :]) * LOG2E, tri)
    ig_t = gt[0:8, :] * LOG2E
    ct = ct_ref[direction]
    n0 = n_ref[direction]
    n_hi = n0.astype(BF16).astype(F32)
    n_lo = n0 - n_hi
    n_rows = (jnp.where((row16 < HEADS) & (lane16 == row16), n_hi, 0.0)
              + jnp.where((row16 >= HEADS) & (lane16 == row16 - HEADS), n_lo, 0.0)).astype(BF16)
    inter_all = _dot_nt(jnp.concatenate([ct.astype(BF16), n_rows], axis=0), q)
    heads = []
    for h in range(HEADS):
        r = HEADS * direction + h
        cumr = cum_t[r:r + 1, :]
        ucol = g[:, r:r + 1] * LOG2E - cum[:, 2 * HEADS + r:2 * HEADS + r + 1]
        toth = cumr[:, ML_CHUNK - 1:ML_CHUNK] if direction == 0 else cumr[:, 0:1]
        m0h = m_ref[direction, :, h:h + 1]
        logd = jnp.where(valid, cumr + ucol, NEG_BIG)
        inter = cumr + m0h
        m_t = jnp.maximum(jnp.max(logd, axis=0, keepdims=True), inter)
        qm = jnp.where(lane // HEAD_DIM == h, q, jnp.zeros_like(q))
        heads.append(dict(qk=_dot_nt(k, qm), logd=logd, m_t=m_t, g0=jnp.exp2(inter - m_t),
                          a_row=toth - cumr + ig_t[r:r + 1, :], carry=toth + m0h))
    return dict(direction=direction, rows=rows, k=k, ct=ct, n0=n0, inter_all=inter_all, heads=heads)


def _mlstm_outputs(cx, vt_ref, ht_ref, ct_ref, n_ref, m_ref, consts):
    _, _, head_mask, lane, _, _ = consts
    direction, rows, k, inter_all = cx["direction"], cx["rows"], cx["k"], cx["inter_all"]
    vt = vt_ref[:, rows]
    ones_rows = jnp.ones((ONES_ROWS, ML_CHUNK), BF16)
    row16 = _iota((ONES_ROWS, ML_CHUNK), 0)
    w_rows, sp_row = [], jnp.zeros((1, BRANCH), F32)
    for h, hd in enumerate(cx["heads"]):
        m_t, g0 = hd["m_t"], hd["g0"]
        s_t = hd["qk"] * jnp.exp2(hd["logd"] - m_t)
        vt1 = jnp.concatenate([vt[h * HEAD_DIM:(h + 1) * HEAD_DIM, :], ones_rows], axis=0)
        pv = _dot(vt1, s_t.astype(BF16))
        num = pv[0:HEAD_DIM, :] + g0 * inter_all[h * HEAD_DIM:(h + 1) * HEAD_DIM, :]
        den = pv[HEAD_DIM:HEAD_DIM + 1, :] + g0 * (inter_all[BRANCH + h:BRANCH + h + 1, :]
                                                   + inter_all[BRANCH + HEADS + h:BRANCH + HEADS + h + 1, :])
        ht_ref[direction, h * HEAD_DIM:(h + 1) * HEAD_DIM, rows] = (
            num / jnp.maximum(jnp.abs(den), jnp.exp2(-m_t)))
        a_row = hd["a_row"]
        m_loc = jnp.max(a_row, axis=1, keepdims=True)
        m_new = jnp.maximum(hd["carry"], m_loc)
        sp = jnp.exp2(hd["carry"] - m_new)
        w_rows.append(jnp.exp2(a_row - m_loc) * jnp.exp2(m_loc - m_new))
        sp_row = sp_row + jnp.where(lane // HEAD_DIM == h, sp, 0.0)
        m_ref[direction, :, h:h + 1] = m_new

    w_block = jnp.concatenate([jnp.broadcast_to(w, (HEAD_DIM, ML_CHUNK)) for w in w_rows], axis=0)
    vtw = (vt.astype(F32) * w_block).astype(BF16)
    w16 = jnp.zeros((ONES_ROWS, ML_CHUNK), F32)
    for h in range(HEADS):
        w_hi = w_rows[h].astype(BF16).astype(F32)
        w16 = w16 + jnp.where(row16 == h, w_hi, 0.0) + jnp.where(row16 == HEADS + h, w_rows[h] - w_hi, 0.0)
    dall = _dot(jnp.concatenate([vtw, w16.astype(BF16)], axis=0), k)
    ct_ref[direction] = cx["ct"] * sp_row + jnp.where(head_mask, dall[0:BRANCH, :], 0.0)
    dn = jnp.zeros((1, BRANCH), F32)
    for h in range(HEADS):
        dn = dn + jnp.where(lane // HEAD_DIM == h,
                            dall[BRANCH + h:BRANCH + h + 1, :] + dall[BRANCH + HEADS + h:BRANCH + HEADS + h + 1, :], 0.0)
    n_ref[direction] = cx["n0"] * sp_row + dn


def _mlstm_kernel(p_ref, vt_ref, g_ref, gt_ref, gain_ref, o_ref, ht_ref, ct_ref, n_ref, m_ref):
    head_mask = (_iota((BRANCH, BRANCH), 0) // HEAD_DIM) == (_iota((BRANCH, BRANCH), 1) // HEAD_DIM)
    rr = _iota((ML_CHUNK, ML_CHUNK), 0)
    cc = _iota((ML_CHUNK, ML_CHUNK), 1)
    lane = _iota((1, BRANCH), 1)
    row16 = _iota((ONES_ROWS, BRANCH), 0)
    lane16 = _iota((ONES_ROWS, BRANCH), 1) // HEAD_DIM
    consts = []
    for direction in (0, 1):
        tri = ((cc <= rr) if direction == 0 else (cc >= rr)).astype(BF16)
        valid = (rr <= cc) if direction == 0 else (rr >= cc)
        consts.append((tri, valid, head_mask, lane, row16, lane16))
    ct_ref[...] = jnp.zeros_like(ct_ref)
    n_ref[...] = jnp.zeros_like(n_ref)
    m_ref[...] = jnp.zeros_like(m_ref)
    n_ctx, n_all = CTX_LEN // ML_CHUNK, T_ALL // ML_CHUNK

    def body(it, carry):
        for u in range(ML_UNROLL):
            i = it * ML_UNROLL + u
            chunk = (i, jnp.where(i < n_ctx, n_ctx - 1 - i, n_all - 1 + n_ctx - i))
            cxs = [_mlstm_logits(d, chunk[d], p_ref, vt_ref, g_ref, gt_ref, ct_ref, n_ref, m_ref, consts[d])
                   for d in (0, 1)]
            for d in (0, 1):
                _mlstm_outputs(cxs[d], vt_ref, ht_ref, ct_ref, n_ref, m_ref, consts[d])
        return carry

    lax.fori_loop(0, n_all // ML_UNROLL, body, 0)

    def out_body(t, carry):
        rows = pl.ds(pl.multiple_of(t * TILE, TILE), TILE)
        o_ref[rows, :] = _head_rmsnorm((ht_ref[0, :, rows] + ht_ref[1, :, rows]).T,
                                       gain_ref[...]).astype(BF16)
        return carry

    lax.fori_loop(0, N_TILES, out_body, 0)


def _mlstm_call(pd, vtd, pg, pgt, g_row):
    bsz = pd.shape[0]
    return pl.pallas_call(
        _mlstm_kernel,
        grid=(bsz,),
        in_specs=[
            pl.BlockSpec((None, T_ALL, 2 * BRANCH), lambda b: (b, 0, 0)),
            pl.BlockSpec((None, BRANCH, T_ALL), lambda b: (b, 0, 0)),
            pl.BlockSpec((None, T_ALL, SEG_G), lambda b: (b, 0, 0)),
            pl.BlockSpec((None, SEG_G, T_ALL), lambda b: (b, 0, 0)),
            pl.BlockSpec((1, BRANCH), lambda b: (0, 0)),
        ],
        out_specs=pl.BlockSpec((None, T_ALL, BRANCH), lambda b: (b, 0, 0)),
        out_shape=jax.ShapeDtypeStruct((bsz, T_ALL, BRANCH), BF16),
        scratch_shapes=[pltpu.VMEM((2, BRANCH, T_ALL), F32), pltpu.VMEM((2, BRANCH, BRANCH), F32),
                        pltpu.VMEM((2, 1, BRANCH), F32), pltpu.VMEM((2, 1, LANES), F32)],
        compiler_params=_params("arbitrary"),
        name="mlstm",
    )(pd, vtd, pg, pgt, g_row)


def _outproj_kernel(x_ref, ctx_ref, ya_ref, yb_ref, yc_ref, yd_ref, po_ref, mod_ref, w_ref, fg_ref,
                    *out_refs, bsz, last):
    t = pl.program_id(1)
    is_ctx = jnp.logical_and(t == 0, not last)
    row = jnp.where(is_ctx, bsz, pl.program_id(0))
    gate_mod = mod_ref[pl.ds(row, 1), 2 * D_MODEL:3 * D_MODEL]
    po = po_ref[...].astype(F32)
    yd = yd_ref[...].astype(F32) * po[:, 0:BRANCH]
    mixed = jnp.concatenate([ya_ref[...].astype(F32), yb_ref[...].astype(F32), yc_ref[...].astype(F32), yd],
                            axis=-1)
    mixed = (mixed * po[:, BRANCH:]).astype(BF16)
    delta = gate_mod * _dot(mixed, w_ref[...])
    if last:
        xn = x_ref[...] + delta
        ms = jnp.mean(xn * xn, axis=-1, keepdims=True)
        out_refs[0][...] = xn * lax.rsqrt(ms + NORM_EPS) * fg_ref[...]
    else:
        x_out_ref, ctx_out_ref = out_refs

        @pl.when(t == 0)
        def _():
            ctx_out_ref[...] = ctx_ref[...] + delta

        @pl.when(t > 0)
        def _():
            x_out_ref[...] = x_ref[...] + delta


def _outproj_call(x, ctx, ya, yb, yc, yd, po, mod, w_out_bf, final_g, layer, last):
    bsz = x.shape[0]
    tile0 = 1 if last else 0
    rows = mod.shape[1]

    def tok(width, arr):
        off = tile0 if arr.shape[1] == T_ALL else 0
        return pl.BlockSpec((None, TILE, width), lambda b, t: (b, t + off, 0))

    lat_spec = pl.BlockSpec((None, TILE, D_MODEL), lambda b, t: (b, jnp.maximum(t + tile0 - 1, 0), 0))
    ctx_spec = pl.BlockSpec((None, CTX_LEN, D_MODEL), lambda b, t: (b, 0, 0))
    lat_shape = jax.ShapeDtypeStruct((bsz, SEQ, D_MODEL), F32)
    ctx_shape = jax.ShapeDtypeStruct((bsz, CTX_LEN, D_MODEL), F32)
    return pl.pallas_call(
        functools.partial(_outproj_kernel, bsz=bsz, last=last),
        grid=(bsz, N_TILES - tile0),
        in_specs=[lat_spec, ctx_spec, tok(BRANCH, ya), tok(BRANCH, yb), tok(BRANCH, yc), tok(BRANCH, yd),
                  tok(SEG_O, po),
                  pl.BlockSpec((None, rows, 3 * D_MODEL), lambda b, t: (layer, 0, 0)),
                  pl.BlockSpec((None, D_MODEL, D_MODEL), lambda b, t: (layer, 0, 0)),
                  pl.BlockSpec((1, D_MODEL), lambda b, t: (0, 0))],
        out_specs=lat_spec if last else [lat_spec, ctx_spec],
        out_shape=lat_shape if last else [lat_shape, ctx_shape],
        compiler_params=_params("arbitrary", "arbitrary"),
        name="outproj",
    )(x, ctx, ya, yb, yc, yd, po, mod, w_out_bf, final_g.reshape(1, D_MODEL))


def _rope_tables(dim):
    quarter = dim // 4
    half = dim // 2
    pos = np.arange(SEQ)
    row = (pos // GRID_W).astype(np.float32)
    col = (pos % GRID_W).astype(np.float32)
    inv = (np.float32(ROPE_BASE) ** (-np.arange(0, half, 2, dtype=np.float32) / np.float32(half))).astype(np.float32)
    ang_r = row[:, None] * inv[None, :]
    ang_c = col[:, None] * inv[None, :]
    lane = np.arange(LANES) % dim
    part = lane // quarter
    freq = lane % quarter
    ang = np.where(part[None, :] < 2, ang_r[:, freq], ang_c[:, freq]).astype(np.float32)
    cos = np.cos(ang)
    sin = np.sin(ang)
    first = (part % 2 == 0)[None, :]
    s_next = np.where(first, -sin, 0.0)
    s_prev = np.where(first, 0.0, sin)
    tab = np.stack([cos, s_next, s_prev]).astype(np.float32)
    ident = np.stack([np.ones((CTX_LEN, LANES)), np.zeros((CTX_LEN, LANES)),
                      np.zeros((CTX_LEN, LANES))]).astype(np.float32)
    return jnp.asarray(np.concatenate([ident, tab], axis=1))


def _relayout_cols(a):
    seg_a = a[..., 0:768]
    seg_b = a[..., 768:1792]
    qc = a[..., 1792:2048]
    qc = jnp.concatenate([qc[..., 0:64], qc[..., 128:192], qc[..., 64:128], qc[..., 192:256]], axis=-1)
    kvc = a[..., 2048:2304]
    seg_d = a[..., 2304:3072]
    gates = a[..., 3072:3088]
    seg_o = a[..., 3088:4368]
    pad = jnp.zeros(a.shape[:-1] + (SEG_G - 16,), a.dtype)
    return jnp.concatenate([seg_a, qc, kvc, seg_b, seg_d, gates, pad, seg_o], axis=-1)


WPREP_ROWS = 128


def _wprep_kernel(w_ref, o_ref):
    o_ref[...] = _relayout_cols(w_ref[...]).astype(BF16)


def _wprep_call(w_in):
    depth, _, width = w_in.shape
    return pl.pallas_call(
        _wprep_kernel,
        grid=(depth, D_MODEL // WPREP_ROWS),
        in_specs=[pl.BlockSpec((None, WPREP_ROWS, width), lambda l, r: (l, r, 0))],
        out_specs=pl.BlockSpec((None, WPREP_ROWS, PROJ_PAD), lambda l, r: (l, r, 0)),
        out_shape=jax.ShapeDtypeStruct((depth, D_MODEL, PROJ_PAD), BF16),
        compiler_params=_params("arbitrary", "arbitrary"),
        name="wprep",
    )(w_in)


def kernel(x, c, ctx, c_ctx, w_mod, b_mod, norm_g, w_in, b_in, diff_lam, diff_g, hg_lb, hg_g,
           sw_sink, ml_g, w_out, final_g):
    bsz = x.shape[0]
    tab_a = _rope_tables(DA_QK)
    tab_c = _rope_tables(HEAD_DIM)
    rows = ((bsz + 1 + 7) // 8) * 8
    cc = jnp.concatenate([c, c_ctx[None, :], jnp.zeros((rows - bsz - 1, D_MODEL), F32)], axis=0)
    mod = _mod_call(cc, w_mod, b_mod)
    tile4 = lambda g: jnp.tile(g, HEADS).reshape(1, BRANCH)
    w_r = _wprep_call(w_in)
    b_r = _relayout_cols(b_in)
    w_out_bf = w_out.astype(BF16)
    for l in range(DEPTH):
        last = l == DEPTH - 1
        pa, pc, pb, pd, pg, po, vta, vtc, vtd, pgt = _inproj_call(x, ctx, mod, norm_g, w_r, b_r,
                                                                 tab_a, tab_c, l)
        ya = _diffattn_call(pa, vta, diff_lam[l], tile4(diff_g[l]), l, not last)
        yb = _hgrn_call(pb, hg_lb, tile4(hg_g[l]), l)
        yc = _window_call(pc, vtc, sw_sink[l].reshape(1, HEADS), not last)
        yd = _mlstm_call(pd, vtd, pg, pgt, tile4(ml_g[l]))
        res = _outproj_call(x, ctx, ya, yb, yc, yd, po, mod, w_out_bf, final_g, l, last)
        if last:
            return res
        x, ctx = res
```

```python
import functools
import math

import numpy as np
import jax
import jax.numpy as jnp
from jax import lax
from jax.experimental import pallas as pl
from jax.experimental.pallas import tpu as pltpu

F32 = jnp.float32
BF16 = jnp.bfloat16

D_MODEL = 1024
SEQ = 2048
CTX_LEN = 256
T_ALL = CTX_LEN + SEQ
GRID_W = 64
DEPTH = 2
HEADS = 4
HEAD_DIM = 64
BRANCH = HEADS * HEAD_DIM
DA_QK = 32
SW_WINDOW = 128
HG_CHUNK = 16
ROPE_BASE = 10000.0
NORM_EPS = 1e-6
NEG_BIG = -1e30
LOG2E = math.log2(math.e)

TILE = 256
N_TILES = T_ALL // TILE
LANES = 128
ONES_ROWS = 16

SEG_A = 3 * BRANCH
KV_WIDTH = 2 * HEAD_DIM
SEG_C = BRANCH + 2 * KV_WIDTH
SEG_B = 4 * BRANCH
SEG_D = 3 * BRANCH
SEG_G = LANES
SEG_O = BRANCH + D_MODEL
OFF_A = 0
OFF_C = OFF_A + SEG_A
OFF_B = OFF_C + SEG_C
OFF_D = OFF_B + SEG_B
OFF_G = OFF_D + SEG_D
OFF_O = OFF_G + SEG_G
PROJ_PAD = OFF_O + SEG_O

VMEM_LIMIT = 56 * 1024 * 1024


def _params(*sem):
    return pltpu.CompilerParams(dimension_semantics=sem, vmem_limit_bytes=VMEM_LIMIT)


def _dot(a, b):
    return jnp.dot(a, b, preferred_element_type=F32)


def _dot_nt(a, b):
    return lax.dot_general(a, b, (((1,), (1,)), ((), ())), preferred_element_type=F32)


def _dot_tn(a, b):
    return lax.dot_general(a, b, (((0,), (0,)), ((), ())), preferred_element_type=F32)


def _split3(x):
    x1 = x.astype(BF16)
    r1 = x - x1.astype(F32)
    x2 = r1.astype(BF16)
    x3 = (r1 - x2.astype(F32)).astype(BF16)
    return x1, x2, x3


def _dot_exact_l(m01, x):
    x1, x2, x3 = _split3(x)
    return _dot(m01, x1) + _dot(m01, x2) + _dot(m01, x3)


def _dot_exact_r(x, m01):
    x1, x2, x3 = _split3(x)
    return _dot(x1, m01) + _dot(x2, m01) + _dot(x3, m01)


def _dot_exact_nt(x, m01):
    x1, x2, x3 = _split3(x)
    return _dot_nt(x1, m01) + _dot_nt(x2, m01) + _dot_nt(x3, m01)


def _sigmoid(z):
    e = jnp.exp(-jnp.abs(z))
    r = 1.0 / (1.0 + e)
    return jnp.where(z >= 0, r, e * r)


def _log_sigmoid(z):
    return jnp.minimum(z, 0.0) - jnp.log(1.0 + jnp.exp(-jnp.abs(z)))


def _iota(shape, dim):
    return lax.broadcasted_iota(jnp.int32, shape, dim)


def _head_sum_matrix():
    r = _iota((BRANCH, BRANCH), 0) // HEAD_DIM
    c = _iota((BRANCH, BRANCH), 1) // HEAD_DIM
    return (r == c).astype(BF16)


def _head_rmsnorm(o, g_row):
    ss = _dot_exact_r(o * o, _head_sum_matrix())
    return o * lax.rsqrt(ss * (1.0 / HEAD_DIM) + NORM_EPS) * g_row


def _mod_kernel(cc_ref, w_ref, b_ref, o_ref):
    cc = cc_ref[...]
    a = (cc * _sigmoid(cc)).astype(BF16)
    o_ref[...] = _dot(a, w_ref[...].astype(BF16)) + b_ref[...]


def _mod_call(cc, w_mod, b_mod):
    rows = cc.shape[0]
    nblk = 3
    return pl.pallas_call(
        _mod_kernel,
        grid=(DEPTH, nblk),
        in_specs=[
            pl.BlockSpec((rows, D_MODEL), lambda l, j: (0, 0)),
            pl.BlockSpec((None, D_MODEL, D_MODEL), lambda l, j: (l, 0, j)),
            pl.BlockSpec((None, 1, D_MODEL), lambda l, j: (l, 0, j)),
        ],
        out_specs=pl.BlockSpec((None, rows, D_MODEL), lambda l, j: (l, 0, j)),
        out_shape=jax.ShapeDtypeStruct((DEPTH, rows, 3 * D_MODEL), F32),
        compiler_params=_params("arbitrary", "arbitrary"),
        name="mod",
    )(cc, w_mod, b_mod.reshape(DEPTH, 1, 3 * D_MODEL))


def _rope(slab, cos, sin_next, sin_prev, off):
    nxt = pltpu.roll(slab, LANES - off, 1)
    prv = pltpu.roll(slab, off, 1)
    return slab * cos + nxt * sin_next + prv * sin_prev


def _inproj_kernel(x_ref, ctx_ref, mod_ref, ng_ref, w_ref, b_ref, ta_ref, tc_ref,
                   pa_ref, pc_ref, pb_ref, pd_ref, pg_ref, po_ref, vta_ref, vtc_ref, vtd_ref, pgt_ref,
                   *, bsz):
    is_ctx = pl.program_id(1) == 0
    row = jnp.where(is_ctx, bsz, pl.program_id(0))
    x = jnp.where(is_ctx, ctx_ref[...], x_ref[...])
    mrow = mod_ref[pl.ds(row, 1), :]
    shift = mrow[:, 0:D_MODEL]
    scale = mrow[:, D_MODEL:2 * D_MODEL]
    ms = jnp.mean(x * x, axis=-1, keepdims=True)
    h = x * lax.rsqrt(ms + NORM_EPS) * ng_ref[...]
    h = (h * (1.0 + scale) + shift).astype(BF16)

    def proj(off, width):
        return _dot(h, w_ref[:, off:off + width]) + b_ref[:, off:off + width]

    def rope_seg(acc, tab_ref, off, q_scale, k_slabs):
        cos, s_next, s_prev = tab_ref[0], tab_ref[1], tab_ref[2]
        outs = []
        for j in range(2 + k_slabs):
            r = _rope(acc[:, j * LANES:(j + 1) * LANES], cos, s_next, s_prev, off)
            outs.append(r * q_scale if j < 2 else r)
        outs.append(acc[:, (2 + k_slabs) * LANES:])
        return jnp.concatenate(outs, axis=-1)

    acco = proj(OFF_O, SEG_O)
    po_ref[...] = jnp.concatenate(
        [_sigmoid(acco[:, 0:BRANCH]), acco[:, BRANCH:] * _sigmoid(acco[:, BRANCH:])], axis=-1).astype(BF16)
    acca = rope_seg(proj(OFF_A, SEG_A), ta_ref, DA_QK // 4, DA_QK ** -0.5 * LOG2E, 2)
    pa_ref[...] = acca[:, 0:2 * BRANCH].astype(BF16)
    vta_ref[...] = acca[:, 2 * BRANCH:].T.astype(BF16)
    accc = rope_seg(proj(OFF_C, SEG_C), tc_ref, HEAD_DIM // 4, HEAD_DIM ** -0.5 * LOG2E, 1)
    pc_ref[...] = accc[:, 0:BRANCH + KV_WIDTH].astype(BF16)
    vtc_ref[...] = accc[:, BRANCH + KV_WIDTH:].T.astype(BF16)
    accd = proj(OFF_D, SEG_D)
    pd_ref[...] = jnp.concatenate(
        [accd[:, 0:BRANCH], accd[:, BRANCH:2 * BRANCH] * (HEAD_DIM ** -0.5)], axis=-1).astype(BF16)
    vtd_ref[...] = accd[:, 2 * BRANCH:].T.astype(BF16)
    gates = proj(OFF_G, SEG_G)
    pg_ref[...] = gates
    pgt_ref[...] = gates.T
    pb_ref[...] = proj(OFF_B, SEG_B)


def _inproj_call(x, ctx, mod, norm_g, w_r, b_r, tab_a, tab_c, layer):
    bsz = x.shape[0]
    rows = mod.shape[1]
    widths = [(2 * BRANCH, BF16), (BRANCH + KV_WIDTH, BF16), (SEG_B, F32), (2 * BRANCH, BF16), (SEG_G, F32),
              (SEG_O, BF16)]
    out_specs = [pl.BlockSpec((None, TILE, w), lambda b, t: (b, t, 0)) for w, _ in widths]
    out_shape = [jax.ShapeDtypeStruct((bsz, T_ALL, w), dt) for w, dt in widths]
    for rows_t, dt in ((BRANCH, BF16), (KV_WIDTH, BF16), (BRANCH, BF16), (SEG_G, F32)):
        out_specs.append(pl.BlockSpec((None, rows_t, TILE), lambda b, t: (b, 0, t)))
        out_shape.append(jax.ShapeDtypeStruct((bsz, rows_t, T_ALL), dt))
    return pl.pallas_call(
        functools.partial(_inproj_kernel, bsz=bsz),
        grid=(bsz, N_TILES),
        in_specs=[
            pl.BlockSpec((None, TILE, D_MODEL), lambda b, t: (b, jnp.maximum(t - 1, 0), 0)),
            pl.BlockSpec((None, CTX_LEN, D_MODEL), lambda b, t: (b, 0, 0)),
            pl.BlockSpec((None, rows, 3 * D_MODEL), lambda b, t: (layer, 0, 0)),
            pl.BlockSpec((None, 1, D_MODEL), lambda b, t: (layer, 0, 0)),
            pl.BlockSpec((None, D_MODEL, PROJ_PAD), lambda b, t: (layer, 0, 0)),
            pl.BlockSpec((None, 1, PROJ_PAD), lambda b, t: (layer, 0, 0)),
            pl.BlockSpec((3, TILE, LANES), lambda b, t: (0, t, 0)),
            pl.BlockSpec((3, TILE, LANES), lambda b, t: (0, t, 0)),
        ],
        out_specs=out_specs,
        out_shape=out_shape,
        compiler_params=_params("arbitrary", "arbitrary"),
        name="inproj",
    )(x, ctx, mod, norm_g.reshape(-1, 1, D_MODEL), w_r, b_r.reshape(-1, 1, PROJ_PAD), tab_a, tab_c)


def _diffattn_kernel(q_ref, qn_ref, k_ref, vt_ref, lam_ref, g_ref, o_ref, acc_ref, s_ref, m8_ref,
                     *, lam_init, q_tile0):
    step = pl.program_id(1)
    qb = step + q_tile0
    lp = lam_ref[...]
    lam = (jnp.exp(jnp.sum(lp[0:1] * lp[1:2], axis=-1, keepdims=True))
           - jnp.exp(jnp.sum(lp[2:3] * lp[3:4], axis=-1, keepdims=True)) + lam_init)
    q = q_ref[...]
    lane = _iota((1, BRANCH), 1)
    n_pairs = 2 * HEADS
    sub = TILE // 8
    ones_rows = jnp.ones((ONES_ROWS, TILE), BF16)

    def pair_q(qv, hm):
        return jnp.where(lane // DA_QK == hm, qv, jnp.zeros_like(qv))

    def logits(qm, nk, buf):
        m8 = None
        half = max(nk // 2, TILE)
        for r0 in range(0, nk, half):
            st = _dot_nt(k_ref[r0:r0 + half, :], qm)
            s_ref[buf, r0:r0 + half, :] = st
            mh = jnp.max(st.reshape(half // 8, 8, TILE), axis=0)
            m8 = mh if m8 is None else jnp.maximum(m8, mh)
        return m8

    def value_tile(hm, j, mb, ot):
        h = hm // 2
        st = s_ref[hm % 2, j * TILE:(j + 1) * TILE, :]
        e = jnp.exp2(st.reshape(sub, 8, TILE) - mb[None])
        vt = vt_ref[h * HEAD_DIM:(h + 1) * HEAD_DIM, j * TILE:(j + 1) * TILE]
        vt1 = jnp.concatenate([vt, ones_rows], axis=0)
        return ot + _dot(vt1, e.reshape(TILE, TILE).astype(BF16))

    def attend(nk, own_first_logits):
        n_kt = nk // TILE
        m8 = logits(pair_q(q, 0), nk, 0) if own_first_logits else m8_ref[...]
        for hm in range(n_pairs):
            mb = jnp.broadcast_to(jnp.max(m8, axis=0, keepdims=True), (8, TILE))
            if hm + 1 < n_pairs:
                m8 = logits(pair_q(q, hm + 1), nk, (hm + 1) % 2)
            else:
                m8_ref[...] = logits(pair_q(qn_ref[...], 0), T_ALL, 0)
            ot = jnp.zeros((HEAD_DIM + ONES_ROWS, TILE), F32)
            for j in range(n_kt):
                ot = value_tile(hm, j, mb, ot)
            l = ot[HEAD_DIM:HEAD_DIM + 1, :]
            ot = ot[0:HEAD_DIM, :]
            rows = slice((hm // 2) * HEAD_DIM, (hm // 2 + 1) * HEAD_DIM)
            if hm % 2 == 0:
                acc_ref[rows, :] = ot * (1.0 / l)
            else:
                acc_ref[rows, :] -= ot * (lam / l)
        o_ref[...] = (_head_rmsnorm(acc_ref[...].T, g_ref[...]) * (1.0 - lam_init)).astype(BF16)

    if q_tile0 == 0:
        @pl.when(qb == 0)
        def _():
            attend(CTX_LEN, True)
    else:
        @pl.when(step == 0)
        def _():
            m8_ref[...] = logits(pair_q(q, 0), T_ALL, 0)

    @pl.when(qb > 0)
    def _():
        attend(T_ALL, False)


def _diffattn_call(pa, vta, lam_p, g_row, layer_idx, need_ctx):
    bsz = pa.shape[0]
    q_tile0 = 0 if need_ctx else 1
    lam_init = 0.8 - 0.6 * math.exp(-0.3 * layer_idx)
    return pl.pallas_call(
        functools.partial(_diffattn_kernel, lam_init=lam_init, q_tile0=q_tile0),
        grid=(bsz, N_TILES - q_tile0),
        in_specs=[
            pl.BlockSpec((None, TILE, BRANCH), lambda b, t: (b, t + q_tile0, 0)),
            pl.BlockSpec((None, TILE, BRANCH), lambda b, t: (b, jnp.minimum(t + q_tile0 + 1, N_TILES - 1), 0)),
            pl.BlockSpec((None, T_ALL, BRANCH), lambda b, t: (b, 0, 1)),
            pl.BlockSpec((None, BRANCH, T_ALL), lambda b, t: (b, 0, 0)),
            pl.BlockSpec((4, DA_QK), lambda b, t: (0, 0)),
            pl.BlockSpec((1, BRANCH), lambda b, t: (0, 0)),
        ],
        out_specs=pl.BlockSpec((None, TILE, BRANCH), lambda b, t: (b, t, 0)),
        out_shape=jax.ShapeDtypeStruct((bsz, (N_TILES - q_tile0) * TILE, BRANCH), BF16),
        scratch_shapes=[pltpu.VMEM((BRANCH, TILE), F32), pltpu.VMEM((2, T_ALL, TILE), F32),
                        pltpu.VMEM((8, TILE), F32)],
        compiler_params=_params("arbitrary", "arbitrary"),
        name="diffattn",
    )(pa, pa, pa, vta, lam_p, g_row)


BAND = 2 * TILE


def _window_kernel(q_ref, k_ref, vt_ref, sink_ref, o_ref, acc_ref, s_ref, *, q_tile0):
    qb = pl.program_id(1) + q_tile0
    lane = _iota((1, KV_WIDTH), 1)
    ones_rows = jnp.ones((ONES_ROWS, TILE), BF16)
    group = HEADS // (KV_WIDTH // HEAD_DIM)

    def attend(band):
        if band:
            a = (qb - 1) * TILE
            start = jnp.clip(a - SW_WINDOW, 0, SEQ - BAND)
            row0 = pl.multiple_of(CTX_LEN + start, SW_WINDOW)
            kb = k_ref[pl.ds(row0, BAND), :]
            kpos = start + _iota((BAND, 1), 0)
            qpos = a + _iota((1, TILE), 1)
            valid = jnp.abs(qpos - kpos) <= SW_WINDOW
        sinks, maxes = [], []
        for h in range(HEADS):
            kvh, g = h // group, h % group
            qg = q_ref[:, g * KV_WIDTH:(g + 1) * KV_WIDTH]
            qm = jnp.where(lane // HEAD_DIM == kvh, qg, jnp.zeros_like(qg))
            sink = sink_ref[:, h:h + 1] * LOG2E
            sc = _dot_nt(k_ref[0:CTX_LEN, :], qm)
            s_ref[h, 0:CTX_LEN, :] = sc
            m = jnp.maximum(jnp.max(sc, axis=0, keepdims=True), sink)
            if band:
                sb = jnp.where(valid, _dot_nt(kb, qm), NEG_BIG)
                s_ref[h, CTX_LEN:CTX_LEN + BAND, :] = sb
                m = jnp.maximum(m, jnp.max(sb, axis=0, keepdims=True))
            sinks.append(sink)
            maxes.append(m)
        for h in range(HEADS):
            m = maxes[h]
            kv_rows = slice((h // group) * HEAD_DIM, (h // group + 1) * HEAD_DIM)
            vt1 = jnp.concatenate([vt_ref[kv_rows, 0:CTX_LEN], ones_rows], axis=0)
            pv = _dot(vt1, jnp.exp2(s_ref[h, 0:CTX_LEN, :] - m).astype(BF16))
            if band:
                vtb = vt_ref[kv_rows, pl.ds(row0, BAND)]
                ones_b = jnp.ones((ONES_ROWS, BAND), BF16)
                eb = jnp.exp2(s_ref[h, CTX_LEN:CTX_LEN + BAND, :] - m).astype(BF16)
                pv = pv + _dot(jnp.concatenate([vtb, ones_b], axis=0), eb)
            l = pv[HEAD_DIM:HEAD_DIM + 1, :] + jnp.exp2(sinks[h] - m)
            acc_ref[h * HEAD_DIM:(h + 1) * HEAD_DIM, :] = pv[0:HEAD_DIM, :] * (1.0 / l)
        o_ref[...] = acc_ref[...].T.astype(BF16)

    @pl.when(qb == 0)
    def _():
        attend(False)

    @pl.when(qb > 0)
    def _():
        attend(True)


def _window_call(pc, vtc, sink_row, need_ctx):
    bsz = pc.shape[0]
    q_tile0 = 0 if need_ctx else 1
    n_q = N_TILES - q_tile0
    return pl.pallas_call(
        functools.partial(_window_kernel, q_tile0=q_tile0),
        grid=(bsz, n_q),
        in_specs=[
            pl.BlockSpec((None, TILE, BRANCH), lambda b, t: (b, t + q_tile0, 0)),
            pl.BlockSpec((None, T_ALL, KV_WIDTH), lambda b, t: (b, 0, BRANCH // KV_WIDTH)),
            pl.BlockSpec((None, KV_WIDTH, T_ALL), lambda b, t: (b, 0, 0)),
            pl.BlockSpec((1, HEADS), lambda b, t: (0, 0)),
        ],
        out_specs=pl.BlockSpec((None, TILE, BRANCH), lambda b, t: (b, t, 0)),
        out_shape=jax.ShapeDtypeStruct((bsz, n_q * TILE, BRANCH), BF16),
        scratch_shapes=[pltpu.VMEM((BRANCH, TILE), F32), pltpu.VMEM((HEADS, CTX_LEN + BAND, TILE), F32)],
        compiler_params=_params("arbitrary", "arbitrary"),
        name="window",
    )(pc, pc, vtc, sink_row)


CH_PER_TILE = TILE // HG_CHUNK
HG_STATE = 64
ST_PER_TILE = TILE // HG_STATE
HG_UNROLL = 4
HG_FAST_LIMIT = 180.0
HG_FAST_SHIFT = 60.0


def _hgrn_gates(z, lb_terms):
    log2_ksig = _log_sigmoid(-z) * LOG2E
    if lb_terms is None:
        return _log_sigmoid(z) * LOG2E, _sigmoid(-z), log2_ksig
    lb, log_lb, log_1m = lb_terms
    bt = log_1m + _log_sigmoid(z)
    mx = jnp.maximum(log_lb, bt)
    log_f = mx + jnp.log(jnp.exp(log_lb - mx) + jnp.exp(bt - mx))
    return log_f * LOG2E, (1.0 - lb) * _sigmoid(-z), log_1m * LOG2E + log2_ksig


def _hgrn_intra(direction, q3, c3, u3, v3, head_ones):
    half = HG_CHUNK // 2
    slabs, meta = [], []
    for s in range(HG_CHUNK):
        us = jnp.broadcast_to(u3[:, s:s + 1, :], (CH_PER_TILE, half, BRANCH))
        for g in range(2):
            lo_row, hi_row = half * g, half * g + half - 1
            if direction == 0:
                none_valid, all_valid = hi_row < s, lo_row >= s
            else:
                none_valid, all_valid = lo_row > s, hi_row <= s
            if none_valid:
                continue
            d = c3[:, half * g:half * (g + 1), :] - us
            if not all_valid:
                row = _iota((1, half, 1), 1) + half * g
                d = jnp.where((row >= s) if direction == 0 else (row <= s), d, NEG_BIG)
            x = q3[:, half * g:half * (g + 1), :] * jnp.exp2(d)
            slabs.append(x.reshape(CH_PER_TILE * half, BRANCH).astype(BF16))
            meta.append((s, g))
    a_all = _dot(jnp.concatenate(slabs, axis=0), head_ones)
    o = [jnp.zeros((CH_PER_TILE, half, BRANCH), F32) for _ in range(2)]
    n = CH_PER_TILE * half
    vs = None
    for i, (s, g) in enumerate(meta):
        if i == 0 or meta[i - 1][0] != s:
            vs = jnp.broadcast_to(v3[:, s:s + 1, :], (CH_PER_TILE, half, BRANCH))
        o[g] = o[g] + a_all[i * n:(i + 1) * n].reshape(CH_PER_TILE, half, BRANCH) * vs
    return jnp.concatenate(o, axis=1).reshape(TILE, BRANCH)


def _hgrn_safe_block(direction, q, kk, log2_f, log2_k, v, v_heads, masks):
    rr, cc, lane, row_in, tri16, ones16, head_ones = masks
    shape3 = (CH_PER_TILE, HG_CHUNK, BRANCH)
    cum = _dot_exact_l(tri16[direction], log2_f)
    tot = _dot_exact_l(ones16, log2_f)
    o = _hgrn_intra(direction, q.reshape(shape3), cum.reshape(shape3), (cum - log2_k).reshape(shape3),
                    v.reshape(shape3), head_ones)

    def shift_rows(a, n):
        n = n % TILE
        return jnp.concatenate([a[TILE - n:, :], a[:TILE - n, :]], axis=0)

    a_heads = [jnp.zeros((TILE, TILE), F32) for _ in range(HEADS)]
    g = HG_CHUNK
    while g < HG_STATE:
        qt = (q * jnp.exp2(cum)).astype(BF16)
        ke = (kk * jnp.exp2(tot - cum)).astype(BF16)
        later_r = ((rr % (2 * g)) >= g) if direction == 0 else ((rr % (2 * g)) < g)
        later_c = ((cc % (2 * g)) >= g) if direction == 0 else ((cc % (2 * g)) < g)
        pair = ((rr // (2 * g)) == (cc // (2 * g))) & later_r & jnp.logical_not(later_c)
        for h in range(HEADS):
            qh = jnp.where(lane // HEAD_DIM == h, qt, jnp.zeros_like(qt))
            a_heads[h] = jnp.where(pair, _dot_nt(qh, ke), a_heads[h])
        later_row = ((row_in % (2 * g)) >= g) if direction == 0 else ((row_in % (2 * g)) < g)
        sign = 1 if direction == 0 else -1
        tot_other = jnp.where(later_row, shift_rows(tot, sign * g), shift_rows(tot, -sign * g))
        cum = cum + jnp.where(later_row, tot_other, 0.0)
        tot = tot + tot_other
        g *= 2
    a_cat = jnp.concatenate([a.astype(BF16) for a in a_heads], axis=1)
    return o + _dot(a_cat, v_heads)


def _hgrn_kernel(p_ref, lb_ref, g_ref, o_ref, qt_ref, ke_ref, vb_ref, dec_ref, of_ref, oi_ref, st_ref,
                 *, layer_idx):
    rr = _iota((TILE, TILE), 0)
    cc = _iota((TILE, TILE), 1)
    lane = _iota((1, BRANCH), 1)
    row_in = _iota((TILE, 1), 0)
    same16 = (rr // HG_CHUNK) == (cc // HG_CHUNK)
    same64 = (rr // HG_STATE) == (cc // HG_STATE)
    within = [same64 & (cc <= rr), same64 & (cc >= rr)]
    tri64 = [w.astype(BF16) for w in within]
    ones64 = same64.astype(BF16)
    head_mask = (rr // HEAD_DIM) == (cc // HEAD_DIM)
    safe_masks = (rr, cc, lane, row_in, [(same16 & (cc <= rr)).astype(BF16), (same16 & (cc >= rr)).astype(BF16)],
                  same16.astype(BF16), _head_sum_matrix())

    lb_terms = None
    if layer_idx > 0:
        lbp = lb_ref[...]
        lbp = lbp - jnp.max(lbp, axis=0, keepdims=True)
        sm = jnp.exp(lbp)
        sm = sm / jnp.sum(sm, axis=0, keepdims=True)
        lb = jnp.sum(sm[1:layer_idx + 1], axis=0, keepdims=True)
        lb_terms = (lb, jnp.log(lb), jnp.log(1.0 - lb))

    def tile_body(t, carry):
        rows = pl.ds(pl.multiple_of(t * TILE, TILE), TILE)
        q = p_ref[rows, 0:BRANCH] * (HEAD_DIM ** -0.5)
        v = p_ref[rows, 3 * BRANCH:4 * BRANCH]
        vbf = v.astype(BF16)
        vb_ref[rows, :] = vbf
        v_heads = jnp.concatenate([jnp.where(lane // HEAD_DIM == h, vbf, jnp.zeros_like(vbf))
                                   for h in range(HEADS)], axis=0)
        o_intra = None
        fallback = []
        for direction in (0, 1):
            z = p_ref[rows, (1 + direction) * BRANCH:(2 + direction) * BRANCH]
            log2_f, kk, log2_k = _hgrn_gates(z, lb_terms)
            cum = _dot_exact_l(tri64[direction], log2_f)
            tot = _dot_exact_l(ones64, log2_f)
            low = jnp.min(cum, axis=(0, 1), keepdims=True)
            q_fast = (q * jnp.exp2(cum + HG_FAST_SHIFT)).astype(BF16)
            k_fast = (kk * jnp.exp2(jnp.minimum(-cum, HG_FAST_LIMIT) - HG_FAST_SHIFT)).astype(BF16)
            keep = within[direction] & (low >= -HG_FAST_LIMIT)
            a_heads = [jnp.where(keep, _dot_nt(jnp.where(lane // HEAD_DIM == h, q_fast, jnp.zeros_like(q_fast)),
                                               k_fast), 0.0) for h in range(HEADS)]
            a_cat = jnp.concatenate([a.astype(BF16) for a in a_heads], axis=1)
            o_d = _dot(a_cat, v_heads)
            qt_ref[direction, rows, :] = (q * jnp.exp2(cum)).astype(BF16)
            ke_ref[direction, rows, :] = (kk * jnp.exp2(tot - cum)).astype(BF16)
            dec_ref[direction, t] = jnp.exp2(tot.reshape(ST_PER_TILE, HG_STATE, BRANCH)[:, 0, :])
            o_intra = o_d if o_intra is None else o_intra + o_d
            fallback.append((low, kk, log2_f, log2_k))
        of_ref[rows, :] = o_intra

        for direction, (low, kk, log2_f, log2_k) in enumerate(fallback):
            @pl.when(low[0, 0] < -HG_FAST_LIMIT)
            def _():
                of_ref[rows, :] += _hgrn_safe_block(direction, q, kk, log2_f, log2_k, v, v_heads, safe_masks)
        return carry

    lax.fori_loop(0, N_TILES, tile_body, 0)

    st_ref[...] = jnp.zeros_like(st_ref)
    n_ctx, n_all = CTX_LEN // HG_STATE, T_ALL // HG_STATE

    def state_step(i, direction):
        if direction == 0:
            c = i
        else:
            c = jnp.where(i < n_ctx, n_ctx - 1 - i, n_all - 1 + n_ctx - i)
        rows = pl.ds(pl.multiple_of(c * HG_STATE, HG_STATE), HG_STATE)
        st = st_ref[direction]
        oi_ref[direction, rows, :] = _dot_nt(qt_ref[direction, rows, :], st.astype(BF16))
        ds = _dot_tn(vb_ref[rows, :], ke_ref[direction, rows, :])
        dec = dec_ref[direction, c // ST_PER_TILE, pl.ds(c % ST_PER_TILE, 1), :]
        st_ref[direction] = st * dec + jnp.where(head_mask, ds, 0.0)

    def state_body(it, carry):
        for u in range(HG_UNROLL):
            for direction in (0, 1):
                state_step(it * HG_UNROLL + u, direction)
        return carry

    lax.fori_loop(0, n_all // HG_UNROLL, state_body, 0)

    o_ref[...] = _head_rmsnorm(of_ref[...] + oi_ref[0] + oi_ref[1], g_ref[...]).astype(BF16)


def _hgrn_call(pb, hg_lb, g_row, layer_idx):
    bsz = pb.shape[0]
    return pl.pallas_call(
        functools.partial(_hgrn_kernel, layer_idx=layer_idx),
        grid=(bsz,),
        in_specs=[
            pl.BlockSpec((None, T_ALL, SEG_B), lambda b: (b, 0, 0)),
            pl.BlockSpec((DEPTH, BRANCH), lambda b: (0, 0)),
            pl.BlockSpec((1, BRANCH), lambda b: (0, 0)),
        ],
        out_specs=pl.BlockSpec((None, T_ALL, BRANCH), lambda b: (b, 0, 0)),
        out_shape=jax.ShapeDtypeStruct((bsz, T_ALL, BRANCH), BF16),
        scratch_shapes=[pltpu.VMEM((2, T_ALL, BRANCH), BF16), pltpu.VMEM((2, T_ALL, BRANCH), BF16),
                        pltpu.VMEM((T_ALL, BRANCH), BF16), pltpu.VMEM((2, N_TILES, ST_PER_TILE, BRANCH), F32),
                        pltpu.VMEM((T_ALL, BRANCH), F32), pltpu.VMEM((2, T_ALL, BRANCH), F32),
                        pltpu.VMEM((2, BRANCH, BRANCH), F32)],
        compiler_params=_params("arbitrary"),
        name="hgrn",
    )(pb, hg_lb, g_row)


ML_CHUNK = 256
ML_UNROLL = 3


def _mlstm_logits(direction, c, p_ref, vt_ref, g_ref, gt_ref, ct_ref, n_ref, m_ref, consts):
    tri, valid, _, lane, row16, lane16 = consts
    rows = pl.ds(pl.multiple_of(c * ML_CHUNK, ML_CHUNK), ML_CHUNK)
    q = p_ref[rows, 0:BRANCH]
    k = p_ref[rows, BRANCH:2 * BRANCH]
    g = g_ref[rows, :]
    gt = gt_ref[:, rows]
    cum = _dot_exact_l(tri, _log_sigmoid(g) * LOG2E)
    cum_t = _dot_exact_nt(_log_sigmoid(gt[8:16, :]) * LOG2E, tri)
    ig_t = gt[0:8, :] * LOG2E
    ct = ct_ref[direction]
    n0 = n_ref[direction]
    n_hi = n0.astype(BF16).astype(F32)
    n_lo = n0 - n_hi
    n_rows = (jnp.where((row16 < HEADS) & (lane16 == row16), n_hi, 0.0)
              + jnp.where((row16 >= HEADS) & (lane16 == row16 - HEADS), n_lo, 0.0)).astype(BF16)
    inter_all = _dot_nt(jnp.concatenate([ct.astype(BF16), n_rows], axis=0), q)
    heads = []
    for h in range(HEADS):
        r = HEADS * direction + h
        cumr = cum_t[r:r + 1, :]
        ucol = g[:, r:r + 1] * LOG2E - cum[:, 2 * HEADS + r:2 * HEADS + r + 1]
        toth = cumr[:, ML_CHUNK - 1:ML_CHUNK] if direction == 0 else cumr[:, 0:1]
        m0h = m_ref[direction, :, h:h + 1]
        logd = jnp.where(valid, cumr + ucol, NEG_BIG)
        inter = cumr + m0h
        m_t = jnp.maximum(jnp.max(logd, axis=0, keepdims=True), inter)
        qm = jnp.where(lane // HEAD_DIM == h, q, jnp.zeros_like(q))
        heads.append(dict(qk=_dot_nt(k, qm), logd=logd, m_t=m_t, g0=jnp.exp2(inter - m_t),
                          a_row=toth - cumr + ig_t[r:r + 1, :], carry=toth + m0h))
    return dict(direction=direction, rows=rows, k=k, ct=ct, n0=n0, inter_all=inter_all, heads=heads)


def _mlstm_outputs(cx, vt_ref, ht_ref, ct_ref, n_ref, m_ref, consts):
    _, _, head_mask, lane, _, _ = consts
    direction, rows, k, inter_all = cx["direction"], cx["rows"], cx["k"], cx["inter_all"]
    vt = vt_ref[:, rows]
    ones_rows = jnp.ones((ONES_ROWS, ML_CHUNK), BF16)
    row16 = _iota((ONES_ROWS, ML_CHUNK), 0)
    w_rows, sp_row = [], jnp.zeros((1, BRANCH), F32)
    for h, hd in enumerate(cx["heads"]):
        m_t, g0 = hd["m_t"], hd["g0"]
        s_t = hd["qk"] * jnp.exp2(hd["logd"] - m_t)
        vt1 = jnp.concatenate([vt[h * HEAD_DIM:(h + 1) * HEAD_DIM, :], ones_rows], axis=0)
        pv = _dot(vt1, s_t.astype(BF16))
        num = pv[0:HEAD_DIM, :] + g0 * inter_all[h * HEAD_DIM:(h + 1) * HEAD_DIM, :]
        den = pv[HEAD_DIM:HEAD_DIM + 1, :] + g0 * (inter_all[BRANCH + h:BRANCH + h + 1, :]
                                                   + inter_all[BRANCH + HEADS + h:BRANCH + HEADS + h + 1, :])
        ht_ref[direction, h * HEAD_DIM:(h + 1) * HEAD_DIM, rows] = (
            num / jnp.maximum(jnp.abs(den), jnp.exp2(-m_t)))
        a_row = hd["a_row"]
        m_loc = jnp.max(a_row, axis=1, keepdims=True)
        m_new = jnp.maximum(hd["carry"], m_loc)
        sp = jnp.exp2(hd["carry"] - m_new)
        w_rows.append(jnp.exp2(a_row - m_loc) * jnp.exp2(m_loc - m_new))
        sp_row = sp_row + jnp.where(lane // HEAD_DIM == h, sp, 0.0)
        m_ref[direction, :, h:h + 1] = m_new

    w_block = jnp.concatenate([jnp.broadcast_to(w, (HEAD_DIM, ML_CHUNK)) for w in w_rows], axis=0)
    vtw = (vt.astype(F32) * w_block).astype(BF16)
    w16 = jnp.zeros((ONES_ROWS, ML_CHUNK), F32)
    for h in range(HEADS):
        w_hi = w_rows[h].astype(BF16).astype(F32)
        w16 = w16 + jnp.where(row16 == h, w_hi, 0.0) + jnp.where(row16 == HEADS + h, w_rows[h] - w_hi, 0.0)
    dall = _dot(jnp.concatenate([vtw, w16.astype(BF16)], axis=0), k)
    ct_ref[direction] = cx["ct"] * sp_row + jnp.where(head_mask, dall[0:BRANCH, :], 0.0)
    dn = jnp.zeros((1, BRANCH), F32)
    for h in range(HEADS):
        dn = dn + jnp.where(lane // HEAD_DIM == h,
                            dall[BRANCH + h:BRANCH + h + 1, :] + dall[BRANCH + HEADS + h:BRANCH + HEADS + h + 1, :], 0.0)
    n_ref[direction] = cx["n0"] * sp_row + dn


def _mlstm_kernel(p_ref, vt_ref, g_ref, gt_ref, gain_ref, o_ref, ht_ref, ct_ref, n_ref, m_ref):
    head_mask = (_iota((BRANCH, BRANCH), 0) // HEAD_DIM) == (_iota((BRANCH, BRANCH), 1) // HEAD_DIM)
    rr = _iota((ML_CHUNK, ML_CHUNK), 0)
    cc = _iota((ML_CHUNK, ML_CHUNK), 1)
    lane = _iota((1, BRANCH), 1)
    row16 = _iota((ONES_ROWS, BRANCH), 0)
    lane16 = _iota((ONES_ROWS, BRANCH), 1) // HEAD_DIM
    consts = []
    for direction in (0, 1):
        tri = ((cc <= rr) if direction == 0 else (cc >= rr)).astype(BF16)
        valid = (rr <= cc) if direction == 0 else (rr >= cc)
        consts.append((tri, valid, head_mask, lane, row16, lane16))
    ct_ref[...] = jnp.zeros_like(ct_ref)
    n_ref[...] = jnp.zeros_like(n_ref)
    m_ref[...] = jnp.zeros_like(m_ref)
    n_ctx, n_all = CTX_LEN // ML_CHUNK, T_ALL // ML_CHUNK

    def body(it, carry):
        for u in range(ML_UNROLL):
            i = it * ML_UNROLL + u
            chunk = (i, jnp.where(i < n_ctx, n_ctx - 1 - i, n_all - 1 + n_ctx - i))
            cxs = [_mlstm_logits(d, chunk[d], p_ref, vt_ref, g_ref, gt_ref, ct_ref, n_ref, m_ref, consts[d])
                   for d in (0, 1)]
            for d in (0, 1):
                _mlstm_outputs(cxs[d], vt_ref, ht_ref, ct_ref, n_ref, m_ref, consts[d])
        return carry

    lax.fori_loop(0, n_all // ML_UNROLL, body, 0)

    def out_body(t, carry):
        rows = pl.ds(pl.multiple_of(t * TILE, TILE), TILE)
        o_ref[rows, :] = _head_rmsnorm((ht_ref[0, :, rows] + ht_ref[1, :, rows]).T,
                                       gain_ref[...]).astype(BF16)
        return carry

    lax.fori_loop(0, N_TILES, out_body, 0)


def _mlstm_call(pd, vtd, pg, pgt, g_row):
    bsz = pd.shape[0]
    return pl.pallas_call(
        _mlstm_kernel,
        grid=(bsz,),
        in_specs=[
            pl.BlockSpec((None, T_ALL, 2 * BRANCH), lambda b: (b, 0, 0)),
            pl.BlockSpec((None, BRANCH, T_ALL), lambda b: (b, 0, 0)),
            pl.BlockSpec((None, T_ALL, SEG_G), lambda b: (b, 0, 0)),
            pl.BlockSpec((None, SEG_G, T_ALL), lambda b: (b, 0, 0)),
            pl.BlockSpec((1, BRANCH), lambda b: (0, 0)),
        ],
        out_specs=pl.BlockSpec((None, T_ALL, BRANCH), lambda b: (b, 0, 0)),
        out_shape=jax.ShapeDtypeStruct((bsz, T_ALL, BRANCH), BF16),
        scratch_shapes=[pltpu.VMEM((2, BRANCH, T_ALL), F32), pltpu.VMEM((2, BRANCH, BRANCH), F32),
                        pltpu.VMEM((2, 1, BRANCH), F32), pltpu.VMEM((2, 1, LANES), F32)],
        compiler_params=_params("arbitrary"),
        name="mlstm",
    )(pd, vtd, pg, pgt, g_row)


def _outproj_kernel(x_ref, ctx_ref, ya_ref, yb_ref, yc_ref, yd_ref, po_ref, mod_ref, w_ref, fg_ref,
                    *out_refs, bsz, last):
    t = pl.program_id(1)
    is_ctx = jnp.logical_and(t == 0, not last)
    row = jnp.where(is_ctx, bsz, pl.program_id(0))
    gate_mod = mod_ref[pl.ds(row, 1), 2 * D_MODEL:3 * D_MODEL]
    po = po_ref[...].astype(F32)
    yd = yd_ref[...].astype(F32) * po[:, 0:BRANCH]
    mixed = jnp.concatenate([ya_ref[...].astype(F32), yb_ref[...].astype(F32), yc_ref[...].astype(F32), yd],
                            axis=-1)
    mixed = (mixed * po[:, BRANCH:]).astype(BF16)
    delta = gate_mod * _dot(mixed, w_ref[...])
    if last:
        xn = x_ref[...] + delta
        ms = jnp.mean(xn * xn, axis=-1, keepdims=True)
        out_refs[0][...] = xn * lax.rsqrt(ms + NORM_EPS) * fg_ref[...]
    else:
        x_out_ref, ctx_out_ref = out_refs

        @pl.when(t == 0)
        def _():
            ctx_out_ref[...] = ctx_ref[...] + delta

        @pl.when(t > 0)
        def _():
            x_out_ref[...] = x_ref[...] + delta


def _outproj_call(x, ctx, ya, yb, yc, yd, po, mod, w_out_bf, final_g, layer, last):
    bsz = x.shape[0]
    tile0 = 1 if last else 0
    rows = mod.shape[1]

    def tok(width, arr):
        off = tile0 if arr.shape[1] == T_ALL else 0
        return pl.BlockSpec((None, TILE, width), lambda b, t: (b, t + off, 0))

    lat_spec = pl.BlockSpec((None, TILE, D_MODEL), lambda b, t: (b, jnp.maximum(t + tile0 - 1, 0), 0))
    ctx_spec = pl.BlockSpec((None, CTX_LEN, D_MODEL), lambda b, t: (b, 0, 0))
    lat_shape = jax.ShapeDtypeStruct((bsz, SEQ, D_MODEL), F32)
    ctx_shape = jax.ShapeDtypeStruct((bsz, CTX_LEN, D_MODEL), F32)
    return pl.pallas_call(
        functools.partial(_outproj_kernel, bsz=bsz, last=last),
        grid=(bsz, N_TILES - tile0),
        in_specs=[lat_spec, ctx_spec, tok(BRANCH, ya), tok(BRANCH, yb), tok(BRANCH, yc), tok(BRANCH, yd),
                  tok(SEG_O, po),
                  pl.BlockSpec((None, rows, 3 * D_MODEL), lambda b, t: (layer, 0, 0)),
                  pl.BlockSpec((None, D_MODEL, D_MODEL), lambda b, t: (layer, 0, 0)),
                  pl.BlockSpec((1, D_MODEL), lambda b, t: (0, 0))],
        out_specs=lat_spec if last else [lat_spec, ctx_spec],
        out_shape=lat_shape if last else [lat_shape, ctx_shape],
        compiler_params=_params("arbitrary", "arbitrary"),
        name="outproj",
    )(x, ctx, ya, yb, yc, yd, po, mod, w_out_bf, final_g.reshape(1, D_MODEL))


def _rope_tables(dim):
    quarter = dim // 4
    half = dim // 2
    pos = np.arange(SEQ)
    row = (pos // GRID_W).astype(np.float32)
    col = (pos % GRID_W).astype(np.float32)
    inv = (np.float32(ROPE_BASE) ** (-np.arange(0, half, 2, dtype=np.float32) / np.float32(half))).astype(np.float32)
    ang_r = row[:, None] * inv[None, :]
    ang_c = col[:, None] * inv[None, :]
    lane = np.arange(LANES) % dim
    part = lane // quarter
    freq = lane % quarter
    ang = np.where(part[None, :] < 2, ang_r[:, freq], ang_c[:, freq]).astype(np.float32)
    cos = np.cos(ang)
    sin = np.sin(ang)
    first = (part % 2 == 0)[None, :]
    s_next = np.where(first, -sin, 0.0)
    s_prev = np.where(first, 0.0, sin)
    tab = np.stack([cos, s_next, s_prev]).astype(np.float32)
    ident = np.stack([np.ones((CTX_LEN, LANES)), np.zeros((CTX_LEN, LANES)),
                      np.zeros((CTX_LEN, LANES))]).astype(np.float32)
    return jnp.asarray(np.concatenate([ident, tab], axis=1))


def _relayout_cols(a):
    seg_a = a[..., 0:768]
    seg_b = a[..., 768:1792]
    qc = a[..., 1792:2048]
    qc = jnp.concatenate([qc[..., 0:64], qc[..., 128:192], qc[..., 64:128], qc[..., 192:256]], axis=-1)
    kvc = a[..., 2048:2304]
    seg_d = a[..., 2304:3072]
    gates = a[..., 3072:3088]
    seg_o = a[..., 3088:4368]
    pad = jnp.zeros(a.shape[:-1] + (SEG_G - 16,), a.dtype)
    return jnp.concatenate([seg_a, qc, kvc, seg_b, seg_d, gates, pad, seg_o], axis=-1)


def _segment_sources():
    ranges = [(0, 768),
              (1792, 1856), (1920, 1984), (1856, 1920), (1984, 2048),
              (2048, 2304), (768, 1792), (2304, 3072), (3072, 3088), None, (3088, 4368)]
    blocks, cur, room = [], [], LANES
    for rg in ranges:
        lo, hi = (0, SEG_G - 16) if rg is None else rg
        while lo < hi:
            n = min(room, hi - lo)
            cur.append(None if rg is None else (lo, lo + n))
            lo, room = lo + n, room - n
            if room == 0:
                blocks.append(cur)
                cur, room = [], LANES
    assert not cur and len(blocks) * LANES == PROJ_PAD
    return blocks


def _wprep_kernel(wt_ref, o_ref):
    for j, pieces in enumerate(_segment_sources()):
        rows = [jnp.zeros((SEG_G - 16, D_MODEL), F32) if p is None else wt_ref[p[0]:p[1], :] for p in pieces]
        blk = rows[0] if len(rows) == 1 else jnp.concatenate(rows, axis=0)
        o_ref[:, j * LANES:(j + 1) * LANES] = blk.T.astype(BF16)


def _wprep_call(w_in):
    depth, _, width = w_in.shape
    return pl.pallas_call(
        _wprep_kernel,
        grid=(depth,),
        in_specs=[pl.BlockSpec((None, width, D_MODEL), lambda l: (l, 0, 0), pipeline_mode=pl.Buffered(1))],
        out_specs=pl.BlockSpec((None, D_MODEL, PROJ_PAD), lambda l: (l, 0, 0)),
        out_shape=jax.ShapeDtypeStruct((depth, D_MODEL, PROJ_PAD), BF16),
        compiler_params=_params("arbitrary"),
        name="wprep",
    )(jnp.swapaxes(w_in, 1, 2))


def kernel(x, c, ctx, c_ctx, w_mod, b_mod, norm_g, w_in, b_in, diff_lam, diff_g, hg_lb, hg_g,
           sw_sink, ml_g, w_out, final_g):
    bsz = x.shape[0]
    tab_a = _rope_tables(DA_QK)
    tab_c = _rope_tables(HEAD_DIM)
    rows = ((bsz + 1 + 7) // 8) * 8
    cc = jnp.concatenate([c, c_ctx[None, :], jnp.zeros((rows - bsz - 1, D_MODEL), F32)], axis=0)
    mod = _mod_call(cc, w_mod, b_mod)
    tile4 = lambda g: jnp.tile(g, HEADS).reshape(1, BRANCH)
    w_r = _wprep_call(w_in)
    b_r = _relayout_cols(b_in)
    w_out_bf = w_out.astype(BF16)
    for l in range(DEPTH):
        last = l == DEPTH - 1
        pa, pc, pb, pd, pg, po, vta, vtc, vtd, pgt = _inproj_call(x, ctx, mod, norm_g, w_r, b_r,
                                                                 tab_a, tab_c, l)
        ya = _diffattn_call(pa, vta, diff_lam[l], tile4(diff_g[l]), l, not last)
        yb = _hgrn_call(pb, hg_lb, tile4(hg_g[l]), l)
        yc = _window_call(pc, vtc, sw_sink[l].reshape(1, HEADS), not last)
        yd = _mlstm_call(pd, vtd, pg, pgt, tile4(ml_g[l]))
        res = _outproj_call(x, ctx, ya, yb, yc, yd, po, mod, w_out_bf, final_g, l, last)
        if last:
            return res
        x, ctx = res
```

```python
import functools
import math

import numpy as np
import jax
import jax.numpy as jnp
from jax import lax
from jax.experimental import pallas as pl
from jax.experimental.pallas import tpu as pltpu

F32 = jnp.float32
BF16 = jnp.bfloat16

D_MODEL = 1024
SEQ = 2048
CTX_LEN = 256
T_ALL = CTX_LEN + SEQ
GRID_W = 64
DEPTH = 2
HEADS = 4
HEAD_DIM = 64
BRANCH = HEADS * HEAD_DIM
DA_QK = 32
SW_WINDOW = 128
HG_CHUNK = 16
ROPE_BASE = 10000.0
NORM_EPS = 1e-6
NEG_BIG = -1e30
LOG2E = math.log2(math.e)

TILE = 256
N_TILES = T_ALL // TILE
LANES = 128
ONES_ROWS = 16

SEG_A = 3 * BRANCH
KV_WIDTH = 2 * HEAD_DIM
SEG_C = BRANCH + 2 * KV_WIDTH
SEG_B = 4 * BRANCH
SEG_D = 3 * BRANCH
SEG_G = LANES
SEG_O = BRANCH + D_MODEL
OFF_A = 0
OFF_C = OFF_A + SEG_A
OFF_B = OFF_C + SEG_C
OFF_D = OFF_B + SEG_B
OFF_G = OFF_D + SEG_D
OFF_O = OFF_G + SEG_G
PROJ_PAD = OFF_O + SEG_O

VMEM_LIMIT = 56 * 1024 * 1024


def _params(*sem):
    return pltpu.CompilerParams(dimension_semantics=sem, vmem_limit_bytes=VMEM_LIMIT)


def _dot(a, b):
    return jnp.dot(a, b, preferred_element_type=F32)


def _dot_nt(a, b):
    return lax.dot_general(a, b, (((1,), (1,)), ((), ())), preferred_element_type=F32)


def _dot_tn(a, b):
    return lax.dot_general(a, b, (((0,), (0,)), ((), ())), preferred_element_type=F32)


def _split3(x):
    x1 = x.astype(BF16)
    r1 = x - x1.astype(F32)
    x2 = r1.astype(BF16)
    x3 = (r1 - x2.astype(F32)).astype(BF16)
    return x1, x2, x3


def _dot_exact_l(m01, x):
    x1, x2, x3 = _split3(x)
    return _dot(m01, x1) + _dot(m01, x2) + _dot(m01, x3)


def _dot_exact_r(x, m01):
    x1, x2, x3 = _split3(x)
    return _dot(x1, m01) + _dot(x2, m01) + _dot(x3, m01)


def _dot_exact_nt(x, m01):
    x1, x2, x3 = _split3(x)
    return _dot_nt(x1, m01) + _dot_nt(x2, m01) + _dot_nt(x3, m01)


def _sigmoid(z):
    e = jnp.exp(-jnp.abs(z))
    r = 1.0 / (1.0 + e)
    return jnp.where(z >= 0, r, e * r)


def _log_sigmoid(z):
    return jnp.minimum(z, 0.0) - jnp.log(1.0 + jnp.exp(-jnp.abs(z)))


def _iota(shape, dim):
    return lax.broadcasted_iota(jnp.int32, shape, dim)


def _head_sum_matrix():
    r = _iota((BRANCH, BRANCH), 0) // HEAD_DIM
    c = _iota((BRANCH, BRANCH), 1) // HEAD_DIM
    return (r == c).astype(BF16)


def _head_rmsnorm(o, g_row):
    ss = _dot_exact_r(o * o, _head_sum_matrix())
    return o * lax.rsqrt(ss * (1.0 / HEAD_DIM) + NORM_EPS) * g_row


def _head_rmsnorm_from_t(ot, g_row):
    n = ot.shape[1]
    o3 = ot.reshape(HEADS, HEAD_DIM, n)
    ss = jnp.sum(o3 * o3, axis=1, keepdims=True)
    o3 = o3 * lax.rsqrt(ss * (1.0 / HEAD_DIM) + NORM_EPS)
    return o3.reshape(BRANCH, n).T * g_row


def _mod_kernel(cc_ref, w_ref, b_ref, o_ref):
    cc = cc_ref[...]
    a = (cc * _sigmoid(cc)).astype(BF16)
    o_ref[...] = _dot(a, w_ref[...].astype(BF16)) + b_ref[...]


def _mod_call(cc, w_mod, b_mod):
    rows = cc.shape[0]
    nblk = 3
    return pl.pallas_call(
        _mod_kernel,
        grid=(DEPTH, nblk),
        in_specs=[
            pl.BlockSpec((rows, D_MODEL), lambda l, j: (0, 0)),
            pl.BlockSpec((None, D_MODEL, D_MODEL), lambda l, j: (l, 0, j)),
            pl.BlockSpec((None, 1, D_MODEL), lambda l, j: (l, 0, j)),
        ],
        out_specs=pl.BlockSpec((None, rows, D_MODEL), lambda l, j: (l, 0, j)),
        out_shape=jax.ShapeDtypeStruct((DEPTH, rows, 3 * D_MODEL), F32),
        compiler_params=_params("arbitrary", "arbitrary"),
        name="mod",
    )(cc, w_mod, b_mod.reshape(DEPTH, 1, 3 * D_MODEL))


def _rope(slab, cos, sin_next, sin_prev, off):
    nxt = pltpu.roll(slab, LANES - off, 1)
    prv = pltpu.roll(slab, off, 1)
    return slab * cos + nxt * sin_next + prv * sin_prev


def _inproj_kernel(x_ref, ctx_ref, mod_ref, ng_ref, w_ref, b_ref, ta_ref, tc_ref,
                   pa_ref, pc_ref, pb_ref, pd_ref, pg_ref, po_ref, vta_ref, vtc_ref, vtd_ref, pgt_ref,
                   *, bsz):
    is_ctx = pl.program_id(1) == 0
    row = jnp.where(is_ctx, bsz, pl.program_id(0))
    x = jnp.where(is_ctx, ctx_ref[...], x_ref[...])
    mrow = mod_ref[pl.ds(row, 1), :]
    shift = mrow[:, 0:D_MODEL]
    scale = mrow[:, D_MODEL:2 * D_MODEL]
    ms = jnp.mean(x * x, axis=-1, keepdims=True)
    h = x * lax.rsqrt(ms + NORM_EPS) * ng_ref[...]
    h = (h * (1.0 + scale) + shift).astype(BF16)

    def proj(off, width):
        return _dot(h, w_ref[:, off:off + width]) + b_ref[:, off:off + width]

    def rope_seg(acc, tab_ref, off, q_scale, k_slabs):
        cos, s_next, s_prev = tab_ref[0], tab_ref[1], tab_ref[2]
        outs = []
        for j in range(2 + k_slabs):
            r = _rope(acc[:, j * LANES:(j + 1) * LANES], cos, s_next, s_prev, off)
            outs.append(r * q_scale if j < 2 else r)
        outs.append(acc[:, (2 + k_slabs) * LANES:])
        return jnp.concatenate(outs, axis=-1)

    acco = proj(OFF_O, SEG_O)
    po_ref[...] = jnp.concatenate(
        [_sigmoid(acco[:, 0:BRANCH]), acco[:, BRANCH:] * _sigmoid(acco[:, BRANCH:])], axis=-1).astype(BF16)
    acca = rope_seg(proj(OFF_A, SEG_A), ta_ref, DA_QK // 4, DA_QK ** -0.5 * LOG2E, 2)
    pa_ref[...] = acca[:, 0:2 * BRANCH].astype(BF16)
    vta_ref[...] = acca[:, 2 * BRANCH:].T.astype(BF16)
    accc = rope_seg(proj(OFF_C, SEG_C), tc_ref, HEAD_DIM // 4, HEAD_DIM ** -0.5 * LOG2E, 1)
    pc_ref[...] = accc[:, 0:BRANCH + KV_WIDTH].astype(BF16)
    vtc_ref[...] = accc[:, BRANCH + KV_WIDTH:].T.astype(BF16)
    accd = proj(OFF_D, SEG_D)
    pd_ref[...] = jnp.concatenate(
        [accd[:, 0:BRANCH], accd[:, BRANCH:2 * BRANCH] * (HEAD_DIM ** -0.5)], axis=-1).astype(BF16)
    vtd_ref[...] = accd[:, 2 * BRANCH:].T.astype(BF16)
    gates = proj(OFF_G, SEG_G)
    pg_ref[...] = gates
    pgt_ref[...] = gates.T
    pb_ref[...] = proj(OFF_B, SEG_B)


def _inproj_call(x, ctx, mod, norm_g, w_r, b_r, tab_a, tab_c, layer):
    bsz = x.shape[0]
    rows = mod.shape[1]
    widths = [(2 * BRANCH, BF16), (BRANCH + KV_WIDTH, BF16), (SEG_B, F32), (2 * BRANCH, BF16), (SEG_G, F32),
              (SEG_O, BF16)]
    out_specs = [pl.BlockSpec((None, TILE, w), lambda b, t: (b, t, 0)) for w, _ in widths]
    out_shape = [jax.ShapeDtypeStruct((bsz, T_ALL, w), dt) for w, dt in widths]
    for rows_t, dt in ((BRANCH, BF16), (KV_WIDTH, BF16), (BRANCH, BF16), (SEG_G, F32)):
        out_specs.append(pl.BlockSpec((None, rows_t, TILE), lambda b, t: (b, 0, t)))
        out_shape.append(jax.ShapeDtypeStruct((bsz, rows_t, T_ALL), dt))
    return pl.pallas_call(
        functools.partial(_inproj_kernel, bsz=bsz),
        grid=(bsz, N_TILES),
        in_specs=[
            pl.BlockSpec((None, TILE, D_MODEL), lambda b, t: (b, jnp.maximum(t - 1, 0), 0)),
            pl.BlockSpec((None, CTX_LEN, D_MODEL), lambda b, t: (b, 0, 0)),
            pl.BlockSpec((None, rows, 3 * D_MODEL), lambda b, t: (layer, 0, 0)),
            pl.BlockSpec((None, 1, D_MODEL), lambda b, t: (layer, 0, 0)),
            pl.BlockSpec((None, D_MODEL, PROJ_PAD), lambda b, t: (layer, 0, 0)),
            pl.BlockSpec((None, 1, PROJ_PAD), lambda b, t: (layer, 0, 0)),
            pl.BlockSpec((3, TILE, LANES), lambda b, t: (0, t, 0)),
            pl.BlockSpec((3, TILE, LANES), lambda b, t: (0, t, 0)),
        ],
        out_specs=out_specs,
        out_shape=out_shape,
        compiler_params=_params("arbitrary", "arbitrary"),
        name="inproj",
    )(x, ctx, mod, norm_g.reshape(-1, 1, D_MODEL), w_r, b_r.reshape(-1, 1, PROJ_PAD), tab_a, tab_c)


def _diffattn_kernel(q_ref, qn_ref, k_ref, vt_ref, lam_ref, g_ref, o_ref, acc_ref, s_ref, m8_ref,
                     *, lam_init, q_tile0):
    step = pl.program_id(1)
    qb = step + q_tile0
    lp = lam_ref[...]
    lam = (jnp.exp(jnp.sum(lp[0:1] * lp[1:2], axis=-1, keepdims=True))
           - jnp.exp(jnp.sum(lp[2:3] * lp[3:4], axis=-1, keepdims=True)) + lam_init)
    q = q_ref[...]
    lane = _iota((1, BRANCH), 1)
    n_pairs = 2 * HEADS
    sub = TILE // 8
    ones_rows = jnp.ones((ONES_ROWS, TILE), BF16)

    def pair_q(qv, hm):
        return jnp.where(lane // DA_QK == hm, qv, jnp.zeros_like(qv))

    def logits(qm, nk, buf):
        m8 = None
        half = max(nk // 2, TILE)
        for r0 in range(0, nk, half):
            st = _dot_nt(k_ref[r0:r0 + half, :], qm)
            s_ref[buf, r0:r0 + half, :] = st
            mh = jnp.max(st.reshape(half // 8, 8, TILE), axis=0)
            m8 = mh if m8 is None else jnp.maximum(m8, mh)
        return m8

    def value_tile(hm, j, mb, ot):
        h = hm // 2
        st = s_ref[hm % 2, j * TILE:(j + 1) * TILE, :]
        e = jnp.exp2(st.reshape(sub, 8, TILE) - mb[None])
        vt = vt_ref[h * HEAD_DIM:(h + 1) * HEAD_DIM, j * TILE:(j + 1) * TILE]
        vt1 = jnp.concatenate([vt, ones_rows], axis=0)
        return ot + _dot(vt1, e.reshape(TILE, TILE).astype(BF16))

    def attend(nk, own_first_logits):
        n_kt = nk // TILE
        m8 = logits(pair_q(q, 0), nk, 0) if own_first_logits else m8_ref[...]
        for hm in range(n_pairs):
            mb = jnp.broadcast_to(jnp.max(m8, axis=0, keepdims=True), (8, TILE))
            if hm + 1 < n_pairs:
                m8 = logits(pair_q(q, hm + 1), nk, (hm + 1) % 2)
            else:
                m8_ref[...] = logits(pair_q(qn_ref[...], 0), T_ALL, 0)
            ot = jnp.zeros((HEAD_DIM + ONES_ROWS, TILE), F32)
            for j in range(n_kt):
                ot = value_tile(hm, j, mb, ot)
            l = ot[HEAD_DIM:HEAD_DIM + 1, :]
            ot = ot[0:HEAD_DIM, :]
            rows = slice((hm // 2) * HEAD_DIM, (hm // 2 + 1) * HEAD_DIM)
            if hm % 2 == 0:
                acc_ref[rows, :] = ot * (1.0 / l)
            else:
                acc_ref[rows, :] -= ot * (lam / l)
        o_ref[...] = (_head_rmsnorm_from_t(acc_ref[...], g_ref[...]) * (1.0 - lam_init)).astype(BF16)

    if q_tile0 == 0:
        @pl.when(qb == 0)
        def _():
            attend(CTX_LEN, True)
    else:
        @pl.when(step == 0)
        def _():
            m8_ref[...] = logits(pair_q(q, 0), T_ALL, 0)

    @pl.when(qb > 0)
    def _():
        attend(T_ALL, False)


def _diffattn_call(pa, vta, lam_p, g_row, layer_idx, need_ctx):
    bsz = pa.shape[0]
    q_tile0 = 0 if need_ctx else 1
    lam_init = 0.8 - 0.6 * math.exp(-0.3 * layer_idx)
    return pl.pallas_call(
        functools.partial(_diffattn_kernel, lam_init=lam_init, q_tile0=q_tile0),
        grid=(bsz, N_TILES - q_tile0),
        in_specs=[
            pl.BlockSpec((None, TILE, BRANCH), lambda b, t: (b, t + q_tile0, 0)),
            pl.BlockSpec((None, TILE, BRANCH), lambda b, t: (b, jnp.minimum(t + q_tile0 + 1, N_TILES - 1), 0)),
            pl.BlockSpec((None, T_ALL, BRANCH), lambda b, t: (b, 0, 1)),
            pl.BlockSpec((None, BRANCH, T_ALL), lambda b, t: (b, 0, 0)),
            pl.BlockSpec((4, DA_QK), lambda b, t: (0, 0)),
            pl.BlockSpec((1, BRANCH), lambda b, t: (0, 0)),
        ],
        out_specs=pl.BlockSpec((None, TILE, BRANCH), lambda b, t: (b, t, 0)),
        out_shape=jax.ShapeDtypeStruct((bsz, (N_TILES - q_tile0) * TILE, BRANCH), BF16),
        scratch_shapes=[pltpu.VMEM((BRANCH, TILE), F32), pltpu.VMEM((2, T_ALL, TILE), F32),
                        pltpu.VMEM((8, TILE), F32)],
        compiler_params=_params("arbitrary", "arbitrary"),
        name="diffattn",
    )(pa, pa, pa, vta, lam_p, g_row)


BAND = 2 * TILE


def _window_kernel(q_ref, k_ref, vt_ref, sink_ref, o_ref, acc_ref, s_ref, *, q_tile0):
    qb = pl.program_id(1) + q_tile0
    lane = _iota((1, KV_WIDTH), 1)
    ones_rows = jnp.ones((ONES_ROWS, TILE), BF16)
    group = HEADS // (KV_WIDTH // HEAD_DIM)

    def attend(band):
        if band:
            a = (qb - 1) * TILE
            start = jnp.clip(a - SW_WINDOW, 0, SEQ - BAND)
            row0 = pl.multiple_of(CTX_LEN + start, SW_WINDOW)
            kb = k_ref[pl.ds(row0, BAND), :]
            kpos = start + _iota((BAND, 1), 0)
            qpos = a + _iota((1, TILE), 1)
            valid = jnp.abs(qpos - kpos) <= SW_WINDOW
        sinks, maxes = [], []
        for h in range(HEADS):
            kvh, g = h // group, h % group
            qg = q_ref[:, g * KV_WIDTH:(g + 1) * KV_WIDTH]
            qm = jnp.where(lane // HEAD_DIM == kvh, qg, jnp.zeros_like(qg))
            sink = sink_ref[:, h:h + 1] * LOG2E
            sc = _dot_nt(k_ref[0:CTX_LEN, :], qm)
            s_ref[h, 0:CTX_LEN, :] = sc
            m = jnp.maximum(jnp.max(sc, axis=0, keepdims=True), sink)
            if band:
                sb = jnp.where(valid, _dot_nt(kb, qm), NEG_BIG)
                s_ref[h, CTX_LEN:CTX_LEN + BAND, :] = sb
                m = jnp.maximum(m, jnp.max(sb, axis=0, keepdims=True))
            sinks.append(sink)
            maxes.append(m)
        for h in range(HEADS):
            m = maxes[h]
            kv_rows = slice((h // group) * HEAD_DIM, (h // group + 1) * HEAD_DIM)
            vt1 = jnp.concatenate([vt_ref[kv_rows, 0:CTX_LEN], ones_rows], axis=0)
            pv = _dot(vt1, jnp.exp2(s_ref[h, 0:CTX_LEN, :] - m).astype(BF16))
            if band:
                vtb = vt_ref[kv_rows, pl.ds(row0, BAND)]
                ones_b = jnp.ones((ONES_ROWS, BAND), BF16)
                eb = jnp.exp2(s_ref[h, CTX_LEN:CTX_LEN + BAND, :] - m).astype(BF16)
                pv = pv + _dot(jnp.concatenate([vtb, ones_b], axis=0), eb)
            l = pv[HEAD_DIM:HEAD_DIM + 1, :] + jnp.exp2(sinks[h] - m)
            acc_ref[h * HEAD_DIM:(h + 1) * HEAD_DIM, :] = pv[0:HEAD_DIM, :] * (1.0 / l)
        o_ref[...] = acc_ref[...].T.astype(BF16)

    @pl.when(qb == 0)
    def _():
        attend(False)

    @pl.when(qb > 0)
    def _():
        attend(True)


def _window_call(pc, vtc, sink_row, need_ctx):
    bsz = pc.shape[0]
    q_tile0 = 0 if need_ctx else 1
    n_q = N_TILES - q_tile0
    return pl.pallas_call(
        functools.partial(_window_kernel, q_tile0=q_tile0),
        grid=(bsz, n_q),
        in_specs=[
            pl.BlockSpec((None, TILE, BRANCH), lambda b, t: (b, t + q_tile0, 0)),
            pl.BlockSpec((None, T_ALL, KV_WIDTH), lambda b, t: (b, 0, BRANCH // KV_WIDTH)),
            pl.BlockSpec((None, KV_WIDTH, T_ALL), lambda b, t: (b, 0, 0)),
            pl.BlockSpec((1, HEADS), lambda b, t: (0, 0)),
        ],
        out_specs=pl.BlockSpec((None, TILE, BRANCH), lambda b, t: (b, t, 0)),
        out_shape=jax.ShapeDtypeStruct((bsz, n_q * TILE, BRANCH), BF16),
        scratch_shapes=[pltpu.VMEM((BRANCH, TILE), F32), pltpu.VMEM((HEADS, CTX_LEN + BAND, TILE), F32)],
        compiler_params=_params("arbitrary", "arbitrary"),
        name="window",
    )(pc, pc, vtc, sink_row)


CH_PER_TILE = TILE // HG_CHUNK
HG_STATE = 64
ST_PER_TILE = TILE // HG_STATE
HG_UNROLL = 9
HG_FAST_LIMIT = 180.0
HG_FAST_SHIFT = 60.0


def _hgrn_gates(z, lb_terms):
    log2_ksig = _log_sigmoid(-z) * LOG2E
    if lb_terms is None:
        return _log_sigmoid(z) * LOG2E, _sigmoid(-z), log2_ksig
    lb, log_lb, log_1m = lb_terms
    bt = log_1m + _log_sigmoid(z)
    mx = jnp.maximum(log_lb, bt)
    log_f = mx + jnp.log(jnp.exp(log_lb - mx) + jnp.exp(bt - mx))
    return log_f * LOG2E, (1.0 - lb) * _sigmoid(-z), log_1m * LOG2E + log2_ksig


def _hgrn_intra(direction, q3, c3, u3, v3, head_ones):
    half = HG_CHUNK // 2
    slabs, meta = [], []
    for s in range(HG_CHUNK):
        us = jnp.broadcast_to(u3[:, s:s + 1, :], (CH_PER_TILE, half, BRANCH))
        for g in range(2):
            lo_row, hi_row = half * g, half * g + half - 1
            if direction == 0:
                none_valid, all_valid = hi_row < s, lo_row >= s
            else:
                none_valid, all_valid = lo_row > s, hi_row <= s
            if none_valid:
                continue
            d = c3[:, half * g:half * (g + 1), :] - us
            if not all_valid:
                row = _iota((1, half, 1), 1) + half * g
                d = jnp.where((row >= s) if direction == 0 else (row <= s), d, NEG_BIG)
            x = q3[:, half * g:half * (g + 1), :] * jnp.exp2(d)
            slabs.append(x.reshape(CH_PER_TILE * half, BRANCH).astype(BF16))
            meta.append((s, g))
    a_all = _dot(jnp.concatenate(slabs, axis=0), head_ones)
    o = [jnp.zeros((CH_PER_TILE, half, BRANCH), F32) for _ in range(2)]
    n = CH_PER_TILE * half
    vs = None
    for i, (s, g) in enumerate(meta):
        if i == 0 or meta[i - 1][0] != s:
            vs = jnp.broadcast_to(v3[:, s:s + 1, :], (CH_PER_TILE, half, BRANCH))
        o[g] = o[g] + a_all[i * n:(i + 1) * n].reshape(CH_PER_TILE, half, BRANCH) * vs
    return jnp.concatenate(o, axis=1).reshape(TILE, BRANCH)


def _hgrn_safe_block(direction, q, kk, log2_f, log2_k, v, v_heads, masks):
    rr, cc, lane, row_in, tri16, ones16, head_ones = masks
    shape3 = (CH_PER_TILE, HG_CHUNK, BRANCH)
    cum = _dot_exact_l(tri16[direction], log2_f)
    tot = _dot_exact_l(ones16, log2_f)
    o = _hgrn_intra(direction, q.reshape(shape3), cum.reshape(shape3), (cum - log2_k).reshape(shape3),
                    v.reshape(shape3), head_ones)

    def shift_rows(a, n):
        n = n % TILE
        return jnp.concatenate([a[TILE - n:, :], a[:TILE - n, :]], axis=0)

    a_heads = [jnp.zeros((TILE, TILE), F32) for _ in range(HEADS)]
    g = HG_CHUNK
    while g < HG_STATE:
        qt = (q * jnp.exp2(cum)).astype(BF16)
        ke = (kk * jnp.exp2(tot - cum)).astype(BF16)
        later_r = ((rr % (2 * g)) >= g) if direction == 0 else ((rr % (2 * g)) < g)
        later_c = ((cc % (2 * g)) >= g) if direction == 0 else ((cc % (2 * g)) < g)
        pair = ((rr // (2 * g)) == (cc // (2 * g))) & later_r & jnp.logical_not(later_c)
        for h in range(HEADS):
            qh = jnp.where(lane // HEAD_DIM == h, qt, jnp.zeros_like(qt))
            a_heads[h] = jnp.where(pair, _dot_nt(qh, ke), a_heads[h])
        later_row = ((row_in % (2 * g)) >= g) if direction == 0 else ((row_in % (2 * g)) < g)
        sign = 1 if direction == 0 else -1
        tot_other = jnp.where(later_row, shift_rows(tot, sign * g), shift_rows(tot, -sign * g))
        cum = cum + jnp.where(later_row, tot_other, 0.0)
        tot = tot + tot_other
        g *= 2
    a_cat = jnp.concatenate([a.astype(BF16) for a in a_heads], axis=1)
    return o + _dot(a_cat, v_heads)


def _hgrn_kernel(p_ref, lb_ref, g_ref, o_ref, qt_ref, ke_ref, vb_ref, dec_ref, of_ref, oi_ref, st_ref,
                 *, layer_idx):
    rr = _iota((TILE, TILE), 0)
    cc = _iota((TILE, TILE), 1)
    lane = _iota((1, BRANCH), 1)
    row_in = _iota((TILE, 1), 0)
    same16 = (rr // HG_CHUNK) == (cc // HG_CHUNK)
    same64 = (rr // HG_STATE) == (cc // HG_STATE)
    within = [same64 & (cc <= rr), same64 & (cc >= rr)]
    tri64 = [w.astype(BF16) for w in within]
    ones64 = same64.astype(BF16)
    head_mask = (rr // HEAD_DIM) == (cc // HEAD_DIM)
    safe_masks = (rr, cc, lane, row_in, [(same16 & (cc <= rr)).astype(BF16), (same16 & (cc >= rr)).astype(BF16)],
                  same16.astype(BF16), _head_sum_matrix())

    lb_terms = None
    if layer_idx > 0:
        lbp = lb_ref[...]
        lbp = lbp - jnp.max(lbp, axis=0, keepdims=True)
        sm = jnp.exp(lbp)
        sm = sm / jnp.sum(sm, axis=0, keepdims=True)
        lb = jnp.sum(sm[1:layer_idx + 1], axis=0, keepdims=True)
        lb_terms = (lb, jnp.log(lb), jnp.log(1.0 - lb))

    def tile_body(t, carry):
        rows = pl.ds(pl.multiple_of(t * TILE, TILE), TILE)
        q = p_ref[rows, 0:BRANCH] * (HEAD_DIM ** -0.5)
        v = p_ref[rows, 3 * BRANCH:4 * BRANCH]
        vbf = v.astype(BF16)
        vb_ref[rows, :] = vbf
        v_heads = jnp.concatenate([jnp.where(lane // HEAD_DIM == h, vbf, jnp.zeros_like(vbf))
                                   for h in range(HEADS)], axis=0)
        o_intra = None
        fallback = []
        for direction in (0, 1):
            z = p_ref[rows, (1 + direction) * BRANCH:(2 + direction) * BRANCH]
            log2_f, kk, log2_k = _hgrn_gates(z, lb_terms)
            cum = _dot_exact_l(tri64[direction], log2_f)
            tot = _dot_exact_l(ones64, log2_f)
            low = jnp.min(cum, axis=(0, 1), keepdims=True)
            q_fast = (q * jnp.exp2(cum + HG_FAST_SHIFT)).astype(BF16)
            k_fast = (kk * jnp.exp2(jnp.minimum(-cum, HG_FAST_LIMIT) - HG_FAST_SHIFT)).astype(BF16)
            keep = within[direction] & (low >= -HG_FAST_LIMIT)
            a_heads = [jnp.where(keep, _dot_nt(jnp.where(lane // HEAD_DIM == h, q_fast, jnp.zeros_like(q_fast)),
                                               k_fast), 0.0) for h in range(HEADS)]
            a_cat = jnp.concatenate([a.astype(BF16) for a in a_heads], axis=1)
            o_d = _dot(a_cat, v_heads)
            qt_ref[direction, rows, :] = (q * jnp.exp2(cum)).astype(BF16)
            ke_ref[direction, rows, :] = (kk * jnp.exp2(tot - cum)).astype(BF16)
            dec_ref[direction, t] = jnp.exp2(tot.reshape(ST_PER_TILE, HG_STATE, BRANCH)[:, 0, :])
            o_intra = o_d if o_intra is None else o_intra + o_d
            fallback.append((low, kk, log2_f, log2_k))
        of_ref[rows, :] = o_intra

        for direction, (low, kk, log2_f, log2_k) in enumerate(fallback):
            @pl.when(low[0, 0] < -HG_FAST_LIMIT)
            def _():
                of_ref[rows, :] += _hgrn_safe_block(direction, q, kk, log2_f, log2_k, v, v_heads, safe_masks)
        return carry

    lax.fori_loop(0, N_TILES, tile_body, 0)

    st_ref[...] = jnp.zeros_like(st_ref)
    n_ctx, n_all = CTX_LEN // HG_STATE, T_ALL // HG_STATE

    def state_step(i, direction):
        if direction == 0:
            c = i
        else:
            c = jnp.where(i < n_ctx, n_ctx - 1 - i, n_all - 1 + n_ctx - i)
        rows = pl.ds(pl.multiple_of(c * HG_STATE, HG_STATE), HG_STATE)
        st = st_ref[direction]
        oi_ref[direction, rows, :] = _dot_nt(qt_ref[direction, rows, :], st.astype(BF16))
        ds = _dot_tn(vb_ref[rows, :], ke_ref[direction, rows, :])
        dec = dec_ref[direction, c // ST_PER_TILE, pl.ds(c % ST_PER_TILE, 1), :]
        st_ref[direction] = st * dec + jnp.where(head_mask, ds, 0.0)

    def state_body(it, carry):
        for u in range(HG_UNROLL):
            for direction in (0, 1):
                state_step(it * HG_UNROLL + u, direction)
        return carry

    lax.fori_loop(0, n_all // HG_UNROLL, state_body, 0)

    o_ref[...] = _head_rmsnorm(of_ref[...] + oi_ref[0] + oi_ref[1], g_ref[...]).astype(BF16)


def _hgrn_call(pb, hg_lb, g_row, layer_idx):
    bsz = pb.shape[0]
    return pl.pallas_call(
        functools.partial(_hgrn_kernel, layer_idx=layer_idx),
        grid=(bsz,),
        in_specs=[
            pl.BlockSpec((None, T_ALL, SEG_B), lambda b: (b, 0, 0)),
            pl.BlockSpec((DEPTH, BRANCH), lambda b: (0, 0)),
            pl.BlockSpec((1, BRANCH), lambda b: (0, 0)),
        ],
        out_specs=pl.BlockSpec((None, T_ALL, BRANCH), lambda b: (b, 0, 0)),
        out_shape=jax.ShapeDtypeStruct((bsz, T_ALL, BRANCH), BF16),
        scratch_shapes=[pltpu.VMEM((2, T_ALL, BRANCH), BF16), pltpu.VMEM((2, T_ALL, BRANCH), BF16),
                        pltpu.VMEM((T_ALL, BRANCH), BF16), pltpu.VMEM((2, N_TILES, ST_PER_TILE, BRANCH), F32),
                        pltpu.VMEM((T_ALL, BRANCH), F32), pltpu.VMEM((2, T_ALL, BRANCH), F32),
                        pltpu.VMEM((2, BRANCH, BRANCH), F32)],
        compiler_params=_params("arbitrary"),
        name="hgrn",
    )(pb, hg_lb, g_row)


ML_CHUNK = 256
ML_UNROLL = 3


def _mlstm_logits(direction, c, p_ref, vt_ref, g_ref, gt_ref, ct_ref, n_ref, m_ref, consts):
    tri, valid, _, lane, row16, lane16 = consts
    rows = pl.ds(pl.multiple_of(c * ML_CHUNK, ML_CHUNK), ML_CHUNK)
    q = p_ref[rows, 0:BRANCH]
    k = p_ref[rows, BRANCH:2 * BRANCH]
    g = g_ref[rows, :]
    gt = gt_ref[:, rows]
    cum = _dot_exact_l(tri, _log_sigmoid(g) * LOG2E)
    cum_t = _dot_exact_nt(_log_sigmoid(gt[8:16, :]) * LOG2E, tri)
    ig_t = gt[0:8, :] * LOG2E
    ct = ct_ref[direction]
    n0 = n_ref[direction]
    n_hi = n0.astype(BF16).astype(F32)
    n_lo = n0 - n_hi
    n_rows = (jnp.where((row16 < HEADS) & (lane16 == row16), n_hi, 0.0)
              + jnp.where((row16 >= HEADS) & (lane16 == row16 - HEADS), n_lo, 0.0)).astype(BF16)
    inter_all = _dot_nt(jnp.concatenate([ct.astype(BF16), n_rows], axis=0), q)
    heads = []
    for h in range(HEADS):
        r = HEADS * direction + h
        cumr = cum_t[r:r + 1, :]
        ucol = g[:, r:r + 1] * LOG2E - cum[:, 2 * HEADS + r:2 * HEADS + r + 1]
        toth = cumr[:, ML_CHUNK - 1:ML_CHUNK] if direction == 0 else cumr[:, 0:1]
        m0h = m_ref[direction, :, h:h + 1]
        logd = jnp.where(valid, cumr + ucol, NEG_BIG)
        inter = cumr + m0h
        m_t = jnp.maximum(jnp.max(logd, axis=0, keepdims=True), inter)
        qm = jnp.where(lane // HEAD_DIM == h, q, jnp.zeros_like(q))
        heads.append(dict(qk=_dot_nt(k, qm), logd=logd, m_t=m_t, g0=jnp.exp2(inter - m_t),
                          a_row=toth - cumr + ig_t[r:r + 1, :], carry=toth + m0h))
    return dict(direction=direction, rows=rows, k=k, ct=ct, n0=n0, inter_all=inter_all, heads=heads)


def _mlstm_outputs(cx, vt_ref, ht_ref, ct_ref, n_ref, m_ref, consts):
    _, _, head_mask, lane, _, _ = consts
    direction, rows, k, inter_all = cx["direction"], cx["rows"], cx["k"], cx["inter_all"]
    vt = vt_ref[:, rows]
    ones_rows = jnp.ones((ONES_ROWS, ML_CHUNK), BF16)
    row16 = _iota((ONES_ROWS, ML_CHUNK), 0)
    w_rows, sp_row = [], jnp.zeros((1, BRANCH), F32)
    for h, hd in enumerate(cx["heads"]):
        m_t, g0 = hd["m_t"], hd["g0"]
        s_t = hd["qk"] * jnp.exp2(hd["logd"] - m_t)
        vt1 = jnp.concatenate([vt[h * HEAD_DIM:(h + 1) * HEAD_DIM, :], ones_rows], axis=0)
        pv = _dot(vt1, s_t.astype(BF16))
        num = pv[0:HEAD_DIM, :] + g0 * inter_all[h * HEAD_DIM:(h + 1) * HEAD_DIM, :]
        den = pv[HEAD_DIM:HEAD_DIM + 1, :] + g0 * (inter_all[BRANCH + h:BRANCH + h + 1, :]
                                                   + inter_all[BRANCH + HEADS + h:BRANCH + HEADS + h + 1, :])
        ht_ref[direction, h * HEAD_DIM:(h + 1) * HEAD_DIM, rows] = (
            num / jnp.maximum(jnp.abs(den), jnp.exp2(-m_t)))
        a_row = hd["a_row"]
        m_loc = jnp.max(a_row, axis=1, keepdims=True)
        m_new = jnp.maximum(hd["carry"], m_loc)
        sp = jnp.exp2(hd["carry"] - m_new)
        w_rows.append(jnp.exp2(a_row - m_loc) * jnp.exp2(m_loc - m_new))
        sp_row = sp_row + jnp.where(lane // HEAD_DIM == h, sp, 0.0)
        m_ref[direction, :, h:h + 1] = m_new

    w_block = jnp.concatenate([jnp.broadcast_to(w, (HEAD_DIM, ML_CHUNK)) for w in w_rows], axis=0)
    vtw = (vt.astype(F32) * w_block).astype(BF16)
    w16 = jnp.zeros((ONES_ROWS, ML_CHUNK), F32)
    for h in range(HEADS):
        w_hi = w_rows[h].astype(BF16).astype(F32)
        w16 = w16 + jnp.where(row16 == h, w_hi, 0.0) + jnp.where(row16 == HEADS + h, w_rows[h] - w_hi, 0.0)
    dall = _dot(jnp.concatenate([vtw, w16.astype(BF16)], axis=0), k)
    ct_ref[direction] = cx["ct"] * sp_row + jnp.where(head_mask, dall[0:BRANCH, :], 0.0)
    dn = jnp.zeros((1, BRANCH), F32)
    for h in range(HEADS):
        dn = dn + jnp.where(lane // HEAD_DIM == h,
                            dall[BRANCH + h:BRANCH + h + 1, :] + dall[BRANCH + HEADS + h:BRANCH + HEADS + h + 1, :], 0.0)
    n_ref[direction] = cx["n0"] * sp_row + dn


def _mlstm_kernel(p_ref, vt_ref, g_ref, gt_ref, gain_ref, o_ref, ht_ref, ct_ref, n_ref, m_ref):
    head_mask = (_iota((BRANCH, BRANCH), 0) // HEAD_DIM) == (_iota((BRANCH, BRANCH), 1) // HEAD_DIM)
    rr = _iota((ML_CHUNK, ML_CHUNK), 0)
    cc = _iota((ML_CHUNK, ML_CHUNK), 1)
    lane = _iota((1, BRANCH), 1)
    row16 = _iota((ONES_ROWS, BRANCH), 0)
    lane16 = _iota((ONES_ROWS, BRANCH), 1) // HEAD_DIM
    consts = []
    for direction in (0, 1):
        tri = ((cc <= rr) if direction == 0 else (cc >= rr)).astype(BF16)
        valid = (rr <= cc) if direction == 0 else (rr >= cc)
        consts.append((tri, valid, head_mask, lane, row16, lane16))
    ct_ref[...] = jnp.zeros_like(ct_ref)
    n_ref[...] = jnp.zeros_like(n_ref)
    m_ref[...] = jnp.zeros_like(m_ref)
    n_ctx, n_all = CTX_LEN // ML_CHUNK, T_ALL // ML_CHUNK

    def body(it, carry):
        for u in range(ML_UNROLL):
            i = it * ML_UNROLL + u
            chunk = (i, jnp.where(i < n_ctx, n_ctx - 1 - i, n_all - 1 + n_ctx - i))
            cxs = [_mlstm_logits(d, chunk[d], p_ref, vt_ref, g_ref, gt_ref, ct_ref, n_ref, m_ref, consts[d])
                   for d in (0, 1)]
            for d in (0, 1):
                _mlstm_outputs(cxs[d], vt_ref, ht_ref, ct_ref, n_ref, m_ref, consts[d])
        return carry

    lax.fori_loop(0, n_all // ML_UNROLL, body, 0)

    def out_body(t, carry):
        rows = pl.ds(pl.multiple_of(t * TILE, TILE), TILE)
        o_ref[rows, :] = _head_rmsnorm_from_t(ht_ref[0, :, rows] + ht_ref[1, :, rows],
                                              gain_ref[...]).astype(BF16)
        return carry

    lax.fori_loop(0, N_TILES, out_body, 0)


def _mlstm_call(pd, vtd, pg, pgt, g_row):
    bsz = pd.shape[0]
    return pl.pallas_call(
        _mlstm_kernel,
        grid=(bsz,),
        in_specs=[
            pl.BlockSpec((None, T_ALL, 2 * BRANCH), lambda b: (b, 0, 0)),
            pl.BlockSpec((None, BRANCH, T_ALL), lambda b: (b, 0, 0)),
            pl.BlockSpec((None, T_ALL, SEG_G), lambda b: (b, 0, 0)),
            pl.BlockSpec((None, SEG_G, T_ALL), lambda b: (b, 0, 0)),
            pl.BlockSpec((1, BRANCH), lambda b: (0, 0)),
        ],
        out_specs=pl.BlockSpec((None, T_ALL, BRANCH), lambda b: (b, 0, 0)),
        out_shape=jax.ShapeDtypeStruct((bsz, T_ALL, BRANCH), BF16),
        scratch_shapes=[pltpu.VMEM((2, BRANCH, T_ALL), F32), pltpu.VMEM((2, BRANCH, BRANCH), F32),
                        pltpu.VMEM((2, 1, BRANCH), F32), pltpu.VMEM((2, 1, LANES), F32)],
        compiler_params=_params("arbitrary"),
        name="mlstm",
    )(pd, vtd, pg, pgt, g_row)


def _outproj_kernel(x_ref, ctx_ref, ya_ref, yb_ref, yc_ref, yd_ref, po_ref, mod_ref, w_ref, fg_ref,
                    *out_refs, bsz, last):
    t = pl.program_id(1)
    is_ctx = jnp.logical_and(t == 0, not last)
    row = jnp.where(is_ctx, bsz, pl.program_id(0))
    gate_mod = mod_ref[pl.ds(row, 1), 2 * D_MODEL:3 * D_MODEL]
    po = po_ref[...].astype(F32)
    yd = yd_ref[...].astype(F32) * po[:, 0:BRANCH]
    mixed = jnp.concatenate([ya_ref[...].astype(F32), yb_ref[...].astype(F32), yc_ref[...].astype(F32), yd],
                            axis=-1)
    mixed = (mixed * po[:, BRANCH:]).astype(BF16)
    delta = gate_mod * _dot(mixed, w_ref[...])
    if last:
        xn = x_ref[...] + delta
        ms = jnp.mean(xn * xn, axis=-1, keepdims=True)
        out_refs[0][...] = xn * lax.rsqrt(ms + NORM_EPS) * fg_ref[...]
    else:
        x_out_ref, ctx_out_ref = out_refs

        @pl.when(t == 0)
        def _():
            ctx_out_ref[...] = ctx_ref[...] + delta

        @pl.when(t > 0)
        def _():
            x_out_ref[...] = x_ref[...] + delta


def _outproj_call(x, ctx, ya, yb, yc, yd, po, mod, w_out_bf, final_g, layer, last):
    bsz = x.shape[0]
    tile0 = 1 if last else 0
    rows = mod.shape[1]

    def tok(width, arr):
        off = tile0 if arr.shape[1] == T_ALL else 0
        return pl.BlockSpec((None, TILE, width), lambda b, t: (b, t + off, 0))

    lat_spec = pl.BlockSpec((None, TILE, D_MODEL), lambda b, t: (b, jnp.maximum(t + tile0 - 1, 0), 0))
    ctx_spec = pl.BlockSpec((None, CTX_LEN, D_MODEL), lambda b, t: (b, 0, 0))
    lat_shape = jax.ShapeDtypeStruct((bsz, SEQ, D_MODEL), F32)
    ctx_shape = jax.ShapeDtypeStruct((bsz, CTX_LEN, D_MODEL), F32)
    return pl.pallas_call(
        functools.partial(_outproj_kernel, bsz=bsz, last=last),
        grid=(bsz, N_TILES - tile0),
        in_specs=[lat_spec, ctx_spec, tok(BRANCH, ya), tok(BRANCH, yb), tok(BRANCH, yc), tok(BRANCH, yd),
                  tok(SEG_O, po),
                  pl.BlockSpec((None, rows, 3 * D_MODEL), lambda b, t: (layer, 0, 0)),
                  pl.BlockSpec((None, D_MODEL, D_MODEL), lambda b, t: (layer, 0, 0)),
                  pl.BlockSpec((1, D_MODEL), lambda b, t: (0, 0))],
        out_specs=lat_spec if last else [lat_spec, ctx_spec],
        out_shape=lat_shape if last else [lat_shape, ctx_shape],
        compiler_params=_params("arbitrary", "arbitrary"),
        name="outproj",
    )(x, ctx, ya, yb, yc, yd, po, mod, w_out_bf, final_g.reshape(1, D_MODEL))


def _rope_tables(dim):
    quarter = dim // 4
    half = dim // 2
    pos = np.arange(SEQ)
    row = (pos // GRID_W).astype(np.float32)
    col = (pos % GRID_W).astype(np.float32)
    inv = (np.float32(ROPE_BASE) ** (-np.arange(0, half, 2, dtype=np.float32) / np.float32(half))).astype(np.float32)
    ang_r = row[:, None] * inv[None, :]
    ang_c = col[:, None] * inv[None, :]
    lane = np.arange(LANES) % dim
    part = lane // quarter
    freq = lane % quarter
    ang = np.where(part[None, :] < 2, ang_r[:, freq], ang_c[:, freq]).astype(np.float32)
    cos = np.cos(ang)
    sin = np.sin(ang)
    first = (part % 2 == 0)[None, :]
    s_next = np.where(first, -sin, 0.0)
    s_prev = np.where(first, 0.0, sin)
    tab = np.stack([cos, s_next, s_prev]).astype(np.float32)
    ident = np.stack([np.ones((CTX_LEN, LANES)), np.zeros((CTX_LEN, LANES)),
                      np.zeros((CTX_LEN, LANES))]).astype(np.float32)
    return jnp.asarray(np.concatenate([ident, tab], axis=1))


def _relayout_cols(a):
    seg_a = a[..., 0:768]
    seg_b = a[..., 768:1792]
    qc = a[..., 1792:2048]
    qc = jnp.concatenate([qc[..., 0:64], qc[..., 128:192], qc[..., 64:128], qc[..., 192:256]], axis=-1)
    kvc = a[..., 2048:2304]
    seg_d = a[..., 2304:3072]
    gates = a[..., 3072:3088]
    seg_o = a[..., 3088:4368]
    pad = jnp.zeros(a.shape[:-1] + (SEG_G - 16,), a.dtype)
    return jnp.concatenate([seg_a, qc, kvc, seg_b, seg_d, gates, pad, seg_o], axis=-1)


def _segment_sources():
    ranges = [(0, 768),
              (1792, 1856), (1920, 1984), (1856, 1920), (1984, 2048),
              (2048, 2304), (768, 1792), (2304, 3072), (3072, 3088), None, (3088, 4368)]
    blocks, cur, room = [], [], LANES
    for rg in ranges:
        lo, hi = (0, SEG_G - 16) if rg is None else rg
        while lo < hi:
            n = min(room, hi - lo)
            cur.append(None if rg is None else (lo, lo + n))
            lo, room = lo + n, room - n
            if room == 0:
                blocks.append(cur)
                cur, room = [], LANES
    assert not cur and len(blocks) * LANES == PROJ_PAD
    return blocks


def _wprep_kernel(wt_ref, o_ref):
    for j, pieces in enumerate(_segment_sources()):
        rows = [jnp.zeros((SEG_G - 16, D_MODEL), F32) if p is None else wt_ref[p[0]:p[1], :] for p in pieces]
        blk = rows[0] if len(rows) == 1 else jnp.concatenate(rows, axis=0)
        o_ref[:, j * LANES:(j + 1) * LANES] = blk.T.astype(BF16)


def _wprep_call(w_in):
    depth, _, width = w_in.shape
    return pl.pallas_call(
        _wprep_kernel,
        grid=(depth,),
        in_specs=[pl.BlockSpec((None, width, D_MODEL), lambda l: (l, 0, 0), pipeline_mode=pl.Buffered(1))],
        out_specs=pl.BlockSpec((None, D_MODEL, PROJ_PAD), lambda l: (l, 0, 0)),
        out_shape=jax.ShapeDtypeStruct((depth, D_MODEL, PROJ_PAD), BF16),
        compiler_params=_params("arbitrary"),
        name="wprep",
    )(jnp.swapaxes(w_in, 1, 2))


def kernel(x, c, ctx, c_ctx, w_mod, b_mod, norm_g, w_in, b_in, diff_lam, diff_g, hg_lb, hg_g,
           sw_sink, ml_g, w_out, final_g):
    bsz = x.shape[0]
    tab_a = _rope_tables(DA_QK)
    tab_c = _rope_tables(HEAD_DIM)
    rows = ((bsz + 1 + 7) // 8) * 8
    cc = jnp.concatenate([c, c_ctx[None, :], jnp.zeros((rows - bsz - 1, D_MODEL), F32)], axis=0)
    mod = _mod_call(cc, w_mod, b_mod)
    tile4 = lambda g: jnp.tile(g, HEADS).reshape(1, BRANCH)
    w_r = _wprep_call(w_in)
    b_r = _relayout_cols(b_in)
    w_out_bf = w_out.astype(BF16)
    for l in range(DEPTH):
        last = l == DEPTH - 1
        pa, pc, pb, pd, pg, po, vta, vtc, vtd, pgt = _inproj_call(x, ctx, mod, norm_g, w_r, b_r,
                                                                 tab_a, tab_c, l)
        ya = _diffattn_call(pa, vta, diff_lam[l], tile4(diff_g[l]), l, not last)
        yb = _hgrn_call(pb, hg_lb, tile4(hg_g[l]), l)
        yc = _window_call(pc, vtc, sw_sink[l].reshape(1, HEADS), not last)
        yd = _mlstm_call(pd, vtd, pg, pgt, tile4(ml_g[l]))
        res = _outproj_call(x, ctx, ya, yb, yc, yd, po, mod, w_out_bf, final_g, l, last)
        if last:
            return res
        x, ctx = res
```

```python
import functools
import math

import numpy as np
import jax
import jax.numpy as jnp
from jax import lax
from jax.experimental import pallas as pl
from jax.experimental.pallas import tpu as pltpu

F32 = jnp.float32
BF16 = jnp.bfloat16

D_MODEL = 1024
SEQ = 2048
CTX_LEN = 256
T_ALL = CTX_LEN + SEQ
GRID_W = 64
DEPTH = 2
HEADS = 4
HEAD_DIM = 64
BRANCH = HEADS * HEAD_DIM
DA_QK = 32
SW_WINDOW = 128
HG_CHUNK = 16
ROPE_BASE = 10000.0
NORM_EPS = 1e-6
NEG_BIG = -1e30
LOG2E = math.log2(math.e)

TILE = 256
N_TILES = T_ALL // TILE
LANES = 128
ONES_ROWS = 16

SEG_A = 3 * BRANCH
KV_WIDTH = 2 * HEAD_DIM
SEG_C = BRANCH + 2 * KV_WIDTH
SEG_B = 4 * BRANCH
SEG_D = 3 * BRANCH
SEG_G = LANES
SEG_O = BRANCH + D_MODEL
OFF_A = 0
OFF_C = OFF_A + SEG_A
OFF_B = OFF_C + SEG_C
OFF_D = OFF_B + SEG_B
OFF_G = OFF_D + SEG_D
OFF_O = OFF_G + SEG_G
PROJ_PAD = OFF_O + SEG_O

VMEM_LIMIT = 56 * 1024 * 1024


def _params(*sem):
    return pltpu.CompilerParams(dimension_semantics=sem, vmem_limit_bytes=VMEM_LIMIT)


def _dot(a, b):
    return jnp.dot(a, b, preferred_element_type=F32)


def _dot_nt(a, b):
    return lax.dot_general(a, b, (((1,), (1,)), ((), ())), preferred_element_type=F32)


def _dot_tn(a, b):
    return lax.dot_general(a, b, (((0,), (0,)), ((), ())), preferred_element_type=F32)


def _split3(x):
    x1 = x.astype(BF16)
    r1 = x - x1.astype(F32)
    x2 = r1.astype(BF16)
    x3 = (r1 - x2.astype(F32)).astype(BF16)
    return x1, x2, x3


def _dot_exact_l(m01, x):
    x1, x2, x3 = _split3(x)
    return _dot(m01, x1) + _dot(m01, x2) + _dot(m01, x3)


def _dot_exact_r(x, m01):
    x1, x2, x3 = _split3(x)
    return _dot(x1, m01) + _dot(x2, m01) + _dot(x3, m01)


def _dot_exact_nt(x, m01):
    x1, x2, x3 = _split3(x)
    return _dot_nt(x1, m01) + _dot_nt(x2, m01) + _dot_nt(x3, m01)


def _sigmoid(z):
    e = jnp.exp(-jnp.abs(z))
    r = 1.0 / (1.0 + e)
    return jnp.where(z >= 0, r, e * r)


def _log_sigmoid(z):
    return jnp.minimum(z, 0.0) - jnp.log(1.0 + jnp.exp(-jnp.abs(z)))


def _iota(shape, dim):
    return lax.broadcasted_iota(jnp.int32, shape, dim)


def _head_sum_matrix():
    r = _iota((BRANCH, BRANCH), 0) // HEAD_DIM
    c = _iota((BRANCH, BRANCH), 1) // HEAD_DIM
    return (r == c).astype(BF16)


def _head_rmsnorm(o, g_row):
    ss = _dot_exact_r(o * o, _head_sum_matrix())
    return o * lax.rsqrt(ss * (1.0 / HEAD_DIM) + NORM_EPS) * g_row


def _head_rmsnorm_from_t(ot, g_row):
    n = ot.shape[1]
    o3 = ot.reshape(HEADS, HEAD_DIM, n)
    ss = jnp.sum(o3 * o3, axis=1, keepdims=True)
    o3 = o3 * lax.rsqrt(ss * (1.0 / HEAD_DIM) + NORM_EPS)
    return o3.reshape(BRANCH, n).T * g_row


def _mod_kernel(cc_ref, w_ref, b_ref, o_ref):
    cc = cc_ref[...]
    a = (cc * _sigmoid(cc)).astype(BF16)
    o_ref[...] = _dot(a, w_ref[...].astype(BF16)) + b_ref[...]


def _mod_call(cc, w_mod, b_mod):
    rows = cc.shape[0]
    nblk = 3
    return pl.pallas_call(
        _mod_kernel,
        grid=(DEPTH, nblk),
        in_specs=[
            pl.BlockSpec((rows, D_MODEL), lambda l, j: (0, 0)),
            pl.BlockSpec((None, D_MODEL, D_MODEL), lambda l, j: (l, 0, j)),
            pl.BlockSpec((None, 1, D_MODEL), lambda l, j: (l, 0, j)),
        ],
        out_specs=pl.BlockSpec((None, rows, D_MODEL), lambda l, j: (l, 0, j)),
        out_shape=jax.ShapeDtypeStruct((DEPTH, rows, 3 * D_MODEL), F32),
        compiler_params=_params("arbitrary", "arbitrary"),
        name="mod",
    )(cc, w_mod, b_mod.reshape(DEPTH, 1, 3 * D_MODEL))


def _rope(slab, cos, sin_next, sin_prev, off):
    nxt = pltpu.roll(slab, LANES - off, 1)
    prv = pltpu.roll(slab, off, 1)
    return slab * cos + nxt * sin_next + prv * sin_prev


def _inproj_kernel(x_ref, ctx_ref, mod_ref, ng_ref, w_ref, c_ref, ta_ref, tc_ref,
                   pa_ref, pc_ref, pb_ref, pd_ref, pg_ref, po_ref, vta_ref, vtc_ref, vtd_ref, pgt_ref,
                   *, bsz):
    is_ctx = pl.program_id(1) == 0
    row = jnp.where(is_ctx, bsz, pl.program_id(0))
    x = jnp.where(is_ctx, ctx_ref[...], x_ref[...])
    scale = mod_ref[pl.ds(row, 1), D_MODEL:2 * D_MODEL]
    xg = (x * (ng_ref[...] * (1.0 + scale))).astype(BF16)
    r = lax.rsqrt(jnp.mean(x * x, axis=-1, keepdims=True) + NORM_EPS)
    crow = c_ref[pl.ds(row, 1), :]

    def proj(off, width):
        return _dot(xg, w_ref[:, off:off + width]) * r + crow[:, off:off + width]

    def rope_seg(acc, tab_ref, off, q_scale, k_slabs):
        cos, s_next, s_prev = tab_ref[0], tab_ref[1], tab_ref[2]
        outs = []
        for j in range(2 + k_slabs):
            r = _rope(acc[:, j * LANES:(j + 1) * LANES], cos, s_next, s_prev, off)
            outs.append(r * q_scale if j < 2 else r)
        outs.append(acc[:, (2 + k_slabs) * LANES:])
        return jnp.concatenate(outs, axis=-1)

    acco = proj(OFF_O, SEG_O)
    po_ref[...] = jnp.concatenate(
        [_sigmoid(acco[:, 0:BRANCH]), acco[:, BRANCH:] * _sigmoid(acco[:, BRANCH:])], axis=-1).astype(BF16)
    acca = rope_seg(proj(OFF_A, SEG_A), ta_ref, DA_QK // 4, DA_QK ** -0.5 * LOG2E, 2)
    pa_ref[...] = acca[:, 0:2 * BRANCH].astype(BF16)
    vta_ref[...] = acca[:, 2 * BRANCH:].T.astype(BF16)
    accc = rope_seg(proj(OFF_C, SEG_C), tc_ref, HEAD_DIM // 4, HEAD_DIM ** -0.5 * LOG2E, 1)
    pc_ref[...] = accc[:, 0:BRANCH + KV_WIDTH].astype(BF16)
    vtc_ref[...] = accc[:, BRANCH + KV_WIDTH:].T.astype(BF16)
    accd = proj(OFF_D, SEG_D)
    pd_ref[...] = jnp.concatenate(
        [accd[:, 0:BRANCH], accd[:, BRANCH:2 * BRANCH] * (HEAD_DIM ** -0.5)], axis=-1).astype(BF16)
    vtd_ref[...] = accd[:, 2 * BRANCH:].T.astype(BF16)
    gates = proj(OFF_G, SEG_G)
    pg_ref[...] = gates
    pgt_ref[...] = gates.T
    pb_ref[...] = proj(OFF_B, SEG_B)


def _crow_kernel(mod_ref, w_ref, b_ref, o_ref):
    shift = mod_ref[:, 0:D_MODEL]
    hi = shift.astype(BF16)
    lo = (shift - hi.astype(F32)).astype(BF16)
    o_ref[...] = _dot(hi, w_ref[...]) + _dot(lo, w_ref[...]) + b_ref[...]


def _crow_call(mod, w_r, b_r):
    depth, rows, _ = mod.shape
    return pl.pallas_call(
        _crow_kernel,
        grid=(depth,),
        in_specs=[pl.BlockSpec((None, rows, 3 * D_MODEL), lambda l: (l, 0, 0)),
                  pl.BlockSpec((None, D_MODEL, PROJ_PAD), lambda l: (l, 0, 0)),
                  pl.BlockSpec((None, 1, PROJ_PAD), lambda l: (l, 0, 0))],
        out_specs=pl.BlockSpec((None, rows, PROJ_PAD), lambda l: (l, 0, 0)),
        out_shape=jax.ShapeDtypeStruct((depth, rows, PROJ_PAD), F32),
        compiler_params=_params("arbitrary"),
        name="crow",
    )(mod, w_r, b_r.reshape(depth, 1, PROJ_PAD))


def _inproj_call(x, ctx, mod, norm_g, w_r, c_rows, tab_a, tab_c, layer):
    bsz = x.shape[0]
    rows = mod.shape[1]
    widths = [(2 * BRANCH, BF16), (BRANCH + KV_WIDTH, BF16), (SEG_B, F32), (2 * BRANCH, BF16), (SEG_G, F32),
              (SEG_O, BF16)]
    out_specs = [pl.BlockSpec((None, TILE, w), lambda b, t: (b, t, 0)) for w, _ in widths]
    out_shape = [jax.ShapeDtypeStruct((bsz, T_ALL, w), dt) for w, dt in widths]
    for rows_t, dt in ((BRANCH, BF16), (KV_WIDTH, BF16), (BRANCH, BF16), (SEG_G, F32)):
        out_specs.append(pl.BlockSpec((None, rows_t, TILE), lambda b, t: (b, 0, t)))
        out_shape.append(jax.ShapeDtypeStruct((bsz, rows_t, T_ALL), dt))
    return pl.pallas_call(
        functools.partial(_inproj_kernel, bsz=bsz),
        grid=(bsz, N_TILES),
        in_specs=[
            pl.BlockSpec((None, TILE, D_MODEL), lambda b, t: (b, jnp.maximum(t - 1, 0), 0)),
            pl.BlockSpec((None, CTX_LEN, D_MODEL), lambda b, t: (b, 0, 0)),
            pl.BlockSpec((None, rows, 3 * D_MODEL), lambda b, t: (layer, 0, 0)),
            pl.BlockSpec((None, 1, D_MODEL), lambda b, t: (layer, 0, 0)),
            pl.BlockSpec((None, D_MODEL, PROJ_PAD), lambda b, t: (layer, 0, 0)),
            pl.BlockSpec((None, rows, PROJ_PAD), lambda b, t: (layer, 0, 0)),
            pl.BlockSpec((3, TILE, LANES), lambda b, t: (0, t, 0)),
            pl.BlockSpec((3, TILE, LANES), lambda b, t: (0, t, 0)),
        ],
        out_specs=out_specs,
        out_shape=out_shape,
        compiler_params=_params("arbitrary", "arbitrary"),
        name="inproj",
    )(x, ctx, mod, norm_g.reshape(-1, 1, D_MODEL), w_r, c_rows, tab_a, tab_c)


def _diffattn_kernel(q_ref, qn_ref, k_ref, vt_ref, lam_ref, g_ref, o_ref, acc_ref, s_ref, m8_ref,
                     *, lam_init, q_tile0):
    step = pl.program_id(1)
    qb = step + q_tile0
    lp = lam_ref[...]
    lam = (jnp.exp(jnp.sum(lp[0:1] * lp[1:2], axis=-1, keepdims=True))
           - jnp.exp(jnp.sum(lp[2:3] * lp[3:4], axis=-1, keepdims=True)) + lam_init)
    q = q_ref[...]
    lane = _iota((1, BRANCH), 1)
    n_pairs = 2 * HEADS
    sub = TILE // 8
    ones_rows = jnp.ones((ONES_ROWS, TILE), BF16)

    def pair_q(qv, hm):
        return jnp.where(lane // DA_QK == hm, qv, jnp.zeros_like(qv))

    def logits(qm, nk, buf):
        m8 = None
        half = max(nk // 2, TILE)
        for r0 in range(0, nk, half):
            st = _dot_nt(k_ref[r0:r0 + half, :], qm)
            s_ref[buf, r0:r0 + half, :] = st
            mh = jnp.max(st.reshape(half // 8, 8, TILE), axis=0)
            m8 = mh if m8 is None else jnp.maximum(m8, mh)
        return m8

    def value_tile(hm, j, mb, ot):
        h = hm // 2
        st = s_ref[hm % 2, j * TILE:(j + 1) * TILE, :]
        e = jnp.exp2(st.reshape(sub, 8, TILE) - mb[None])
        vt = vt_ref[h * HEAD_DIM:(h + 1) * HEAD_DIM, j * TILE:(j + 1) * TILE]
        vt1 = jnp.concatenate([vt, ones_rows], axis=0)
        return ot + _dot(vt1, e.reshape(TILE, TILE).astype(BF16))

    def attend(nk, own_first_logits):
        n_kt = nk // TILE
        m8 = logits(pair_q(q, 0), nk, 0) if own_first_logits else m8_ref[...]
        for hm in range(n_pairs):
            mb = jnp.broadcast_to(jnp.max(m8, axis=0, keepdims=True), (8, TILE))
            if hm + 1 < n_pairs:
                m8 = logits(pair_q(q, hm + 1), nk, (hm + 1) % 2)
            else:
                m8_ref[...] = logits(pair_q(qn_ref[...], 0), T_ALL, 0)
            ot = jnp.zeros((HEAD_DIM + ONES_ROWS, TILE), F32)
            for j in range(n_kt):
                ot = value_tile(hm, j, mb, ot)
            l = ot[HEAD_DIM:HEAD_DIM + 1, :]
            ot = ot[0:HEAD_DIM, :]
            rows = slice((hm // 2) * HEAD_DIM, (hm // 2 + 1) * HEAD_DIM)
            if hm % 2 == 0:
                acc_ref[rows, :] = ot * (1.0 / l)
            else:
                acc_ref[rows, :] -= ot * (lam / l)
        o_ref[...] = (_head_rmsnorm_from_t(acc_ref[...], g_ref[...]) * (1.0 - lam_init)).astype(BF16)

    if q_tile0 == 0:
        @pl.when(qb == 0)
        def _():
            attend(CTX_LEN, True)
    else:
        @pl.when(step == 0)
        def _():
            m8_ref[...] = logits(pair_q(q, 0), T_ALL, 0)

    @pl.when(qb > 0)
    def _():
        attend(T_ALL, False)


def _diffattn_call(pa, vta, lam_p, g_row, layer_idx, need_ctx):
    bsz = pa.shape[0]
    q_tile0 = 0 if need_ctx else 1
    lam_init = 0.8 - 0.6 * math.exp(-0.3 * layer_idx)
    return pl.pallas_call(
        functools.partial(_diffattn_kernel, lam_init=lam_init, q_tile0=q_tile0),
        grid=(bsz, N_TILES - q_tile0),
        in_specs=[
            pl.BlockSpec((None, TILE, BRANCH), lambda b, t: (b, t + q_tile0, 0)),
            pl.BlockSpec((None, TILE, BRANCH), lambda b, t: (b, jnp.minimum(t + q_tile0 + 1, N_TILES - 1), 0)),
            pl.BlockSpec((None, T_ALL, BRANCH), lambda b, t: (b, 0, 1)),
            pl.BlockSpec((None, BRANCH, T_ALL), lambda b, t: (b, 0, 0)),
            pl.BlockSpec((4, DA_QK), lambda b, t: (0, 0)),
            pl.BlockSpec((1, BRANCH), lambda b, t: (0, 0)),
        ],
        out_specs=pl.BlockSpec((None, TILE, BRANCH), lambda b, t: (b, t, 0)),
        out_shape=jax.ShapeDtypeStruct((bsz, (N_TILES - q_tile0) * TILE, BRANCH), BF16),
        scratch_shapes=[pltpu.VMEM((BRANCH, TILE), F32), pltpu.VMEM((2, T_ALL, TILE), F32),
                        pltpu.VMEM((8, TILE), F32)],
        compiler_params=_params("arbitrary", "arbitrary"),
        name="diffattn",
    )(pa, pa, pa, vta, lam_p, g_row)


BAND = 2 * TILE


def _window_kernel(q_ref, k_ref, vt_ref, sink_ref, o_ref, acc_ref, s_ref, *, q_tile0):
    qb = pl.program_id(1) + q_tile0
    lane = _iota((1, KV_WIDTH), 1)
    ones_rows = jnp.ones((ONES_ROWS, TILE), BF16)
    group = HEADS // (KV_WIDTH // HEAD_DIM)

    def attend(band):
        if band:
            a = (qb - 1) * TILE
            start = jnp.clip(a - SW_WINDOW, 0, SEQ - BAND)
            row0 = pl.multiple_of(CTX_LEN + start, SW_WINDOW)
            kb = k_ref[pl.ds(row0, BAND), :]
            kpos = start + _iota((BAND, 1), 0)
            qpos = a + _iota((1, TILE), 1)
            valid = jnp.abs(qpos - kpos) <= SW_WINDOW
        sinks, maxes = [], []
        for h in range(HEADS):
            kvh, g = h // group, h % group
            qg = q_ref[:, g * KV_WIDTH:(g + 1) * KV_WIDTH]
            qm = jnp.where(lane // HEAD_DIM == kvh, qg, jnp.zeros_like(qg))
            sink = sink_ref[:, h:h + 1] * LOG2E
            sc = _dot_nt(k_ref[0:CTX_LEN, :], qm)
            s_ref[h, 0:CTX_LEN, :] = sc
            m = jnp.maximum(jnp.max(sc, axis=0, keepdims=True), sink)
            if band:
                sb = jnp.where(valid, _dot_nt(kb, qm), NEG_BIG)
                s_ref[h, CTX_LEN:CTX_LEN + BAND, :] = sb
                m = jnp.maximum(m, jnp.max(sb, axis=0, keepdims=True))
            sinks.append(sink)
            maxes.append(m)
        for h in range(HEADS):
            m = maxes[h]
            kv_rows = slice((h // group) * HEAD_DIM, (h // group + 1) * HEAD_DIM)
            vt1 = jnp.concatenate([vt_ref[kv_rows, 0:CTX_LEN], ones_rows], axis=0)
            pv = _dot(vt1, jnp.exp2(s_ref[h, 0:CTX_LEN, :] - m).astype(BF16))
            if band:
                vtb = vt_ref[kv_rows, pl.ds(row0, BAND)]
                ones_b = jnp.ones((ONES_ROWS, BAND), BF16)
                eb = jnp.exp2(s_ref[h, CTX_LEN:CTX_LEN + BAND, :] - m).astype(BF16)
                pv = pv + _dot(jnp.concatenate([vtb, ones_b], axis=0), eb)
            l = pv[HEAD_DIM:HEAD_DIM + 1, :] + jnp.exp2(sinks[h] - m)
            acc_ref[h * HEAD_DIM:(h + 1) * HEAD_DIM, :] = pv[0:HEAD_DIM, :] * (1.0 / l)
        o_ref[...] = acc_ref[...].T.astype(BF16)

    @pl.when(qb == 0)
    def _():
        attend(False)

    @pl.when(qb > 0)
    def _():
        attend(True)


def _window_call(pc, vtc, sink_row, need_ctx):
    bsz = pc.shape[0]
    q_tile0 = 0 if need_ctx else 1
    n_q = N_TILES - q_tile0
    return pl.pallas_call(
        functools.partial(_window_kernel, q_tile0=q_tile0),
        grid=(bsz, n_q),
        in_specs=[
            pl.BlockSpec((None, TILE, BRANCH), lambda b, t: (b, t + q_tile0, 0)),
            pl.BlockSpec((None, T_ALL, KV_WIDTH), lambda b, t: (b, 0, BRANCH // KV_WIDTH)),
            pl.BlockSpec((None, KV_WIDTH, T_ALL), lambda b, t: (b, 0, 0)),
            pl.BlockSpec((1, HEADS), lambda b, t: (0, 0)),
        ],
        out_specs=pl.BlockSpec((None, TILE, BRANCH), lambda b, t: (b, t, 0)),
        out_shape=jax.ShapeDtypeStruct((bsz, n_q * TILE, BRANCH), BF16),
        scratch_shapes=[pltpu.VMEM((BRANCH, TILE), F32), pltpu.VMEM((HEADS, CTX_LEN + BAND, TILE), F32)],
        compiler_params=_params("arbitrary", "arbitrary"),
        name="window",
    )(pc, pc, vtc, sink_row)


CH_PER_TILE = TILE // HG_CHUNK
HG_STATE = 64
ST_PER_TILE = TILE // HG_STATE
HG_UNROLL = 9
HG_FAST_LIMIT = 180.0
HG_FAST_SHIFT = 60.0


def _hgrn_gates(z, lb_terms):
    log2_ksig = _log_sigmoid(-z) * LOG2E
    if lb_terms is None:
        return _log_sigmoid(z) * LOG2E, _sigmoid(-z), log2_ksig
    lb, log_lb, log_1m = lb_terms
    bt = log_1m + _log_sigmoid(z)
    mx = jnp.maximum(log_lb, bt)
    log_f = mx + jnp.log(jnp.exp(log_lb - mx) + jnp.exp(bt - mx))
    return log_f * LOG2E, (1.0 - lb) * _sigmoid(-z), log_1m * LOG2E + log2_ksig


def _hgrn_intra(direction, q3, c3, u3, v3, head_ones):
    half = HG_CHUNK // 2
    slabs, meta = [], []
    for s in range(HG_CHUNK):
        us = jnp.broadcast_to(u3[:, s:s + 1, :], (CH_PER_TILE, half, BRANCH))
        for g in range(2):
            lo_row, hi_row = half * g, half * g + half - 1
            if direction == 0:
                none_valid, all_valid = hi_row < s, lo_row >= s
            else:
                none_valid, all_valid = lo_row > s, hi_row <= s
            if none_valid:
                continue
            d = c3[:, half * g:half * (g + 1), :] - us
            if not all_valid:
                row = _iota((1, half, 1), 1) + half * g
                d = jnp.where((row >= s) if direction == 0 else (row <= s), d, NEG_BIG)
            x = q3[:, half * g:half * (g + 1), :] * jnp.exp2(d)
            slabs.append(x.reshape(CH_PER_TILE * half, BRANCH).astype(BF16))
            meta.append((s, g))
    a_all = _dot(jnp.concatenate(slabs, axis=0), head_ones)
    o = [jnp.zeros((CH_PER_TILE, half, BRANCH), F32) for _ in range(2)]
    n = CH_PER_TILE * half
    vs = None
    for i, (s, g) in enumerate(meta):
        if i == 0 or meta[i - 1][0] != s:
            vs = jnp.broadcast_to(v3[:, s:s + 1, :], (CH_PER_TILE, half, BRANCH))
        o[g] = o[g] + a_all[i * n:(i + 1) * n].reshape(CH_PER_TILE, half, BRANCH) * vs
    return jnp.concatenate(o, axis=1).reshape(TILE, BRANCH)


def _hgrn_safe_block(direction, q, kk, log2_f, log2_k, v, v_heads, masks):
    rr, cc, lane, row_in, tri16, ones16, head_ones = masks
    shape3 = (CH_PER_TILE, HG_CHUNK, BRANCH)
    cum = _dot_exact_l(tri16[direction], log2_f)
    tot = _dot_exact_l(ones16, log2_f)
    o = _hgrn_intra(direction, q.reshape(shape3), cum.reshape(shape3), (cum - log2_k).reshape(shape3),
                    v.reshape(shape3), head_ones)

    def shift_rows(a, n):
        n = n % TILE
        return jnp.concatenate([a[TILE - n:, :], a[:TILE - n, :]], axis=0)

    a_heads = [jnp.zeros((TILE, TILE), F32) for _ in range(HEADS)]
    g = HG_CHUNK
    while g < HG_STATE:
        qt = (q * jnp.exp2(cum)).astype(BF16)
        ke = (kk * jnp.exp2(tot - cum)).astype(BF16)
        later_r = ((rr % (2 * g)) >= g) if direction == 0 else ((rr % (2 * g)) < g)
        later_c = ((cc % (2 * g)) >= g) if direction == 0 else ((cc % (2 * g)) < g)
        pair = ((rr // (2 * g)) == (cc // (2 * g))) & later_r & jnp.logical_not(later_c)
        for h in range(HEADS):
            qh = jnp.where(lane // HEAD_DIM == h, qt, jnp.zeros_like(qt))
            a_heads[h] = jnp.where(pair, _dot_nt(qh, ke), a_heads[h])
        later_row = ((row_in % (2 * g)) >= g) if direction == 0 else ((row_in % (2 * g)) < g)
        sign = 1 if direction == 0 else -1
        tot_other = jnp.where(later_row, shift_rows(tot, sign * g), shift_rows(tot, -sign * g))
        cum = cum + jnp.where(later_row, tot_other, 0.0)
        tot = tot + tot_other
        g *= 2
    a_cat = jnp.concatenate([a.astype(BF16) for a in a_heads], axis=1)
    return o + _dot(a_cat, v_heads)


def _hgrn_kernel(p_ref, lb_ref, g_ref, o_ref, qt_ref, ke_ref, vb_ref, dec_ref, of_ref, oi_ref, st_ref,
                 *, layer_idx):
    rr = _iota((TILE, TILE), 0)
    cc = _iota((TILE, TILE), 1)
    lane = _iota((1, BRANCH), 1)
    row_in = _iota((TILE, 1), 0)
    same16 = (rr // HG_CHUNK) == (cc // HG_CHUNK)
    same64 = (rr // HG_STATE) == (cc // HG_STATE)
    within = [same64 & (cc <= rr), same64 & (cc >= rr)]
    tri64 = [w.astype(BF16) for w in within]
    ones64 = same64.astype(BF16)
    head_mask = (rr // HEAD_DIM) == (cc // HEAD_DIM)
    safe_masks = (rr, cc, lane, row_in, [(same16 & (cc <= rr)).astype(BF16), (same16 & (cc >= rr)).astype(BF16)],
                  same16.astype(BF16), _head_sum_matrix())

    lb_terms = None
    if layer_idx > 0:
        lbp = lb_ref[...]
        lbp = lbp - jnp.max(lbp, axis=0, keepdims=True)
        sm = jnp.exp(lbp)
        sm = sm / jnp.sum(sm, axis=0, keepdims=True)
        lb = jnp.sum(sm[1:layer_idx + 1], axis=0, keepdims=True)
        lb_terms = (lb, jnp.log(lb), jnp.log(1.0 - lb))

    def tile_body(t, carry):
        rows = pl.ds(pl.multiple_of(t * TILE, TILE), TILE)
        q = p_ref[rows, 0:BRANCH] * (HEAD_DIM ** -0.5)
        v = p_ref[rows, 3 * BRANCH:4 * BRANCH]
        vbf = v.astype(BF16)
        vb_ref[rows, :] = vbf
        v_heads = jnp.concatenate([jnp.where(lane // HEAD_DIM == h, vbf, jnp.zeros_like(vbf))
                                   for h in range(HEADS)], axis=0)
        o_intra = None
        fallback = []
        for direction in (0, 1):
            z = p_ref[rows, (1 + direction) * BRANCH:(2 + direction) * BRANCH]
            log2_f, kk, log2_k = _hgrn_gates(z, lb_terms)
            cum = _dot_exact_l(tri64[direction], log2_f)
            tot = _dot_exact_l(ones64, log2_f)
            low = jnp.min(cum, axis=(0, 1), keepdims=True)
            q_fast = (q * jnp.exp2(cum + HG_FAST_SHIFT)).astype(BF16)
            k_fast = (kk * jnp.exp2(jnp.minimum(-cum, HG_FAST_LIMIT) - HG_FAST_SHIFT)).astype(BF16)
            keep = within[direction] & (low >= -HG_FAST_LIMIT)
            a_heads = [jnp.where(keep, _dot_nt(jnp.where(lane // HEAD_DIM == h, q_fast, jnp.zeros_like(q_fast)),
                                               k_fast), 0.0) for h in range(HEADS)]
            a_cat = jnp.concatenate([a.astype(BF16) for a in a_heads], axis=1)
            o_d = _dot(a_cat, v_heads)
            qt_ref[direction, rows, :] = (q * jnp.exp2(cum)).astype(BF16)
            ke_ref[direction, rows, :] = (kk * jnp.exp2(tot - cum)).astype(BF16)
            dec_ref[direction, t] = jnp.exp2(tot.reshape(ST_PER_TILE, HG_STATE, BRANCH)[:, 0, :])
            o_intra = o_d if o_intra is None else o_intra + o_d
            fallback.append((low, kk, log2_f, log2_k))
        of_ref[rows, :] = o_intra

        for direction, (low, kk, log2_f, log2_k) in enumerate(fallback):
            @pl.when(low[0, 0] < -HG_FAST_LIMIT)
            def _():
                of_ref[rows, :] += _hgrn_safe_block(direction, q, kk, log2_f, log2_k, v, v_heads, safe_masks)
        return carry

    lax.fori_loop(0, N_TILES, tile_body, 0)

    st_ref[...] = jnp.zeros_like(st_ref)
    n_ctx, n_all = CTX_LEN // HG_STATE, T_ALL // HG_STATE

    def state_step(i, direction):
        if direction == 0:
            c = i
        else:
            c = jnp.where(i < n_ctx, n_ctx - 1 - i, n_all - 1 + n_ctx - i)
        rows = pl.ds(pl.multiple_of(c * HG_STATE, HG_STATE), HG_STATE)
        st = st_ref[direction]
        oi_ref[direction, rows, :] = _dot_nt(qt_ref[direction, rows, :], st.astype(BF16))
        ds = _dot_tn(vb_ref[rows, :], ke_ref[direction, rows, :])
        dec = dec_ref[direction, c // ST_PER_TILE, pl.ds(c % ST_PER_TILE, 1), :]
        st_ref[direction] = st * dec + jnp.where(head_mask, ds, 0.0)

    def state_body(it, carry):
        for u in range(HG_UNROLL):
            for direction in (0, 1):
                state_step(it * HG_UNROLL + u, direction)
        return carry

    lax.fori_loop(0, n_all // HG_UNROLL, state_body, 0)

    o_ref[...] = _head_rmsnorm(of_ref[...] + oi_ref[0] + oi_ref[1], g_ref[...]).astype(BF16)


def _hgrn_call(pb, hg_lb, g_row, layer_idx):
    bsz = pb.shape[0]
    return pl.pallas_call(
        functools.partial(_hgrn_kernel, layer_idx=layer_idx),
        grid=(bsz,),
        in_specs=[
            pl.BlockSpec((None, T_ALL, SEG_B), lambda b: (b, 0, 0)),
            pl.BlockSpec((DEPTH, BRANCH), lambda b: (0, 0)),
            pl.BlockSpec((1, BRANCH), lambda b: (0, 0)),
        ],
        out_specs=pl.BlockSpec((None, T_ALL, BRANCH), lambda b: (b, 0, 0)),
        out_shape=jax.ShapeDtypeStruct((bsz, T_ALL, BRANCH), BF16),
        scratch_shapes=[pltpu.VMEM((2, T_ALL, BRANCH), BF16), pltpu.VMEM((2, T_ALL, BRANCH), BF16),
                        pltpu.VMEM((T_ALL, BRANCH), BF16), pltpu.VMEM((2, N_TILES, ST_PER_TILE, BRANCH), F32),
                        pltpu.VMEM((T_ALL, BRANCH), F32), pltpu.VMEM((2, T_ALL, BRANCH), F32),
                        pltpu.VMEM((2, BRANCH, BRANCH), F32)],
        compiler_params=_params("arbitrary"),
        name="hgrn",
    )(pb, hg_lb, g_row)


ML_CHUNK = 256
ML_UNROLL = 3


def _mlstm_logits(direction, c, p_ref, vt_ref, g_ref, gt_ref, ct_ref, n_ref, m_ref, consts):
    tri, valid, _, lane, row16, lane16 = consts
    rows = pl.ds(pl.multiple_of(c * ML_CHUNK, ML_CHUNK), ML_CHUNK)
    q = p_ref[rows, 0:BRANCH]
    k = p_ref[rows, BRANCH:2 * BRANCH]
    g = g_ref[rows, :]
    gt = gt_ref[:, rows]
    cum = _dot_exact_l(tri, _log_sigmoid(g) * LOG2E)
    cum_t = _dot_exact_nt(_log_sigmoid(gt[8:16, :]) * LOG2E, tri)
    ig_t = gt[0:8, :] * LOG2E
    ct = ct_ref[direction]
    n0 = n_ref[direction]
    n_hi = n0.astype(BF16).astype(F32)
    n_lo = n0 - n_hi
    n_rows = (jnp.where((row16 < HEADS) & (lane16 == row16), n_hi, 0.0)
              + jnp.where((row16 >= HEADS) & (lane16 == row16 - HEADS), n_lo, 0.0)).astype(BF16)
    inter_all = _dot_nt(jnp.concatenate([ct.astype(BF16), n_rows], axis=0), q)
    heads = []
    for h in range(HEADS):
        r = HEADS * direction + h
        cumr = cum_t[r:r + 1, :]
        ucol = g[:, r:r + 1] * LOG2E - cum[:, 2 * HEADS + r:2 * HEADS + r + 1]
        toth = cumr[:, ML_CHUNK - 1:ML_CHUNK] if direction == 0 else cumr[:, 0:1]
        m0h = m_ref[direction, :, h:h + 1]
        logd = jnp.where(valid, cumr + ucol, NEG_BIG)
        inter = cumr + m0h
        m_t = jnp.maximum(jnp.max(logd, axis=0, keepdims=True), inter)
        qm = jnp.where(lane // HEAD_DIM == h, q, jnp.zeros_like(q))
        heads.append(dict(qk=_dot_nt(k, qm), logd=logd, m_t=m_t, g0=jnp.exp2(inter - m_t),
                          a_row=toth - cumr + ig_t[r:r + 1, :], carry=toth + m0h))
    return dict(direction=direction, rows=rows, k=k, ct=ct, n0=n0, inter_all=inter_all, heads=heads)


def _mlstm_outputs(cx, vt_ref, ht_ref, ct_ref, n_ref, m_ref, consts):
    _, _, head_mask, lane, _, _ = consts
    direction, rows, k, inter_all = cx["direction"], cx["rows"], cx["k"], cx["inter_all"]
    vt = vt_ref[:, rows]
    ones_rows = jnp.ones((ONES_ROWS, ML_CHUNK), BF16)
    row16 = _iota((ONES_ROWS, ML_CHUNK), 0)
    w_rows, sp_row = [], jnp.zeros((1, BRANCH), F32)
    for h, hd in enumerate(cx["heads"]):
        m_t, g0 = hd["m_t"], hd["g0"]
        s_t = hd["qk"] * jnp.exp2(hd["logd"] - m_t)
        vt1 = jnp.concatenate([vt[h * HEAD_DIM:(h + 1) * HEAD_DIM, :], ones_rows], axis=0)
        pv = _dot(vt1, s_t.astype(BF16))
        num = pv[0:HEAD_DIM, :] + g0 * inter_all[h * HEAD_DIM:(h + 1) * HEAD_DIM, :]
        den = pv[HEAD_DIM:HEAD_DIM + 1, :] + g0 * (inter_all[BRANCH + h:BRANCH + h + 1, :]
                                                   + inter_all[BRANCH + HEADS + h:BRANCH + HEADS + h + 1, :])
        ht_ref[direction, h * HEAD_DIM:(h + 1) * HEAD_DIM, rows] = (
            num / jnp.maximum(jnp.abs(den), jnp.exp2(-m_t)))
        a_row = hd["a_row"]
        m_loc = jnp.max(a_row, axis=1, keepdims=True)
        m_new = jnp.maximum(hd["carry"], m_loc)
        sp = jnp.exp2(hd["carry"] - m_new)
        w_rows.append(jnp.exp2(a_row - m_loc) * jnp.exp2(m_loc - m_new))
        sp_row = sp_row + jnp.where(lane // HEAD_DIM == h, sp, 0.0)
        m_ref[direction, :, h:h + 1] = m_new

    w_block = jnp.concatenate([jnp.broadcast_to(w, (HEAD_DIM, ML_CHUNK)) for w in w_rows], axis=0)
    vtw = (vt.astype(F32) * w_block).astype(BF16)
    w16 = jnp.zeros((ONES_ROWS, ML_CHUNK), F32)
    for h in range(HEADS):
        w_hi = w_rows[h].astype(BF16).astype(F32)
        w16 = w16 + jnp.where(row16 == h, w_hi, 0.0) + jnp.where(row16 == HEADS + h, w_rows[h] - w_hi, 0.0)
    dall = _dot(jnp.concatenate([vtw, w16.astype(BF16)], axis=0), k)
    ct_ref[direction] = cx["ct"] * sp_row + jnp.where(head_mask, dall[0:BRANCH, :], 0.0)
    dn = jnp.zeros((1, BRANCH), F32)
    for h in range(HEADS):
        dn = dn + jnp.where(lane // HEAD_DIM == h,
                            dall[BRANCH + h:BRANCH + h + 1, :] + dall[BRANCH + HEADS + h:BRANCH + HEADS + h + 1, :], 0.0)
    n_ref[direction] = cx["n0"] * sp_row + dn


def _mlstm_kernel(p_ref, vt_ref, g_ref, gt_ref, gain_ref, o_ref, ht_ref, ct_ref, n_ref, m_ref):
    head_mask = (_iota((BRANCH, BRANCH), 0) // HEAD_DIM) == (_iota((BRANCH, BRANCH), 1) // HEAD_DIM)
    rr = _iota((ML_CHUNK, ML_CHUNK), 0)
    cc = _iota((ML_CHUNK, ML_CHUNK), 1)
    lane = _iota((1, BRANCH), 1)
    row16 = _iota((ONES_ROWS, BRANCH), 0)
    lane16 = _iota((ONES_ROWS, BRANCH), 1) // HEAD_DIM
    consts = []
    for direction in (0, 1):
        tri = ((cc <= rr) if direction == 0 else (cc >= rr)).astype(BF16)
        valid = (rr <= cc) if direction == 0 else (rr >= cc)
        consts.append((tri, valid, head_mask, lane, row16, lane16))
    ct_ref[...] = jnp.zeros_like(ct_ref)
    n_ref[...] = jnp.zeros_like(n_ref)
    m_ref[...] = jnp.zeros_like(m_ref)
    n_ctx, n_all = CTX_LEN // ML_CHUNK, T_ALL // ML_CHUNK

    def body(it, carry):
        for u in range(ML_UNROLL):
            i = it * ML_UNROLL + u
            chunk = (i, jnp.where(i < n_ctx, n_ctx - 1 - i, n_all - 1 + n_ctx - i))
            cxs = [_mlstm_logits(d, chunk[d], p_ref, vt_ref, g_ref, gt_ref, ct_ref, n_ref, m_ref, consts[d])
                   for d in (0, 1)]
            for d in (0, 1):
                _mlstm_outputs(cxs[d], vt_ref, ht_ref, ct_ref, n_ref, m_ref, consts[d])
        return carry

    lax.fori_loop(0, n_all // ML_UNROLL, body, 0)

    def out_body(t, carry):
        rows = pl.ds(pl.multiple_of(t * TILE, TILE), TILE)
        o_ref[rows, :] = _head_rmsnorm_from_t(ht_ref[0, :, rows] + ht_ref[1, :, rows],
                                              gain_ref[...]).astype(BF16)
        return carry

    lax.fori_loop(0, N_TILES, out_body, 0)


def _mlstm_call(pd, vtd, pg, pgt, g_row):
    bsz = pd.shape[0]
    return pl.pallas_call(
        _mlstm_kernel,
        grid=(bsz,),
        in_specs=[
            pl.BlockSpec((None, T_ALL, 2 * BRANCH), lambda b: (b, 0, 0)),
            pl.BlockSpec((None, BRANCH, T_ALL), lambda b: (b, 0, 0)),
            pl.BlockSpec((None, T_ALL, SEG_G), lambda b: (b, 0, 0)),
            pl.BlockSpec((None, SEG_G, T_ALL), lambda b: (b, 0, 0)),
            pl.BlockSpec((1, BRANCH), lambda b: (0, 0)),
        ],
        out_specs=pl.BlockSpec((None, T_ALL, BRANCH), lambda b: (b, 0, 0)),
        out_shape=jax.ShapeDtypeStruct((bsz, T_ALL, BRANCH), BF16),
        scratch_shapes=[pltpu.VMEM((2, BRANCH, T_ALL), F32), pltpu.VMEM((2, BRANCH, BRANCH), F32),
                        pltpu.VMEM((2, 1, BRANCH), F32), pltpu.VMEM((2, 1, LANES), F32)],
        compiler_params=_params("arbitrary"),
        name="mlstm",
    )(pd, vtd, pg, pgt, g_row)


def _outproj_kernel(x_ref, ctx_ref, ya_ref, yb_ref, yc_ref, yd_ref, po_ref, mod_ref, w_ref, fg_ref,
                    *out_refs, bsz, last):
    t = pl.program_id(1)
    is_ctx = jnp.logical_and(t == 0, not last)
    row = jnp.where(is_ctx, bsz, pl.program_id(0))
    gate_mod = mod_ref[pl.ds(row, 1), 2 * D_MODEL:3 * D_MODEL]
    po = po_ref[...].astype(F32)
    yd = yd_ref[...].astype(F32) * po[:, 0:BRANCH]
    mixed = jnp.concatenate([ya_ref[...].astype(F32), yb_ref[...].astype(F32), yc_ref[...].astype(F32), yd],
                            axis=-1)
    mixed = (mixed * po[:, BRANCH:]).astype(BF16)
    delta = gate_mod * _dot(mixed, w_ref[...])
    if last:
        xn = x_ref[...] + delta
        ms = jnp.mean(xn * xn, axis=-1, keepdims=True)
        out_refs[0][...] = xn * lax.rsqrt(ms + NORM_EPS) * fg_ref[...]
    else:
        x_out_ref, ctx_out_ref = out_refs

        @pl.when(t == 0)
        def _():
            ctx_out_ref[...] = ctx_ref[...] + delta

        @pl.when(t > 0)
        def _():
            x_out_ref[...] = x_ref[...] + delta


def _outproj_call(x, ctx, ya, yb, yc, yd, po, mod, w_out_bf, final_g, layer, last):
    bsz = x.shape[0]
    tile0 = 1 if last else 0
    rows = mod.shape[1]

    def tok(width, arr):
        off = tile0 if arr.shape[1] == T_ALL else 0
        return pl.BlockSpec((None, TILE, width), lambda b, t: (b, t + off, 0))

    lat_spec = pl.BlockSpec((None, TILE, D_MODEL), lambda b, t: (b, jnp.maximum(t + tile0 - 1, 0), 0))
    ctx_spec = pl.BlockSpec((None, CTX_LEN, D_MODEL), lambda b, t: (b, 0, 0))
    lat_shape = jax.ShapeDtypeStruct((bsz, SEQ, D_MODEL), F32)
    ctx_shape = jax.ShapeDtypeStruct((bsz, CTX_LEN, D_MODEL), F32)
    return pl.pallas_call(
        functools.partial(_outproj_kernel, bsz=bsz, last=last),
        grid=(bsz, N_TILES - tile0),
        in_specs=[lat_spec, ctx_spec, tok(BRANCH, ya), tok(BRANCH, yb), tok(BRANCH, yc), tok(BRANCH, yd),
                  tok(SEG_O, po),
                  pl.BlockSpec((None, rows, 3 * D_MODEL), lambda b, t: (layer, 0, 0)),
                  pl.BlockSpec((None, D_MODEL, D_MODEL), lambda b, t: (layer, 0, 0)),
                  pl.BlockSpec((1, D_MODEL), lambda b, t: (0, 0))],
        out_specs=lat_spec if last else [lat_spec, ctx_spec],
        out_shape=lat_shape if last else [lat_shape, ctx_shape],
        compiler_params=_params("arbitrary", "arbitrary"),
        name="outproj",
    )(x, ctx, ya, yb, yc, yd, po, mod, w_out_bf, final_g.reshape(1, D_MODEL))


def _rope_tables(dim):
    quarter = dim // 4
    half = dim // 2
    pos = np.arange(SEQ)
    row = (pos // GRID_W).astype(np.float32)
    col = (pos % GRID_W).astype(np.float32)
    inv = (np.float32(ROPE_BASE) ** (-np.arange(0, half, 2, dtype=np.float32) / np.float32(half))).astype(np.float32)
    ang_r = row[:, None] * inv[None, :]
    ang_c = col[:, None] * inv[None, :]
    lane = np.arange(LANES) % dim
    part = lane // quarter
    freq = lane % quarter
    ang = np.where(part[None, :] < 2, ang_r[:, freq], ang_c[:, freq]).astype(np.float32)
    cos = np.cos(ang)
    sin = np.sin(ang)
    first = (part % 2 == 0)[None, :]
    s_next = np.where(first, -sin, 0.0)
    s_prev = np.where(first, 0.0, sin)
    tab = np.stack([cos, s_next, s_prev]).astype(np.float32)
    ident = np.stack([np.ones((CTX_LEN, LANES)), np.zeros((CTX_LEN, LANES)),
                      np.zeros((CTX_LEN, LANES))]).astype(np.float32)
    return jnp.asarray(np.concatenate([ident, tab], axis=1))


def _relayout_cols(a):
    seg_a = a[..., 0:768]
    seg_b = a[..., 768:1792]
    qc = a[..., 1792:2048]
    qc = jnp.concatenate([qc[..., 0:64], qc[..., 128:192], qc[..., 64:128], qc[..., 192:256]], axis=-1)
    kvc = a[..., 2048:2304]
    seg_d = a[..., 2304:3072]
    gates = a[..., 3072:3088]
    seg_o = a[..., 3088:4368]
    pad = jnp.zeros(a.shape[:-1] + (SEG_G - 16,), a.dtype)
    return jnp.concatenate([seg_a, qc, kvc, seg_b, seg_d, gates, pad, seg_o], axis=-1)


def _segment_sources():
    ranges = [(0, 768),
              (1792, 1856), (1920, 1984), (1856, 1920), (1984, 2048),
              (2048, 2304), (768, 1792), (2304, 3072), (3072, 3088), None, (3088, 4368)]
    blocks, cur, room = [], [], LANES
    for rg in ranges:
        lo, hi = (0, SEG_G - 16) if rg is None else rg
        while lo < hi:
            n = min(room, hi - lo)
            cur.append(None if rg is None else (lo, lo + n))
            lo, room = lo + n, room - n
            if room == 0:
                blocks.append(cur)
                cur, room = [], LANES
    assert not cur and len(blocks) * LANES == PROJ_PAD
    return blocks


def _wprep_kernel(wt_ref, o_ref):
    for j, pieces in enumerate(_segment_sources()):
        rows = [jnp.zeros((SEG_G - 16, D_MODEL), F32) if p is None else wt_ref[p[0]:p[1], :] for p in pieces]
        blk = rows[0] if len(rows) == 1 else jnp.concatenate(rows, axis=0)
        o_ref[:, j * LANES:(j + 1) * LANES] = blk.T.astype(BF16)


def _wprep_call(w_in):
    depth, _, width = w_in.shape
    return pl.pallas_call(
        _wprep_kernel,
        grid=(depth,),
        in_specs=[pl.BlockSpec((None, width, D_MODEL), lambda l: (l, 0, 0), pipeline_mode=pl.Buffered(1))],
        out_specs=pl.BlockSpec((None, D_MODEL, PROJ_PAD), lambda l: (l, 0, 0)),
        out_shape=jax.ShapeDtypeStruct((depth, D_MODEL, PROJ_PAD), BF16),
        compiler_params=_params("arbitrary"),
        name="wprep",
    )(jnp.swapaxes(w_in, 1, 2))


def kernel(x, c, ctx, c_ctx, w_mod, b_mod, norm_g, w_in, b_in, diff_lam, diff_g, hg_lb, hg_g,
           sw_sink, ml_g, w_out, final_g):
    bsz = x.shape[0]
    tab_a = _rope_tables(DA_QK)
    tab_c = _rope_tables(HEAD_DIM)
    rows = ((bsz + 1 + 7) // 8) * 8
    cc = jnp.concatenate([c, c_ctx[None, :], jnp.zeros((rows - bsz - 1, D_MODEL), F32)], axis=0)
    mod = _mod_call(cc, w_mod, b_mod)
    tile4 = lambda g: jnp.tile(g, HEADS).reshape(1, BRANCH)
    w_r = _wprep_call(w_in)
    c_rows = _crow_call(mod, w_r, _relayout_cols(b_in))
    w_out_bf = w_out.astype(BF16)
    for l in range(DEPTH):
        last = l == DEPTH - 1
        pa, pc, pb, pd, pg, po, vta, vtc, vtd, pgt = _inproj_call(x, ctx, mod, norm_g, w_r, c_rows,
                                                                 tab_a, tab_c, l)
        ya = _diffattn_call(pa, vta, diff_lam[l], tile4(diff_g[l]), l, not last)
        yb = _hgrn_call(pb, hg_lb, tile4(hg_g[l]), l)
        yc = _window_call(pc, vtc, sw_sink[l].reshape(1, HEADS), not last)
        yd = _mlstm_call(pd, vtd, pg, pgt, tile4(ml_g[l]))
        res = _outproj_call(x, ctx, ya, yb, yc, yd, po, mod, w_out_bf, final_g, l, last)
        if last:
            return res
        x, ctx = res
```

```python
import functools
import math

import numpy as np
import jax
import jax.numpy as jnp
from jax import lax
from jax.experimental import pallas as pl
from jax.experimental.pallas import tpu as pltpu

F32 = jnp.float32
BF16 = jnp.bfloat16

D_MODEL = 1024
SEQ = 2048
CTX_LEN = 256
T_ALL = CTX_LEN + SEQ
GRID_W = 64
DEPTH = 2
HEADS = 4
HEAD_DIM = 64
BRANCH = HEADS * HEAD_DIM
DA_QK = 32
SW_WINDOW = 128
HG_CHUNK = 16
ROPE_BASE = 10000.0
NORM_EPS = 1e-6
NEG_BIG = -1e30
LOG2E = math.log2(math.e)

TILE = 256
N_TILES = T_ALL // TILE
LANES = 128
ONES_ROWS = 16

SEG_A = 3 * BRANCH
KV_WIDTH = 2 * HEAD_DIM
SEG_C = BRANCH + 2 * KV_WIDTH
SEG_B = 4 * BRANCH
SEG_D = 3 * BRANCH
SEG_G = LANES
SEG_O = BRANCH + D_MODEL
OFF_A = 0
OFF_C = OFF_A + SEG_A
OFF_B = OFF_C + SEG_C
OFF_D = OFF_B + SEG_B
OFF_G = OFF_D + SEG_D
OFF_O = OFF_G + SEG_G
PROJ_PAD = OFF_O + SEG_O

VMEM_LIMIT = 56 * 1024 * 1024


def _params(*sem):
    return pltpu.CompilerParams(dimension_semantics=sem, vmem_limit_bytes=VMEM_LIMIT)


def _dot(a, b):
    return jnp.dot(a, b, preferred_element_type=F32)


def _dot_nt(a, b):
    return lax.dot_general(a, b, (((1,), (1,)), ((), ())), preferred_element_type=F32)


def _dot_tn(a, b):
    return lax.dot_general(a, b, (((0,), (0,)), ((), ())), preferred_element_type=F32)


def _split3(x):
    x1 = x.astype(BF16)
    r1 = x - x1.astype(F32)
    x2 = r1.astype(BF16)
    x3 = (r1 - x2.astype(F32)).astype(BF16)
    return x1, x2, x3


def _dot_exact_l(m01, x):
    x1, x2, x3 = _split3(x)
    return _dot(m01, x1) + _dot(m01, x2) + _dot(m01, x3)


def _dot_exact_r(x, m01):
    x1, x2, x3 = _split3(x)
    return _dot(x1, m01) + _dot(x2, m01) + _dot(x3, m01)


def _dot_exact_nt(x, m01):
    x1, x2, x3 = _split3(x)
    return _dot_nt(x1, m01) + _dot_nt(x2, m01) + _dot_nt(x3, m01)


def _sigmoid(z):
    e = jnp.exp(-jnp.abs(z))
    r = 1.0 / (1.0 + e)
    return jnp.where(z >= 0, r, e * r)


def _log_sigmoid(z):
    return jnp.minimum(z, 0.0) - jnp.log(1.0 + jnp.exp(-jnp.abs(z)))


def _iota(shape, dim):
    return lax.broadcasted_iota(jnp.int32, shape, dim)


def _head_sum_matrix():
    r = _iota((BRANCH, BRANCH), 0) // HEAD_DIM
    c = _iota((BRANCH, BRANCH), 1) // HEAD_DIM
    return (r == c).astype(BF16)


def _head_rmsnorm(o, g_row):
    ss = _dot_exact_r(o * o, _head_sum_matrix())
    return o * lax.rsqrt(ss * (1.0 / HEAD_DIM) + NORM_EPS) * g_row


def _head_rmsnorm_from_t(ot, g_row):
    n = ot.shape[1]
    o3 = ot.reshape(HEADS, HEAD_DIM, n)
    ss = jnp.sum(o3 * o3, axis=1, keepdims=True)
    o3 = o3 * lax.rsqrt(ss * (1.0 / HEAD_DIM) + NORM_EPS)
    return o3.reshape(BRANCH, n).T * g_row


def _mod_kernel(cc_ref, w_ref, b_ref, o_ref):
    cc = cc_ref[...]
    a = (cc * _sigmoid(cc)).astype(BF16)
    o_ref[...] = _dot(a, w_ref[...].astype(BF16)) + b_ref[...]


def _mod_call(cc, w_mod, b_mod):
    rows = cc.shape[0]
    nblk = 3
    return pl.pallas_call(
        _mod_kernel,
        grid=(DEPTH, nblk),
        in_specs=[
            pl.BlockSpec((rows, D_MODEL), lambda l, j: (0, 0)),
            pl.BlockSpec((None, D_MODEL, D_MODEL), lambda l, j: (l, 0, j)),
            pl.BlockSpec((None, 1, D_MODEL), lambda l, j: (l, 0, j)),
        ],
        out_specs=pl.BlockSpec((None, rows, D_MODEL), lambda l, j: (l, 0, j)),
        out_shape=jax.ShapeDtypeStruct((DEPTH, rows, 3 * D_MODEL), F32),
        compiler_params=_params("arbitrary", "arbitrary"),
        name="mod",
    )(cc, w_mod, b_mod.reshape(DEPTH, 1, 3 * D_MODEL))


def _rope(slab, cos, sin_next, sin_prev, off):
    nxt = pltpu.roll(slab, LANES - off, 1)
    prv = pltpu.roll(slab, off, 1)
    return slab * cos + nxt * sin_next + prv * sin_prev


def _inproj_kernel(x_ref, ctx_ref, mod_ref, ng_ref, w_ref, b_ref, ta_ref, tc_ref,
                   pa_ref, pc_ref, pb_ref, pd_ref, pg_ref, po_ref, vta_ref, vtc_ref, vtd_ref, pgt_ref,
                   *, bsz):
    is_ctx = pl.program_id(1) == 0
    row = jnp.where(is_ctx, bsz, pl.program_id(0))
    x = jnp.where(is_ctx, ctx_ref[...], x_ref[...])
    mrow = mod_ref[pl.ds(row, 1), :]
    shift = mrow[:, 0:D_MODEL]
    scale = mrow[:, D_MODEL:2 * D_MODEL]
    ms = jnp.mean(x * x, axis=-1, keepdims=True)
    h = x * lax.rsqrt(ms + NORM_EPS) * ng_ref[...]
    h = (h * (1.0 + scale) + shift).astype(BF16)

    def proj(off, width):
        return _dot(h, w_ref[:, off:off + width]) + b_ref[:, off:off + width]

    def rope_seg(acc, tab_ref, off, q_scale, k_slabs):
        cos, s_next, s_prev = tab_ref[0], tab_ref[1], tab_ref[2]
        outs = []
        for j in range(2 + k_slabs):
            r = _rope(acc[:, j * LANES:(j + 1) * LANES], cos, s_next, s_prev, off)
            outs.append(r * q_scale if j < 2 else r)
        outs.append(acc[:, (2 + k_slabs) * LANES:])
        return jnp.concatenate(outs, axis=-1)

    acco = proj(OFF_O, SEG_O)
    po_ref[...] = jnp.concatenate(
        [_sigmoid(acco[:, 0:BRANCH]), acco[:, BRANCH:] * _sigmoid(acco[:, BRANCH:])], axis=-1).astype(BF16)
    acca = rope_seg(proj(OFF_A, SEG_A), ta_ref, DA_QK // 4, DA_QK ** -0.5 * LOG2E, 2)
    pa_ref[...] = acca[:, 0:2 * BRANCH].astype(BF16)
    vta_ref[...] = acca[:, 2 * BRANCH:].T.astype(BF16)
    accc = rope_seg(proj(OFF_C, SEG_C), tc_ref, HEAD_DIM // 4, HEAD_DIM ** -0.5 * LOG2E, 1)
    pc_ref[...] = accc[:, 0:BRANCH + KV_WIDTH].astype(BF16)
    vtc_ref[...] = accc[:, BRANCH + KV_WIDTH:].T.astype(BF16)
    accd = proj(OFF_D, SEG_D)
    pd_ref[...] = jnp.concatenate(
        [accd[:, 0:BRANCH], accd[:, BRANCH:2 * BRANCH] * (HEAD_DIM ** -0.5)], axis=-1).astype(BF16)
    vtd_ref[...] = accd[:, 2 * BRANCH:].T.astype(BF16)
    gates = proj(OFF_G, SEG_G)
    pg_ref[...] = gates
    pgt_ref[...] = gates.T
    pb_ref[...] = proj(OFF_B, SEG_B)


def _inproj_call(x, ctx, mod, norm_g, w_r, b_r, tab_a, tab_c, layer):
    bsz = x.shape[0]
    rows = mod.shape[1]
    widths = [(2 * BRANCH, BF16), (BRANCH + KV_WIDTH, BF16), (SEG_B, F32), (2 * BRANCH, BF16), (SEG_G, F32),
              (SEG_O, BF16)]
    out_specs = [pl.BlockSpec((None, TILE, w), lambda b, t: (b, t, 0)) for w, _ in widths]
    out_shape = [jax.ShapeDtypeStruct((bsz, T_ALL, w), dt) for w, dt in widths]
    for rows_t, dt in ((BRANCH, BF16), (KV_WIDTH, BF16), (BRANCH, BF16), (SEG_G, F32)):
        out_specs.append(pl.BlockSpec((None, rows_t, TILE), lambda b, t: (b, 0, t)))
        out_shape.append(jax.ShapeDtypeStruct((bsz, rows_t, T_ALL), dt))
    return pl.pallas_call(
        functools.partial(_inproj_kernel, bsz=bsz),
        grid=(bsz, N_TILES),
        in_specs=[
            pl.BlockSpec((None, TILE, D_MODEL), lambda b, t: (b, jnp.maximum(t - 1, 0), 0)),
            pl.BlockSpec((None, CTX_LEN, D_MODEL), lambda b, t: (b, 0, 0)),
            pl.BlockSpec((None, rows, 3 * D_MODEL), lambda b, t: (layer, 0, 0)),
            pl.BlockSpec((None, 1, D_MODEL), lambda b, t: (layer, 0, 0)),
            pl.BlockSpec((None, D_MODEL, PROJ_PAD), lambda b, t: (layer, 0, 0)),
            pl.BlockSpec((None, 1, PROJ_PAD), lambda b, t: (layer, 0, 0)),
            pl.BlockSpec((3, TILE, LANES), lambda b, t: (0, t, 0)),
            pl.BlockSpec((3, TILE, LANES), lambda b, t: (0, t, 0)),
        ],
        out_specs=out_specs,
        out_shape=out_shape,
        compiler_params=_params("arbitrary", "arbitrary"),
        name="inproj",
    )(x, ctx, mod, norm_g.reshape(-1, 1, D_MODEL), w_r, b_r.reshape(-1, 1, PROJ_PAD), tab_a, tab_c)


def _diffattn_kernel(q_ref, qn_ref, k_ref, vt_ref, lam_ref, g_ref, o_ref, acc_ref, s_ref, m8_ref,
                     *, lam_init, q_tile0):
    step = pl.program_id(1)
    qb = step + q_tile0
    lp = lam_ref[...]
    lam = (jnp.exp(jnp.sum(lp[0:1] * lp[1:2], axis=-1, keepdims=True))
           - jnp.exp(jnp.sum(lp[2:3] * lp[3:4], axis=-1, keepdims=True)) + lam_init)
    q = q_ref[...]
    lane = _iota((1, BRANCH), 1)
    n_pairs = 2 * HEADS
    sub = TILE // 8
    ones_rows = jnp.ones((ONES_ROWS, TILE), BF16)

    def pair_q(qv, hm):
        return jnp.where(lane // DA_QK == hm, qv, jnp.zeros_like(qv))

    def logits(qm, nk, buf):
        m8 = None
        half = max(nk // 2, TILE)
        for r0 in range(0, nk, half):
            st = _dot_nt(k_ref[r0:r0 + half, :], qm)
            s_ref[buf, r0:r0 + half, :] = st
            mh = jnp.max(st.reshape(half // 8, 8, TILE), axis=0)
            m8 = mh if m8 is None else jnp.maximum(m8, mh)
        return m8

    def value_tile(hm, j, mb, ot):
        h = hm // 2
        st = s_ref[hm % 2, j * TILE:(j + 1) * TILE, :]
        e = jnp.exp2(st.reshape(sub, 8, TILE) - mb[None])
        vt = vt_ref[h * HEAD_DIM:(h + 1) * HEAD_DIM, j * TILE:(j + 1) * TILE]
        vt1 = jnp.concatenate([vt, ones_rows], axis=0)
        return ot + _dot(vt1, e.reshape(TILE, TILE).astype(BF16))

    def attend(nk, own_first_logits):
        n_kt = nk // TILE
        m8 = logits(pair_q(q, 0), nk, 0) if own_first_logits else m8_ref[...]
        for hm in range(n_pairs):
            mb = jnp.broadcast_to(jnp.max(m8, axis=0, keepdims=True), (8, TILE))
            if hm + 1 < n_pairs:
                m8 = logits(pair_q(q, hm + 1), nk, (hm + 1) % 2)
            else:
                m8_ref[...] = logits(pair_q(qn_ref[...], 0), T_ALL, 0)
            ot = jnp.zeros((HEAD_DIM + ONES_ROWS, TILE), F32)
            for j in range(n_kt):
                ot = value_tile(hm, j, mb, ot)
            l = ot[HEAD_DIM:HEAD_DIM + 1, :]
            ot = ot[0:HEAD_DIM, :]
            rows = slice((hm // 2) * HEAD_DIM, (hm // 2 + 1) * HEAD_DIM)
            if hm % 2 == 0:
                acc_ref[rows, :] = ot * (1.0 / l)
            else:
                acc_ref[rows, :] -= ot * (lam / l)
        o_ref[...] = (_head_rmsnorm_from_t(acc_ref[...], g_ref[...]) * (1.0 - lam_init)).astype(BF16)

    if q_tile0 == 0:
        @pl.when(qb == 0)
        def _():
            attend(CTX_LEN, True)
    else:
        @pl.when(step == 0)
        def _():
            m8_ref[...] = logits(pair_q(q, 0), T_ALL, 0)

    @pl.when(qb > 0)
    def _():
        attend(T_ALL, False)


def _diffattn_call(pa, vta, lam_p, g_row, layer_idx, need_ctx):
    bsz = pa.shape[0]
    q_tile0 = 0 if need_ctx else 1
    lam_init = 0.8 - 0.6 * math.exp(-0.3 * layer_idx)
    return pl.pallas_call(
        functools.partial(_diffattn_kernel, lam_init=lam_init, q_tile0=q_tile0),
        grid=(bsz, N_TILES - q_tile0),
        in_specs=[
            pl.BlockSpec((None, TILE, BRANCH), lambda b, t: (b, t + q_tile0, 0)),
            pl.BlockSpec((None, TILE, BRANCH), lambda b, t: (b, jnp.minimum(t + q_tile0 + 1, N_TILES - 1), 0)),
            pl.BlockSpec((None, T_ALL, BRANCH), lambda b, t: (b, 0, 1)),
            pl.BlockSpec((None, BRANCH, T_ALL), lambda b, t: (b, 0, 0)),
            pl.BlockSpec((4, DA_QK), lambda b, t: (0, 0)),
            pl.BlockSpec((1, BRANCH), lambda b, t: (0, 0)),
        ],
        out_specs=pl.BlockSpec((None, TILE, BRANCH), lambda b, t: (b, t, 0)),
        out_shape=jax.ShapeDtypeStruct((bsz, (N_TILES - q_tile0) * TILE, BRANCH), BF16),
        scratch_shapes=[pltpu.VMEM((BRANCH, TILE), F32), pltpu.VMEM((2, T_ALL, TILE), F32),
                        pltpu.VMEM((8, TILE), F32)],
        compiler_params=_params("arbitrary", "arbitrary"),
        name="diffattn",
    )(pa, pa, pa, vta, lam_p, g_row)


BAND = 2 * TILE


def _window_kernel(q_ref, k_ref, vt_ref, sink_ref, o_ref, acc_ref, s_ref, *, q_tile0):
    qb = pl.program_id(1) + q_tile0
    lane = _iota((1, KV_WIDTH), 1)
    ones_rows = jnp.ones((ONES_ROWS, TILE), BF16)
    group = HEADS // (KV_WIDTH // HEAD_DIM)

    def attend(band):
        if band:
            a = (qb - 1) * TILE
            start = jnp.clip(a - SW_WINDOW, 0, SEQ - BAND)
            row0 = pl.multiple_of(CTX_LEN + start, SW_WINDOW)
            kb = k_ref[pl.ds(row0, BAND), :]
            kpos = start + _iota((BAND, 1), 0)
            qpos = a + _iota((1, TILE), 1)
            valid = jnp.abs(qpos - kpos) <= SW_WINDOW
        sinks, maxes = [], []
        for h in range(HEADS):
            kvh, g = h // group, h % group
            qg = q_ref[:, g * KV_WIDTH:(g + 1) * KV_WIDTH]
            qm = jnp.where(lane // HEAD_DIM == kvh, qg, jnp.zeros_like(qg))
            sink = sink_ref[:, h:h + 1] * LOG2E
            sc = _dot_nt(k_ref[0:CTX_LEN, :], qm)
            s_ref[h, 0:CTX_LEN, :] = sc
            m = jnp.maximum(jnp.max(sc, axis=0, keepdims=True), sink)
            if band:
                sb = jnp.where(valid, _dot_nt(kb, qm), NEG_BIG)
                s_ref[h, CTX_LEN:CTX_LEN + BAND, :] = sb
                m = jnp.maximum(m, jnp.max(sb, axis=0, keepdims=True))
            sinks.append(sink)
            maxes.append(m)
        for h in range(HEADS):
            m = maxes[h]
            kv_rows = slice((h // group) * HEAD_DIM, (h // group + 1) * HEAD_DIM)
            vt1 = jnp.concatenate([vt_ref[kv_rows, 0:CTX_LEN], ones_rows], axis=0)
            pv = _dot(vt1, jnp.exp2(s_ref[h, 0:CTX_LEN, :] - m).astype(BF16))
            if band:
                vtb = vt_ref[kv_rows, pl.ds(row0, BAND)]
                ones_b = jnp.ones((ONES_ROWS, BAND), BF16)
                eb = jnp.exp2(s_ref[h, CTX_LEN:CTX_LEN + BAND, :] - m).astype(BF16)
                pv = pv + _dot(jnp.concatenate([vtb, ones_b], axis=0), eb)
            l = pv[HEAD_DIM:HEAD_DIM + 1, :] + jnp.exp2(sinks[h] - m)
            acc_ref[h * HEAD_DIM:(h + 1) * HEAD_DIM, :] = pv[0:HEAD_DIM, :] * (1.0 / l)
        o_ref[...] = acc_ref[...].T.astype(BF16)

    @pl.when(qb == 0)
    def _():
        attend(False)

    @pl.when(qb > 0)
    def _():
        attend(True)


def _window_call(pc, vtc, sink_row, need_ctx):
    bsz = pc.shape[0]
    q_tile0 = 0 if need_ctx else 1
    n_q = N_TILES - q_tile0
    return pl.pallas_call(
        functools.partial(_window_kernel, q_tile0=q_tile0),
        grid=(bsz, n_q),
        in_specs=[
            pl.BlockSpec((None, TILE, BRANCH), lambda b, t: (b, t + q_tile0, 0)),
            pl.BlockSpec((None, T_ALL, KV_WIDTH), lambda b, t: (b, 0, BRANCH // KV_WIDTH)),
            pl.BlockSpec((None, KV_WIDTH, T_ALL), lambda b, t: (b, 0, 0)),
            pl.BlockSpec((1, HEADS), lambda b, t: (0, 0)),
        ],
        out_specs=pl.BlockSpec((None, TILE, BRANCH), lambda b, t: (b, t, 0)),
        out_shape=jax.ShapeDtypeStruct((bsz, n_q * TILE, BRANCH), BF16),
        scratch_shapes=[pltpu.VMEM((BRANCH, TILE), F32), pltpu.VMEM((HEADS, CTX_LEN + BAND, TILE), F32)],
        compiler_params=_params("arbitrary", "arbitrary"),
        name="window",
    )(pc, pc, vtc, sink_row)


CH_PER_TILE = TILE // HG_CHUNK
HG_STATE = 64
ST_PER_TILE = TILE // HG_STATE
HG_UNROLL = 9
HG_TILE_UNROLL = 3
HG_FAST_LIMIT = 180.0
HG_FAST_SHIFT = 60.0


def _hgrn_gates(z, lb_terms):
    log2_ksig = _log_sigmoid(-z) * LOG2E
    if lb_terms is None:
        return _log_sigmoid(z) * LOG2E, _sigmoid(-z), log2_ksig
    lb, log_lb, log_1m = lb_terms
    bt = log_1m + _log_sigmoid(z)
    mx = jnp.maximum(log_lb, bt)
    log_f = mx + jnp.log(jnp.exp(log_lb - mx) + jnp.exp(bt - mx))
    return log_f * LOG2E, (1.0 - lb) * _sigmoid(-z), log_1m * LOG2E + log2_ksig


def _hgrn_intra(direction, q3, c3, u3, v3, head_ones):
    half = HG_CHUNK // 2
    slabs, meta = [], []
    for s in range(HG_CHUNK):
        us = jnp.broadcast_to(u3[:, s:s + 1, :], (CH_PER_TILE, half, BRANCH))
        for g in range(2):
            lo_row, hi_row = half * g, half * g + half - 1
            if direction == 0:
                none_valid, all_valid = hi_row < s, lo_row >= s
            else:
                none_valid, all_valid = lo_row > s, hi_row <= s
            if none_valid:
                continue
            d = c3[:, half * g:half * (g + 1), :] - us
            if not all_valid:
                row = _iota((1, half, 1), 1) + half * g
                d = jnp.where((row >= s) if direction == 0 else (row <= s), d, NEG_BIG)
            x = q3[:, half * g:half * (g + 1), :] * jnp.exp2(d)
            slabs.append(x.reshape(CH_PER_TILE * half, BRANCH).astype(BF16))
            meta.append((s, g))
    a_all = _dot(jnp.concatenate(slabs, axis=0), head_ones)
    o = [jnp.zeros((CH_PER_TILE, half, BRANCH), F32) for _ in range(2)]
    n = CH_PER_TILE * half
    vs = None
    for i, (s, g) in enumerate(meta):
        if i == 0 or meta[i - 1][0] != s:
            vs = jnp.broadcast_to(v3[:, s:s + 1, :], (CH_PER_TILE, half, BRANCH))
        o[g] = o[g] + a_all[i * n:(i + 1) * n].reshape(CH_PER_TILE, half, BRANCH) * vs
    return jnp.concatenate(o, axis=1).reshape(TILE, BRANCH)


def _hgrn_safe_block(direction, q, kk, log2_f, log2_k, v, v_heads, masks):
    rr, cc, lane, row_in, tri16, ones16, head_ones = masks
    shape3 = (CH_PER_TILE, HG_CHUNK, BRANCH)
    cum = _dot_exact_l(tri16[direction], log2_f)
    tot = _dot_exact_l(ones16, log2_f)
    o = _hgrn_intra(direction, q.reshape(shape3), cum.reshape(shape3), (cum - log2_k).reshape(shape3),
                    v.reshape(shape3), head_ones)

    def shift_rows(a, n):
        n = n % TILE
        return jnp.concatenate([a[TILE - n:, :], a[:TILE - n, :]], axis=0)

    a_heads = [jnp.zeros((TILE, TILE), F32) for _ in range(HEADS)]
    g = HG_CHUNK
    while g < HG_STATE:
        qt = (q * jnp.exp2(cum)).astype(BF16)
        ke = (kk * jnp.exp2(tot - cum)).astype(BF16)
        later_r = ((rr % (2 * g)) >= g) if direction == 0 else ((rr % (2 * g)) < g)
        later_c = ((cc % (2 * g)) >= g) if direction == 0 else ((cc % (2 * g)) < g)
        pair = ((rr // (2 * g)) == (cc // (2 * g))) & later_r & jnp.logical_not(later_c)
        for h in range(HEADS):
            qh = jnp.where(lane // HEAD_DIM == h, qt, jnp.zeros_like(qt))
            a_heads[h] = jnp.where(pair, _dot_nt(qh, ke), a_heads[h])
        later_row = ((row_in % (2 * g)) >= g) if direction == 0 else ((row_in % (2 * g)) < g)
        sign = 1 if direction == 0 else -1
        tot_other = jnp.where(later_row, shift_rows(tot, sign * g), shift_rows(tot, -sign * g))
        cum = cum + jnp.where(later_row, tot_other, 0.0)
        tot = tot + tot_other
        g *= 2
    a_cat = jnp.concatenate([a.astype(BF16) for a in a_heads], axis=1)
    return o + _dot(a_cat, v_heads)


def _hgrn_kernel(p_ref, lb_ref, g_ref, o_ref, qt_ref, ke_ref, vb_ref, dec_ref, of_ref, oi_ref, st_ref,
                 *, layer_idx):
    rr = _iota((TILE, TILE), 0)
    cc = _iota((TILE, TILE), 1)
    lane = _iota((1, BRANCH), 1)
    row_in = _iota((TILE, 1), 0)
    same16 = (rr // HG_CHUNK) == (cc // HG_CHUNK)
    same64 = (rr // HG_STATE) == (cc // HG_STATE)
    within = [same64 & (cc <= rr), same64 & (cc >= rr)]
    tri64 = [w.astype(BF16) for w in within]
    ones64 = same64.astype(BF16)
    head_mask = (rr // HEAD_DIM) == (cc // HEAD_DIM)
    safe_masks = (rr, cc, lane, row_in, [(same16 & (cc <= rr)).astype(BF16), (same16 & (cc >= rr)).astype(BF16)],
                  same16.astype(BF16), _head_sum_matrix())

    lb_terms = None
    if layer_idx > 0:
        lbp = lb_ref[...]
        lbp = lbp - jnp.max(lbp, axis=0, keepdims=True)
        sm = jnp.exp(lbp)
        sm = sm / jnp.sum(sm, axis=0, keepdims=True)
        lb = jnp.sum(sm[1:layer_idx + 1], axis=0, keepdims=True)
        lb_terms = (lb, jnp.log(lb), jnp.log(1.0 - lb))

    def tile_body(t, carry):
        rows = pl.ds(pl.multiple_of(t * TILE, TILE), TILE)
        q = p_ref[rows, 0:BRANCH] * (HEAD_DIM ** -0.5)
        v = p_ref[rows, 3 * BRANCH:4 * BRANCH]
        vbf = v.astype(BF16)
        vb_ref[rows, :] = vbf
        v_heads = jnp.concatenate([jnp.where(lane // HEAD_DIM == h, vbf, jnp.zeros_like(vbf))
                                   for h in range(HEADS)], axis=0)
        o_intra = None
        fallback = []
        for direction in (0, 1):
            z = p_ref[rows, (1 + direction) * BRANCH:(2 + direction) * BRANCH]
            log2_f, kk, log2_k = _hgrn_gates(z, lb_terms)
            cum = _dot_exact_l(tri64[direction], log2_f)
            tot = _dot_exact_l(ones64, log2_f)
            low = jnp.min(cum, axis=(0, 1), keepdims=True)
            q_fast = (q * jnp.exp2(cum + HG_FAST_SHIFT)).astype(BF16)
            k_fast = (kk * jnp.exp2(jnp.minimum(-cum, HG_FAST_LIMIT) - HG_FAST_SHIFT)).astype(BF16)
            keep = within[direction] & (low >= -HG_FAST_LIMIT)
            a_heads = [jnp.where(keep, _dot_nt(jnp.where(lane // HEAD_DIM == h, q_fast, jnp.zeros_like(q_fast)),
                                               k_fast), 0.0) for h in range(HEADS)]
            a_cat = jnp.concatenate([a.astype(BF16) for a in a_heads], axis=1)
            o_d = _dot(a_cat, v_heads)
            qt_ref[direction, rows, :] = (q * jnp.exp2(cum)).astype(BF16)
            ke_ref[direction, rows, :] = (kk * jnp.exp2(tot - cum)).astype(BF16)
            dec_ref[direction, t] = jnp.exp2(tot.reshape(ST_PER_TILE, HG_STATE, BRANCH)[:, 0, :])
            o_intra = o_d if o_intra is None else o_intra + o_d
            fallback.append((low, kk, log2_f, log2_k))
        of_ref[rows, :] = o_intra
        return rows, q, v, v_heads, fallback

    def tile_group(it, carry):
        done = [tile_body(it * HG_TILE_UNROLL + u, carry) for u in range(HG_TILE_UNROLL)]
        for rows, q, v, v_heads, fallback in done:
            for direction, (low, kk, log2_f, log2_k) in enumerate(fallback):
                @pl.when(low[0, 0] < -HG_FAST_LIMIT)
                def _():
                    of_ref[rows, :] += _hgrn_safe_block(direction, q, kk, log2_f, log2_k, v, v_heads,
                                                        safe_masks)
        return carry

    lax.fori_loop(0, N_TILES // HG_TILE_UNROLL, tile_group, 0)

    st_ref[...] = jnp.zeros_like(st_ref)
    n_ctx, n_all = CTX_LEN // HG_STATE, T_ALL // HG_STATE

    def state_step(i, direction):
        if direction == 0:
            c = i
        else:
            c = jnp.where(i < n_ctx, n_ctx - 1 - i, n_all - 1 + n_ctx - i)
        rows = pl.ds(pl.multiple_of(c * HG_STATE, HG_STATE), HG_STATE)
        st = st_ref[direction]
        oi_ref[direction, rows, :] = _dot_nt(qt_ref[direction, rows, :], st.astype(BF16))
        ds = _dot_tn(vb_ref[rows, :], ke_ref[direction, rows, :])
        dec = dec_ref[direction, c // ST_PER_TILE, pl.ds(c % ST_PER_TILE, 1), :]
        st_ref[direction] = st * dec + jnp.where(head_mask, ds, 0.0)

    def state_body(it, carry):
        for u in range(HG_UNROLL):
            for direction in (0, 1):
                state_step(it * HG_UNROLL + u, direction)
        return carry

    lax.fori_loop(0, n_all // HG_UNROLL, state_body, 0)

    o_ref[...] = _head_rmsnorm(of_ref[...] + oi_ref[0] + oi_ref[1], g_ref[...]).astype(BF16)


def _hgrn_call(pb, hg_lb, g_row, layer_idx):
    bsz = pb.shape[0]
    return pl.pallas_call(
        functools.partial(_hgrn_kernel, layer_idx=layer_idx),
        grid=(bsz,),
        in_specs=[
            pl.BlockSpec((None, T_ALL, SEG_B), lambda b: (b, 0, 0)),
            pl.BlockSpec((DEPTH, BRANCH), lambda b: (0, 0)),
            pl.BlockSpec((1, BRANCH), lambda b: (0, 0)),
        ],
        out_specs=pl.BlockSpec((None, T_ALL, BRANCH), lambda b: (b, 0, 0)),
        out_shape=jax.ShapeDtypeStruct((bsz, T_ALL, BRANCH), BF16),
        scratch_shapes=[pltpu.VMEM((2, T_ALL, BRANCH), BF16), pltpu.VMEM((2, T_ALL, BRANCH), BF16),
                        pltpu.VMEM((T_ALL, BRANCH), BF16), pltpu.VMEM((2, N_TILES, ST_PER_TILE, BRANCH), F32),
                        pltpu.VMEM((T_ALL, BRANCH), F32), pltpu.VMEM((2, T_ALL, BRANCH), F32),
                        pltpu.VMEM((2, BRANCH, BRANCH), F32)],
        compiler_params=_params("arbitrary"),
        name="hgrn",
    )(pb, hg_lb, g_row)


ML_CHUNK = 256
ML_UNROLL = 3


def _mlstm_logits(direction, c, p_ref, vt_ref, g_ref, gt_ref, ct_ref, n_ref, m_ref, consts):
    tri, valid, _, lane, row16, lane16 = consts
    rows = pl.ds(pl.multiple_of(c * ML_CHUNK, ML_CHUNK), ML_CHUNK)
    q = p_ref[rows, 0:BRANCH]
    k = p_ref[rows, BRANCH:2 * BRANCH]
    g = g_ref[rows, :]
    gt = gt_ref[:, rows]
    cum = _dot_exact_l(tri, _log_sigmoid(g) * LOG2E)
    cum_t = _dot_exact_nt(_log_sigmoid(gt[8:16, :]) * LOG2E, tri)
    ig_t = gt[0:8, :] * LOG2E
    ct = ct_ref[direction]
    n0 = n_ref[direction]
    n_hi = n0.astype(BF16).astype(F32)
    n_lo = n0 - n_hi
    n_rows = (jnp.where((row16 < HEADS) & (lane16 == row16), n_hi, 0.0)
              + jnp.where((row16 >= HEADS) & (lane16 == row16 - HEADS), n_lo, 0.0)).astype(BF16)
    inter_all = _dot_nt(jnp.concatenate([ct.astype(BF16), n_rows], axis=0), q)
    heads = []
    for h in range(HEADS):
        r = HEADS * direction + h
        cumr = cum_t[r:r + 1, :]
        ucol = g[:, r:r + 1] * LOG2E - cum[:, 2 * HEADS + r:2 * HEADS + r + 1]
        toth = cumr[:, ML_CHUNK - 1:ML_CHUNK] if direction == 0 else cumr[:, 0:1]
        m0h = m_ref[direction, :, h:h + 1]
        logd = jnp.where(valid, cumr + ucol, NEG_BIG)
        inter = cumr + m0h
        m_t = jnp.maximum(jnp.max(logd, axis=0, keepdims=True), inter)
        qm = jnp.where(lane // HEAD_DIM == h, q, jnp.zeros_like(q))
        heads.append(dict(qk=_dot_nt(k, qm), logd=logd, m_t=m_t, g0=jnp.exp2(inter - m_t),
                          a_row=toth - cumr + ig_t[r:r + 1, :], carry=toth + m0h))
    return dict(direction=direction, rows=rows, k=k, ct=ct, n0=n0, inter_all=inter_all, heads=heads)


def _mlstm_outputs(cx, vt_ref, ht_ref, ct_ref, n_ref, m_ref, consts):
    _, _, head_mask, lane, _, _ = consts
    direction, rows, k, inter_all = cx["direction"], cx["rows"], cx["k"], cx["inter_all"]
    vt = vt_ref[:, rows]
    ones_rows = jnp.ones((ONES_ROWS, ML_CHUNK), BF16)
    row16 = _iota((ONES_ROWS, ML_CHUNK), 0)
    w_rows, sp_row = [], jnp.zeros((1, BRANCH), F32)
    for h, hd in enumerate(cx["heads"]):
        m_t, g0 = hd["m_t"], hd["g0"]
        s_t = hd["qk"] * jnp.exp2(hd["logd"] - m_t)
        vt1 = jnp.concatenate([vt[h * HEAD_DIM:(h + 1) * HEAD_DIM, :], ones_rows], axis=0)
        pv = _dot(vt1, s_t.astype(BF16))
        num = pv[0:HEAD_DIM, :] + g0 * inter_all[h * HEAD_DIM:(h + 1) * HEAD_DIM, :]
        den = pv[HEAD_DIM:HEAD_DIM + 1, :] + g0 * (inter_all[BRANCH + h:BRANCH + h + 1, :]
                                                   + inter_all[BRANCH + HEADS + h:BRANCH + HEADS + h + 1, :])
        ht_ref[direction, h * HEAD_DIM:(h + 1) * HEAD_DIM, rows] = (
            num / jnp.maximum(jnp.abs(den), jnp.exp2(-m_t)))
        a_row = hd["a_row"]
        m_loc = jnp.max(a_row, axis=1, keepdims=True)
        m_new = jnp.maximum(hd["carry"], m_loc)
        sp = jnp.exp2(hd["carry"] - m_new)
        w_rows.append(jnp.exp2(a_row - m_loc) * jnp.exp2(m_loc - m_new))
        sp_row = sp_row + jnp.where(lane // HEAD_DIM == h, sp, 0.0)
        m_ref[direction, :, h:h + 1] = m_new

    w_block = jnp.concatenate([jnp.broadcast_to(w, (HEAD_DIM, ML_CHUNK)) for w in w_rows], axis=0)
    vtw = (vt.astype(F32) * w_block).astype(BF16)
    w16 = jnp.zeros((ONES_ROWS, ML_CHUNK), F32)
    for h in range(HEADS):
        w_hi = w_rows[h].astype(BF16).astype(F32)
        w16 = w16 + jnp.where(row16 == h, w_hi, 0.0) + jnp.where(row16 == HEADS + h, w_rows[h] - w_hi, 0.0)
    dall = _dot(jnp.concatenate([vtw, w16.astype(BF16)], axis=0), k)
    ct_ref[direction] = cx["ct"] * sp_row + jnp.where(head_mask, dall[0:BRANCH, :], 0.0)
    dn = jnp.zeros((1, BRANCH), F32)
    for h in range(HEADS):
        dn = dn + jnp.where(lane // HEAD_DIM == h,
                            dall[BRANCH + h:BRANCH + h + 1, :] + dall[BRANCH + HEADS + h:BRANCH + HEADS + h + 1, :], 0.0)
    n_ref[direction] = cx["n0"] * sp_row + dn


def _mlstm_kernel(p_ref, vt_ref, g_ref, gt_ref, gain_ref, o_ref, ht_ref, ct_ref, n_ref, m_ref):
    head_mask = (_iota((BRANCH, BRANCH), 0) // HEAD_DIM) == (_iota((BRANCH, BRANCH), 1) // HEAD_DIM)
    rr = _iota((ML_CHUNK, ML_CHUNK), 0)
    cc = _iota((ML_CHUNK, ML_CHUNK), 1)
    lane = _iota((1, BRANCH), 1)
    row16 = _iota((ONES_ROWS, BRANCH), 0)
    lane16 = _iota((ONES_ROWS, BRANCH), 1) // HEAD_DIM
    consts = []
    for direction in (0, 1):
        tri = ((cc <= rr) if direction == 0 else (cc >= rr)).astype(BF16)
        valid = (rr <= cc) if direction == 0 else (rr >= cc)
        consts.append((tri, valid, head_mask, lane, row16, lane16))
    ct_ref[...] = jnp.zeros_like(ct_ref)
    n_ref[...] = jnp.zeros_like(n_ref)
    m_ref[...] = jnp.zeros_like(m_ref)
    n_ctx, n_all = CTX_LEN // ML_CHUNK, T_ALL // ML_CHUNK

    def body(it, carry):
        for u in range(ML_UNROLL):
            i = it * ML_UNROLL + u
            chunk = (i, jnp.where(i < n_ctx, n_ctx - 1 - i, n_all - 1 + n_ctx - i))
            cxs = [_mlstm_logits(d, chunk[d], p_ref, vt_ref, g_ref, gt_ref, ct_ref, n_ref, m_ref, consts[d])
                   for d in (0, 1)]
            for d in (0, 1):
                _mlstm_outputs(cxs[d], vt_ref, ht_ref, ct_ref, n_ref, m_ref, consts[d])
        return carry

    lax.fori_loop(0, n_all // ML_UNROLL, body, 0)

    def out_body(t, carry):
        rows = pl.ds(pl.multiple_of(t * TILE, TILE), TILE)
        o_ref[rows, :] = _head_rmsnorm_from_t(ht_ref[0, :, rows] + ht_ref[1, :, rows],
                                              gain_ref[...]).astype(BF16)
        return carry

    lax.fori_loop(0, N_TILES, out_body, 0)


def _mlstm_call(pd, vtd, pg, pgt, g_row):
    bsz = pd.shape[0]
    return pl.pallas_call(
        _mlstm_kernel,
        grid=(bsz,),
        in_specs=[
            pl.BlockSpec((None, T_ALL, 2 * BRANCH), lambda b: (b, 0, 0)),
            pl.BlockSpec((None, BRANCH, T_ALL), lambda b: (b, 0, 0)),
            pl.BlockSpec((None, T_ALL, SEG_G), lambda b: (b, 0, 0)),
            pl.BlockSpec((None, SEG_G, T_ALL), lambda b: (b, 0, 0)),
            pl.BlockSpec((1, BRANCH), lambda b: (0, 0)),
        ],
        out_specs=pl.BlockSpec((None, T_ALL, BRANCH), lambda b: (b, 0, 0)),
        out_shape=jax.ShapeDtypeStruct((bsz, T_ALL, BRANCH), BF16),
        scratch_shapes=[pltpu.VMEM((2, BRANCH, T_ALL), F32), pltpu.VMEM((2, BRANCH, BRANCH), F32),
                        pltpu.VMEM((2, 1, BRANCH), F32), pltpu.VMEM((2, 1, LANES), F32)],
        compiler_params=_params("arbitrary"),
        name="mlstm",
    )(pd, vtd, pg, pgt, g_row)


def _outproj_kernel(x_ref, ctx_ref, ya_ref, yb_ref, yc_ref, yd_ref, po_ref, mod_ref, w_ref, fg_ref,
                    *out_refs, bsz, last):
    t = pl.program_id(1)
    is_ctx = jnp.logical_and(t == 0, not last)
    row = jnp.where(is_ctx, bsz, pl.program_id(0))
    gate_mod = mod_ref[pl.ds(row, 1), 2 * D_MODEL:3 * D_MODEL]
    po = po_ref[...].astype(F32)
    yd = yd_ref[...].astype(F32) * po[:, 0:BRANCH]
    mixed = jnp.concatenate([ya_ref[...].astype(F32), yb_ref[...].astype(F32), yc_ref[...].astype(F32), yd],
                            axis=-1)
    mixed = (mixed * po[:, BRANCH:]).astype(BF16)
    delta = gate_mod * _dot(mixed, w_ref[...])
    if last:
        xn = x_ref[...] + delta
        ms = jnp.mean(xn * xn, axis=-1, keepdims=True)
        out_refs[0][...] = xn * lax.rsqrt(ms + NORM_EPS) * fg_ref[...]
    else:
        x_out_ref, ctx_out_ref = out_refs

        @pl.when(t == 0)
        def _():
            ctx_out_ref[...] = ctx_ref[...] + delta

        @pl.when(t > 0)
        def _():
            x_out_ref[...] = x_ref[...] + delta


def _outproj_call(x, ctx, ya, yb, yc, yd, po, mod, w_out_bf, final_g, layer, last):
    bsz = x.shape[0]
    tile0 = 1 if last else 0
    rows = mod.shape[1]

    def tok(width, arr):
        off = tile0 if arr.shape[1] == T_ALL else 0
        return pl.BlockSpec((None, TILE, width), lambda b, t: (b, t + off, 0))

    lat_spec = pl.BlockSpec((None, TILE, D_MODEL), lambda b, t: (b, jnp.maximum(t + tile0 - 1, 0), 0))
    ctx_spec = pl.BlockSpec((None, CTX_LEN, D_MODEL), lambda b, t: (b, 0, 0))
    lat_shape = jax.ShapeDtypeStruct((bsz, SEQ, D_MODEL), F32)
    ctx_shape = jax.ShapeDtypeStruct((bsz, CTX_LEN, D_MODEL), F32)
    return pl.pallas_call(
        functools.partial(_outproj_kernel, bsz=bsz, last=last),
        grid=(bsz, N_TILES - tile0),
        in_specs=[lat_spec, ctx_spec, tok(BRANCH, ya), tok(BRANCH, yb), tok(BRANCH, yc), tok(BRANCH, yd),
                  tok(SEG_O, po),
                  pl.BlockSpec((None, rows, 3 * D_MODEL), lambda b, t: (layer, 0, 0)),
                  pl.BlockSpec((None, D_MODEL, D_MODEL), lambda b, t: (layer, 0, 0)),
                  pl.BlockSpec((1, D_MODEL), lambda b, t: (0, 0))],
        out_specs=lat_spec if last else [lat_spec, ctx_spec],
        out_shape=lat_shape if last else [lat_shape, ctx_shape],
        compiler_params=_params("arbitrary", "arbitrary"),
        name="outproj",
    )(x, ctx, ya, yb, yc, yd, po, mod, w_out_bf, final_g.reshape(1, D_MODEL))


def _rope_tables(dim):
    quarter = dim // 4
    half = dim // 2
    pos = np.arange(SEQ)
    row = (pos // GRID_W).astype(np.float32)
    col = (pos % GRID_W).astype(np.float32)
    inv = (np.float32(ROPE_BASE) ** (-np.arange(0, half, 2, dtype=np.float32) / np.float32(half))).astype(np.float32)
    ang_r = row[:, None] * inv[None, :]
    ang_c = col[:, None] * inv[None, :]
    lane = np.arange(LANES) % dim
    part = lane // quarter
    freq = lane % quarter
    ang = np.where(part[None, :] < 2, ang_r[:, freq], ang_c[:, freq]).astype(np.float32)
    cos = np.cos(ang)
    sin = np.sin(ang)
    first = (part % 2 == 0)[None, :]
    s_next = np.where(first, -sin, 0.0)
    s_prev = np.where(first, 0.0, sin)
    tab = np.stack([cos, s_next, s_prev]).astype(np.float32)
    ident = np.stack([np.ones((CTX_LEN, LANES)), np.zeros((CTX_LEN, LANES)),
                      np.zeros((CTX_LEN, LANES))]).astype(np.float32)
    return jnp.asarray(np.concatenate([ident, tab], axis=1))


def _relayout_cols(a):
    seg_a = a[..., 0:768]
    seg_b = a[..., 768:1792]
    qc = a[..., 1792:2048]
    qc = jnp.concatenate([qc[..., 0:64], qc[..., 128:192], qc[..., 64:128], qc[..., 192:256]], axis=-1)
    kvc = a[..., 2048:2304]
    seg_d = a[..., 2304:3072]
    gates = a[..., 3072:3088]
    seg_o = a[..., 3088:4368]
    pad = jnp.zeros(a.shape[:-1] + (SEG_G - 16,), a.dtype)
    return jnp.concatenate([seg_a, qc, kvc, seg_b, seg_d, gates, pad, seg_o], axis=-1)


def _segment_sources():
    ranges = [(0, 768),
              (1792, 1856), (1920, 1984), (1856, 1920), (1984, 2048),
              (2048, 2304), (768, 1792), (2304, 3072), (3072, 3088), None, (3088, 4368)]
    blocks, cur, room = [], [], LANES
    for rg in ranges:
        lo, hi = (0, SEG_G - 16) if rg is None else rg
        while lo < hi:
            n = min(room, hi - lo)
            cur.append(None if rg is None else (lo, lo + n))
            lo, room = lo + n, room - n
            if room == 0:
                blocks.append(cur)
                cur, room = [], LANES
    assert not cur and len(blocks) * LANES == PROJ_PAD
    return blocks


def _wprep_kernel(wt_ref, o_ref):
    for j, pieces in enumerate(_segment_sources()):
        rows = [jnp.zeros((SEG_G - 16, D_MODEL), F32) if p is None else wt_ref[p[0]:p[1], :] for p in pieces]
        blk = rows[0] if len(rows) == 1 else jnp.concatenate(rows, axis=0)
        o_ref[:, j * LANES:(j + 1) * LANES] = blk.T.astype(BF16)


def _wprep_call(w_in):
    depth, _, width = w_in.shape
    return pl.pallas_call(
        _wprep_kernel,
        grid=(depth,),
        in_specs=[pl.BlockSpec((None, width, D_MODEL), lambda l: (l, 0, 0), pipeline_mode=pl.Buffered(1))],
        out_specs=pl.BlockSpec((None, D_MODEL, PROJ_PAD), lambda l: (l, 0, 0)),
        out_shape=jax.ShapeDtypeStruct((depth, D_MODEL, PROJ_PAD), BF16),
        compiler_params=_params("arbitrary"),
        name="wprep",
    )(jnp.swapaxes(w_in, 1, 2))


def kernel(x, c, ctx, c_ctx, w_mod, b_mod, norm_g, w_in, b_in, diff_lam, diff_g, hg_lb, hg_g,
           sw_sink, ml_g, w_out, final_g):
    bsz = x.shape[0]
    tab_a = _rope_tables(DA_QK)
    tab_c = _rope_tables(HEAD_DIM)
    rows = ((bsz + 1 + 7) // 8) * 8
    cc = jnp.concatenate([c, c_ctx[None, :], jnp.zeros((rows - bsz - 1, D_MODEL), F32)], axis=0)
    mod = _mod_call(cc, w_mod, b_mod)
    tile4 = lambda g: jnp.tile(g, HEADS).reshape(1, BRANCH)
    w_r = _wprep_call(w_in)
    b_r = _relayout_cols(b_in)
    w_out_bf = w_out.astype(BF16)
    for l in range(DEPTH):
        last = l == DEPTH - 1
        pa, pc, pb, pd, pg, po, vta, vtc, vtd, pgt = _inproj_call(x, ctx, mod, norm_g, w_r, b_r,
                                                                 tab_a, tab_c, l)
        ya = _diffattn_call(pa, vta, diff_lam[l], tile4(diff_g[l]), l, not last)
        yb = _hgrn_call(pb, hg_lb, tile4(hg_g[l]), l)
        yc = _window_call(pc, vtc, sw_sink[l].reshape(1, HEADS), not last)
        yd = _mlstm_call(pd, vtd, pg, pgt, tile4(ml_g[l]))
        res = _outproj_call(x, ctx, ya, yb, yc, yd, po, mod, w_out_bf, final_g, l, last)
        if last:
            return res
        x, ctx = res
```

```python
import functools
import math

import numpy as np
import jax
import jax.numpy as jnp
from jax import lax
from jax.experimental import pallas as pl
from jax.experimental.pallas import tpu as pltpu

F32 = jnp.float32
BF16 = jnp.bfloat16

D_MODEL = 1024
SEQ = 2048
CTX_LEN = 256
T_ALL = CTX_LEN + SEQ
GRID_W = 64
DEPTH = 2
HEADS = 4
HEAD_DIM = 64
BRANCH = HEADS * HEAD_DIM
DA_QK = 32
SW_WINDOW = 128
HG_CHUNK = 16
ROPE_BASE = 10000.0
NORM_EPS = 1e-6
NEG_BIG = -1e30
LOG2E = math.log2(math.e)

TILE = 256
N_TILES = T_ALL // TILE
LANES = 128
ONES_ROWS = 16

SEG_A = 3 * BRANCH
KV_WIDTH = 2 * HEAD_DIM
SEG_C = BRANCH + 2 * KV_WIDTH
SEG_B = 4 * BRANCH
SEG_D = 3 * BRANCH
SEG_G = LANES
SEG_O = BRANCH + D_MODEL
OFF_A = 0
OFF_C = OFF_A + SEG_A
OFF_B = OFF_C + SEG_C
OFF_D = OFF_B + SEG_B
OFF_G = OFF_D + SEG_D
OFF_O = OFF_G + SEG_G
PROJ_PAD = OFF_O + SEG_O

VMEM_LIMIT = 56 * 1024 * 1024


def _params(*sem):
    return pltpu.CompilerParams(dimension_semantics=sem, vmem_limit_bytes=VMEM_LIMIT)


def _dot(a, b):
    return jnp.dot(a, b, preferred_element_type=F32)


def _dot_nt(a, b):
    return lax.dot_general(a, b, (((1,), (1,)), ((), ())), preferred_element_type=F32)


def _dot_tn(a, b):
    return lax.dot_general(a, b, (((0,), (0,)), ((), ())), preferred_element_type=F32)


def _split3(x):
    x1 = x.astype(BF16)
    r1 = x - x1.astype(F32)
    x2 = r1.astype(BF16)
    x3 = (r1 - x2.astype(F32)).astype(BF16)
    return x1, x2, x3


def _dot_exact_l(m01, x):
    x1, x2, x3 = _split3(x)
    return _dot(m01, x1) + _dot(m01, x2) + _dot(m01, x3)


def _dot_exact_r(x, m01):
    x1, x2, x3 = _split3(x)
    return _dot(x1, m01) + _dot(x2, m01) + _dot(x3, m01)


def _dot_exact_nt(x, m01):
    x1, x2, x3 = _split3(x)
    return _dot_nt(x1, m01) + _dot_nt(x2, m01) + _dot_nt(x3, m01)


def _sigmoid(z):
    e = jnp.exp(-jnp.abs(z))
    r = 1.0 / (1.0 + e)
    return jnp.where(z >= 0, r, e * r)


def _log_sigmoid(z):
    return jnp.minimum(z, 0.0) - jnp.log(1.0 + jnp.exp(-jnp.abs(z)))


def _iota(shape, dim):
    return lax.broadcasted_iota(jnp.int32, shape, dim)


def _head_sum_matrix():
    r = _iota((BRANCH, BRANCH), 0) // HEAD_DIM
    c = _iota((BRANCH, BRANCH), 1) // HEAD_DIM
    return (r == c).astype(BF16)


def _head_rmsnorm(o, g_row):
    ss = _dot_exact_r(o * o, _head_sum_matrix())
    return o * lax.rsqrt(ss * (1.0 / HEAD_DIM) + NORM_EPS) * g_row


def _head_rmsnorm_from_t(ot, g_row):
    n = ot.shape[1]
    o3 = ot.reshape(HEADS, HEAD_DIM, n)
    ss = jnp.sum(o3 * o3, axis=1, keepdims=True)
    o3 = o3 * lax.rsqrt(ss * (1.0 / HEAD_DIM) + NORM_EPS)
    return o3.reshape(BRANCH, n).T * g_row


def _mod_kernel(cc_ref, w_ref, b_ref, o_ref):
    cc = cc_ref[...]
    a = (cc * _sigmoid(cc)).astype(BF16)
    o_ref[...] = _dot(a, w_ref[...].astype(BF16)) + b_ref[...]


def _mod_call(cc, w_mod, b_mod):
    rows = cc.shape[0]
    nblk = 3
    return pl.pallas_call(
        _mod_kernel,
        grid=(DEPTH, nblk),
        in_specs=[
            pl.BlockSpec((rows, D_MODEL), lambda l, j: (0, 0)),
            pl.BlockSpec((None, D_MODEL, D_MODEL), lambda l, j: (l, 0, j)),
            pl.BlockSpec((None, 1, D_MODEL), lambda l, j: (l, 0, j)),
        ],
        out_specs=pl.BlockSpec((None, rows, D_MODEL), lambda l, j: (l, 0, j)),
        out_shape=jax.ShapeDtypeStruct((DEPTH, rows, 3 * D_MODEL), F32),
        compiler_params=_params("arbitrary", "arbitrary"),
        name="mod",
    )(cc, w_mod, b_mod.reshape(DEPTH, 1, 3 * D_MODEL))


def _rope(slab, cos, sin_next, sin_prev, off):
    nxt = pltpu.roll(slab, LANES - off, 1)
    prv = pltpu.roll(slab, off, 1)
    return slab * cos + nxt * sin_next + prv * sin_prev


def _inproj_kernel(x_ref, ctx_ref, mod_ref, ng_ref, w_ref, b_ref, ta_ref, tc_ref,
                   pa_ref, pc_ref, pb_ref, pd_ref, pg_ref, po_ref, vta_ref, vtc_ref, vtd_ref, pgt_ref,
                   *, bsz):
    is_ctx = pl.program_id(1) == 0
    row = jnp.where(is_ctx, bsz, pl.program_id(0))
    x = jnp.where(is_ctx, ctx_ref[...], x_ref[...])
    mrow = mod_ref[pl.ds(row, 1), :]
    shift = mrow[:, 0:D_MODEL]
    scale = mrow[:, D_MODEL:2 * D_MODEL]
    ms = jnp.mean(x * x, axis=-1, keepdims=True)
    h = x * lax.rsqrt(ms + NORM_EPS) * ng_ref[...]
    h = (h * (1.0 + scale) + shift).astype(BF16)

    def proj(off, width):
        return _dot(h, w_ref[:, off:off + width]) + b_ref[:, off:off + width]

    def rope_seg(acc, tab_ref, off, q_scale, k_slabs):
        cos, s_next, s_prev = tab_ref[0], tab_ref[1], tab_ref[2]
        outs = []
        for j in range(2 + k_slabs):
            r = _rope(acc[:, j * LANES:(j + 1) * LANES], cos, s_next, s_prev, off)
            outs.append(r * q_scale if j < 2 else r)
        outs.append(acc[:, (2 + k_slabs) * LANES:])
        return jnp.concatenate(outs, axis=-1)

    acco = proj(OFF_O, SEG_O)
    po_ref[...] = jnp.concatenate(
        [_sigmoid(acco[:, 0:BRANCH]), acco[:, BRANCH:] * _sigmoid(acco[:, BRANCH:])], axis=-1).astype(BF16)
    acca = rope_seg(proj(OFF_A, SEG_A), ta_ref, DA_QK // 4, DA_QK ** -0.5 * LOG2E, 2)
    pa_ref[...] = acca[:, 0:2 * BRANCH].astype(BF16)
    vta_ref[...] = acca[:, 2 * BRANCH:].T.astype(BF16)
    accc = rope_seg(proj(OFF_C, SEG_C), tc_ref, HEAD_DIM // 4, HEAD_DIM ** -0.5 * LOG2E, 1)
    pc_ref[...] = accc[:, 0:BRANCH + KV_WIDTH].astype(BF16)
    vtc_ref[...] = accc[:, BRANCH + KV_WIDTH:].T.astype(BF16)
    accd = proj(OFF_D, SEG_D)
    pd_ref[...] = jnp.concatenate(
        [accd[:, 0:BRANCH], accd[:, BRANCH:2 * BRANCH] * (HEAD_DIM ** -0.5)], axis=-1).astype(BF16)
    vtd_ref[...] = accd[:, 2 * BRANCH:].T.astype(BF16)
    gates = proj(OFF_G, SEG_G)
    pg_ref[...] = gates
    pgt_ref[...] = gates.T
    pb_ref[...] = proj(OFF_B, SEG_B)


def _inproj_call(x, ctx, mod, norm_g, w_r, b_r, tab_a, tab_c, layer):
    bsz = x.shape[0]
    rows = mod.shape[1]
    widths = [(2 * BRANCH, BF16), (BRANCH + KV_WIDTH, BF16), (SEG_B, F32), (2 * BRANCH, BF16), (SEG_G, F32),
              (SEG_O, BF16)]
    out_specs = [pl.BlockSpec((None, TILE, w), lambda b, t: (b, t, 0)) for w, _ in widths]
    out_shape = [jax.ShapeDtypeStruct((bsz, T_ALL, w), dt) for w, dt in widths]
    for rows_t, dt in ((BRANCH, BF16), (KV_WIDTH, BF16), (BRANCH, BF16), (SEG_G, F32)):
        out_specs.append(pl.BlockSpec((None, rows_t, TILE), lambda b, t: (b, 0, t)))
        out_shape.append(jax.ShapeDtypeStruct((bsz, rows_t, T_ALL), dt))
    return pl.pallas_call(
        functools.partial(_inproj_kernel, bsz=bsz),
        grid=(bsz, N_TILES),
        in_specs=[
            pl.BlockSpec((None, TILE, D_MODEL), lambda b, t: (b, jnp.maximum(t - 1, 0), 0)),
            pl.BlockSpec((None, CTX_LEN, D_MODEL), lambda b, t: (b, 0, 0)),
            pl.BlockSpec((None, rows, 3 * D_MODEL), lambda b, t: (layer, 0, 0)),
            pl.BlockSpec((None, 1, D_MODEL), lambda b, t: (layer, 0, 0)),
            pl.BlockSpec((None, D_MODEL, PROJ_PAD), lambda b, t: (layer, 0, 0)),
            pl.BlockSpec((None, 1, PROJ_PAD), lambda b, t: (layer, 0, 0)),
            pl.BlockSpec((3, TILE, LANES), lambda b, t: (0, t, 0)),
            pl.BlockSpec((3, TILE, LANES), lambda b, t: (0, t, 0)),
        ],
        out_specs=out_specs,
        out_shape=out_shape,
        compiler_params=_params("arbitrary", "arbitrary"),
        name="inproj",
    )(x, ctx, mod, norm_g.reshape(-1, 1, D_MODEL), w_r, b_r.reshape(-1, 1, PROJ_PAD), tab_a, tab_c)


def _diffattn_kernel(q_ref, qn_ref, k_ref, vt_ref, lam_ref, g_ref, o_ref, acc_ref, s_ref, m8_ref,
                     *, lam_init, q_tile0):
    step = pl.program_id(1)
    qb = step + q_tile0
    lp = lam_ref[...]
    lam = (jnp.exp(jnp.sum(lp[0:1] * lp[1:2], axis=-1, keepdims=True))
           - jnp.exp(jnp.sum(lp[2:3] * lp[3:4], axis=-1, keepdims=True)) + lam_init)
    q = q_ref[...]
    lane = _iota((1, BRANCH), 1)
    n_pairs = 2 * HEADS
    sub = TILE // 8
    ones_rows = jnp.ones((ONES_ROWS, TILE), BF16)

    def pair_q(qv, hm):
        return jnp.where(lane // DA_QK == hm, qv, jnp.zeros_like(qv))

    def logits(qm, nk, buf):
        m8 = None
        half = max(nk // 2, TILE)
        for r0 in range(0, nk, half):
            st = _dot_nt(k_ref[r0:r0 + half, :], qm)
            s_ref[buf, r0:r0 + half, :] = st
            mh = jnp.max(st.reshape(half // 8, 8, TILE), axis=0)
            m8 = mh if m8 is None else jnp.maximum(m8, mh)
        return m8

    def value_tile(hm, j, mb, ot):
        h = hm // 2
        st = s_ref[hm % 2, j * TILE:(j + 1) * TILE, :]
        e = jnp.exp2(st.reshape(sub, 8, TILE) - mb[None])
        vt = vt_ref[h * HEAD_DIM:(h + 1) * HEAD_DIM, j * TILE:(j + 1) * TILE]
        vt1 = jnp.concatenate([vt, ones_rows], axis=0)
        return ot + _dot(vt1, e.reshape(TILE, TILE).astype(BF16))

    def attend(nk, own_first_logits):
        n_kt = nk // TILE
        m8 = logits(pair_q(q, 0), nk, 0) if own_first_logits else m8_ref[...]
        for hm in range(n_pairs):
            mb = jnp.broadcast_to(jnp.max(m8, axis=0, keepdims=True), (8, TILE))
            if hm + 1 < n_pairs:
                m8 = logits(pair_q(q, hm + 1), nk, (hm + 1) % 2)
            else:
                m8_ref[...] = logits(pair_q(qn_ref[...], 0), T_ALL, 0)
            ot = jnp.zeros((HEAD_DIM + ONES_ROWS, TILE), F32)
            for j in range(n_kt):
                ot = value_tile(hm, j, mb, ot)
            l = ot[HEAD_DIM:HEAD_DIM + 1, :]
            ot = ot[0:HEAD_DIM, :]
            rows = slice((hm // 2) * HEAD_DIM, (hm // 2 + 1) * HEAD_DIM)
            if hm % 2 == 0:
                acc_ref[rows, :] = ot * (1.0 / l)
            else:
                acc_ref[rows, :] -= ot * (lam / l)
        o_ref[...] = (_head_rmsnorm_from_t(acc_ref[...], g_ref[...]) * (1.0 - lam_init)).astype(BF16)

    if q_tile0 == 0:
        @pl.when(qb == 0)
        def _():
            attend(CTX_LEN, True)
    else:
        @pl.when(step == 0)
        def _():
            m8_ref[...] = logits(pair_q(q, 0), T_ALL, 0)

    @pl.when(qb > 0)
    def _():
        attend(T_ALL, False)


def _diffattn_call(pa, vta, lam_p, g_row, layer_idx, need_ctx):
    bsz = pa.shape[0]
    q_tile0 = 0 if need_ctx else 1
    lam_init = 0.8 - 0.6 * math.exp(-0.3 * layer_idx)
    return pl.pallas_call(
        functools.partial(_diffattn_kernel, lam_init=lam_init, q_tile0=q_tile0),
        grid=(bsz, N_TILES - q_tile0),
        in_specs=[
            pl.BlockSpec((None, TILE, BRANCH), lambda b, t: (b, t + q_tile0, 0)),
            pl.BlockSpec((None, TILE, BRANCH), lambda b, t: (b, jnp.minimum(t + q_tile0 + 1, N_TILES - 1), 0)),
            pl.BlockSpec((None, T_ALL, BRANCH), lambda b, t: (b, 0, 1)),
            pl.BlockSpec((None, BRANCH, T_ALL), lambda b, t: (b, 0, 0)),
            pl.BlockSpec((4, DA_QK), lambda b, t: (0, 0)),
            pl.BlockSpec((1, BRANCH), lambda b, t: (0, 0)),
        ],
        out_specs=pl.BlockSpec((None, TILE, BRANCH), lambda b, t: (b, t, 0)),
        out_shape=jax.ShapeDtypeStruct((bsz, (N_TILES - q_tile0) * TILE, BRANCH), BF16),
        scratch_shapes=[pltpu.VMEM((BRANCH, TILE), F32), pltpu.VMEM((2, T_ALL, TILE), F32),
                        pltpu.VMEM((8, TILE), F32)],
        compiler_params=_params("arbitrary", "arbitrary"),
        name="diffattn",
    )(pa, pa, pa, vta, lam_p, g_row)


BAND = 2 * TILE


def _window_kernel(q_ref, k_ref, vt_ref, sink_ref, o_ref, acc_ref, s_ref, *, q_tile0):
    qb = pl.program_id(1) + q_tile0
    lane = _iota((1, KV_WIDTH), 1)
    ones_rows = jnp.ones((ONES_ROWS, TILE), BF16)
    group = HEADS // (KV_WIDTH // HEAD_DIM)

    def attend(band):
        if band:
            a = (qb - 1) * TILE
            start = jnp.clip(a - SW_WINDOW, 0, SEQ - BAND)
            row0 = pl.multiple_of(CTX_LEN + start, SW_WINDOW)
            kb = k_ref[pl.ds(row0, BAND), :]
            kpos = start + _iota((BAND, 1), 0)
            qpos = a + _iota((1, TILE), 1)
            valid = jnp.abs(qpos - kpos) <= SW_WINDOW
        sinks, maxes = [], []
        for h in range(HEADS):
            kvh, g = h // group, h % group
            qg = q_ref[:, g * KV_WIDTH:(g + 1) * KV_WIDTH]
            qm = jnp.where(lane // HEAD_DIM == kvh, qg, jnp.zeros_like(qg))
            sink = sink_ref[:, h:h + 1] * LOG2E
            sc = _dot_nt(k_ref[0:CTX_LEN, :], qm)
            s_ref[h, 0:CTX_LEN, :] = sc
            m = jnp.maximum(jnp.max(sc, axis=0, keepdims=True), sink)
            if band:
                sb = jnp.where(valid, _dot_nt(kb, qm), NEG_BIG)
                s_ref[h, CTX_LEN:CTX_LEN + BAND, :] = sb
                m = jnp.maximum(m, jnp.max(sb, axis=0, keepdims=True))
            sinks.append(sink)
            maxes.append(m)
        for h in range(HEADS):
            m = maxes[h]
            kv_rows = slice((h // group) * HEAD_DIM, (h // group + 1) * HEAD_DIM)
            vt1 = jnp.concatenate([vt_ref[kv_rows, 0:CTX_LEN], ones_rows], axis=0)
            pv = _dot(vt1, jnp.exp2(s_ref[h, 0:CTX_LEN, :] - m).astype(BF16))
            if band:
                vtb = vt_ref[kv_rows, pl.ds(row0, BAND)]
                ones_b = jnp.ones((ONES_ROWS, BAND), BF16)
                eb = jnp.exp2(s_ref[h, CTX_LEN:CTX_LEN + BAND, :] - m).astype(BF16)
                pv = pv + _dot(jnp.concatenate([vtb, ones_b], axis=0), eb)
            l = pv[HEAD_DIM:HEAD_DIM + 1, :] + jnp.exp2(sinks[h] - m)
            acc_ref[h * HEAD_DIM:(h + 1) * HEAD_DIM, :] = pv[0:HEAD_DIM, :] * (1.0 / l)
        o_ref[...] = acc_ref[...].T.astype(BF16)

    @pl.when(qb == 0)
    def _():
        attend(False)

    @pl.when(qb > 0)
    def _():
        attend(True)


def _window_call(pc, vtc, sink_row, need_ctx):
    bsz = pc.shape[0]
    q_tile0 = 0 if need_ctx else 1
    n_q = N_TILES - q_tile0
    return pl.pallas_call(
        functools.partial(_window_kernel, q_tile0=q_tile0),
        grid=(bsz, n_q),
        in_specs=[
            pl.BlockSpec((None, TILE, BRANCH), lambda b, t: (b, t + q_tile0, 0)),
            pl.BlockSpec((None, T_ALL, KV_WIDTH), lambda b, t: (b, 0, BRANCH // KV_WIDTH)),
            pl.BlockSpec((None, KV_WIDTH, T_ALL), lambda b, t: (b, 0, 0)),
            pl.BlockSpec((1, HEADS), lambda b, t: (0, 0)),
        ],
        out_specs=pl.BlockSpec((None, TILE, BRANCH), lambda b, t: (b, t, 0)),
        out_shape=jax.ShapeDtypeStruct((bsz, n_q * TILE, BRANCH), BF16),
        scratch_shapes=[pltpu.VMEM((BRANCH, TILE), F32), pltpu.VMEM((HEADS, CTX_LEN + BAND, TILE), F32)],
        compiler_params=_params("arbitrary", "arbitrary"),
        name="window",
    )(pc, pc, vtc, sink_row)


CH_PER_TILE = TILE // HG_CHUNK
HG_STATE = 64
ST_PER_TILE = TILE // HG_STATE
HG_UNROLL = 9
HG_TILE_UNROLL = 3
HG_FAST_LIMIT = 180.0
HG_FAST_SHIFT = 60.0


def _hgrn_gates(z, lb_terms):
    log2_ksig = _log_sigmoid(-z) * LOG2E
    if lb_terms is None:
        return _log_sigmoid(z) * LOG2E, _sigmoid(-z), log2_ksig
    lb, log_lb, log_1m = lb_terms
    bt = log_1m + _log_sigmoid(z)
    mx = jnp.maximum(log_lb, bt)
    log_f = mx + jnp.log(jnp.exp(log_lb - mx) + jnp.exp(bt - mx))
    return log_f * LOG2E, (1.0 - lb) * _sigmoid(-z), log_1m * LOG2E + log2_ksig


def _hgrn_intra(direction, q3, c3, u3, v3, head_ones):
    half = HG_CHUNK // 2
    slabs, meta = [], []
    for s in range(HG_CHUNK):
        us = jnp.broadcast_to(u3[:, s:s + 1, :], (CH_PER_TILE, half, BRANCH))
        for g in range(2):
            lo_row, hi_row = half * g, half * g + half - 1
            if direction == 0:
                none_valid, all_valid = hi_row < s, lo_row >= s
            else:
                none_valid, all_valid = lo_row > s, hi_row <= s
            if none_valid:
                continue
            d = c3[:, half * g:half * (g + 1), :] - us
            if not all_valid:
                row = _iota((1, half, 1), 1) + half * g
                d = jnp.where((row >= s) if direction == 0 else (row <= s), d, NEG_BIG)
            x = q3[:, half * g:half * (g + 1), :] * jnp.exp2(d)
            slabs.append(x.reshape(CH_PER_TILE * half, BRANCH).astype(BF16))
            meta.append((s, g))
    a_all = _dot(jnp.concatenate(slabs, axis=0), head_ones)
    o = [jnp.zeros((CH_PER_TILE, half, BRANCH), F32) for _ in range(2)]
    n = CH_PER_TILE * half
    vs = None
    for i, (s, g) in enumerate(meta):
        if i == 0 or meta[i - 1][0] != s:
            vs = jnp.broadcast_to(v3[:, s:s + 1, :], (CH_PER_TILE, half, BRANCH))
        o[g] = o[g] + a_all[i * n:(i + 1) * n].reshape(CH_PER_TILE, half, BRANCH) * vs
    return jnp.concatenate(o, axis=1).reshape(TILE, BRANCH)


def _hgrn_safe_block(direction, q, kk, log2_f, log2_k, v, v_heads, masks):
    rr, cc, lane, row_in, tri16, ones16, head_ones = masks
    shape3 = (CH_PER_TILE, HG_CHUNK, BRANCH)
    cum = _dot_exact_l(tri16[direction], log2_f)
    tot = _dot_exact_l(ones16, log2_f)
    o = _hgrn_intra(direction, q.reshape(shape3), cum.reshape(shape3), (cum - log2_k).reshape(shape3),
                    v.reshape(shape3), head_ones)

    def shift_rows(a, n):
        n = n % TILE
        return jnp.concatenate([a[TILE - n:, :], a[:TILE - n, :]], axis=0)

    a_heads = [jnp.zeros((TILE, TILE), F32) for _ in range(HEADS)]
    g = HG_CHUNK
    while g < HG_STATE:
        qt = (q * jnp.exp2(cum)).astype(BF16)
        ke = (kk * jnp.exp2(tot - cum)).astype(BF16)
        later_r = ((rr % (2 * g)) >= g) if direction == 0 else ((rr % (2 * g)) < g)
        later_c = ((cc % (2 * g)) >= g) if direction == 0 else ((cc % (2 * g)) < g)
        pair = ((rr // (2 * g)) == (cc // (2 * g))) & later_r & jnp.logical_not(later_c)
        for h in range(HEADS):
            qh = jnp.where(lane // HEAD_DIM == h, qt, jnp.zeros_like(qt))
            a_heads[h] = jnp.where(pair, _dot_nt(qh, ke), a_heads[h])
        later_row = ((row_in % (2 * g)) >= g) if direction == 0 else ((row_in % (2 * g)) < g)
        sign = 1 if direction == 0 else -1
        tot_other = jnp.where(later_row, shift_rows(tot, sign * g), shift_rows(tot, -sign * g))
        cum = cum + jnp.where(later_row, tot_other, 0.0)
        tot = tot + tot_other
        g *= 2
    a_cat = jnp.concatenate([a.astype(BF16) for a in a_heads], axis=1)
    return o + _dot(a_cat, v_heads)


def _hgrn_kernel(p_ref, lb_ref, g_ref, o_ref, qt_ref, ke_ref, vb_ref, dec_ref, of_ref, oi_ref, st_ref,
                 *, layer_idx):
    rr = _iota((TILE, TILE), 0)
    cc = _iota((TILE, TILE), 1)
    lane = _iota((1, BRANCH), 1)
    row_in = _iota((TILE, 1), 0)
    same16 = (rr // HG_CHUNK) == (cc // HG_CHUNK)
    same64 = (rr // HG_STATE) == (cc // HG_STATE)
    within = [same64 & (cc <= rr), same64 & (cc >= rr)]
    tri64 = [w.astype(BF16) for w in within]
    ones64 = same64.astype(BF16)
    head_mask = (rr // HEAD_DIM) == (cc // HEAD_DIM)
    safe_masks = (rr, cc, lane, row_in, [(same16 & (cc <= rr)).astype(BF16), (same16 & (cc >= rr)).astype(BF16)],
                  same16.astype(BF16), _head_sum_matrix())

    lb_terms = None
    if layer_idx > 0:
        lbp = lb_ref[...]
        lbp = lbp - jnp.max(lbp, axis=0, keepdims=True)
        sm = jnp.exp(lbp)
        sm = sm / jnp.sum(sm, axis=0, keepdims=True)
        lb = jnp.sum(sm[1:layer_idx + 1], axis=0, keepdims=True)
        lb_terms = (lb, jnp.log(lb), jnp.log(1.0 - lb))

    def tile_body(t, carry):
        rows = pl.ds(pl.multiple_of(t * TILE, TILE), TILE)
        q = p_ref[rows, 0:BRANCH] * (HEAD_DIM ** -0.5)
        v = p_ref[rows, 3 * BRANCH:4 * BRANCH]
        vbf = v.astype(BF16)
        vb_ref[rows, :] = vbf
        v_heads = jnp.concatenate([jnp.where(lane // HEAD_DIM == h, vbf, jnp.zeros_like(vbf))
                                   for h in range(HEADS)], axis=0)
        o_intra = None
        fallback = []
        for direction in (0, 1):
            z = p_ref[rows, (1 + direction) * BRANCH:(2 + direction) * BRANCH]
            log2_f, kk, log2_k = _hgrn_gates(z, lb_terms)
            cum = _dot_exact_l(tri64[direction], log2_f)
            tot = _dot_exact_l(ones64, log2_f)
            low = jnp.min(cum, axis=(0, 1), keepdims=True)
            q_fast = (q * jnp.exp2(cum + HG_FAST_SHIFT)).astype(BF16)
            k_fast = (kk * jnp.exp2(jnp.minimum(-cum, HG_FAST_LIMIT) - HG_FAST_SHIFT)).astype(BF16)
            keep = within[direction] & (low >= -HG_FAST_LIMIT)
            a_heads = [jnp.where(keep, _dot_nt(jnp.where(lane // HEAD_DIM == h, q_fast, jnp.zeros_like(q_fast)),
                                               k_fast), 0.0) for h in range(HEADS)]
            a_cat = jnp.concatenate([a.astype(BF16) for a in a_heads], axis=1)
            o_d = _dot(a_cat, v_heads)
            qt_ref[direction, rows, :] = (q * jnp.exp2(cum)).astype(BF16)
            ke_ref[direction, rows, :] = (kk * jnp.exp2(tot - cum)).astype(BF16)
            dec_ref[direction, t] = jnp.exp2(tot.reshape(ST_PER_TILE, HG_STATE, BRANCH)[:, 0, :])
            o_intra = o_d if o_intra is None else o_intra + o_d
            fallback.append((low, kk, log2_f, log2_k))
        of_ref[rows, :] = o_intra
        return rows, q, v, v_heads, fallback

    def tile_group(it, carry):
        done = [tile_body(it * HG_TILE_UNROLL + u, carry) for u in range(HG_TILE_UNROLL)]
        for rows, q, v, v_heads, fallback in done:
            for direction, (low, kk, log2_f, log2_k) in enumerate(fallback):
                @pl.when(low[0, 0] < -HG_FAST_LIMIT)
                def _():
                    of_ref[rows, :] += _hgrn_safe_block(direction, q, kk, log2_f, log2_k, v, v_heads,
                                                        safe_masks)
        return carry

    lax.fori_loop(0, N_TILES // HG_TILE_UNROLL, tile_group, 0)

    st_ref[...] = jnp.zeros_like(st_ref)
    n_ctx, n_all = CTX_LEN // HG_STATE, T_ALL // HG_STATE

    def state_step(i, direction):
        if direction == 0:
            c = i
        else:
            c = jnp.where(i < n_ctx, n_ctx - 1 - i, n_all - 1 + n_ctx - i)
        rows = pl.ds(pl.multiple_of(c * HG_STATE, HG_STATE), HG_STATE)
        st = st_ref[direction]
        oi_ref[direction, rows, :] = _dot_nt(qt_ref[direction, rows, :], st.astype(BF16))
        ds = _dot_tn(vb_ref[rows, :], ke_ref[direction, rows, :])
        dec = dec_ref[direction, c // ST_PER_TILE, pl.ds(c % ST_PER_TILE, 1), :]
        st_ref[direction] = st * dec + jnp.where(head_mask, ds, 0.0)

    def state_body(it, carry):
        for u in range(HG_UNROLL):
            for direction in (0, 1):
                state_step(it * HG_UNROLL + u, direction)
        return carry

    lax.fori_loop(0, n_all // HG_UNROLL, state_body, 0)

    o_ref[...] = _head_rmsnorm(of_ref[...] + oi_ref[0] + oi_ref[1], g_ref[...]).astype(BF16)


def _hgrn_call(pb, hg_lb, g_row, layer_idx):
    bsz = pb.shape[0]
    return pl.pallas_call(
        functools.partial(_hgrn_kernel, layer_idx=layer_idx),
        grid=(bsz,),
        in_specs=[
            pl.BlockSpec((None, T_ALL, SEG_B), lambda b: (b, 0, 0)),
            pl.BlockSpec((DEPTH, BRANCH), lambda b: (0, 0)),
            pl.BlockSpec((1, BRANCH), lambda b: (0, 0)),
        ],
        out_specs=pl.BlockSpec((None, T_ALL, BRANCH), lambda b: (b, 0, 0)),
        out_shape=jax.ShapeDtypeStruct((bsz, T_ALL, BRANCH), BF16),
        scratch_shapes=[pltpu.VMEM((2, T_ALL, BRANCH), BF16), pltpu.VMEM((2, T_ALL, BRANCH), BF16),
                        pltpu.VMEM((T_ALL, BRANCH), BF16), pltpu.VMEM((2, N_TILES, ST_PER_TILE, BRANCH), F32),
                        pltpu.VMEM((T_ALL, BRANCH), F32), pltpu.VMEM((2, T_ALL, BRANCH), F32),
                        pltpu.VMEM((2, BRANCH, BRANCH), F32)],
        compiler_params=_params("arbitrary"),
        name="hgrn",
    )(pb, hg_lb, g_row)


ML_CHUNK = 256
ML_UNROLL = 3


def _mlstm_logits(direction, c, p_ref, vt_ref, g_ref, gt_ref, ct_ref, n_ref, m_ref, consts):
    tri, valid, _, lane, row16, lane16 = consts
    rows = pl.ds(pl.multiple_of(c * ML_CHUNK, ML_CHUNK), ML_CHUNK)
    q = p_ref[rows, 0:BRANCH]
    k = p_ref[rows, BRANCH:2 * BRANCH]
    g = g_ref[rows, :]
    gt = gt_ref[:, rows]
    cum = _dot_exact_l(tri, _log_sigmoid(g) * LOG2E)
    cum_t = _dot_exact_nt(_log_sigmoid(gt[8:16, :]) * LOG2E, tri)
    ig_t = gt[0:8, :] * LOG2E
    ct = ct_ref[direction]
    n0 = n_ref[direction]
    n_hi = n0.astype(BF16).astype(F32)
    n_lo = n0 - n_hi
    n_rows = (jnp.where((row16 < HEADS) & (lane16 == row16), n_hi, 0.0)
              + jnp.where((row16 >= HEADS) & (lane16 == row16 - HEADS), n_lo, 0.0)).astype(BF16)
    inter_all = _dot_nt(jnp.concatenate([ct.astype(BF16), n_rows], axis=0), q)
    heads = []
    for h in range(HEADS):
        r = HEADS * direction + h
        cumr = cum_t[r:r + 1, :]
        ucol = g[:, r:r + 1] * LOG2E - cum[:, 2 * HEADS + r:2 * HEADS + r + 1]
        toth = cumr[:, ML_CHUNK - 1:ML_CHUNK] if direction == 0 else cumr[:, 0:1]
        m0h = m_ref[direction, :, h:h + 1]
        logd = jnp.where(valid, cumr + ucol, NEG_BIG)
        inter = cumr + m0h
        m_t = jnp.maximum(jnp.max(logd, axis=0, keepdims=True), inter)
        qm = jnp.where(lane // HEAD_DIM == h, q, jnp.zeros_like(q))
        heads.append(dict(qk=_dot_nt(k, qm), logd=logd, m_t=m_t, g0=jnp.exp2(inter - m_t),
                          a_row=toth - cumr + ig_t[r:r + 1, :], carry=toth + m0h))
    return dict(direction=direction, rows=rows, k=k, ct=ct, n0=n0, inter_all=inter_all, heads=heads)


def _mlstm_outputs(cx, vt_ref, ht_ref, ct_ref, n_ref, m_ref, consts):
    _, _, head_mask, lane, _, _ = consts
    direction, rows, k, inter_all = cx["direction"], cx["rows"], cx["k"], cx["inter_all"]
    vt = vt_ref[:, rows]
    ones_rows = jnp.ones((ONES_ROWS, ML_CHUNK), BF16)
    row16 = _iota((ONES_ROWS, ML_CHUNK), 0)
    w_rows, sp_row = [], jnp.zeros((1, BRANCH), F32)
    for h, hd in enumerate(cx["heads"]):
        m_t, g0 = hd["m_t"], hd["g0"]
        s_t = hd["qk"] * jnp.exp2(hd["logd"] - m_t)
        vt1 = jnp.concatenate([vt[h * HEAD_DIM:(h + 1) * HEAD_DIM, :], ones_rows], axis=0)
        pv = _dot(vt1, s_t.astype(BF16))
        num = pv[0:HEAD_DIM, :] + g0 * inter_all[h * HEAD_DIM:(h + 1) * HEAD_DIM, :]
        den = pv[HEAD_DIM:HEAD_DIM + 1, :] + g0 * (inter_all[BRANCH + h:BRANCH + h + 1, :]
                                                   + inter_all[BRANCH + HEADS + h:BRANCH + HEADS + h + 1, :])
        ht_ref[direction, h * HEAD_DIM:(h + 1) * HEAD_DIM, rows] = (
            num / jnp.maximum(jnp.abs(den), jnp.exp2(-m_t)))
        a_row = hd["a_row"]
        m_loc = jnp.max(a_row, axis=1, keepdims=True)
        m_new = jnp.maximum(hd["carry"], m_loc)
        sp = jnp.exp2(hd["carry"] - m_new)
        w_rows.append(jnp.exp2(a_row - m_loc) * jnp.exp2(m_loc - m_new))
        sp_row = sp_row + jnp.where(lane // HEAD_DIM == h, sp, 0.0)
        m_ref[direction, :, h:h + 1] = m_new

    w_block = jnp.concatenate([jnp.broadcast_to(w, (HEAD_DIM, ML_CHUNK)) for w in w_rows], axis=0)
    vtw = (vt.astype(F32) * w_block).astype(BF16)
    w16 = jnp.zeros((ONES_ROWS, ML_CHUNK), F32)
    for h in range(HEADS):
        w_hi = w_rows[h].astype(BF16).astype(F32)
        w16 = w16 + jnp.where(row16 == h, w_hi, 0.0) + jnp.where(row16 == HEADS + h, w_rows[h] - w_hi, 0.0)
    dall = _dot(jnp.concatenate([vtw, w16.astype(BF16)], axis=0), k)
    ct_ref[direction] = cx["ct"] * sp_row + jnp.where(head_mask, dall[0:BRANCH, :], 0.0)
    dn = jnp.zeros((1, BRANCH), F32)
    for h in range(HEADS):
        dn = dn + jnp.where(lane // HEAD_DIM == h,
                            dall[BRANCH + h:BRANCH + h + 1, :] + dall[BRANCH + HEADS + h:BRANCH + HEADS + h + 1, :], 0.0)
    n_ref[direction] = cx["n0"] * sp_row + dn


def _mlstm_kernel(p_ref, vt_ref, g_ref, gt_ref, gain_ref, o_ref, ht_ref, ct_ref, n_ref, m_ref):
    head_mask = (_iota((BRANCH, BRANCH), 0) // HEAD_DIM) == (_iota((BRANCH, BRANCH), 1) // HEAD_DIM)
    rr = _iota((ML_CHUNK, ML_CHUNK), 0)
    cc = _iota((ML_CHUNK, ML_CHUNK), 1)
    lane = _iota((1, BRANCH), 1)
    row16 = _iota((ONES_ROWS, BRANCH), 0)
    lane16 = _iota((ONES_ROWS, BRANCH), 1) // HEAD_DIM
    consts = []
    for direction in (0, 1):
        tri = ((cc <= rr) if direction == 0 else (cc >= rr)).astype(BF16)
        valid = (rr <= cc) if direction == 0 else (rr >= cc)
        consts.append((tri, valid, head_mask, lane, row16, lane16))
    ct_ref[...] = jnp.zeros_like(ct_ref)
    n_ref[...] = jnp.zeros_like(n_ref)
    m_ref[...] = jnp.zeros_like(m_ref)
    n_ctx, n_all = CTX_LEN // ML_CHUNK, T_ALL // ML_CHUNK

    def body(it, carry):
        for u in range(ML_UNROLL):
            i = it * ML_UNROLL + u
            chunk = (i, jnp.where(i < n_ctx, n_ctx - 1 - i, n_all - 1 + n_ctx - i))
            cxs = [_mlstm_logits(d, chunk[d], p_ref, vt_ref, g_ref, gt_ref, ct_ref, n_ref, m_ref, consts[d])
                   for d in (0, 1)]
            for d in (0, 1):
                _mlstm_outputs(cxs[d], vt_ref, ht_ref, ct_ref, n_ref, m_ref, consts[d])
        return carry

    lax.fori_loop(0, n_all // ML_UNROLL, body, 0)

    def out_body(t, carry):
        rows = pl.ds(pl.multiple_of(t * TILE, TILE), TILE)
        o_ref[rows, :] = _head_rmsnorm_from_t(ht_ref[0, :, rows] + ht_ref[1, :, rows],
                                              gain_ref[...]).astype(BF16)
        return carry

    lax.fori_loop(0, N_TILES, out_body, 0)


def _mlstm_call(pd, vtd, pg, pgt, g_row):
    bsz = pd.shape[0]
    return pl.pallas_call(
        _mlstm_kernel,
        grid=(bsz,),
        in_specs=[
            pl.BlockSpec((None, T_ALL, 2 * BRANCH), lambda b: (b, 0, 0)),
            pl.BlockSpec((None, BRANCH, T_ALL), lambda b: (b, 0, 0)),
            pl.BlockSpec((None, T_ALL, SEG_G), lambda b: (b, 0, 0)),
            pl.BlockSpec((None, SEG_G, T_ALL), lambda b: (b, 0, 0)),
            pl.BlockSpec((1, BRANCH), lambda b: (0, 0)),
        ],
        out_specs=pl.BlockSpec((None, T_ALL, BRANCH), lambda b: (b, 0, 0)),
        out_shape=jax.ShapeDtypeStruct((bsz, T_ALL, BRANCH), BF16),
        scratch_shapes=[pltpu.VMEM((2, BRANCH, T_ALL), F32), pltpu.VMEM((2, BRANCH, BRANCH), F32),
                        pltpu.VMEM((2, 1, BRANCH), F32), pltpu.VMEM((2, 1, LANES), F32)],
        compiler_params=_params("arbitrary"),
        name="mlstm",
    )(pd, vtd, pg, pgt, g_row)


def _outproj_kernel(x_ref, ctx_ref, ya_ref, yb_ref, yc_ref, yd_ref, po_ref, mod_ref, w_ref, fg_ref,
                    *out_refs, bsz, last):
    t = pl.program_id(1)
    is_ctx = jnp.logical_and(t == 0, not last)
    row = jnp.where(is_ctx, bsz, pl.program_id(0))
    gate_mod = mod_ref[pl.ds(row, 1), 2 * D_MODEL:3 * D_MODEL]
    po = po_ref[...].astype(F32)
    yd = yd_ref[...].astype(F32) * po[:, 0:BRANCH]
    mixed = jnp.concatenate([ya_ref[...].astype(F32), yb_ref[...].astype(F32), yc_ref[...].astype(F32), yd],
                            axis=-1)
    mixed = (mixed * po[:, BRANCH:]).astype(BF16)
    delta = gate_mod * _dot(mixed, w_ref[...])
    if last:
        xn = x_ref[...] + delta
        ms = jnp.mean(xn * xn, axis=-1, keepdims=True)
        out_refs[0][...] = xn * lax.rsqrt(ms + NORM_EPS) * fg_ref[...]
    else:
        x_out_ref, ctx_out_ref = out_refs

        @pl.when(t == 0)
        def _():
            ctx_out_ref[...] = ctx_ref[...] + delta

        @pl.when(t > 0)
        def _():
            x_out_ref[...] = x_ref[...] + delta


def _outproj_call(x, ctx, ya, yb, yc, yd, po, mod, w_out_bf, final_g, layer, last):
    bsz = x.shape[0]
    tile0 = 1 if last else 0
    rows = mod.shape[1]

    def tok(width, arr):
        off = tile0 if arr.shape[1] == T_ALL else 0
        return pl.BlockSpec((None, TILE, width), lambda b, t: (b, t + off, 0))

    lat_spec = pl.BlockSpec((None, TILE, D_MODEL), lambda b, t: (b, jnp.maximum(t + tile0 - 1, 0), 0))
    ctx_spec = pl.BlockSpec((None, CTX_LEN, D_MODEL), lambda b, t: (b, 0, 0))
    lat_shape = jax.ShapeDtypeStruct((bsz, SEQ, D_MODEL), F32)
    ctx_shape = jax.ShapeDtypeStruct((bsz, CTX_LEN, D_MODEL), F32)
    return pl.pallas_call(
        functools.partial(_outproj_kernel, bsz=bsz, last=last),
        grid=(bsz, N_TILES - tile0),
        in_specs=[lat_spec, ctx_spec, tok(BRANCH, ya), tok(BRANCH, yb), tok(BRANCH, yc), tok(BRANCH, yd),
                  tok(SEG_O, po),
                  pl.BlockSpec((None, rows, 3 * D_MODEL), lambda b, t: (layer, 0, 0)),
                  pl.BlockSpec((None, D_MODEL, D_MODEL), lambda b, t: (layer, 0, 0)),
                  pl.BlockSpec((1, D_MODEL), lambda b, t: (0, 0))],
        out_specs=lat_spec if last else [lat_spec, ctx_spec],
        out_shape=lat_shape if last else [lat_shape, ctx_shape],
        compiler_params=_params("arbitrary", "arbitrary"),
        name="outproj",
    )(x, ctx, ya, yb, yc, yd, po, mod, w_out_bf, final_g.reshape(1, D_MODEL))


def _winout_kernel(q_ref, k_ref, vt_ref, sink_ref, x_ref, ctx_ref, ya_ref, yb_ref, yd_ref, po_ref, mod_ref,
                   w_ref, fg_ref, *rest, bsz, last):
    n_out = 1 if last else 2
    out_refs, (acc_ref, s_ref, yc_ref) = rest[:n_out], rest[n_out:]
    _window_kernel(q_ref, k_ref, vt_ref, sink_ref, yc_ref, acc_ref, s_ref, q_tile0=1 if last else 0)
    _outproj_kernel(x_ref, ctx_ref, ya_ref, yb_ref, yc_ref, yd_ref, po_ref, mod_ref, w_ref, fg_ref, *out_refs,
                    bsz=bsz, last=last)


def _winout_call(pc, vtc, sink_row, x, ctx, ya, yb, yd, po, mod, w_out_bf, final_g, layer, last):
    bsz = x.shape[0]
    tile0 = 1 if last else 0
    rows = mod.shape[1]

    def tok(width, arr):
        off = tile0 if arr.shape[1] == T_ALL else 0
        return pl.BlockSpec((None, TILE, width), lambda b, t: (b, t + off, 0))

    lat_spec = pl.BlockSpec((None, TILE, D_MODEL), lambda b, t: (b, jnp.maximum(t + tile0 - 1, 0), 0))
    ctx_spec = pl.BlockSpec((None, CTX_LEN, D_MODEL), lambda b, t: (b, 0, 0))
    lat_shape = jax.ShapeDtypeStruct((bsz, SEQ, D_MODEL), F32)
    ctx_shape = jax.ShapeDtypeStruct((bsz, CTX_LEN, D_MODEL), F32)
    return pl.pallas_call(
        functools.partial(_winout_kernel, bsz=bsz, last=last),
        grid=(bsz, N_TILES - tile0),
        in_specs=[pl.BlockSpec((None, TILE, BRANCH), lambda b, t: (b, t + tile0, 0)),
                  pl.BlockSpec((None, T_ALL, KV_WIDTH), lambda b, t: (b, 0, BRANCH // KV_WIDTH)),
                  pl.BlockSpec((None, KV_WIDTH, T_ALL), lambda b, t: (b, 0, 0)),
                  pl.BlockSpec((1, HEADS), lambda b, t: (0, 0)),
                  lat_spec, ctx_spec, tok(BRANCH, ya), tok(BRANCH, yb), tok(BRANCH, yd), tok(SEG_O, po),
                  pl.BlockSpec((None, rows, 3 * D_MODEL), lambda b, t: (layer, 0, 0)),
                  pl.BlockSpec((None, D_MODEL, D_MODEL), lambda b, t: (layer, 0, 0)),
                  pl.BlockSpec((1, D_MODEL), lambda b, t: (0, 0))],
        out_specs=lat_spec if last else [lat_spec, ctx_spec],
        out_shape=lat_shape if last else [lat_shape, ctx_shape],
        scratch_shapes=[pltpu.VMEM((BRANCH, TILE), F32), pltpu.VMEM((HEADS, CTX_LEN + BAND, TILE), F32),
                        pltpu.VMEM((TILE, BRANCH), BF16)],
        compiler_params=_params("arbitrary", "arbitrary"),
        name="winout",
    )(pc, pc, vtc, sink_row, x, ctx, ya, yb, yd, po, mod, w_out_bf, final_g.reshape(1, D_MODEL))


def _rope_tables(dim):
    quarter = dim // 4
    half = dim // 2
    pos = np.arange(SEQ)
    row = (pos // GRID_W).astype(np.float32)
    col = (pos % GRID_W).astype(np.float32)
    inv = (np.float32(ROPE_BASE) ** (-np.arange(0, half, 2, dtype=np.float32) / np.float32(half))).astype(np.float32)
    ang_r = row[:, None] * inv[None, :]
    ang_c = col[:, None] * inv[None, :]
    lane = np.arange(LANES) % dim
    part = lane // quarter
    freq = lane % quarter
    ang = np.where(part[None, :] < 2, ang_r[:, freq], ang_c[:, freq]).astype(np.float32)
    cos = np.cos(ang)
    sin = np.sin(ang)
    first = (part % 2 == 0)[None, :]
    s_next = np.where(first, -sin, 0.0)
    s_prev = np.where(first, 0.0, sin)
    tab = np.stack([cos, s_next, s_prev]).astype(np.float32)
    ident = np.stack([np.ones((CTX_LEN, LANES)), np.zeros((CTX_LEN, LANES)),
                      np.zeros((CTX_LEN, LANES))]).astype(np.float32)
    return jnp.asarray(np.concatenate([ident, tab], axis=1))


def _relayout_cols(a):
    seg_a = a[..., 0:768]
    seg_b = a[..., 768:1792]
    qc = a[..., 1792:2048]
    qc = jnp.concatenate([qc[..., 0:64], qc[..., 128:192], qc[..., 64:128], qc[..., 192:256]], axis=-1)
    kvc = a[..., 2048:2304]
    seg_d = a[..., 2304:3072]
    gates = a[..., 3072:3088]
    seg_o = a[..., 3088:4368]
    pad = jnp.zeros(a.shape[:-1] + (SEG_G - 16,), a.dtype)
    return jnp.concatenate([seg_a, qc, kvc, seg_b, seg_d, gates, pad, seg_o], axis=-1)


def _segment_sources():
    ranges = [(0, 768),
              (1792, 1856), (1920, 1984), (1856, 1920), (1984, 2048),
              (2048, 2304), (768, 1792), (2304, 3072), (3072, 3088), None, (3088, 4368)]
    blocks, cur, room = [], [], LANES
    for rg in ranges:
        lo, hi = (0, SEG_G - 16) if rg is None else rg
        while lo < hi:
            n = min(room, hi - lo)
            cur.append(None if rg is None else (lo, lo + n))
            lo, room = lo + n, room - n
            if room == 0:
                blocks.append(cur)
                cur, room = [], LANES
    assert not cur and len(blocks) * LANES == PROJ_PAD
    return blocks


def _wprep_kernel(wt_ref, o_ref):
    for j, pieces in enumerate(_segment_sources()):
        rows = [jnp.zeros((SEG_G - 16, D_MODEL), F32) if p is None else wt_ref[p[0]:p[1], :] for p in pieces]
        blk = rows[0] if len(rows) == 1 else jnp.concatenate(rows, axis=0)
        o_ref[:, j * LANES:(j + 1) * LANES] = blk.T.astype(BF16)


def _wprep_call(w_in):
    depth, _, width = w_in.shape
    return pl.pallas_call(
        _wprep_kernel,
        grid=(depth,),
        in_specs=[pl.BlockSpec((None, width, D_MODEL), lambda l: (l, 0, 0), pipeline_mode=pl.Buffered(1))],
        out_specs=pl.BlockSpec((None, D_MODEL, PROJ_PAD), lambda l: (l, 0, 0)),
        out_shape=jax.ShapeDtypeStruct((depth, D_MODEL, PROJ_PAD), BF16),
        compiler_params=_params("arbitrary"),
        name="wprep",
    )(jnp.swapaxes(w_in, 1, 2))


def kernel(x, c, ctx, c_ctx, w_mod, b_mod, norm_g, w_in, b_in, diff_lam, diff_g, hg_lb, hg_g,
           sw_sink, ml_g, w_out, final_g):
    bsz = x.shape[0]
    tab_a = _rope_tables(DA_QK)
    tab_c = _rope_tables(HEAD_DIM)
    rows = ((bsz + 1 + 7) // 8) * 8
    cc = jnp.concatenate([c, c_ctx[None, :], jnp.zeros((rows - bsz - 1, D_MODEL), F32)], axis=0)
    mod = _mod_call(cc, w_mod, b_mod)
    tile4 = lambda g: jnp.tile(g, HEADS).reshape(1, BRANCH)
    w_r = _wprep_call(w_in)
    b_r = _relayout_cols(b_in)
    w_out_bf = w_out.astype(BF16)
    for l in range(DEPTH):
        last = l == DEPTH - 1
        pa, pc, pb, pd, pg, po, vta, vtc, vtd, pgt = _inproj_call(x, ctx, mod, norm_g, w_r, b_r,
                                                                 tab_a, tab_c, l)
        ya = _diffattn_call(pa, vta, diff_lam[l], tile4(diff_g[l]), l, not last)
        yb = _hgrn_call(pb, hg_lb, tile4(hg_g[l]), l)
        yd = _mlstm_call(pd, vtd, pg, pgt, tile4(ml_g[l]))
        res = _winout_call(pc, vtc, sw_sink[l].reshape(1, HEADS), x, ctx, ya, yb, yd, po, mod, w_out_bf,
                           final_g, l, last)
        if last:
            return res
        x, ctx = res
```

```python
import functools
import math

import numpy as np
import jax
import jax.numpy as jnp
from jax import lax
from jax.experimental import pallas as pl
from jax.experimental.pallas import tpu as pltpu

F32 = jnp.float32
BF16 = jnp.bfloat16

D_MODEL = 1024
SEQ = 2048
CTX_LEN = 256
T_ALL = CTX_LEN + SEQ
GRID_W = 64
DEPTH = 2
HEADS = 4
HEAD_DIM = 64
BRANCH = HEADS * HEAD_DIM
DA_QK = 32
SW_WINDOW = 128
HG_CHUNK = 16
ROPE_BASE = 10000.0
NORM_EPS = 1e-6
NEG_BIG = -1e30
LOG2E = math.log2(math.e)

TILE = 256
N_TILES = T_ALL // TILE
LANES = 128
ONES_ROWS = 16

SEG_A = 3 * BRANCH
KV_WIDTH = 2 * HEAD_DIM
SEG_C = BRANCH + 2 * KV_WIDTH
SEG_B = 4 * BRANCH
SEG_D = 3 * BRANCH
SEG_G = LANES
SEG_O = BRANCH + D_MODEL
OFF_A = 0
OFF_C = OFF_A + SEG_A
OFF_B = OFF_C + SEG_C
OFF_D = OFF_B + SEG_B
OFF_G = OFF_D + SEG_D
OFF_O = OFF_G + SEG_G
PROJ_PAD = OFF_O + SEG_O

VMEM_LIMIT = 56 * 1024 * 1024


def _params(*sem):
    return pltpu.CompilerParams(dimension_semantics=sem, vmem_limit_bytes=VMEM_LIMIT)


def _dot(a, b):
    return jnp.dot(a, b, preferred_element_type=F32)


def _dot_nt(a, b):
    return lax.dot_general(a, b, (((1,), (1,)), ((), ())), preferred_element_type=F32)


def _dot_tn(a, b):
    return lax.dot_general(a, b, (((0,), (0,)), ((), ())), preferred_element_type=F32)


def _split3(x):
    x1 = x.astype(BF16)
    r1 = x - x1.astype(F32)
    x2 = r1.astype(BF16)
    x3 = (r1 - x2.astype(F32)).astype(BF16)
    return x1, x2, x3


def _dot_exact_l(m01, x):
    x1, x2, x3 = _split3(x)
    return _dot(m01, x1) + _dot(m01, x2) + _dot(m01, x3)


def _dot_exact_r(x, m01):
    x1, x2, x3 = _split3(x)
    return _dot(x1, m01) + _dot(x2, m01) + _dot(x3, m01)


def _dot_exact_nt(x, m01):
    x1, x2, x3 = _split3(x)
    return _dot_nt(x1, m01) + _dot_nt(x2, m01) + _dot_nt(x3, m01)


def _sigmoid(z):
    e = jnp.exp(-jnp.abs(z))
    r = 1.0 / (1.0 + e)
    return jnp.where(z >= 0, r, e * r)


def _log_sigmoid(z):
    return jnp.minimum(z, 0.0) - jnp.log(1.0 + jnp.exp(-jnp.abs(z)))


def _iota(shape, dim):
    return lax.broadcasted_iota(jnp.int32, shape, dim)


def _head_sum_matrix():
    r = _iota((BRANCH, BRANCH), 0) // HEAD_DIM
    c = _iota((BRANCH, BRANCH), 1) // HEAD_DIM
    return (r == c).astype(BF16)


def _head_rmsnorm(o, g_row):
    ss = _dot_exact_r(o * o, _head_sum_matrix())
    return o * lax.rsqrt(ss * (1.0 / HEAD_DIM) + NORM_EPS) * g_row


def _head_rmsnorm_from_t(ot, g_row):
    n = ot.shape[1]
    o3 = ot.reshape(HEADS, HEAD_DIM, n)
    ss = jnp.sum(o3 * o3, axis=1, keepdims=True)
    o3 = o3 * lax.rsqrt(ss * (1.0 / HEAD_DIM) + NORM_EPS)
    return o3.reshape(BRANCH, n).T * g_row


def _mod_kernel(cc_ref, w_ref, b_ref, o_ref):
    cc = cc_ref[...]
    a = (cc * _sigmoid(cc)).astype(BF16)
    o_ref[...] = _dot(a, w_ref[...].astype(BF16)) + b_ref[...]


def _mod_call(cc, w_mod, b_mod):
    rows = cc.shape[0]
    nblk = 3
    return pl.pallas_call(
        _mod_kernel,
        grid=(DEPTH, nblk),
        in_specs=[
            pl.BlockSpec((rows, D_MODEL), lambda l, j: (0, 0)),
            pl.BlockSpec((None, D_MODEL, D_MODEL), lambda l, j: (l, 0, j)),
            pl.BlockSpec((None, 1, D_MODEL), lambda l, j: (l, 0, j)),
        ],
        out_specs=pl.BlockSpec((None, rows, D_MODEL), lambda l, j: (l, 0, j)),
        out_shape=jax.ShapeDtypeStruct((DEPTH, rows, 3 * D_MODEL), F32),
        compiler_params=_params("arbitrary", "arbitrary"),
        name="mod",
    )(cc, w_mod, b_mod.reshape(DEPTH, 1, 3 * D_MODEL))


def _rope(slab, cos, sin_next, sin_prev, off):
    nxt = pltpu.roll(slab, LANES - off, 1)
    prv = pltpu.roll(slab, off, 1)
    return slab * cos + nxt * sin_next + prv * sin_prev


def _inproj_kernel(x_ref, ctx_ref, mod_ref, ng_ref, w_ref, b_ref, ta_ref, tc_ref,
                   pa_ref, pc_ref, pb_ref, pd_ref, pg_ref, po_ref, vta_ref, vtc_ref, vtd_ref, pgt_ref,
                   *, bsz):
    is_ctx = pl.program_id(1) == 0
    row = jnp.where(is_ctx, bsz, pl.program_id(0))
    x = jnp.where(is_ctx, ctx_ref[...], x_ref[...])
    mrow = mod_ref[pl.ds(row, 1), :]
    shift = mrow[:, 0:D_MODEL]
    scale = mrow[:, D_MODEL:2 * D_MODEL]
    ms = jnp.mean(x * x, axis=-1, keepdims=True)
    h = x * lax.rsqrt(ms + NORM_EPS) * ng_ref[...]
    h = (h * (1.0 + scale) + shift).astype(BF16)

    def proj(off, width):
        return _dot(h, w_ref[:, off:off + width]) + b_ref[:, off:off + width]

    def rope_seg(acc, tab_ref, off, q_scale, k_slabs):
        cos, s_next, s_prev = tab_ref[0], tab_ref[1], tab_ref[2]
        outs = []
        for j in range(2 + k_slabs):
            r = _rope(acc[:, j * LANES:(j + 1) * LANES], cos, s_next, s_prev, off)
            outs.append(r * q_scale if j < 2 else r)
        outs.append(acc[:, (2 + k_slabs) * LANES:])
        return jnp.concatenate(outs, axis=-1)

    acco = proj(OFF_O, SEG_O)
    po_ref[...] = jnp.concatenate(
        [_sigmoid(acco[:, 0:BRANCH]), acco[:, BRANCH:] * _sigmoid(acco[:, BRANCH:])], axis=-1).astype(BF16)
    acca = rope_seg(proj(OFF_A, SEG_A), ta_ref, DA_QK // 4, DA_QK ** -0.5 * LOG2E, 2)
    pa_ref[...] = acca[:, 0:2 * BRANCH].astype(BF16)
    vta_ref[...] = acca[:, 2 * BRANCH:].T.astype(BF16)
    accc = rope_seg(proj(OFF_C, SEG_C), tc_ref, HEAD_DIM // 4, HEAD_DIM ** -0.5 * LOG2E, 1)
    pc_ref[...] = accc[:, 0:BRANCH + KV_WIDTH].astype(BF16)
    vtc_ref[...] = accc[:, BRANCH + KV_WIDTH:].T.astype(BF16)
    accd = proj(OFF_D, SEG_D)
    pd_ref[...] = jnp.concatenate(
        [accd[:, 0:BRANCH], accd[:, BRANCH:2 * BRANCH] * (HEAD_DIM ** -0.5)], axis=-1).astype(BF16)
    vtd_ref[...] = accd[:, 2 * BRANCH:].T.astype(BF16)
    gates = proj(OFF_G, SEG_G)
    pg_ref[...] = gates
    pgt_ref[...] = gates.T
    pb_ref[...] = proj(OFF_B, SEG_B)


def _inproj_call(x, ctx, mod, norm_g, w_r, b_r, tab_a, tab_c, layer):
    bsz = x.shape[0]
    rows = mod.shape[1]
    widths = [(2 * BRANCH, BF16), (BRANCH + KV_WIDTH, BF16), (SEG_B, F32), (2 * BRANCH, BF16), (SEG_G, F32),
              (SEG_O, BF16)]
    out_specs = [pl.BlockSpec((None, TILE, w), lambda b, t: (b, t, 0)) for w, _ in widths]
    out_shape = [jax.ShapeDtypeStruct((bsz, T_ALL, w), dt) for w, dt in widths]
    for rows_t, dt in ((BRANCH, BF16), (KV_WIDTH, BF16), (BRANCH, BF16), (SEG_G, F32)):
        out_specs.append(pl.BlockSpec((None, rows_t, TILE), lambda b, t: (b, 0, t)))
        out_shape.append(jax.ShapeDtypeStruct((bsz, rows_t, T_ALL), dt))
    return pl.pallas_call(
        functools.partial(_inproj_kernel, bsz=bsz),
        grid=(bsz, N_TILES),
        in_specs=[
            pl.BlockSpec((None, TILE, D_MODEL), lambda b, t: (b, jnp.maximum(t - 1, 0), 0)),
            pl.BlockSpec((None, CTX_LEN, D_MODEL), lambda b, t: (b, 0, 0)),
            pl.BlockSpec((None, rows, 3 * D_MODEL), lambda b, t: (layer, 0, 0)),
            pl.BlockSpec((None, 1, D_MODEL), lambda b, t: (layer, 0, 0)),
            pl.BlockSpec((None, D_MODEL, PROJ_PAD), lambda b, t: (layer, 0, 0)),
            pl.BlockSpec((None, 1, PROJ_PAD), lambda b, t: (layer, 0, 0)),
            pl.BlockSpec((3, TILE, LANES), lambda b, t: (0, t, 0)),
            pl.BlockSpec((3, TILE, LANES), lambda b, t: (0, t, 0)),
        ],
        out_specs=out_specs,
        out_shape=out_shape,
        compiler_params=_params("arbitrary", "arbitrary"),
        name="inproj",
    )(x, ctx, mod, norm_g.reshape(-1, 1, D_MODEL), w_r, b_r.reshape(-1, 1, PROJ_PAD), tab_a, tab_c)


def _diffattn_kernel(q_ref, qn_ref, k_ref, vt_ref, lam_ref, g_ref, o_ref, acc_ref, s_ref, m8_ref,
                     *, lam_init, q_tile0):
    step = pl.program_id(1)
    qb = step + q_tile0
    lp = lam_ref[...]
    lam = (jnp.exp(jnp.sum(lp[0:1] * lp[1:2], axis=-1, keepdims=True))
           - jnp.exp(jnp.sum(lp[2:3] * lp[3:4], axis=-1, keepdims=True)) + lam_init)
    q = q_ref[...]
    lane = _iota((1, BRANCH), 1)
    n_pairs = 2 * HEADS
    sub = TILE // 8
    ones_rows = jnp.ones((ONES_ROWS, TILE), BF16)

    def pair_q(qv, hm):
        return jnp.where(lane // DA_QK == hm, qv, jnp.zeros_like(qv))

    def logits(qm, nk, buf):
        m8 = None
        half = max(nk // 2, TILE)
        for r0 in range(0, nk, half):
            st = _dot_nt(k_ref[r0:r0 + half, :], qm)
            s_ref[buf, r0:r0 + half, :] = st
            mh = jnp.max(st.reshape(half // 8, 8, TILE), axis=0)
            m8 = mh if m8 is None else jnp.maximum(m8, mh)
        return m8

    def value_tile(hm, j, mb, ot):
        h = hm // 2
        st = s_ref[hm % 2, j * TILE:(j + 1) * TILE, :]
        e = jnp.exp2(st.reshape(sub, 8, TILE) - mb[None])
        vt = vt_ref[h * HEAD_DIM:(h + 1) * HEAD_DIM, j * TILE:(j + 1) * TILE]
        vt1 = jnp.concatenate([vt, ones_rows], axis=0)
        return ot + _dot(vt1, e.reshape(TILE, TILE).astype(BF16))

    def attend(nk, own_first_logits):
        n_kt = nk // TILE
        m8 = logits(pair_q(q, 0), nk, 0) if own_first_logits else m8_ref[...]
        for hm in range(n_pairs):
            mb = jnp.broadcast_to(jnp.max(m8, axis=0, keepdims=True), (8, TILE))
            if hm + 1 < n_pairs:
                m8 = logits(pair_q(q, hm + 1), nk, (hm + 1) % 2)
            else:
                m8_ref[...] = logits(pair_q(qn_ref[...], 0), T_ALL, 0)
            ot = jnp.zeros((HEAD_DIM + ONES_ROWS, TILE), F32)
            for j in range(n_kt):
                ot = value_tile(hm, j, mb, ot)
            l = ot[HEAD_DIM:HEAD_DIM + 1, :]
            ot = ot[0:HEAD_DIM, :]
            rows = slice((hm // 2) * HEAD_DIM, (hm // 2 + 1) * HEAD_DIM)
            if hm % 2 == 0:
                acc_ref[rows, :] = ot * (1.0 / l)
            else:
                acc_ref[rows, :] -= ot * (lam / l)
        o_ref[...] = (_head_rmsnorm_from_t(acc_ref[...], g_ref[...]) * (1.0 - lam_init)).astype(BF16)

    if q_tile0 == 0:
        @pl.when(qb == 0)
        def _():
            attend(CTX_LEN, True)
    else:
        @pl.when(step == 0)
        def _():
            m8_ref[...] = logits(pair_q(q, 0), T_ALL, 0)

    @pl.when(qb > 0)
    def _():
        attend(T_ALL, False)


def _diffattn_call(pa, vta, lam_p, g_row, layer_idx, need_ctx):
    bsz = pa.shape[0]
    q_tile0 = 0 if need_ctx else 1
    lam_init = 0.8 - 0.6 * math.exp(-0.3 * layer_idx)
    return pl.pallas_call(
        functools.partial(_diffattn_kernel, lam_init=lam_init, q_tile0=q_tile0),
        grid=(bsz, N_TILES - q_tile0),
        in_specs=[
            pl.BlockSpec((None, TILE, BRANCH), lambda b, t: (b, t + q_tile0, 0)),
            pl.BlockSpec((None, TILE, BRANCH), lambda b, t: (b, jnp.minimum(t + q_tile0 + 1, N_TILES - 1), 0)),
            pl.BlockSpec((None, T_ALL, BRANCH), lambda b, t: (b, 0, 1)),
            pl.BlockSpec((None, BRANCH, T_ALL), lambda b, t: (b, 0, 0)),
            pl.BlockSpec((4, DA_QK), lambda b, t: (0, 0)),
            pl.BlockSpec((1, BRANCH), lambda b, t: (0, 0)),
        ],
        out_specs=pl.BlockSpec((None, TILE, BRANCH), lambda b, t: (b, t, 0)),
        out_shape=jax.ShapeDtypeStruct((bsz, (N_TILES - q_tile0) * TILE, BRANCH), BF16),
        scratch_shapes=[pltpu.VMEM((BRANCH, TILE), F32), pltpu.VMEM((2, T_ALL, TILE), F32),
                        pltpu.VMEM((8, TILE), F32)],
        compiler_params=_params("arbitrary", "arbitrary"),
        name="diffattn",
    )(pa, pa, pa, vta, lam_p, g_row)


BAND = 2 * TILE


def _window_kernel(q_ref, k_ref, vt_ref, sink_ref, o_ref, acc_ref, s_ref, *, q_tile0):
    qb = pl.program_id(1) + q_tile0
    lane = _iota((1, KV_WIDTH), 1)
    ones_rows = jnp.ones((ONES_ROWS, TILE), BF16)
    group = HEADS // (KV_WIDTH // HEAD_DIM)

    def attend(band):
        if band:
            a = (qb - 1) * TILE
            start = jnp.clip(a - SW_WINDOW, 0, SEQ - BAND)
            row0 = pl.multiple_of(CTX_LEN + start, SW_WINDOW)
            kb = k_ref[pl.ds(row0, BAND), :]
            kpos = start + _iota((BAND, 1), 0)
            qpos = a + _iota((1, TILE), 1)
            valid = jnp.abs(qpos - kpos) <= SW_WINDOW
        sinks, maxes = [], []
        for h in range(HEADS):
            kvh, g = h // group, h % group
            qg = q_ref[:, g * KV_WIDTH:(g + 1) * KV_WIDTH]
            qm = jnp.where(lane // HEAD_DIM == kvh, qg, jnp.zeros_like(qg))
            sink = sink_ref[:, h:h + 1] * LOG2E
            sc = _dot_nt(k_ref[0:CTX_LEN, :], qm)
            s_ref[h, 0:CTX_LEN, :] = sc
            m = jnp.maximum(jnp.max(sc, axis=0, keepdims=True), sink)
            if band:
                sb = jnp.where(valid, _dot_nt(kb, qm), NEG_BIG)
                s_ref[h, CTX_LEN:CTX_LEN + BAND, :] = sb
                m = jnp.maximum(m, jnp.max(sb, axis=0, keepdims=True))
            sinks.append(sink)
            maxes.append(m)
        for h in range(HEADS):
            m = maxes[h]
            kv_rows = slice((h // group) * HEAD_DIM, (h // group + 1) * HEAD_DIM)
            vt1 = jnp.concatenate([vt_ref[kv_rows, 0:CTX_LEN], ones_rows], axis=0)
            pv = _dot(vt1, jnp.exp2(s_ref[h, 0:CTX_LEN, :] - m).astype(BF16))
            if band:
                vtb = vt_ref[kv_rows, pl.ds(row0, BAND)]
                ones_b = jnp.ones((ONES_ROWS, BAND), BF16)
                eb = jnp.exp2(s_ref[h, CTX_LEN:CTX_LEN + BAND, :] - m).astype(BF16)
                pv = pv + _dot(jnp.concatenate([vtb, ones_b], axis=0), eb)
            l = pv[HEAD_DIM:HEAD_DIM + 1, :] + jnp.exp2(sinks[h] - m)
            acc_ref[h * HEAD_DIM:(h + 1) * HEAD_DIM, :] = pv[0:HEAD_DIM, :] * (1.0 / l)
        o_ref[...] = acc_ref[...].T.astype(BF16)

    if q_tile0 > 0:
        attend(True)
    else:
        @pl.when(qb == 0)
        def _():
            attend(False)

        @pl.when(qb > 0)
        def _():
            attend(True)


def _window_call(pc, vtc, sink_row, need_ctx):
    bsz = pc.shape[0]
    q_tile0 = 0 if need_ctx else 1
    n_q = N_TILES - q_tile0
    return pl.pallas_call(
        functools.partial(_window_kernel, q_tile0=q_tile0),
        grid=(bsz, n_q),
        in_specs=[
            pl.BlockSpec((None, TILE, BRANCH), lambda b, t: (b, t + q_tile0, 0)),
            pl.BlockSpec((None, T_ALL, KV_WIDTH), lambda b, t: (b, 0, BRANCH // KV_WIDTH)),
            pl.BlockSpec((None, KV_WIDTH, T_ALL), lambda b, t: (b, 0, 0)),
            pl.BlockSpec((1, HEADS), lambda b, t: (0, 0)),
        ],
        out_specs=pl.BlockSpec((None, TILE, BRANCH), lambda b, t: (b, t, 0)),
        out_shape=jax.ShapeDtypeStruct((bsz, n_q * TILE, BRANCH), BF16),
        scratch_shapes=[pltpu.VMEM((BRANCH, TILE), F32), pltpu.VMEM((HEADS, CTX_LEN + BAND, TILE), F32)],
        compiler_params=_params("arbitrary", "arbitrary"),
        name="window",
    )(pc, pc, vtc, sink_row)


CH_PER_TILE = TILE // HG_CHUNK
HG_STATE = 64
ST_PER_TILE = TILE // HG_STATE
HG_UNROLL = 9
HG_TILE_UNROLL = 3
HG_FAST_LIMIT = 180.0
HG_FAST_SHIFT = 60.0


def _hgrn_gates(z, lb_terms):
    log2_ksig = _log_sigmoid(-z) * LOG2E
    if lb_terms is None:
        return _log_sigmoid(z) * LOG2E, _sigmoid(-z), log2_ksig
    lb, log_lb, log_1m = lb_terms
    bt = log_1m + _log_sigmoid(z)
    mx = jnp.maximum(log_lb, bt)
    log_f = mx + jnp.log(jnp.exp(log_lb - mx) + jnp.exp(bt - mx))
    return log_f * LOG2E, (1.0 - lb) * _sigmoid(-z), log_1m * LOG2E + log2_ksig


def _hgrn_intra(direction, q3, c3, u3, v3, head_ones):
    half = HG_CHUNK // 2
    slabs, meta = [], []
    for s in range(HG_CHUNK):
        us = jnp.broadcast_to(u3[:, s:s + 1, :], (CH_PER_TILE, half, BRANCH))
        for g in range(2):
            lo_row, hi_row = half * g, half * g + half - 1
            if direction == 0:
                none_valid, all_valid = hi_row < s, lo_row >= s
            else:
                none_valid, all_valid = lo_row > s, hi_row <= s
            if none_valid:
                continue
            d = c3[:, half * g:half * (g + 1), :] - us
            if not all_valid:
                row = _iota((1, half, 1), 1) + half * g
                d = jnp.where((row >= s) if direction == 0 else (row <= s), d, NEG_BIG)
            x = q3[:, half * g:half * (g + 1), :] * jnp.exp2(d)
            slabs.append(x.reshape(CH_PER_TILE * half, BRANCH).astype(BF16))
            meta.append((s, g))
    a_all = _dot(jnp.concatenate(slabs, axis=0), head_ones)
    o = [jnp.zeros((CH_PER_TILE, half, BRANCH), F32) for _ in range(2)]
    n = CH_PER_TILE * half
    vs = None
    for i, (s, g) in enumerate(meta):
        if i == 0 or meta[i - 1][0] != s:
            vs = jnp.broadcast_to(v3[:, s:s + 1, :], (CH_PER_TILE, half, BRANCH))
        o[g] = o[g] + a_all[i * n:(i + 1) * n].reshape(CH_PER_TILE, half, BRANCH) * vs
    return jnp.concatenate(o, axis=1).reshape(TILE, BRANCH)


def _hgrn_safe_block(direction, q, kk, log2_f, log2_k, v, v_heads, masks):
    rr, cc, lane, row_in, tri16, ones16, head_ones = masks
    shape3 = (CH_PER_TILE, HG_CHUNK, BRANCH)
    cum = _dot_exact_l(tri16[direction], log2_f)
    tot = _dot_exact_l(ones16, log2_f)
    o = _hgrn_intra(direction, q.reshape(shape3), cum.reshape(shape3), (cum - log2_k).reshape(shape3),
                    v.reshape(shape3), head_ones)

    def shift_rows(a, n):
        n = n % TILE
        return jnp.concatenate([a[TILE - n:, :], a[:TILE - n, :]], axis=0)

    a_heads = [jnp.zeros((TILE, TILE), F32) for _ in range(HEADS)]
    g = HG_CHUNK
    while g < HG_STATE:
        qt = (q * jnp.exp2(cum)).astype(BF16)
        ke = (kk * jnp.exp2(tot - cum)).astype(BF16)
        later_r = ((rr % (2 * g)) >= g) if direction == 0 else ((rr % (2 * g)) < g)
        later_c = ((cc % (2 * g)) >= g) if direction == 0 else ((cc % (2 * g)) < g)
        pair = ((rr // (2 * g)) == (cc // (2 * g))) & later_r & jnp.logical_not(later_c)
        for h in range(HEADS):
            qh = jnp.where(lane // HEAD_DIM == h, qt, jnp.zeros_like(qt))
            a_heads[h] = jnp.where(pair, _dot_nt(qh, ke), a_heads[h])
        later_row = ((row_in % (2 * g)) >= g) if direction == 0 else ((row_in % (2 * g)) < g)
        sign = 1 if direction == 0 else -1
        tot_other = jnp.where(later_row, shift_rows(tot, sign * g), shift_rows(tot, -sign * g))
        cum = cum + jnp.where(later_row, tot_other, 0.0)
        tot = tot + tot_other
        g *= 2
    a_cat = jnp.concatenate([a.astype(BF16) for a in a_heads], axis=1)
    return o + _dot(a_cat, v_heads)


def _hgrn_kernel(p_ref, lb_ref, g_ref, o_ref, qt_ref, ke_ref, vb_ref, dec_ref, of_ref, oi_ref, st_ref,
                 *, layer_idx):
    rr = _iota((TILE, TILE), 0)
    cc = _iota((TILE, TILE), 1)
    lane = _iota((1, BRANCH), 1)
    row_in = _iota((TILE, 1), 0)
    same16 = (rr // HG_CHUNK) == (cc // HG_CHUNK)
    same64 = (rr // HG_STATE) == (cc // HG_STATE)
    within = [same64 & (cc <= rr), same64 & (cc >= rr)]
    tri64 = [w.astype(BF16) for w in within]
    ones64 = same64.astype(BF16)
    head_mask = (rr // HEAD_DIM) == (cc // HEAD_DIM)
    safe_masks = (rr, cc, lane, row_in, [(same16 & (cc <= rr)).astype(BF16), (same16 & (cc >= rr)).astype(BF16)],
                  same16.astype(BF16), _head_sum_matrix())

    lb_terms = None
    if layer_idx > 0:
        lbp = lb_ref[...]
        lbp = lbp - jnp.max(lbp, axis=0, keepdims=True)
        sm = jnp.exp(lbp)
        sm = sm / jnp.sum(sm, axis=0, keepdims=True)
        lb = jnp.sum(sm[1:layer_idx + 1], axis=0, keepdims=True)
        lb_terms = (lb, jnp.log(lb), jnp.log(1.0 - lb))

    def tile_body(t, carry):
        rows = pl.ds(pl.multiple_of(t * TILE, TILE), TILE)
        q = p_ref[rows, 0:BRANCH] * (HEAD_DIM ** -0.5)
        v = p_ref[rows, 3 * BRANCH:4 * BRANCH]
        vbf = v.astype(BF16)
        vb_ref[rows, :] = vbf
        v_heads = jnp.concatenate([jnp.where(lane // HEAD_DIM == h, vbf, jnp.zeros_like(vbf))
                                   for h in range(HEADS)], axis=0)
        o_intra = None
        fallback = []
        for direction in (0, 1):
            z = p_ref[rows, (1 + direction) * BRANCH:(2 + direction) * BRANCH]
            log2_f, kk, log2_k = _hgrn_gates(z, lb_terms)
            cum = _dot_exact_l(tri64[direction], log2_f)
            tot = _dot_exact_l(ones64, log2_f)
            low = jnp.min(cum, axis=(0, 1), keepdims=True)
            q_fast = (q * jnp.exp2(cum + HG_FAST_SHIFT)).astype(BF16)
            k_fast = (kk * jnp.exp2(jnp.minimum(-cum, HG_FAST_LIMIT) - HG_FAST_SHIFT)).astype(BF16)
            keep = within[direction] & (low >= -HG_FAST_LIMIT)
            a_heads = [jnp.where(keep, _dot_nt(jnp.where(lane // HEAD_DIM == h, q_fast, jnp.zeros_like(q_fast)),
                                               k_fast), 0.0) for h in range(HEADS)]
            a_cat = jnp.concatenate([a.astype(BF16) for a in a_heads], axis=1)
            o_d = _dot(a_cat, v_heads)
            qt_ref[direction, rows, :] = (q * jnp.exp2(cum)).astype(BF16)
            ke_ref[direction, rows, :] = (kk * jnp.exp2(tot - cum)).astype(BF16)
            dec_ref[direction, t] = jnp.exp2(tot.reshape(ST_PER_TILE, HG_STATE, BRANCH)[:, 0, :])
            o_intra = o_d if o_intra is None else o_intra + o_d
            fallback.append((low, kk, log2_f, log2_k))
        of_ref[rows, :] = o_intra
        return rows, q, v, v_heads, fallback

    def tile_group(it, carry):
        done = [tile_body(it * HG_TILE_UNROLL + u, carry) for u in range(HG_TILE_UNROLL)]
        for rows, q, v, v_heads, fallback in done:
            for direction, (low, kk, log2_f, log2_k) in enumerate(fallback):
                @pl.when(low[0, 0] < -HG_FAST_LIMIT)
                def _():
                    of_ref[rows, :] += _hgrn_safe_block(direction, q, kk, log2_f, log2_k, v, v_heads,
                                                        safe_masks)
        return carry

    lax.fori_loop(0, N_TILES // HG_TILE_UNROLL, tile_group, 0)

    st_ref[...] = jnp.zeros_like(st_ref)
    n_ctx, n_all = CTX_LEN // HG_STATE, T_ALL // HG_STATE

    def state_step(i, direction):
        if direction == 0:
            c = i
        else:
            c = jnp.where(i < n_ctx, n_ctx - 1 - i, n_all - 1 + n_ctx - i)
        rows = pl.ds(pl.multiple_of(c * HG_STATE, HG_STATE), HG_STATE)
        st = st_ref[direction]
        oi_ref[direction, rows, :] = _dot_nt(qt_ref[direction, rows, :], st.astype(BF16))
        ds = _dot_tn(vb_ref[rows, :], ke_ref[direction, rows, :])
        dec = dec_ref[direction, c // ST_PER_TILE, pl.ds(c % ST_PER_TILE, 1), :]
        st_ref[direction] = st * dec + jnp.where(head_mask, ds, 0.0)

    def state_body(it, carry):
        for u in range(HG_UNROLL):
            for direction in (0, 1):
                state_step(it * HG_UNROLL + u, direction)
        return carry

    lax.fori_loop(0, n_all // HG_UNROLL, state_body, 0)

    o_ref[...] = _head_rmsnorm(of_ref[...] + oi_ref[0] + oi_ref[1], g_ref[...]).astype(BF16)


def _hgrn_call(pb, hg_lb, g_row, layer_idx):
    bsz = pb.shape[0]
    return pl.pallas_call(
        functools.partial(_hgrn_kernel, layer_idx=layer_idx),
        grid=(bsz,),
        in_specs=[
            pl.BlockSpec((None, T_ALL, SEG_B), lambda b: (b, 0, 0)),
            pl.BlockSpec((DEPTH, BRANCH), lambda b: (0, 0)),
            pl.BlockSpec((1, BRANCH), lambda b: (0, 0)),
        ],
        out_specs=pl.BlockSpec((None, T_ALL, BRANCH), lambda b: (b, 0, 0)),
        out_shape=jax.ShapeDtypeStruct((bsz, T_ALL, BRANCH), BF16),
        scratch_shapes=[pltpu.VMEM((2, T_ALL, BRANCH), BF16), pltpu.VMEM((2, T_ALL, BRANCH), BF16),
                        pltpu.VMEM((T_ALL, BRANCH), BF16), pltpu.VMEM((2, N_TILES, ST_PER_TILE, BRANCH), F32),
                        pltpu.VMEM((T_ALL, BRANCH), F32), pltpu.VMEM((2, T_ALL, BRANCH), F32),
                        pltpu.VMEM((2, BRANCH, BRANCH), F32)],
        compiler_params=_params("arbitrary"),
        name="hgrn",
    )(pb, hg_lb, g_row)


ML_CHUNK = 256
ML_UNROLL = 3


def _mlstm_logits(direction, c, p_ref, vt_ref, g_ref, gt_ref, ct_ref, n_ref, m_ref, consts):
    tri, valid, _, lane, row16, lane16 = consts
    rows = pl.ds(pl.multiple_of(c * ML_CHUNK, ML_CHUNK), ML_CHUNK)
    q = p_ref[rows, 0:BRANCH]
    k = p_ref[rows, BRANCH:2 * BRANCH]
    g = g_ref[rows, :]
    gt = gt_ref[:, rows]
    cum = _dot_exact_l(tri, _log_sigmoid(g) * LOG2E)
    cum_t = _dot_exact_nt(_log_sigmoid(gt[8:16, :]) * LOG2E, tri)
    ig_t = gt[0:8, :] * LOG2E
    ct = ct_ref[direction]
    n0 = n_ref[direction]
    n_hi = n0.astype(BF16).astype(F32)
    n_lo = n0 - n_hi
    n_rows = (jnp.where((row16 < HEADS) & (lane16 == row16), n_hi, 0.0)
              + jnp.where((row16 >= HEADS) & (lane16 == row16 - HEADS), n_lo, 0.0)).astype(BF16)
    inter_all = _dot_nt(jnp.concatenate([ct.astype(BF16), n_rows], axis=0), q)
    heads = []
    for h in range(HEADS):
        r = HEADS * direction + h
        cumr = cum_t[r:r + 1, :]
        ucol = g[:, r:r + 1] * LOG2E - cum[:, 2 * HEADS + r:2 * HEADS + r + 1]
        toth = cumr[:, ML_CHUNK - 1:ML_CHUNK] if direction == 0 else cumr[:, 0:1]
        m0h = m_ref[direction, :, h:h + 1]
        logd = jnp.where(valid, cumr + ucol, NEG_BIG)
        inter = cumr + m0h
        m_t = jnp.maximum(jnp.max(logd, axis=0, keepdims=True), inter)
        qm = jnp.where(lane // HEAD_DIM == h, q, jnp.zeros_like(q))
        heads.append(dict(qk=_dot_nt(k, qm), logd=logd, m_t=m_t, g0=jnp.exp2(inter - m_t),
                          a_row=toth - cumr + ig_t[r:r + 1, :], carry=toth + m0h))
    return dict(direction=direction, rows=rows, k=k, ct=ct, n0=n0, inter_all=inter_all, heads=heads)


def _mlstm_outputs(cx, vt_ref, ht_ref, ct_ref, n_ref, m_ref, consts):
    _, _, head_mask, lane, _, _ = consts
    direction, rows, k, inter_all = cx["direction"], cx["rows"], cx["k"], cx["inter_all"]
    vt = vt_ref[:, rows]
    ones_rows = jnp.ones((ONES_ROWS, ML_CHUNK), BF16)
    row16 = _iota((ONES_ROWS, ML_CHUNK), 0)
    w_rows, sp_row = [], jnp.zeros((1, BRANCH), F32)
    for h, hd in enumerate(cx["heads"]):
        m_t, g0 = hd["m_t"], hd["g0"]
        s_t = hd["qk"] * jnp.exp2(hd["logd"] - m_t)
        vt1 = jnp.concatenate([vt[h * HEAD_DIM:(h + 1) * HEAD_DIM, :], ones_rows], axis=0)
        pv = _dot(vt1, s_t.astype(BF16))
        num = pv[0:HEAD_DIM, :] + g0 * inter_all[h * HEAD_DIM:(h + 1) * HEAD_DIM, :]
        den = pv[HEAD_DIM:HEAD_DIM + 1, :] + g0 * (inter_all[BRANCH + h:BRANCH + h + 1, :]
                                                   + inter_all[BRANCH + HEADS + h:BRANCH + HEADS + h + 1, :])
        ht_ref[direction, h * HEAD_DIM:(h + 1) * HEAD_DIM, rows] = (
            num / jnp.maximum(jnp.abs(den), jnp.exp2(-m_t)))
        a_row = hd["a_row"]
        m_loc = jnp.max(a_row, axis=1, keepdims=True)
        m_new = jnp.maximum(hd["carry"], m_loc)
        sp = jnp.exp2(hd["carry"] - m_new)
        w_rows.append(jnp.exp2(a_row - m_loc) * jnp.exp2(m_loc - m_new))
        sp_row = sp_row + jnp.where(lane // HEAD_DIM == h, sp, 0.0)
        m_ref[direction, :, h:h + 1] = m_new

    w_block = jnp.concatenate([jnp.broadcast_to(w, (HEAD_DIM, ML_CHUNK)) for w in w_rows], axis=0)
    vtw = (vt.astype(F32) * w_block).astype(BF16)
    w16 = jnp.zeros((ONES_ROWS, ML_CHUNK), F32)
    for h in range(HEADS):
        w_hi = w_rows[h].astype(BF16).astype(F32)
        w16 = w16 + jnp.where(row16 == h, w_hi, 0.0) + jnp.where(row16 == HEADS + h, w_rows[h] - w_hi, 0.0)
    dall = _dot(jnp.concatenate([vtw, w16.astype(BF16)], axis=0), k)
    ct_ref[direction] = cx["ct"] * sp_row + jnp.where(head_mask, dall[0:BRANCH, :], 0.0)
    dn = jnp.zeros((1, BRANCH), F32)
    for h in range(HEADS):
        dn = dn + jnp.where(lane // HEAD_DIM == h,
                            dall[BRANCH + h:BRANCH + h + 1, :] + dall[BRANCH + HEADS + h:BRANCH + HEADS + h + 1, :], 0.0)
    n_ref[direction] = cx["n0"] * sp_row + dn


def _mlstm_kernel(p_ref, vt_ref, g_ref, gt_ref, gain_ref, o_ref, ht_ref, ct_ref, n_ref, m_ref):
    head_mask = (_iota((BRANCH, BRANCH), 0) // HEAD_DIM) == (_iota((BRANCH, BRANCH), 1) // HEAD_DIM)
    rr = _iota((ML_CHUNK, ML_CHUNK), 0)
    cc = _iota((ML_CHUNK, ML_CHUNK), 1)
    lane = _iota((1, BRANCH), 1)
    row16 = _iota((ONES_ROWS, BRANCH), 0)
    lane16 = _iota((ONES_ROWS, BRANCH), 1) // HEAD_DIM
    consts = []
    for direction in (0, 1):
        tri = ((cc <= rr) if direction == 0 else (cc >= rr)).astype(BF16)
        valid = (rr <= cc) if direction == 0 else (rr >= cc)
        consts.append((tri, valid, head_mask, lane, row16, lane16))
    ct_ref[...] = jnp.zeros_like(ct_ref)
    n_ref[...] = jnp.zeros_like(n_ref)
    m_ref[...] = jnp.zeros_like(m_ref)
    n_ctx, n_all = CTX_LEN // ML_CHUNK, T_ALL // ML_CHUNK

    def body(it, carry):
        for u in range(ML_UNROLL):
            i = it * ML_UNROLL + u
            chunk = (i, jnp.where(i < n_ctx, n_ctx - 1 - i, n_all - 1 + n_ctx - i))
            cxs = [_mlstm_logits(d, chunk[d], p_ref, vt_ref, g_ref, gt_ref, ct_ref, n_ref, m_ref, consts[d])
                   for d in (0, 1)]
            for d in (0, 1):
                _mlstm_outputs(cxs[d], vt_ref, ht_ref, ct_ref, n_ref, m_ref, consts[d])
        return carry

    lax.fori_loop(0, n_all // ML_UNROLL, body, 0)

    def out_body(t, carry):
        rows = pl.ds(pl.multiple_of(t * TILE, TILE), TILE)
        o_ref[rows, :] = _head_rmsnorm_from_t(ht_ref[0, :, rows] + ht_ref[1, :, rows],
                                              gain_ref[...]).astype(BF16)
        return carry

    lax.fori_loop(0, N_TILES, out_body, 0)


def _mlstm_call(pd, vtd, pg, pgt, g_row):
    bsz = pd.shape[0]
    return pl.pallas_call(
        _mlstm_kernel,
        grid=(bsz,),
        in_specs=[
            pl.BlockSpec((None, T_ALL, 2 * BRANCH), lambda b: (b, 0, 0)),
            pl.BlockSpec((None, BRANCH, T_ALL), lambda b: (b, 0, 0)),
            pl.BlockSpec((None, T_ALL, SEG_G), lambda b: (b, 0, 0)),
            pl.BlockSpec((None, SEG_G, T_ALL), lambda b: (b, 0, 0)),
            pl.BlockSpec((1, BRANCH), lambda b: (0, 0)),
        ],
        out_specs=pl.BlockSpec((None, T_ALL, BRANCH), lambda b: (b, 0, 0)),
        out_shape=jax.ShapeDtypeStruct((bsz, T_ALL, BRANCH), BF16),
        scratch_shapes=[pltpu.VMEM((2, BRANCH, T_ALL), F32), pltpu.VMEM((2, BRANCH, BRANCH), F32),
                        pltpu.VMEM((2, 1, BRANCH), F32), pltpu.VMEM((2, 1, LANES), F32)],
        compiler_params=_params("arbitrary"),
        name="mlstm",
    )(pd, vtd, pg, pgt, g_row)


def _outproj_kernel(x_ref, ctx_ref, ya_ref, yb_ref, yc_ref, yd_ref, po_ref, mod_ref, w_ref, fg_ref,
                    *out_refs, bsz, last):
    t = pl.program_id(1)
    is_ctx = jnp.logical_and(t == 0, not last)
    row = jnp.where(is_ctx, bsz, pl.program_id(0))
    gate_mod = mod_ref[pl.ds(row, 1), 2 * D_MODEL:3 * D_MODEL]
    po = po_ref[...].astype(F32)
    yd = yd_ref[...].astype(F32) * po[:, 0:BRANCH]
    mixed = jnp.concatenate([ya_ref[...].astype(F32), yb_ref[...].astype(F32), yc_ref[...].astype(F32), yd],
                            axis=-1)
    mixed = (mixed * po[:, BRANCH:]).astype(BF16)
    delta = gate_mod * _dot(mixed, w_ref[...])
    if last:
        xn = x_ref[...] + delta
        ms = jnp.mean(xn * xn, axis=-1, keepdims=True)
        out_refs[0][...] = xn * lax.rsqrt(ms + NORM_EPS) * fg_ref[...]
    else:
        x_out_ref, ctx_out_ref = out_refs

        @pl.when(t == 0)
        def _():
            ctx_out_ref[...] = ctx_ref[...] + delta

        @pl.when(t > 0)
        def _():
            x_out_ref[...] = x_ref[...] + delta


def _outproj_call(x, ctx, ya, yb, yc, yd, po, mod, w_out_bf, final_g, layer, last):
    bsz = x.shape[0]
    tile0 = 1 if last else 0
    rows = mod.shape[1]

    def tok(width, arr):
        off = tile0 if arr.shape[1] == T_ALL else 0
        return pl.BlockSpec((None, TILE, width), lambda b, t: (b, t + off, 0))

    lat_spec = pl.BlockSpec((None, TILE, D_MODEL), lambda b, t: (b, jnp.maximum(t + tile0 - 1, 0), 0))
    ctx_spec = pl.BlockSpec((None, CTX_LEN, D_MODEL), lambda b, t: (b, 0, 0))
    lat_shape = jax.ShapeDtypeStruct((bsz, SEQ, D_MODEL), F32)
    ctx_shape = jax.ShapeDtypeStruct((bsz, CTX_LEN, D_MODEL), F32)
    return pl.pallas_call(
        functools.partial(_outproj_kernel, bsz=bsz, last=last),
        grid=(bsz, N_TILES - tile0),
        in_specs=[lat_spec, ctx_spec, tok(BRANCH, ya), tok(BRANCH, yb), tok(BRANCH, yc), tok(BRANCH, yd),
                  tok(SEG_O, po),
                  pl.BlockSpec((None, rows, 3 * D_MODEL), lambda b, t: (layer, 0, 0)),
                  pl.BlockSpec((None, D_MODEL, D_MODEL), lambda b, t: (layer, 0, 0)),
                  pl.BlockSpec((1, D_MODEL), lambda b, t: (0, 0))],
        out_specs=lat_spec if last else [lat_spec, ctx_spec],
        out_shape=lat_shape if last else [lat_shape, ctx_shape],
        compiler_params=_params("arbitrary", "arbitrary"),
        name="outproj",
    )(x, ctx, ya, yb, yc, yd, po, mod, w_out_bf, final_g.reshape(1, D_MODEL))


def _winout_kernel(q_ref, k_ref, vt_ref, sink_ref, x_ref, ctx_ref, ya_ref, yb_ref, yd_ref, po_ref, mod_ref,
                   w_ref, fg_ref, *rest, bsz, last):
    n_out = 1 if last else 2
    out_refs, (acc_ref, s_ref, yc_ref) = rest[:n_out], rest[n_out:]
    _window_kernel(q_ref, k_ref, vt_ref, sink_ref, yc_ref, acc_ref, s_ref, q_tile0=1 if last else 0)
    _outproj_kernel(x_ref, ctx_ref, ya_ref, yb_ref, yc_ref, yd_ref, po_ref, mod_ref, w_ref, fg_ref, *out_refs,
                    bsz=bsz, last=last)


def _winout_call(pc, vtc, sink_row, x, ctx, ya, yb, yd, po, mod, w_out_bf, final_g, layer, last):
    bsz = x.shape[0]
    tile0 = 1 if last else 0
    rows = mod.shape[1]

    def tok(width, arr):
        off = tile0 if arr.shape[1] == T_ALL else 0
        return pl.BlockSpec((None, TILE, width), lambda b, t: (b, t + off, 0))

    lat_spec = pl.BlockSpec((None, TILE, D_MODEL), lambda b, t: (b, jnp.maximum(t + tile0 - 1, 0), 0))
    ctx_spec = pl.BlockSpec((None, CTX_LEN, D_MODEL), lambda b, t: (b, 0, 0))
    lat_shape = jax.ShapeDtypeStruct((bsz, SEQ, D_MODEL), F32)
    ctx_shape = jax.ShapeDtypeStruct((bsz, CTX_LEN, D_MODEL), F32)
    return pl.pallas_call(
        functools.partial(_winout_kernel, bsz=bsz, last=last),
        grid=(bsz, N_TILES - tile0),
        in_specs=[pl.BlockSpec((None, TILE, BRANCH), lambda b, t: (b, t + tile0, 0)),
                  pl.BlockSpec((None, T_ALL, KV_WIDTH), lambda b, t: (b, 0, BRANCH // KV_WIDTH)),
                  pl.BlockSpec((None, KV_WIDTH, T_ALL), lambda b, t: (b, 0, 0)),
                  pl.BlockSpec((1, HEADS), lambda b, t: (0, 0)),
                  lat_spec, ctx_spec, tok(BRANCH, ya), tok(BRANCH, yb), tok(BRANCH, yd), tok(SEG_O, po),
                  pl.BlockSpec((None, rows, 3 * D_MODEL), lambda b, t: (layer, 0, 0)),
                  pl.BlockSpec((None, D_MODEL, D_MODEL), lambda b, t: (layer, 0, 0)),
                  pl.BlockSpec((1, D_MODEL), lambda b, t: (0, 0))],
        out_specs=lat_spec if last else [lat_spec, ctx_spec],
        out_shape=lat_shape if last else [lat_shape, ctx_shape],
        scratch_shapes=[pltpu.VMEM((BRANCH, TILE), F32), pltpu.VMEM((HEADS, CTX_LEN + BAND, TILE), F32),
                        pltpu.VMEM((TILE, BRANCH), BF16)],
        compiler_params=_params("arbitrary", "arbitrary"),
        name="winout",
    )(pc, pc, vtc, sink_row, x, ctx, ya, yb, yd, po, mod, w_out_bf, final_g.reshape(1, D_MODEL))


def _rope_tables(dim):
    quarter = dim // 4
    half = dim // 2
    pos = np.arange(SEQ)
    row = (pos // GRID_W).astype(np.float32)
    col = (pos % GRID_W).astype(np.float32)
    inv = (np.float32(ROPE_BASE) ** (-np.arange(0, half, 2, dtype=np.float32) / np.float32(half))).astype(np.float32)
    ang_r = row[:, None] * inv[None, :]
    ang_c = col[:, None] * inv[None, :]
    lane = np.arange(LANES) % dim
    part = lane // quarter
    freq = lane % quarter
    ang = np.where(part[None, :] < 2, ang_r[:, freq], ang_c[:, freq]).astype(np.float32)
    cos = np.cos(ang)
    sin = np.sin(ang)
    first = (part % 2 == 0)[None, :]
    s_next = np.where(first, -sin, 0.0)
    s_prev = np.where(first, 0.0, sin)
    tab = np.stack([cos, s_next, s_prev]).astype(np.float32)
    ident = np.stack([np.ones((CTX_LEN, LANES)), np.zeros((CTX_LEN, LANES)),
                      np.zeros((CTX_LEN, LANES))]).astype(np.float32)
    return jnp.asarray(np.concatenate([ident, tab], axis=1))


def _relayout_cols(a):
    seg_a = a[..., 0:768]
    seg_b = a[..., 768:1792]
    qc = a[..., 1792:2048]
    qc = jnp.concatenate([qc[..., 0:64], qc[..., 128:192], qc[..., 64:128], qc[..., 192:256]], axis=-1)
    kvc = a[..., 2048:2304]
    seg_d = a[..., 2304:3072]
    gates = a[..., 3072:3088]
    seg_o = a[..., 3088:4368]
    pad = jnp.zeros(a.shape[:-1] + (SEG_G - 16,), a.dtype)
    return jnp.concatenate([seg_a, qc, kvc, seg_b, seg_d, gates, pad, seg_o], axis=-1)


def _segment_sources():
    ranges = [(0, 768),
              (1792, 1856), (1920, 1984), (1856, 1920), (1984, 2048),
              (2048, 2304), (768, 1792), (2304, 3072), (3072, 3088), None, (3088, 4368)]
    blocks, cur, room = [], [], LANES
    for rg in ranges:
        lo, hi = (0, SEG_G - 16) if rg is None else rg
        while lo < hi:
            n = min(room, hi - lo)
            cur.append(None if rg is None else (lo, lo + n))
            lo, room = lo + n, room - n
            if room == 0:
                blocks.append(cur)
                cur, room = [], LANES
    assert not cur and len(blocks) * LANES == PROJ_PAD
    return blocks


def _wprep_kernel(wt_ref, o_ref):
    for j, pieces in enumerate(_segment_sources()):
        rows = [jnp.zeros((SEG_G - 16, D_MODEL), F32) if p is None else wt_ref[p[0]:p[1], :] for p in pieces]
        blk = rows[0] if len(rows) == 1 else jnp.concatenate(rows, axis=0)
        o_ref[:, j * LANES:(j + 1) * LANES] = blk.T.astype(BF16)


def _wprep_call(w_in):
    depth, _, width = w_in.shape
    return pl.pallas_call(
        _wprep_kernel,
        grid=(depth,),
        in_specs=[pl.BlockSpec((None, width, D_MODEL), lambda l: (l, 0, 0), pipeline_mode=pl.Buffered(1))],
        out_specs=pl.BlockSpec((None, D_MODEL, PROJ_PAD), lambda l: (l, 0, 0)),
        out_shape=jax.ShapeDtypeStruct((depth, D_MODEL, PROJ_PAD), BF16),
        compiler_params=_params("arbitrary"),
        name="wprep",
    )(jnp.swapaxes(w_in, 1, 2))


def kernel(x, c, ctx, c_ctx, w_mod, b_mod, norm_g, w_in, b_in, diff_lam, diff_g, hg_lb, hg_g,
           sw_sink, ml_g, w_out, final_g):
    bsz = x.shape[0]
    tab_a = _rope_tables(DA_QK)
    tab_c = _rope_tables(HEAD_DIM)
    rows = ((bsz + 1 + 7) // 8) * 8
    cc = jnp.concatenate([c, c_ctx[None, :], jnp.zeros((rows - bsz - 1, D_MODEL), F32)], axis=0)
    mod = _mod_call(cc, w_mod, b_mod)
    tile4 = lambda g: jnp.tile(g, HEADS).reshape(1, BRANCH)
    w_r = _wprep_call(w_in)
    b_r = _relayout_cols(b_in)
    w_out_bf = w_out.astype(BF16)
    for l in range(DEPTH):
        last = l == DEPTH - 1
        pa, pc, pb, pd, pg, po, vta, vtc, vtd, pgt = _inproj_call(x, ctx, mod, norm_g, w_r, b_r,
                                                                 tab_a, tab_c, l)
        ya = _diffattn_call(pa, vta, diff_lam[l], tile4(diff_g[l]), l, not last)
        yb = _hgrn_call(pb, hg_lb, tile4(hg_g[l]), l)
        yd = _mlstm_call(pd, vtd, pg, pgt, tile4(ml_g[l]))
        res = _winout_call(pc, vtc, sw_sink[l].reshape(1, HEADS), x, ctx, ya, yb, yd, po, mod, w_out_bf,
                           final_g, l, last)
        if last:
            return res
        x, ctx = res
```
